```python
import jax, jax.numpy as jnp
from jax import lax
import numpy as np

D_MODEL = 2048
BATCH = 4
SEQ = 2048
DEPTH = 2
DEC_BATCH = 128
DEC_SEQ = 4
PAST_LEN = 16384
PAGE_SIZE = 128

N_MIXERS = 2
N_GLA = (DEPTH + 1) // 2
N_S5 = DEPTH // 2
N_META = 16
EPS = 1e-6
GLA_HEADS = 4
GLA_KEY = D_MODEL // 2
GLA_VAL = D_MODEL
GLA_DK = GLA_KEY // GLA_HEADS
GLA_DV = GLA_VAL // GLA_HEADS
GLA_GATE_RANK = 16
GLA_TAU = 16.0
GLA_CHUNK = 16
GLA_IN = 2 * GLA_KEY + 2 * GLA_VAL + GLA_GATE_RANK
GLA_SPLITS = (GLA_KEY, 2 * GLA_KEY, 2 * GLA_KEY + GLA_VAL, 2 * GLA_KEY + 2 * GLA_VAL)
S5_GROUP = 16
S5_GROUPS = D_MODEL // S5_GROUP
S5_STATE = 64
S5_BLOCK = 16
MOE_GROUPS = 4
MOE_EPG = 8
MOE_EXPERTS = MOE_GROUPS * MOE_EPG
MOE_TOPK = 2
D_EXPERT = 256

kernel_name = "hybrid_gla_s5_hmoe_decode_step"


def rms_norm(x, g):
    xf = x.astype(jnp.float32)
    r = lax.rsqrt(jnp.mean(xf * xf, axis=-1, keepdims=True) + EPS)
    return (xf * r).astype(x.dtype) * g


def gla_chunk(S0, qkvg):
    q, k, v, g = qkvg
    qf, kf, vf = q.astype(jnp.float32), k.astype(jnp.float32), v.astype(jnp.float32)
    b = jnp.cumsum(g.astype(jnp.float32), axis=1)
    L = q.shape[1]
    o_inter = jnp.einsum('blhk,bhkv->blhv', qf * jnp.exp(b), S0)
    causal = jnp.tril(jnp.ones((L, L), dtype=bool))
    diff = b[:, :, None] - b[:, None, :]
    decay = jnp.where(causal[None, :, :, None, None], jnp.exp(jnp.minimum(diff, 0.0)), 0.0)
    scores = jnp.einsum('bthk,bshk,btshk->bhts', qf, kf, decay)
    o_intra = jnp.einsum('bhts,bshv->bthv', scores, vf)
    b_last = b[:, -1]
    S_new = jnp.exp(b_last)[..., None] * S0 + jnp.einsum(
        'bshk,bshv->bhkv', kf * jnp.exp(b_last[:, None] - b), vf)
    return S_new, o_inter + o_intra


def gla_sequence(S0, q, k, v, g):
    Bn, L = q.shape[:2]
    c = GLA_CHUNK if L % GLA_CHUNK == 0 else L
    nc = L // c

    def to_chunks(t):
        return jnp.moveaxis(t.reshape((Bn, nc, c) + t.shape[2:]), 1, 0)

    S, o = lax.scan(gla_chunk, S0.astype(jnp.float32),
                    (to_chunks(q), to_chunks(k), to_chunks(v), to_chunks(g)))
    o = jnp.moveaxis(o, 0, 1).reshape(Bn, L, GLA_HEADS, GLA_DV)
    return S, o


def gla_mixer(xn, S0, w_in, w_a2, b_a, g_o, w_o):
    Bn, L, _ = xn.shape
    proj = xn @ w_in
    q, k, v, r, a = jnp.split(proj, GLA_SPLITS, axis=-1)
    g = jax.nn.log_sigmoid((a @ w_a2 + b_a).astype(jnp.float32)) / GLA_TAU
    q = q.reshape(Bn, L, GLA_HEADS, GLA_DK) * (GLA_DK ** -0.5)
    k = k.reshape(Bn, L, GLA_HEADS, GLA_DK)
    v = v.reshape(Bn, L, GLA_HEADS, GLA_DV)
    g = g.reshape(Bn, L, GLA_HEADS, GLA_DK)
    S, o = gla_sequence(S0, q, k, v, g)
    o = o * lax.rsqrt(jnp.mean(o * o, axis=-1, keepdims=True) + EPS)
    o = o * g_o.reshape(GLA_HEADS, GLA_DV).astype(jnp.float32)
    o = o.reshape(Bn, L, GLA_VAL).astype(xn.dtype) * jax.nn.silu(r)
    return S, o @ w_o


def s5_discretize(lam_re, lam_im, log_dt, b_re, b_im):
    lr, li = lam_re.astype(jnp.float32), lam_im.astype(jnp.float32)
    dt = jnp.exp(log_dt.astype(jnp.float32))[:, None]
    mag = jnp.exp(lr * dt)
    ang = li * dt
    ab_re, ab_im = mag * jnp.cos(ang), mag * jnp.sin(ang)
    nr, ni = ab_re - 1.0, ab_im
    den = lr * lr + li * li
    f_re = (nr * lr + ni * li) / den
    f_im = (ni * lr - nr * li) / den
    br, bi = b_re.astype(jnp.float32), b_im.astype(jnp.float32)
    bb_re = f_re[..., None] * br - f_im[..., None] * bi
    bb_im = f_re[..., None] * bi + f_im[..., None] * br
    return ab_re, ab_im, bb_re, bb_im


def _cplx_affine_combine(e1, e2):
    a1r, a1i, b1r, b1i = e1
    a2r, a2i, b2r, b2i = e2
    return (a2r * a1r - a2i * a1i, a2r * a1i + a2i * a1r,
            a2r * b1r - a2i * b1i + b2r, a2r * b1i + a2i * b1r + b2i)


def s5_block(h, u, ab_re, ab_im, bb_re, bb_im, c_re, c_im, d):
    h_re, h_im = h
    Bn, L, _ = u.shape
    uf = u.astype(jnp.float32)
    ug = uf.reshape(Bn, L, S5_GROUPS, S5_GROUP)
    bu_re = jnp.einsum('blgc,gpc->blgp', ug, bb_re)
    bu_im = jnp.einsum('blgc,gpc->blgp', ug, bb_im)
    a_re = jnp.broadcast_to(ab_re, bu_re.shape)
    a_im = jnp.broadcast_to(ab_im, bu_re.shape)
    ar, ai, br, bi = lax.associative_scan(_cplx_affine_combine, (a_re, a_im, bu_re, bu_im), axis=1)
    hr = ar * h_re[:, None] - ai * h_im[:, None] + br
    hi = ar * h_im[:, None] + ai * h_re[:, None] + bi
    y = jnp.einsum('blgp,gcp->blgc', hr, c_re) - jnp.einsum('blgp,gcp->blgc', hi, c_im)
    y = y.reshape(Bn, L, D_MODEL) + d * uf
    return (hr[:, -1], hi[:, -1]), y


def s5_mixer(xn, h0_re, h0_im, lam_re, lam_im, log_dt, b_re, b_im, c_re, c_im, d, w_glu, b_glu):
    ab_re, ab_im, bb_re, bb_im = s5_discretize(lam_re, lam_im, log_dt, b_re, b_im)
    cr, ci, df = c_re.astype(jnp.float32), c_im.astype(jnp.float32), d.astype(jnp.float32)
    Bn, L, _ = xn.shape
    c = S5_BLOCK if L % S5_BLOCK == 0 else L
    nb = L // c
    ub = jnp.moveaxis(xn.reshape(Bn, nb, c, D_MODEL), 1, 0)

    def step(h, u_blk):
        return s5_block(h, u_blk, ab_re, ab_im, bb_re, bb_im, cr, ci, df)

    (hr, hi), y = lax.scan(step, (h0_re.astype(jnp.float32), h0_im.astype(jnp.float32)), ub)
    y = jnp.moveaxis(y, 0, 1).reshape(Bn, L, D_MODEL)
    y = jax.nn.gelu(y).astype(xn.dtype)
    out = y * jax.nn.sigmoid(y @ w_glu + b_glu)
    return (hr, hi), out


def hmoe(xn, w_rg, b_rg, w_re, b_re, w_gate, w_up, w_down):
    shp = xn.shape
    t = xn.reshape(-1, D_MODEL)
    pg = jax.nn.softmax((t @ w_rg + b_rg).astype(jnp.float32), axis=-1)
    gidx = jnp.argmax(pg, axis=-1)
    ptop = jnp.take_along_axis(pg, gidx[:, None], axis=-1)
    le = (jnp.einsum('td,gde->tge', t, w_re) + b_re).astype(jnp.float32)
    le = jnp.take_along_axis(le, gidx[:, None, None], axis=1)[:, 0]
    top_v, top_i = lax.top_k(le, MOE_TOPK)
    wts = jax.nn.softmax(top_v, axis=-1) * ptop
    eidx = gidx[:, None] * MOE_EPG + top_i
    combine = jnp.sum(jax.nn.one_hot(eidx, MOE_EXPERTS, dtype=jnp.float32) * wts[..., None], axis=1)
    hg = jnp.einsum('td,edf->tef', t, w_gate)
    hu = jnp.einsum('td,edf->tef', t, w_up)
    h = jax.nn.silu(hg) * hu * combine.astype(t.dtype)[..., None]
    out = jnp.einsum('tef,efd->td', h, w_down)
    return out.reshape(shp)


def setup_inputs(seed: int = 0) -> dict:
    key = jax.random.key(seed)
    ks = iter(jax.random.split(key, 40))
    f32 = jnp.float32

    def nrm(shape, scale):
        return jax.random.normal(next(ks), shape, f32) * scale

    n_idx = jnp.arange(S5_STATE, dtype=f32)
    lam_re = -0.5 + nrm((N_S5, S5_GROUPS, S5_STATE), 0.01)
    lam_im = jnp.pi * n_idx + nrm((N_S5, S5_GROUPS, S5_STATE), 0.01)
    log_dt = jax.random.uniform(next(ks), (N_S5, S5_GROUPS), f32, np.log(1e-3), np.log(1e-1))
    return {
        "x_prompt": nrm((BATCH, SEQ, D_MODEL), 1.0),
        "x_sample": nrm((DEC_BATCH, DEC_SEQ, D_MODEL), 1.0),
        "state_gla": nrm((N_GLA, DEC_BATCH, GLA_HEADS, GLA_DK, GLA_DV), 0.5),
        "state_s5_re": nrm((N_S5, DEC_BATCH, S5_GROUPS, S5_STATE), 0.3),
        "state_s5_im": nrm((N_S5, DEC_BATCH, S5_GROUPS, S5_STATE), 0.3),
        "meta_tokens": nrm((N_META, D_MODEL), 1.0),
        "norm_mix_g": 1.0 + nrm((DEPTH, D_MODEL), 0.02),
        "norm_ffn_g": 1.0 + nrm((DEPTH, D_MODEL), 0.02),
        "norm_final_g": 1.0 + nrm((D_MODEL,), 0.02),
        "gla_w_in": nrm((N_GLA, D_MODEL, GLA_IN), D_MODEL ** -0.5),
        "gla_w_a2": nrm((N_GLA, GLA_GATE_RANK, GLA_KEY), GLA_GATE_RANK ** -0.5),
        "gla_b_a": nrm((N_GLA, GLA_KEY), 0.5),
        "gla_g_o": 1.0 + nrm((N_GLA, GLA_VAL), 0.02),
        "gla_w_o": nrm((N_GLA, GLA_VAL, D_MODEL), GLA_VAL ** -0.5),
        "s5_lambda_re": lam_re,
        "s5_lambda_im": lam_im,
        "s5_log_dt": log_dt,
        "s5_b_re": nrm((N_S5, S5_GROUPS, S5_STATE, S5_GROUP), (2 * S5_GROUP) ** -0.5),
        "s5_b_im": nrm((N_S5, S5_GROUPS, S5_STATE, S5_GROUP), (2 * S5_GROUP) ** -0.5),
        "s5_c_re": nrm((N_S5, S5_GROUPS, S5_GROUP, S5_STATE), (2 * S5_STATE) ** -0.5),
        "s5_c_im": nrm((N_S5, S5_GROUPS, S5_GROUP, S5_STATE), (2 * S5_STATE) ** -0.5),
        "s5_d": nrm((N_S5, D_MODEL), 1.0),
        "s5_w_glu": nrm((N_S5, D_MODEL, D_MODEL), D_MODEL ** -0.5),
        "s5_b_glu": nrm((N_S5, D_MODEL), 0.02),
        "moe_w_rg": nrm((DEPTH, D_MODEL, MOE_GROUPS), D_MODEL ** -0.5),
        "moe_b_rg": nrm((DEPTH, MOE_GROUPS), 0.01),
        "moe_w_re": nrm((DEPTH, MOE_GROUPS, D_MODEL, MOE_EPG), D_MODEL ** -0.5),
        "moe_b_re": nrm((DEPTH, MOE_GROUPS, MOE_EPG), 0.01),
        "moe_w_gate": nrm((DEPTH, MOE_EXPERTS, D_MODEL, D_EXPERT), D_MODEL ** -0.5),
        "moe_w_up": nrm((DEPTH, MOE_EXPERTS, D_MODEL, D_EXPERT), D_MODEL ** -0.5),
        "moe_w_down": nrm((DEPTH, MOE_EXPERTS, D_EXPERT, D_MODEL), D_EXPERT ** -0.5),
    }


def reference(x_prompt, x_sample, state_gla, state_s5_re, state_s5_im, meta_tokens,
              norm_mix_g, norm_ffn_g, norm_final_g,
              gla_w_in, gla_w_a2, gla_b_a, gla_g_o, gla_w_o,
              s5_lambda_re, s5_lambda_im, s5_log_dt, s5_b_re, s5_b_im, s5_c_re, s5_c_im,
              s5_d, s5_w_glu, s5_b_glu,
              moe_w_rg, moe_b_rg, moe_w_re, moe_b_re, moe_w_gate, moe_w_up, moe_w_down):

    def run_trunk(x, gla_s, s5r_s, s5i_s):
        new_gla, new_s5r, new_s5i = [], [], []
        for i in range(DEPTH):
            j = i // N_MIXERS
            xn = rms_norm(x, norm_mix_g[i])
            if i % N_MIXERS == 0:
                S, o = gla_mixer(xn, gla_s[j], gla_w_in[j], gla_w_a2[j], gla_b_a[j],
                                 gla_g_o[j], gla_w_o[j])
                new_gla.append(S)
            else:
                (hr, hi), o = s5_mixer(xn, s5r_s[j], s5i_s[j], s5_lambda_re[j], s5_lambda_im[j],
                                       s5_log_dt[j], s5_b_re[j], s5_b_im[j], s5_c_re[j],
                                       s5_c_im[j], s5_d[j], s5_w_glu[j], s5_b_glu[j])
                new_s5r.append(hr)
                new_s5i.append(hi)
            x = x + o.astype(x.dtype)
            xn = rms_norm(x, norm_ffn_g[i])
            x = x + hmoe(xn, moe_w_rg[i], moe_b_rg[i], moe_w_re[i], moe_b_re[i],
                         moe_w_gate[i], moe_w_up[i], moe_w_down[i]).astype(x.dtype)
        y = rms_norm(x, norm_final_g)
        return y, jnp.stack(new_gla), jnp.stack(new_s5r), jnp.stack(new_s5i)

    Bp = x_prompt.shape[0]
    meta = jnp.broadcast_to(meta_tokens.astype(x_prompt.dtype)[None], (Bp, N_META, D_MODEL))
    xp = jnp.concatenate([meta, x_prompt], axis=1)
    zg = jnp.zeros((N_GLA, Bp, GLA_HEADS, GLA_DK, GLA_DV), jnp.float32)
    zs = jnp.zeros((N_S5, Bp, S5_GROUPS, S5_STATE), jnp.float32)
    yp, gla_p, s5r_p, s5i_p = run_trunk(xp, zg, zs, zs)
    y_prompt = yp[:, N_META:]

    y_sample, gla_s, s5r_s, s5i_s = run_trunk(x_sample, state_gla, state_s5_re, state_s5_im)

    return (y_prompt, y_sample, gla_p, s5r_p, s5i_p, gla_s, s5r_s, s5i_s)
```

```python
import functools

import jax
import jax.numpy as jnp
from jax import lax
from jax.experimental import pallas as pl
from jax.experimental.pallas import tpu as pltpu

f32 = jnp.float32
bf16 = jnp.bfloat16

D_MODEL = 2048
BATCH = 4
SEQ = 2048
DEC_BATCH = 128
DEC_SEQ = 4
N_META = 16
EPS = 1e-6
GLA_HEADS = 4
GLA_DK = 256
GLA_DV = 512
GLA_KEY = GLA_HEADS * GLA_DK
GLA_VAL = GLA_HEADS * GLA_DV
GLA_RANK = 16
GLA_TAU = 16.0
GLA_QKVR = 2 * GLA_KEY + 2 * GLA_VAL

N_MAIN = BATCH * SEQ
N_METAROWS = BATCH * N_META
N_SAMPLE = DEC_BATCH * DEC_SEQ
ROW_META = N_MAIN
ROW_SAMPLE = N_MAIN + N_METAROWS
N_REAL = ROW_SAMPLE + N_SAMPLE
ROW_TILE = 512
N_ROWS = -(-N_REAL // ROW_TILE) * ROW_TILE

VMEM_LIMIT = 56 * 1024 * 1024


def _cp(sem, vmem=VMEM_LIMIT):
    return pltpu.CompilerParams(dimension_semantics=sem, vmem_limit_bytes=vmem)


def _dot(a, b):
    return jnp.dot(a, b, preferred_element_type=f32)


def _dot_nt(a, b):
    return lax.dot_general(a, b, (((1,), (1,)), ((), ())), preferred_element_type=f32)


def _dot_tn(a, b):
    return lax.dot_general(a, b, (((0,), (0,)), ((), ())), preferred_element_type=f32)


def _sigmoid(x):
    return 1.0 / (1.0 + jnp.exp(-x))


def _split3(x):
    hi = x.astype(bf16)
    r1 = x - hi.astype(f32)
    mid = r1.astype(bf16)
    lo = (r1 - mid.astype(f32)).astype(bf16)
    return hi, mid, lo


def _cumsum_rows(g, C):
    if C <= 16:
        row = lax.broadcasted_iota(jnp.int32, (C, 1), 0)
        b = jnp.zeros_like(g)
        for s in range(C):
            b = b + jnp.where(row >= s, g[s:s + 1, :], 0.0)
        return b
    row = lax.broadcasted_iota(jnp.int32, (C, C), 0)
    col = lax.broadcasted_iota(jnp.int32, (C, C), 1)
    tri = jnp.where(row >= col, 1.0, 0.0).astype(bf16)
    hi, mid, lo = _split3(g)
    return _dot(tri, hi) + _dot(tri, mid) + _dot(tri, lo)


def _gla_scores(q, k, b, C, d, tree=True):
    ti = lax.broadcasted_iota(jnp.int32, (C, C), 0)
    si = lax.broadcasted_iota(jnp.int32, (C, C), 1)
    row = lax.broadcasted_iota(jnp.int32, (C, 1), 0)
    scores = jnp.zeros((C, C), f32)
    for dl in range(d):
        ks = k if dl == 0 else pltpu.roll(k, dl, axis=0)
        bs = b if dl == 0 else pltpu.roll(b, dl, axis=0)
        term = q * ks * jnp.exp(jnp.minimum(b - bs, 0.0))
        colv = jnp.sum(term, axis=1, keepdims=True)
        m = (si == ti - dl) & ((ti % d) >= dl)
        scores = jnp.where(m, colv, scores)
    z = b
    s = 1
    while tree and 2 * s < C:
        z = jnp.where((row & s) != 0, pltpu.roll(z, s, axis=0), z)
        s *= 2
        h = s
        if h < d:
            continue
        bnext = pltpu.roll(z, C - h, axis=0)
        qh = (q * jnp.exp(jnp.minimum(b - z, 0.0))).astype(bf16)
        kh = (k * jnp.exp(jnp.minimum(bnext - b, 0.0))).astype(bf16)
        sh = _dot_nt(qh, kh)
        tb = ti // h
        m = ((tb % 2) == 1) & ((si // h) == tb - 1)
        scores = jnp.where(m, sh, scores)
    return scores


def _gla_chunk(q, k, v, g, S, C, d, tree=True):
    b = _cumsum_rows(g, C)
    o = _dot((q * jnp.exp(b)).astype(bf16), S.astype(bf16))
    scores = _gla_scores(q, k, b, C, d, tree)
    o = o + _dot(scores.astype(bf16), v)
    b_last = b[C - 1:C, :]
    kd = (k * jnp.exp(b_last - b)).astype(bf16)
    r2 = lax.broadcasted_iota(jnp.int32, (GLA_DK, GLA_DK), 0)
    c2 = lax.broadcasted_iota(jnp.int32, (GLA_DK, GLA_DK), 1)
    dec_col = jnp.sum(jnp.where(r2 == c2, jnp.exp(b_last), 0.0), axis=1, keepdims=True)
    S_new = dec_col * S + _dot_tn(kd, v)
    return o, S_new


def _gla_head_epilogue(o, r, go):
    ms = jnp.mean(o * o, axis=1, keepdims=True)
    on = o * lax.rsqrt(ms + EPS) * go
    return on * (r * _sigmoid(r))


def _gla_seq_kernel(q_ref, k_ref, v_ref, r_ref, g_ref, go_ref, s0_ref, _og_in, og_ref, sout_ref, s_scr, *, C, d):
    c = pl.program_id(1)

    @pl.when(c == 0)
    def _():
        s_scr[...] = s0_ref[0]

    def head(h, carry):
        ck = pl.ds(pl.multiple_of(h * GLA_DK, GLA_DK), GLA_DK)
        cv = pl.ds(pl.multiple_of(h * GLA_DV, GLA_DV), GLA_DV)
        q = q_ref[:, ck].astype(f32) * (GLA_DK ** -0.5)
        k = k_ref[:, ck].astype(f32)
        o, S_new = _gla_chunk(q, k, v_ref[:, cv], g_ref[:, ck], s_scr[h], C, d)
        s_scr[h] = S_new
        og_ref[:, cv] = _gla_head_epilogue(o, r_ref[:, cv].astype(f32), go_ref[:, cv]).astype(og_ref.dtype)
        return carry

    lax.fori_loop(0, GLA_HEADS, head, 0)

    @pl.when(c == pl.num_programs(1) - 1)
    def _():
        sout_ref[0] = s_scr[...]


def _gla_seq_call(proj, glog, go, s0, og_buf, *, row0, C, n_chunks, d):
    blk0 = row0 // C
    rows = lambda b, c: blk0 + b * n_chunks + c
    kern = functools.partial(_gla_seq_kernel, C=C, d=d)
    return pl.pallas_call(
        kern,
        grid=(BATCH, n_chunks),
        in_specs=[
            pl.BlockSpec((C, GLA_KEY), lambda b, c: (rows(b, c), 0)),
            pl.BlockSpec((C, GLA_KEY), lambda b, c: (rows(b, c), 1)),
            pl.BlockSpec((C, GLA_VAL), lambda b, c: (rows(b, c), 1)),
            pl.BlockSpec((C, GLA_VAL), lambda b, c: (rows(b, c), 2)),
            pl.BlockSpec((C, GLA_KEY), lambda b, c: (rows(b, c), 0)),
            pl.BlockSpec((1, GLA_VAL), lambda b, c: (0, 0)),
            pl.BlockSpec((1, GLA_HEADS, GLA_DK, GLA_DV), lambda b, c: (b, 0, 0, 0)),
            pl.BlockSpec(memory_space=pl.ANY),
        ],
        out_specs=[
            pl.BlockSpec((C, GLA_VAL), lambda b, c: (rows(b, c), 0)),
            pl.BlockSpec((1, GLA_HEADS, GLA_DK, GLA_DV), lambda b, c: (b, 0, 0, 0)),
        ],
        out_shape=[
            jax.ShapeDtypeStruct(og_buf.shape, og_buf.dtype),
            jax.ShapeDtypeStruct((BATCH, GLA_HEADS, GLA_DK, GLA_DV), f32),
        ],
        scratch_shapes=[pltpu.VMEM((GLA_HEADS, GLA_DK, GLA_DV), f32)],
        input_output_aliases={7: 0},
        compiler_params=_cp(("arbitrary", "arbitrary")),
        name=f"gla_seq_c{C}",
    )(proj, proj, proj, proj, glog, go, s0, og_buf)


SAMPLE_BB = 4
SAMPLE_C = SAMPLE_BB * DEC_SEQ


def _gla_sample_kernel(q_ref, k_ref, v_ref, r_ref, g_ref, go_ref, s0_ref, _og_in, og_ref, sout_ref):
    row = lax.broadcasted_iota(jnp.int32, (SAMPLE_C, 1), 0)

    def take(x, bb):
        sh = (SAMPLE_C - DEC_SEQ * bb) % SAMPLE_C
        return jnp.where(row < DEC_SEQ, pltpu.roll(x, sh, axis=0) if sh else x, 0.0)

    def head(h, carry):
        ck = pl.ds(pl.multiple_of(h * GLA_DK, GLA_DK), GLA_DK)
        cv = pl.ds(pl.multiple_of(h * GLA_DV, GLA_DV), GLA_DV)
        q_all = q_ref[:, ck].astype(f32) * (GLA_DK ** -0.5)
        k_all = k_ref[:, ck].astype(f32)
        v_all = v_ref[:, cv].astype(f32)
        r_all = r_ref[:, cv].astype(f32)
        g_all = g_ref[:, ck]
        go = go_ref[:, cv]
        acc = jnp.zeros((SAMPLE_C, GLA_DV), f32)
        for bb in range(SAMPLE_BB):
            o, S_new = _gla_chunk(take(q_all, bb), take(k_all, bb), take(v_all, bb).astype(bf16),
                                  take(g_all, bb), s0_ref[bb, h], SAMPLE_C, DEC_SEQ, tree=False)
            sout_ref[bb, h] = S_new
            y = _gla_head_epilogue(o, take(r_all, bb), go)
            acc = jnp.where(row // DEC_SEQ == bb, pltpu.roll(y, DEC_SEQ * bb, axis=0) if bb else y, acc)
        og_ref[:, cv] = acc.astype(og_ref.dtype)
        return carry

    lax.fori_loop(0, GLA_HEADS, head, 0)


def _gla_sample_call(proj, glog, go, s0, og_buf, *, row0):
    n_seq = s0.shape[0]
    blk0 = row0 // SAMPLE_C
    st_spec = pl.BlockSpec((SAMPLE_BB, GLA_HEADS, GLA_DK, GLA_DV), lambda i: (i, 0, 0, 0))
    return pl.pallas_call(
        _gla_sample_kernel,
        grid=(n_seq // SAMPLE_BB,),
        in_specs=[
            pl.BlockSpec((SAMPLE_C, GLA_KEY), lambda i: (blk0 + i, 0)),
            pl.BlockSpec((SAMPLE_C, GLA_KEY), lambda i: (blk0 + i, 1)),
            pl.BlockSpec((SAMPLE_C, GLA_VAL), lambda i: (blk0 + i, 1)),
            pl.BlockSpec((SAMPLE_C, GLA_VAL), lambda i: (blk0 + i, 2)),
            pl.BlockSpec((SAMPLE_C, GLA_KEY), lambda i: (blk0 + i, 0)),
            pl.BlockSpec((1, GLA_VAL), lambda i: (0, 0)),
            st_spec,
            pl.BlockSpec(memory_space=pl.ANY),
        ],
        out_specs=[pl.BlockSpec((SAMPLE_C, GLA_VAL), lambda i: (blk0 + i, 0)), st_spec],
        out_shape=[jax.ShapeDtypeStruct(og_buf.shape, og_buf.dtype), jax.ShapeDtypeStruct(s0.shape, f32)],
        input_output_aliases={7: 0},
        compiler_params=_cp(("arbitrary",)),
        name="gla_sample",
    )(proj, proj, proj, proj, glog, go, s0, og_buf)


LANES = 128
MOE_GROUPS = 4
MOE_EPG = 8
MOE_EXPERTS = MOE_GROUPS * MOE_EPG
D_EXPERT = 256
ROUTE_E0 = MOE_GROUPS
MIX_TILE = 256
PROJ_NT = 1024


def _rms(x, g):
    r = lax.rsqrt(jnp.mean(x * x, axis=-1, keepdims=True) + EPS)
    return (x * r) * g


def _log_sigmoid(z):
    return jnp.minimum(z, 0.0) - jnp.log1p(jnp.exp(-jnp.abs(z)))


def _norm_gate_kernel(x_ref, gn_ref, wa1_ref, wa2_ref, ba_ref, xn_ref, gl_ref):
    xnb = _rms(x_ref[...], gn_ref[...]).astype(bf16)
    xn_ref[...] = xnb
    a = _dot(xnb, wa1_ref[...].astype(bf16))
    z = _dot(a.astype(bf16), wa2_ref[...].astype(bf16)) + ba_ref[...]
    gl_ref[...] = _log_sigmoid(z) * (1.0 / GLA_TAU)


def _norm_gate_call(x, gn, wa1, wa2, ba):
    n = x.shape[0]
    row = lambda i: (i, 0)
    fix = lambda i: (0, 0)
    return pl.pallas_call(
        _norm_gate_kernel,
        grid=(n // ROW_TILE,),
        in_specs=[pl.BlockSpec((ROW_TILE, D_MODEL), row), pl.BlockSpec((1, D_MODEL), fix),
                  pl.BlockSpec((D_MODEL, LANES), fix), pl.BlockSpec((LANES, GLA_KEY), fix),
                  pl.BlockSpec((1, GLA_KEY), fix)],
        out_specs=[pl.BlockSpec((ROW_TILE, D_MODEL), row), pl.BlockSpec((ROW_TILE, GLA_KEY), row)],
        out_shape=[jax.ShapeDtypeStruct((n, D_MODEL), bf16), jax.ShapeDtypeStruct((n, GLA_KEY), f32)],
        compiler_params=_cp(("arbitrary",)),
        name="norm_gate",
    )(x, gn, wa1, wa2, ba)


def _proj_kernel(xn_ref, w_ref, o_ref, wb_scr):
    @pl.when(pl.program_id(1) == 0)
    def _():
        wb_scr[...] = w_ref[...].astype(bf16)

    o_ref[...] = _dot(xn_ref[...], wb_scr[...]).astype(o_ref.dtype)


def _proj_call(xn, w, n_cols):
    n = xn.shape[0]
    return pl.pallas_call(
        _proj_kernel,
        grid=(n_cols // PROJ_NT, n // ROW_TILE),
        in_specs=[pl.BlockSpec((ROW_TILE, D_MODEL), lambda j, i: (i, 0)),
                  pl.BlockSpec((D_MODEL, PROJ_NT), lambda j, i: (0, j))],
        out_specs=pl.BlockSpec((ROW_TILE, PROJ_NT), lambda j, i: (i, j)),
        out_shape=jax.ShapeDtypeStruct((n, n_cols), bf16),
        scratch_shapes=[pltpu.VMEM((D_MODEL, PROJ_NT), bf16)],
        compiler_params=_cp(("arbitrary", "arbitrary")),
        name="gla_proj",
    )(xn, w)


def _route(xn, wr, br, cnt_ref):
    R = xn.shape[0]
    xh = xn.astype(bf16)
    xl = (xn - xh.astype(f32)).astype(bf16)
    wh = wr.astype(bf16)
    wl = (wr - wh.astype(f32)).astype(bf16)
    logits = _dot(xh, wh) + _dot(xl, wh) + _dot(xh, wl) + br
    lane_i = lax.broadcasted_iota(jnp.int32, (R, LANES), 1)
    lane = lane_i.astype(f32)
    neg = -jnp.inf
    big = float(LANES)
    is_g = lane_i < MOE_GROUPS
    lg = jnp.where(is_g, logits, neg)
    mg = jnp.max(lg, axis=1, keepdims=True)
    gidx = jnp.min(jnp.where(lg == mg, lane, big), axis=1, keepdims=True)
    ptop = 1.0 / jnp.sum(jnp.where(is_g, jnp.exp(logits - mg), 0.0), axis=1, keepdims=True)
    lo = ROUTE_E0 + MOE_EPG * gidx
    le = jnp.where((lane >= lo) & (lane < lo + MOE_EPG), logits, neg)
    v1 = jnp.max(le, axis=1, keepdims=True)
    i1 = jnp.min(jnp.where(le == v1, lane, big), axis=1, keepdims=True)
    le2 = jnp.where(lane == i1, neg, le)
    v2 = jnp.max(le2, axis=1, keepdims=True)
    i2 = jnp.min(jnp.where(le2 == v2, lane, big), axis=1, keepdims=True)
    s = jnp.exp(v2 - v1)
    w0 = ptop / (1.0 + s)
    w1 = ptop * s / (1.0 + s)
    oh = jnp.where((lane == i1) | (lane == i2), 1.0, 0.0)
    ri = lax.broadcasted_iota(jnp.int32, (R, R), 0)
    ci = lax.broadcasted_iota(jnp.int32, (R, R), 1)
    before = jnp.where(ri > ci, 1.0, 0.0).astype(bf16)
    tot = _dot(before, oh.astype(bf16)) + cnt_ref[...]
    rank0 = jnp.sum(jnp.where(lane == i1, tot, 0.0), axis=1, keepdims=True)
    rank1 = jnp.sum(jnp.where(lane == i2, tot, 0.0), axis=1, keepdims=True)
    cnt_ref[...] = cnt_ref[...] + jnp.sum(oh, axis=0, keepdims=True)
    vals = (i1 - ROUTE_E0, i2 - ROUTE_E0, rank0, rank1, w0, w1)
    slab = jnp.zeros((R, LANES), f32)
    for j, v in enumerate(vals):
        slab = jnp.where(lane_i == j, v, slab)
    return slab


def _router_weights(w_rg, b_rg, w_re, b_re):
    w = jnp.concatenate([w_rg, jnp.moveaxis(w_re, 0, 1).reshape(D_MODEL, MOE_EXPERTS)], axis=1)
    b = jnp.concatenate([b_rg, b_re.reshape(MOE_EXPERTS)])
    pad = LANES - w.shape[1]
    return jnp.pad(w, ((0, 0), (0, pad))), jnp.pad(b, (0, pad))[None]


def _mix_out_tail(x1, gffn_ref, wr_ref, br_ref, x1_ref, xn_ref, route_ref, cnt_ref):
    @pl.when(pl.program_id(0) == 0)
    def _():
        cnt_ref[...] = jnp.zeros_like(cnt_ref)

    x1_ref[...] = x1
    xn = _rms(x1, gffn_ref[...])
    xn_ref[...] = xn
    route_ref[...] = _route(xn, wr_ref[...], br_ref[...], cnt_ref)


def _gla_out_kernel(og_ref, x_ref, wo_ref, gffn_ref, wr_ref, br_ref, x1_ref, xn_ref, route_ref, cnt_ref):
    x1 = x_ref[...] + _dot(og_ref[...], wo_ref[...])
    _mix_out_tail(x1, gffn_ref, wr_ref, br_ref, x1_ref, xn_ref, route_ref, cnt_ref)


def _gelu_tanh(x):
    return x * (0.5 * (1.0 + jnp.tanh(0.7978845608028654 * (x + 0.044715 * (x * x * x)))))


def _s5_out_kernel(ys_ref, x_ref, gmix_ref, d_ref, wglu_ref, bglu_ref, gffn_ref, wr_ref, br_ref,
                   x1_ref, xn_ref, route_ref, cnt_ref):
    x = x_ref[...]
    u = _rms(x, gmix_ref[...])
    y = _gelu_tanh(ys_ref[...] + d_ref[...] * u)
    z = _dot(y.astype(bf16), wglu_ref[...]) + bglu_ref[...]
    _mix_out_tail(x + y * _sigmoid(z), gffn_ref, wr_ref, br_ref, x1_ref, xn_ref, route_ref, cnt_ref)


def _mix_out_call(kern, name, row_ins, fix_ins):
    n = row_ins[0].shape[0]
    row = lambda i: (i, 0)
    fix = lambda i: (0, 0)
    out_row = lambda w: pl.BlockSpec((MIX_TILE, w), row)
    return pl.pallas_call(
        kern,
        grid=(n // MIX_TILE,),
        in_specs=[pl.BlockSpec((MIX_TILE, a.shape[1]), row) for a in row_ins]
        + [pl.BlockSpec(a.shape, fix) for a in fix_ins],
        out_specs=[out_row(D_MODEL), out_row(D_MODEL), out_row(LANES), pl.BlockSpec((1, LANES), fix)],
        out_shape=[jax.ShapeDtypeStruct((n, D_MODEL), f32), jax.ShapeDtypeStruct((n, D_MODEL), f32),
                   jax.ShapeDtypeStruct((n, LANES), f32), jax.ShapeDtypeStruct((1, LANES), f32)],
        compiler_params=_cp(("arbitrary",)),
        name=name,
    )(*row_ins, *fix_ins)


EXPERT_TM = 256
MOVE_TILE = 256


def _expert_rows(n_tokens):
    worst = 2 * n_tokens + MOE_EXPERTS * (EXPERT_TM - 1)
    return -(-worst // EXPERT_TM) * EXPERT_TM


def _moe_plan(route, cnt):
    n = route.shape[0]
    n_tiles = _expert_rows(n) // EXPERT_TM
    e = route[:, 0:2].astype(jnp.int32)
    rank = route[:, 2:4].astype(jnp.int32)
    counts = cnt[0, ROUTE_E0:ROUTE_E0 + MOE_EXPERTS].astype(jnp.int32)
    padded = -(-counts // EXPERT_TM) * EXPERT_TM
    ends = jnp.cumsum(padded)
    pos = jnp.take(ends - padded, e) + rank
    tile_ends = ends // EXPERT_TM
    n_used = tile_ends[-1:]
    t = jnp.minimum(jnp.arange(n_tiles, dtype=jnp.int32), n_used[0] - 1)
    tile_expert = jnp.minimum(jnp.searchsorted(tile_ends, t, side="right"), MOE_EXPERTS - 1).astype(jnp.int32)
    return pos.reshape(n // MOVE_TILE, 1, 2 * MOVE_TILE), tile_expert, n_used.astype(jnp.int32)


def _row_copy(src, src_row, dst, dst_row, sem):
    return pltpu.make_async_copy(src.at[pl.ds(src_row, 1)], dst.at[pl.ds(dst_row, 1)], sem)


def _dispatch_kernel(pos_ref, x_ref, _xs_in, xs_ref, sem):
    def start(r, c):
        for s in range(2):
            _row_copy(x_ref, r, xs_ref, pos_ref[0, 0, 2 * r + s], sem).start()
        return c

    def wait(r, c):
        for s in range(2):
            _row_copy(x_ref, r, xs_ref, pos_ref[0, 0, 2 * r + s], sem).wait()
        return c

    lax.fori_loop(0, MOVE_TILE, start, 0, unroll=8)
    lax.fori_loop(0, MOVE_TILE, wait, 0, unroll=8)


def _dispatch_call(xn, pos):
    n = xn.shape[0]
    xs0 = jnp.zeros((_expert_rows(n), D_MODEL), xn.dtype)
    return pl.pallas_call(
        _dispatch_kernel,
        grid=(n // MOVE_TILE,),
        in_specs=[pl.BlockSpec((1, 1, 2 * MOVE_TILE), lambda i: (i, 0, 0), memory_space=pltpu.SMEM),
                  pl.BlockSpec((MOVE_TILE, D_MODEL), lambda i: (i, 0)),
                  pl.BlockSpec(memory_space=pl.ANY)],
        out_specs=pl.BlockSpec(memory_space=pl.ANY),
        out_shape=jax.ShapeDtypeStruct(xs0.shape, xs0.dtype),
        scratch_shapes=[pltpu.SemaphoreType.DMA(())],
        input_output_aliases={2: 0},
        compiler_params=_cp(("arbitrary",)),
        name="moe_dispatch",
    )(pos, xn, xs0)


def _experts_kernel(te_ref, nu_ref, xs_ref, wg_ref, wu_ref, wd_ref, os_ref):
    used = pl.program_id(0) < nu_ref[0]

    @pl.when(used)
    def _():
        x = xs_ref[...].astype(bf16)
        hg = _dot(x, wg_ref[0].astype(bf16))
        hu = _dot(x, wu_ref[0].astype(bf16))
        h = (hg * _sigmoid(hg) * hu).astype(bf16)
        os_ref[...] = _dot(h, wd_ref[0].astype(bf16))

    @pl.when(jnp.logical_not(used))
    def _():
        os_ref[...] = jnp.zeros_like(os_ref)


def _experts_call(xs, tile_expert, n_used, wg, wu, wd):
    n_tiles = xs.shape[0] // EXPERT_TM
    rows = lambda t, te, nu: (jnp.minimum(t, nu[0] - 1), 0)
    wsel = lambda t, te, nu: (te[t], 0, 0)
    return pl.pallas_call(
        _experts_kernel,
        grid_spec=pltpu.PrefetchScalarGridSpec(
            num_scalar_prefetch=2,
            grid=(n_tiles,),
            in_specs=[pl.BlockSpec((EXPERT_TM, D_MODEL), rows),
                      pl.BlockSpec((1, D_MODEL, D_EXPERT), wsel),
                      pl.BlockSpec((1, D_MODEL, D_EXPERT), wsel),
                      pl.BlockSpec((1, D_EXPERT, D_MODEL), wsel)],
            out_specs=pl.BlockSpec((EXPERT_TM, D_MODEL), lambda t, te, nu: (t, 0)),
        ),
        out_shape=jax.ShapeDtypeStruct(xs.shape, f32),
        compiler_params=_cp(("arbitrary",)),
        name="moe_experts",
    )(tile_expert, n_used, xs, wg, wu, wd)


def _combine_kernel(pos_ref, x_ref, route_ref, gn_ref, os_ref, x2_ref, xn_ref, buf, sem):
    def start(r, c):
        for s in range(2):
            _row_copy(os_ref, pos_ref[0, 0, 2 * r + s], buf.at[s], r, sem).start()
        return c

    def wait(r, c):
        for s in range(2):
            _row_copy(os_ref, pos_ref[0, 0, 2 * r + s], buf.at[s], r, sem).wait()
        return c

    lax.fori_loop(0, MOVE_TILE, start, 0, unroll=8)
    lax.fori_loop(0, MOVE_TILE, wait, 0, unroll=8)
    route = route_ref[...]
    x2 = x_ref[...] + route[:, 4:5] * buf[0] + route[:, 5:6] * buf[1]
    x2_ref[...] = x2
    xn_ref[...] = _rms(x2, gn_ref[...]).astype(xn_ref.dtype)


def _combine_call(x1, route, pos, os_rows, gn, xn_dtype):
    n = x1.shape[0]
    row = lambda i: (i, 0)
    return pl.pallas_call(
        _combine_kernel,
        grid=(n // MOVE_TILE,),
        in_specs=[pl.BlockSpec((1, 1, 2 * MOVE_TILE), lambda i: (i, 0, 0), memory_space=pltpu.SMEM),
                  pl.BlockSpec((MOVE_TILE, D_MODEL), row), pl.BlockSpec((MOVE_TILE, LANES), row),
                  pl.BlockSpec((1, D_MODEL), lambda i: (0, 0)), pl.BlockSpec(memory_space=pl.ANY)],
        out_specs=[pl.BlockSpec((MOVE_TILE, D_MODEL), row), pl.BlockSpec((MOVE_TILE, D_MODEL), row)],
        out_shape=[jax.ShapeDtypeStruct((n, D_MODEL), f32), jax.ShapeDtypeStruct((n, D_MODEL), xn_dtype)],
        scratch_shapes=[pltpu.VMEM((2, MOVE_TILE, D_MODEL), f32), pltpu.SemaphoreType.DMA(())],
        compiler_params=_cp(("arbitrary",)),
        name="moe_combine",
    )(pos, x1, route, gn, os_rows)


S5_GROUP = 16
S5_GROUPS = D_MODEL // S5_GROUP
S5_STATE = 64
S5_CB = 128
S5_NB = D_MODEL // S5_CB
S5_GPB = S5_CB // S5_GROUP
S5_SB = S5_GPB * S5_STATE
S5_BC = S5_STATE * S5_GROUP
SUB = 8


def _s5_disc_kernel(lr_ref, li_ref, ldt_ref, bre_ref, bim_ref,
                    abr_ref, abi_ref, ab2r_ref, ab2i_ref, bbr_ref, bbi_ref, abbr_ref, abbi_ref):
    lr, li = lr_ref[...], li_ref[...]
    dt = jnp.exp(ldt_ref[...])
    mag = jnp.exp(lr * dt)
    ang = li * dt
    ab_re, ab_im = mag * jnp.cos(ang), mag * jnp.sin(ang)
    nr, ni = ab_re - 1.0, ab_im
    den = lr * lr + li * li
    f_re = (nr * lr + ni * li) / den
    f_im = (ni * lr - nr * li) / den
    abr_ref[...] = ab_re
    abi_ref[...] = ab_im
    ab2r_ref[...] = ab_re * ab_re - ab_im * ab_im
    ab2i_ref[...] = 2.0 * (ab_re * ab_im)
    pi = lax.broadcasted_iota(jnp.int32, (S5_STATE, S5_BC), 0)
    ci = lax.broadcasted_iota(jnp.int32, (S5_STATE, S5_BC), 1)
    rep = jnp.where(ci // S5_GROUP == pi, 1.0, 0.0).astype(bf16)

    def expand(v):
        hi, mid, lo = _split3(v)
        return _dot(hi, rep) + _dot(mid, rep) + _dot(lo, rep)

    fr, fi, ar, ai = expand(f_re), expand(f_im), expand(ab_re), expand(ab_im)
    br, bi = bre_ref[...], bim_ref[...]
    bb_re = fr * br - fi * bi
    bb_im = fr * bi + fi * br
    bbr_ref[...] = bb_re
    bbi_ref[...] = bb_im
    abbr_ref[...] = ar * bb_re - ai * bb_im
    abbi_ref[...] = ar * bb_im + ai * bb_re


def _s5_weights(lam_re, lam_im, log_dt, b_re, b_im, c_re, c_im):
    st = jax.ShapeDtypeStruct((S5_GROUPS, S5_STATE), f32)
    bc = jax.ShapeDtypeStruct((S5_GROUPS, S5_BC), f32)
    ab_re, ab_im, ab2_re, ab2_im, bb_re, bb_im, abb_re, abb_im = pl.pallas_call(
        _s5_disc_kernel, out_shape=[st, st, st, st, bc, bc, bc, bc], name="s5_discretize",
    )(lam_re, lam_im, log_dt[:, None], b_re.reshape(S5_GROUPS, S5_BC), b_im.reshape(S5_GROUPS, S5_BC))
    eye = jnp.eye(S5_GPB, dtype=f32)

    def in_blocks(m):
        m = m.reshape(S5_NB, S5_GPB, S5_STATE, S5_GROUP)
        return jnp.einsum("jgpc,gh->jgchp", m, eye).reshape(S5_NB, S5_CB, S5_SB)

    def out_blocks(m):
        m = m.reshape(S5_NB, S5_GPB, S5_GROUP, S5_STATE)
        return jnp.einsum("jgcp,gh->jgphc", m, eye).reshape(S5_NB, S5_SB, S5_CB)

    bb2 = jnp.concatenate([
        jnp.concatenate([in_blocks(bb_re), in_blocks(bb_im)], axis=2),
        jnp.concatenate([in_blocks(abb_re), in_blocks(abb_im)], axis=2)], axis=1).astype(bf16)
    cc = jnp.concatenate([out_blocks(c_re), -out_blocks(c_im)], axis=1).astype(bf16)
    rows = [v.reshape(S5_NB, 1, S5_SB) for v in (ab_re, ab_im, ab2_re, ab2_im)]
    abv = jnp.concatenate(rows + [jnp.zeros((S5_NB, SUB - len(rows), S5_SB), f32)], axis=1)
    return bb2, cc, abv


def _s5_seq_kernel(x_ref, halo0_ref, h0_ref, bb2_ref, cc_ref, abv_ref, y_ref, hout_ref,
                   bu_scr, hs_scr, h_scr, halo_scr):
    tb = pl.program_id(1)
    RB = x_ref.shape[0]

    @pl.when(tb == 0)
    def _():
        h_scr[...] = h0_ref[0]
        halo_scr[...] = halo0_ref[...].astype(f32)

    x = x_ref[...].astype(f32)
    xc = jnp.concatenate([halo_scr[...], x], axis=0)
    odd = (lax.broadcasted_iota(jnp.int32, (RB + SUB, 1), 0) & 1) == 1
    xprev = jnp.where(odd, pltpu.roll(xc, 1, axis=0), pltpu.roll(xc, SUB - 1, axis=0))[SUB:]
    halo_scr[...] = x[RB - SUB:]
    lhs = jnp.concatenate([x, xprev], axis=1).astype(bf16)
    bu_scr[...] = _dot(lhs, bb2_ref[0])
    a2r = abv_ref[0, 2:3, :]
    a2i = abv_ref[0, 3:4, :]

    def step(k, carry):
        hr, hi = carry
        r0 = pl.multiple_of(k * SUB, SUB)
        bu = bu_scr[pl.ds(r0, SUB), :]
        nr = a2r * hr - a2i * hi + bu[:, :S5_SB]
        ni = a2r * hi + a2i * hr + bu[:, S5_SB:]
        hs_scr[pl.ds(r0, SUB), :S5_SB] = nr
        hs_scr[pl.ds(r0, SUB), S5_SB:] = ni
        return nr, ni

    hr, hi = lax.fori_loop(0, RB // SUB, step, (h_scr[:, :S5_SB], h_scr[:, S5_SB:]), unroll=4)
    h_scr[:, :S5_SB] = hr
    h_scr[:, S5_SB:] = hi
    y_ref[...] = _dot(hs_scr[...].astype(bf16), cc_ref[0])

    @pl.when(tb == pl.num_programs(1) - 1)
    def _():
        hout_ref[0] = h_scr[...]


def _s5_seq_call(xp, halo0, h0, bb2, cc, abv, rb):
    n = xp.shape[0]
    wsel = lambda j, t: (j, 0, 0)
    return pl.pallas_call(
        _s5_seq_kernel,
        grid=(S5_NB, n // rb),
        in_specs=[pl.BlockSpec((rb, S5_CB), lambda j, t: (t, j)),
                  pl.BlockSpec((SUB, S5_CB), lambda j, t: (0, j)),
                  pl.BlockSpec((1, SUB, 2 * S5_SB), wsel),
                  pl.BlockSpec((1, 2 * S5_CB, 2 * S5_SB), wsel),
                  pl.BlockSpec((1, 2 * S5_SB, S5_CB), wsel),
                  pl.BlockSpec((1, SUB, S5_SB), wsel)],
        out_specs=[pl.BlockSpec((rb, S5_CB), lambda j, t: (t, j)),
                   pl.BlockSpec((1, SUB, 2 * S5_SB), wsel)],
        out_shape=[jax.ShapeDtypeStruct((n, D_MODEL), f32),
                   jax.ShapeDtypeStruct((S5_NB, SUB, 2 * S5_SB), f32)],
        scratch_shapes=[pltpu.VMEM((rb, 2 * S5_SB), f32), pltpu.VMEM((rb, 2 * S5_SB), f32),
                        pltpu.VMEM((SUB, 2 * S5_SB), f32), pltpu.VMEM((SUB, S5_CB), f32)],
        compiler_params=_cp(("arbitrary", "arbitrary")),
        name=f"s5_seq_{n}",
    )(xp, halo0, h0, bb2, cc, abv)


def _s5_sample_kernel(x_ref, hre_ref, him_ref, bb2_ref, cc_ref, abv_ref, y_ref, ore_ref, oim_ref, hs_scr):
    nb = hre_ref.shape[0]
    bu = _dot(x_ref[...], bb2_ref[0, :S5_CB, :])
    ar = abv_ref[0, 0:1, :]
    ai = abv_ref[0, 1:2, :]
    hr, hi = hre_ref[...], him_ref[...]
    for t in range(DEC_SEQ):
        rows = slice(t * nb, (t + 1) * nb)
        hr, hi = (ar * hr - ai * hi + bu[rows, :S5_SB], ar * hi + ai * hr + bu[rows, S5_SB:])
        hs_scr[rows, :S5_SB] = hr
        hs_scr[rows, S5_SB:] = hi
    y_ref[...] = _dot(hs_scr[...].astype(bf16), cc_ref[0])
    ore_ref[...] = hr
    oim_ref[...] = hi


def _s5_sample_call(xt, h_re, h_im, bb2, cc, abv):
    n = xt.shape[0]
    nb = h_re.shape[0]
    wsel = lambda j: (j, 0, 0)
    st = pl.BlockSpec((nb, S5_SB), lambda j: (0, j))
    return pl.pallas_call(
        _s5_sample_kernel,
        grid=(S5_NB,),
        in_specs=[pl.BlockSpec((n, S5_CB), lambda j: (0, j)), st, st,
                  pl.BlockSpec((1, 2 * S5_CB, 2 * S5_SB), wsel),
                  pl.BlockSpec((1, 2 * S5_SB, S5_CB), wsel),
                  pl.BlockSpec((1, SUB, S5_SB), wsel)],
        out_specs=[pl.BlockSpec((n, S5_CB), lambda j: (0, j)), st, st],
        out_shape=[jax.ShapeDtypeStruct((n, D_MODEL), f32),
                   jax.ShapeDtypeStruct(h_re.shape, f32), jax.ShapeDtypeStruct(h_im.shape, f32)],
        scratch_shapes=[pltpu.VMEM((n, 2 * S5_SB), f32)],
        compiler_params=_cp(("arbitrary",)),
        name="s5_sample",
    )(xt, h_re, h_im, bb2, cc, abv)


GLA_CHUNK = 256
S5_RB = 512


def _pair_major(x, nb):
    n, dm = x.shape
    return x.reshape(nb, n // nb // 2, 2, dm).transpose(1, 0, 2, 3).reshape(n, dm)


def _pair_major_inv(x, nb):
    n, dm = x.shape
    return x.reshape(n // nb // 2, nb, 2, dm).transpose(1, 0, 2, 3).reshape(n, dm)


def _moe_layer(layer, x1, xnf, route, cnt, wg, wu, wd, gn, xn_dtype):
    pos, tile_expert, n_used = _moe_plan(route, cnt)
    xs = _dispatch_call(xnf, pos)
    os_rows = _experts_call(xs, tile_expert + layer * MOE_EXPERTS, n_used, wg, wu, wd)
    return _combine_call(x1, route, pos, os_rows, gn, xn_dtype)


def kernel(x_prompt, x_sample, state_gla, state_s5_re, state_s5_im, meta_tokens, norm_mix_g, norm_ffn_g, norm_final_g, gla_w_in, gla_w_a2, gla_b_a, gla_g_o, gla_w_o, s5_lambda_re, s5_lambda_im, s5_log_dt, s5_b_re, s5_b_im, s5_c_re, s5_c_im, s5_d, s5_w_glu, s5_b_glu, moe_w_rg, moe_b_rg, moe_w_re, moe_b_re, moe_w_gate, moe_w_up, moe_w_down):
    row = lambda v: v.reshape(1, -1)
    x0 = jnp.concatenate([
        x_prompt.reshape(N_MAIN, D_MODEL),
        jnp.tile(meta_tokens.astype(x_prompt.dtype), (BATCH, 1)),
        x_sample.reshape(N_SAMPLE, D_MODEL),
        jnp.zeros((N_ROWS - N_REAL, D_MODEL), x_prompt.dtype)], axis=0)
    wg = moe_w_gate.reshape(-1, D_MODEL, D_EXPERT)
    wu = moe_w_up.reshape(-1, D_MODEL, D_EXPERT)
    wd = moe_w_down.reshape(-1, D_EXPERT, D_MODEL)

    w_in = gla_w_in.reshape(D_MODEL, -1)
    wa1 = jnp.pad(w_in[:, GLA_QKVR:], ((0, 0), (0, LANES - GLA_RANK)))
    wa2 = jnp.pad(gla_w_a2.reshape(GLA_RANK, GLA_KEY), ((0, LANES - GLA_RANK), (0, 0)))
    xn, glog = _norm_gate_call(x0, row(norm_mix_g[0]), wa1, wa2, row(gla_b_a))
    proj = _proj_call(xn, w_in, GLA_QKVR)
    go = row(gla_g_o)
    og = jnp.zeros((N_ROWS, GLA_VAL), bf16)
    s_zero = jnp.zeros((BATCH, GLA_HEADS, GLA_DK, GLA_DV), f32)
    og, s_meta = _gla_seq_call(proj, glog, go, s_zero, og, row0=ROW_META, C=N_META, n_chunks=1, d=SUB)
    og, s_prompt = _gla_seq_call(proj, glog, go, s_meta, og, row0=0, C=GLA_CHUNK, n_chunks=SEQ // GLA_CHUNK, d=SUB)
    og, s_sample = _gla_sample_call(proj, glog, go, state_gla.reshape(DEC_BATCH, GLA_HEADS, GLA_DK, GLA_DV), og,
                                    row0=ROW_SAMPLE)
    wr, br = _router_weights(moe_w_rg[0], moe_b_rg[0], moe_w_re[0], moe_b_re[0])
    x1, xnf, route, cnt = _mix_out_call(
        _gla_out_kernel, "gla_out", [og, x0],
        [gla_w_o.reshape(GLA_VAL, D_MODEL).astype(bf16), row(norm_ffn_g[0]), wr, br])
    x2, xn2 = _moe_layer(0, x1, xnf, route, cnt, wg, wu, wd, row(norm_mix_g[1]), bf16)

    bb2, cc, abv = _s5_weights(s5_lambda_re[0], s5_lambda_im[0], s5_log_dt[0], s5_b_re[0], s5_b_im[0],
                               s5_c_re[0], s5_c_im[0])
    xp_meta = _pair_major(xn2[ROW_META:ROW_SAMPLE], BATCH)
    xp_main = _pair_major(xn2[:N_MAIN], BATCH)
    xt_sample = xn2[ROW_SAMPLE:N_REAL].reshape(DEC_BATCH, DEC_SEQ, D_MODEL).transpose(1, 0, 2).reshape(N_SAMPLE, D_MODEL)
    y_meta, h_meta = _s5_seq_call(xp_meta, jnp.zeros((SUB, D_MODEL), bf16), jnp.zeros((S5_NB, SUB, 2 * S5_SB), f32),
                                  bb2, cc, abv, N_METAROWS)
    y_main, h_main = _s5_seq_call(xp_main, xp_meta[-SUB:], h_meta, bb2, cc, abv, S5_RB)
    y_samp, s5r_s, s5i_s = _s5_sample_call(
        xt_sample, state_s5_re.reshape(DEC_BATCH, S5_GROUPS * S5_STATE),
        state_s5_im.reshape(DEC_BATCH, S5_GROUPS * S5_STATE), bb2, cc, abv)
    ys = jnp.concatenate([
        _pair_major_inv(y_main, BATCH), _pair_major_inv(y_meta, BATCH),
        y_samp.reshape(DEC_SEQ, DEC_BATCH, D_MODEL).transpose(1, 0, 2).reshape(N_SAMPLE, D_MODEL),
        jnp.zeros((N_ROWS - N_REAL, D_MODEL), f32)], axis=0)
    wr, br = _router_weights(moe_w_rg[1], moe_b_rg[1], moe_w_re[1], moe_b_re[1])
    x3, xnf, route, cnt = _mix_out_call(
        _s5_out_kernel, "s5_out", [ys, x2],
        [row(norm_mix_g[1]), row(s5_d), s5_w_glu.reshape(D_MODEL, D_MODEL).astype(bf16), row(s5_b_glu),
         row(norm_ffn_g[1]), wr, br])
    _, y = _moe_layer(1, x3, xnf, route, cnt, wg, wu, wd, row(norm_final_g), f32)

    y_prompt = y[:N_MAIN].reshape(BATCH, SEQ, D_MODEL)
    y_sample = y[ROW_SAMPLE:N_REAL].reshape(DEC_BATCH, DEC_SEQ, D_MODEL)
    hfin = h_main.reshape(S5_NB, BATCH, 2, 2, S5_GPB, S5_STATE)[:, :, 1]
    s5r_p = hfin[:, :, 0].transpose(1, 0, 2, 3).reshape(1, BATCH, S5_GROUPS, S5_STATE)
    s5i_p = hfin[:, :, 1].transpose(1, 0, 2, 3).reshape(1, BATCH, S5_GROUPS, S5_STATE)
    return (y_prompt, y_sample, s_prompt[None], s5r_p, s5i_p, s_sample[None],
            s5r_s.reshape(1, DEC_BATCH, S5_GROUPS, S5_STATE), s5i_s.reshape(1, DEC_BATCH, S5_GROUPS, S5_STATE))
```

```python
import functools

import jax
import jax.numpy as jnp
from jax import lax
from jax.experimental import pallas as pl
from jax.experimental.pallas import tpu as pltpu

f32 = jnp.float32
bf16 = jnp.bfloat16

D_MODEL = 2048
BATCH = 4
SEQ = 2048
DEC_BATCH = 128
DEC_SEQ = 4
N_META = 16
EPS = 1e-6
GLA_HEADS = 4
GLA_DK = 256
GLA_DV = 512
GLA_KEY = GLA_HEADS * GLA_DK
GLA_VAL = GLA_HEADS * GLA_DV
GLA_RANK = 16
GLA_TAU = 16.0
GLA_QKVR = 2 * GLA_KEY + 2 * GLA_VAL

N_MAIN = BATCH * SEQ
N_METAROWS = BATCH * N_META
N_SAMPLE = DEC_BATCH * DEC_SEQ
ROW_META = N_MAIN
ROW_SAMPLE = N_MAIN + N_METAROWS
N_REAL = ROW_SAMPLE + N_SAMPLE
ROW_TILE = 512
N_ROWS = -(-N_REAL // ROW_TILE) * ROW_TILE

VMEM_LIMIT = 56 * 1024 * 1024


def _cp(sem, vmem=VMEM_LIMIT):
    return pltpu.CompilerParams(dimension_semantics=sem, vmem_limit_bytes=vmem)


def _dot(a, b):
    return jnp.dot(a, b, preferred_element_type=f32)


def _dot_nt(a, b):
    return lax.dot_general(a, b, (((1,), (1,)), ((), ())), preferred_element_type=f32)


def _dot_tn(a, b):
    return lax.dot_general(a, b, (((0,), (0,)), ((), ())), preferred_element_type=f32)


def _sigmoid(x):
    return 1.0 / (1.0 + jnp.exp(-x))


def _split3(x):
    hi = x.astype(bf16)
    r1 = x - hi.astype(f32)
    mid = r1.astype(bf16)
    lo = (r1 - mid.astype(f32)).astype(bf16)
    return hi, mid, lo


def _cumsum_rows(g, C):
    if C <= 16:
        row = lax.broadcasted_iota(jnp.int32, (C, 1), 0)
        b = jnp.zeros_like(g)
        for s in range(C):
            b = b + jnp.where(row >= s, g[s:s + 1, :], 0.0)
        return b
    row = lax.broadcasted_iota(jnp.int32, (C, C), 0)
    col = lax.broadcasted_iota(jnp.int32, (C, C), 1)
    tri = jnp.where(row >= col, 1.0, 0.0).astype(bf16)
    hi, mid, lo = _split3(g)
    return _dot(tri, hi) + _dot(tri, mid) + _dot(tri, lo)


def _gla_scores(q, k, b, C, d, tree=True):
    ti = lax.broadcasted_iota(jnp.int32, (C, C), 0)
    si = lax.broadcasted_iota(jnp.int32, (C, C), 1)
    row = lax.broadcasted_iota(jnp.int32, (C, 1), 0)
    scores = jnp.zeros((C, C), f32)
    for dl in range(d):
        ks = k if dl == 0 else pltpu.roll(k, dl, axis=0)
        bs = b if dl == 0 else pltpu.roll(b, dl, axis=0)
        term = q * ks * jnp.exp(jnp.minimum(b - bs, 0.0))
        colv = jnp.sum(term, axis=1, keepdims=True)
        m = (si == ti - dl) & ((ti % d) >= dl)
        scores = jnp.where(m, colv, scores)
    z = b
    s = 1
    while tree and 2 * s < C:
        z = jnp.where((row & s) != 0, pltpu.roll(z, s, axis=0), z)
        s *= 2
        h = s
        if h < d:
            continue
        bnext = pltpu.roll(z, C - h, axis=0)
        qh = (q * jnp.exp(jnp.minimum(b - z, 0.0))).astype(bf16)
        kh = (k * jnp.exp(jnp.minimum(bnext - b, 0.0))).astype(bf16)
        sh = _dot_nt(qh, kh)
        tb = ti // h
        m = ((tb % 2) == 1) & ((si // h) == tb - 1)
        scores = jnp.where(m, sh, scores)
    return scores


def _gla_chunk(q, k, v, g, S, C, d, tree=True):
    b = _cumsum_rows(g, C)
    o = _dot((q * jnp.exp(b)).astype(bf16), S.astype(bf16))
    scores = _gla_scores(q, k, b, C, d, tree)
    o = o + _dot(scores.astype(bf16), v)
    b_last = b[C - 1:C, :]
    kd = (k * jnp.exp(b_last - b)).astype(bf16)
    r2 = lax.broadcasted_iota(jnp.int32, (GLA_DK, GLA_DK), 0)
    c2 = lax.broadcasted_iota(jnp.int32, (GLA_DK, GLA_DK), 1)
    dec_col = jnp.sum(jnp.where(r2 == c2, jnp.exp(b_last), 0.0), axis=1, keepdims=True)
    S_new = dec_col * S + _dot_tn(kd, v)
    return o, S_new


def _gla_head_epilogue(o, r, go):
    ms = jnp.mean(o * o, axis=1, keepdims=True)
    on = o * lax.rsqrt(ms + EPS) * go
    return on * (r * _sigmoid(r))


def _gla_seq_kernel(q_ref, k_ref, v_ref, r_ref, g_ref, go_ref, s0_ref, _og_in, og_ref, sout_ref, s_scr, *, C, d):
    c = pl.program_id(1)

    @pl.when(c == 0)
    def _():
        s_scr[...] = s0_ref[0]

    def head(h, carry):
        ck = pl.ds(pl.multiple_of(h * GLA_DK, GLA_DK), GLA_DK)
        cv = pl.ds(pl.multiple_of(h * GLA_DV, GLA_DV), GLA_DV)
        q = q_ref[:, ck].astype(f32) * (GLA_DK ** -0.5)
        k = k_ref[:, ck].astype(f32)
        o, S_new = _gla_chunk(q, k, v_ref[:, cv], g_ref[:, ck], s_scr[h], C, d)
        s_scr[h] = S_new
        og_ref[:, cv] = _gla_head_epilogue(o, r_ref[:, cv].astype(f32), go_ref[:, cv]).astype(og_ref.dtype)
        return carry

    lax.fori_loop(0, GLA_HEADS, head, 0)

    @pl.when(c == pl.num_programs(1) - 1)
    def _():
        sout_ref[0] = s_scr[...]


def _gla_seq_call(proj, glog, go, s0, og_buf, *, row0, C, n_chunks, d):
    blk0 = row0 // C
    rows = lambda b, c: blk0 + b * n_chunks + c
    kern = functools.partial(_gla_seq_kernel, C=C, d=d)
    return pl.pallas_call(
        kern,
        grid=(BATCH, n_chunks),
        in_specs=[
            pl.BlockSpec((C, GLA_KEY), lambda b, c: (rows(b, c), 0)),
            pl.BlockSpec((C, GLA_KEY), lambda b, c: (rows(b, c), 1)),
            pl.BlockSpec((C, GLA_VAL), lambda b, c: (rows(b, c), 1)),
            pl.BlockSpec((C, GLA_VAL), lambda b, c: (rows(b, c), 2)),
            pl.BlockSpec((C, GLA_KEY), lambda b, c: (rows(b, c), 0)),
            pl.BlockSpec((1, GLA_VAL), lambda b, c: (0, 0)),
            pl.BlockSpec((1, GLA_HEADS, GLA_DK, GLA_DV), lambda b, c: (b, 0, 0, 0)),
            pl.BlockSpec(memory_space=pl.ANY),
        ],
        out_specs=[
            pl.BlockSpec((C, GLA_VAL), lambda b, c: (rows(b, c), 0)),
            pl.BlockSpec((1, GLA_HEADS, GLA_DK, GLA_DV), lambda b, c: (b, 0, 0, 0)),
        ],
        out_shape=[
            jax.ShapeDtypeStruct(og_buf.shape, og_buf.dtype),
            jax.ShapeDtypeStruct((BATCH, GLA_HEADS, GLA_DK, GLA_DV), f32),
        ],
        scratch_shapes=[pltpu.VMEM((GLA_HEADS, GLA_DK, GLA_DV), f32)],
        input_output_aliases={7: 0},
        compiler_params=_cp(("arbitrary", "arbitrary")),
        name=f"gla_seq_c{C}",
    )(proj, proj, proj, proj, glog, go, s0, og_buf)


SAMPLE_BB = 4
SAMPLE_C = SAMPLE_BB * DEC_SEQ


def _gla_sample_kernel(q_ref, k_ref, v_ref, r_ref, g_ref, go_ref, s0_ref, _og_in, og_ref, sout_ref):
    row = lax.broadcasted_iota(jnp.int32, (SAMPLE_C, 1), 0)

    def take(x, bb):
        sh = (SAMPLE_C - DEC_SEQ * bb) % SAMPLE_C
        return jnp.where(row < DEC_SEQ, pltpu.roll(x, sh, axis=0) if sh else x, 0.0)

    def head(h, carry):
        ck = pl.ds(pl.multiple_of(h * GLA_DK, GLA_DK), GLA_DK)
        cv = pl.ds(pl.multiple_of(h * GLA_DV, GLA_DV), GLA_DV)
        q_all = q_ref[:, ck].astype(f32) * (GLA_DK ** -0.5)
        k_all = k_ref[:, ck].astype(f32)
        v_all = v_ref[:, cv].astype(f32)
        r_all = r_ref[:, cv].astype(f32)
        g_all = g_ref[:, ck]
        go = go_ref[:, cv]
        acc = jnp.zeros((SAMPLE_C, GLA_DV), f32)
        for bb in range(SAMPLE_BB):
            o, S_new = _gla_chunk(take(q_all, bb), take(k_all, bb), take(v_all, bb).astype(bf16),
                                  take(g_all, bb), s0_ref[bb, h], SAMPLE_C, DEC_SEQ, tree=False)
            sout_ref[bb, h] = S_new
            y = _gla_head_epilogue(o, take(r_all, bb), go)
            acc = jnp.where(row // DEC_SEQ == bb, pltpu.roll(y, DEC_SEQ * bb, axis=0) if bb else y, acc)
        og_ref[:, cv] = acc.astype(og_ref.dtype)
        return carry

    lax.fori_loop(0, GLA_HEADS, head, 0)


def _gla_sample_call(proj, glog, go, s0, og_buf, *, row0):
    n_seq = s0.shape[0]
    blk0 = row0 // SAMPLE_C
    st_spec = pl.BlockSpec((SAMPLE_BB, GLA_HEADS, GLA_DK, GLA_DV), lambda i: (i, 0, 0, 0))
    return pl.pallas_call(
        _gla_sample_kernel,
        grid=(n_seq // SAMPLE_BB,),
        in_specs=[
            pl.BlockSpec((SAMPLE_C, GLA_KEY), lambda i: (blk0 + i, 0)),
            pl.BlockSpec((SAMPLE_C, GLA_KEY), lambda i: (blk0 + i, 1)),
            pl.BlockSpec((SAMPLE_C, GLA_VAL), lambda i: (blk0 + i, 1)),
            pl.BlockSpec((SAMPLE_C, GLA_VAL), lambda i: (blk0 + i, 2)),
            pl.BlockSpec((SAMPLE_C, GLA_KEY), lambda i: (blk0 + i, 0)),
            pl.BlockSpec((1, GLA_VAL), lambda i: (0, 0)),
            st_spec,
            pl.BlockSpec(memory_space=pl.ANY),
        ],
        out_specs=[pl.BlockSpec((SAMPLE_C, GLA_VAL), lambda i: (blk0 + i, 0)), st_spec],
        out_shape=[jax.ShapeDtypeStruct(og_buf.shape, og_buf.dtype), jax.ShapeDtypeStruct(s0.shape, f32)],
        input_output_aliases={7: 0},
        compiler_params=_cp(("arbitrary",)),
        name="gla_sample",
    )(proj, proj, proj, proj, glog, go, s0, og_buf)


LANES = 128
SUB = 8
MOE_GROUPS = 4
MOE_EPG = 8
MOE_EXPERTS = MOE_GROUPS * MOE_EPG
D_EXPERT = 256
ROUTE_E0 = MOE_GROUPS
MIX_TILE = 256
PROJ_NT = 1024


def _rms(x, g):
    r = lax.rsqrt(jnp.mean(x * x, axis=-1, keepdims=True) + EPS)
    return (x * r) * g


def _log_sigmoid(z):
    return jnp.minimum(z, 0.0) - jnp.log1p(jnp.exp(-jnp.abs(z)))


def _norm_gate_kernel(x_ref, gn_ref, wa1_ref, wa2_ref, ba_ref, xn_ref, gl_ref):
    xnb = _rms(x_ref[...], gn_ref[...]).astype(bf16)
    xn_ref[...] = xnb
    a = _dot(xnb, wa1_ref[...].astype(bf16))
    z = _dot(a.astype(bf16), wa2_ref[...].astype(bf16)) + ba_ref[...]
    gl_ref[...] = _log_sigmoid(z) * (1.0 / GLA_TAU)


def _norm_gate_call(x, gn, wa1, wa2, ba):
    n = x.shape[0]
    row = lambda i: (i, 0)
    fix = lambda i: (0, 0)
    return pl.pallas_call(
        _norm_gate_kernel,
        grid=(n // ROW_TILE,),
        in_specs=[pl.BlockSpec((ROW_TILE, D_MODEL), row), pl.BlockSpec((1, D_MODEL), fix),
                  pl.BlockSpec((D_MODEL, LANES), fix), pl.BlockSpec((LANES, GLA_KEY), fix),
                  pl.BlockSpec((1, GLA_KEY), fix)],
        out_specs=[pl.BlockSpec((ROW_TILE, D_MODEL), row), pl.BlockSpec((ROW_TILE, GLA_KEY), row)],
        out_shape=[jax.ShapeDtypeStruct((n, D_MODEL), bf16), jax.ShapeDtypeStruct((n, GLA_KEY), f32)],
        compiler_params=_cp(("arbitrary",)),
        name="norm_gate",
    )(x, gn, wa1, wa2, ba)


def _proj_kernel(xn_ref, w_ref, o_ref, wb_scr):
    @pl.when(pl.program_id(1) == 0)
    def _():
        wb_scr[...] = w_ref[...].astype(bf16)

    o_ref[...] = _dot(xn_ref[...], wb_scr[...]).astype(o_ref.dtype)


def _proj_call(xn, w, n_cols):
    n = xn.shape[0]
    return pl.pallas_call(
        _proj_kernel,
        grid=(n_cols // PROJ_NT, n // ROW_TILE),
        in_specs=[pl.BlockSpec((ROW_TILE, D_MODEL), lambda j, i: (i, 0)),
                  pl.BlockSpec((D_MODEL, PROJ_NT), lambda j, i: (0, j))],
        out_specs=pl.BlockSpec((ROW_TILE, PROJ_NT), lambda j, i: (i, j)),
        out_shape=jax.ShapeDtypeStruct((n, n_cols), bf16),
        scratch_shapes=[pltpu.VMEM((D_MODEL, PROJ_NT), bf16)],
        compiler_params=_cp(("arbitrary", "arbitrary")),
        name="gla_proj",
    )(xn, w)


def _route(xn, wr, br, cnt_ref):
    R = xn.shape[0]
    xh = xn.astype(bf16)
    xl = (xn - xh.astype(f32)).astype(bf16)
    wh = wr.astype(bf16)
    wl = (wr - wh.astype(f32)).astype(bf16)
    logits = _dot(xh, wh) + _dot(xl, wh) + _dot(xh, wl) + br
    lane_i = lax.broadcasted_iota(jnp.int32, (R, LANES), 1)
    lane = lane_i.astype(f32)
    neg = -jnp.inf
    big = float(LANES)
    is_g = lane_i < MOE_GROUPS
    lg = jnp.where(is_g, logits, neg)
    mg = jnp.max(lg, axis=1, keepdims=True)
    gidx = jnp.min(jnp.where(lg == mg, lane, big), axis=1, keepdims=True)
    ptop = 1.0 / jnp.sum(jnp.where(is_g, jnp.exp(logits - mg), 0.0), axis=1, keepdims=True)
    lo = ROUTE_E0 + MOE_EPG * gidx
    le = jnp.where((lane >= lo) & (lane < lo + MOE_EPG), logits, neg)
    v1 = jnp.max(le, axis=1, keepdims=True)
    i1 = jnp.min(jnp.where(le == v1, lane, big), axis=1, keepdims=True)
    le2 = jnp.where(lane == i1, neg, le)
    v2 = jnp.max(le2, axis=1, keepdims=True)
    i2 = jnp.min(jnp.where(le2 == v2, lane, big), axis=1, keepdims=True)
    s = jnp.exp(v2 - v1)
    w0 = ptop / (1.0 + s)
    w1 = ptop * s / (1.0 + s)
    oh = jnp.where((lane == i1) | (lane == i2), 1.0, 0.0)
    ri = lax.broadcasted_iota(jnp.int32, (R, R), 0)
    ci = lax.broadcasted_iota(jnp.int32, (R, R), 1)
    before = jnp.where(ri > ci, 1.0, 0.0).astype(bf16)
    tot = _dot(before, oh.astype(bf16)) + cnt_ref[...]
    rank0 = jnp.sum(jnp.where(lane == i1, tot, 0.0), axis=1, keepdims=True)
    rank1 = jnp.sum(jnp.where(lane == i2, tot, 0.0), axis=1, keepdims=True)
    cnt_ref[...] = cnt_ref[...] + jnp.sum(oh, axis=0, keepdims=True)
    vals = (i1 - ROUTE_E0, i2 - ROUTE_E0, rank0, rank1, w0, w1)
    slab = jnp.zeros((R, LANES), f32)
    for j, v in enumerate(vals):
        slab = jnp.where(lane_i == j, v, slab)
    sub = lax.broadcasted_iota(jnp.int32, (SUB, R), 0)
    plan = jnp.zeros((SUB, R), f32)
    for j, v in enumerate(vals[:4]):
        as_row = jnp.sum(jnp.where(ri == ci, v, 0.0), axis=0, keepdims=True)
        plan = jnp.where(sub == j, as_row, plan)
    return slab, plan


def _router_weights(w_rg, b_rg, w_re, b_re):
    w = jnp.concatenate([w_rg, jnp.moveaxis(w_re, 0, 1).reshape(D_MODEL, MOE_EXPERTS)], axis=1)
    b = jnp.concatenate([b_rg, b_re.reshape(MOE_EXPERTS)])
    pad = LANES - w.shape[1]
    return jnp.pad(w, ((0, 0), (0, pad))), jnp.pad(b, (0, pad))[None]


def _mix_out_tail(x1, gffn_ref, wr_ref, br_ref, x1_ref, xn_ref, route_ref, plan_ref, cnt_ref):
    @pl.when(pl.program_id(0) == 0)
    def _():
        cnt_ref[...] = jnp.zeros_like(cnt_ref)

    x1_ref[...] = x1
    xn = _rms(x1, gffn_ref[...])
    xn_ref[...] = xn
    route_ref[...], plan_ref[...] = _route(xn, wr_ref[...], br_ref[...], cnt_ref)


def _gla_out_kernel(og_ref, x_ref, wo_ref, gffn_ref, wr_ref, br_ref, *outs):
    x1 = x_ref[...] + _dot(og_ref[...], wo_ref[...])
    _mix_out_tail(x1, gffn_ref, wr_ref, br_ref, *outs)


def _gelu_tanh(x):
    return x * (0.5 * (1.0 + jnp.tanh(0.7978845608028654 * (x + 0.044715 * (x * x * x)))))


def _s5_out_kernel(ys_main_ref, ys_tail_ref, x_ref, gmix_ref, d_ref, wglu_ref, bglu_ref, gffn_ref, wr_ref, br_ref,
                   *outs):
    x = x_ref[...]
    u = _rms(x, gmix_ref[...])
    ys = jnp.where(pl.program_id(0) < N_MAIN // MIX_TILE, ys_main_ref[...], ys_tail_ref[...])
    y = _gelu_tanh(ys + d_ref[...] * u)
    z = _dot(y.astype(bf16), wglu_ref[...]) + bglu_ref[...]
    _mix_out_tail(x + y * _sigmoid(z), gffn_ref, wr_ref, br_ref, *outs)


def _mix_out_call(kern, name, n, row_ins, row_maps, fix_ins):
    row = lambda i: (i, 0)
    fix = lambda i: (0, 0)
    out_row = lambda w: pl.BlockSpec((MIX_TILE, w), row)
    nt = n // MIX_TILE
    return pl.pallas_call(
        kern,
        grid=(nt,),
        in_specs=[pl.BlockSpec((MIX_TILE, a.shape[1]), m) for a, m in zip(row_ins, row_maps)]
        + [pl.BlockSpec(a.shape, fix) for a in fix_ins],
        out_specs=[out_row(D_MODEL), out_row(D_MODEL), out_row(LANES), pl.BlockSpec((SUB, MIX_TILE), row),
                   pl.BlockSpec((1, LANES), fix)],
        out_shape=[jax.ShapeDtypeStruct((n, D_MODEL), f32), jax.ShapeDtypeStruct((n, D_MODEL), f32),
                   jax.ShapeDtypeStruct((n, LANES), f32), jax.ShapeDtypeStruct((nt * SUB, MIX_TILE), f32),
                   jax.ShapeDtypeStruct((1, LANES), f32)],
        compiler_params=_cp(("arbitrary",)),
        name=name,
    )(*row_ins, *fix_ins)


EXPERT_TM = 256
MOVE_TILE = MIX_TILE


def _moe_plan(plan, cnt, n):
    i32 = jnp.int32
    v = plan.reshape(n // MOVE_TILE, SUB, MOVE_TILE)[:, :4].astype(i32)
    counts = cnt[0, ROUTE_E0:ROUTE_E0 + MOE_EXPERTS].astype(i32)
    ends = jnp.cumsum(counts)
    off = ends - counts
    ids = jnp.arange(MOE_EXPERTS, dtype=i32)
    pos = jnp.sum(jnp.where(v[:, 0:2, :, None] == ids, off, 0), axis=-1) + v[:, 2:4]
    total = 2 * n
    n_tiles = total // EXPERT_TM
    n_items = n_tiles + MOE_EXPERTS
    inner = (counts > 0) & (off % EXPERT_TM != 0)
    keys = jnp.concatenate([jnp.arange(n_tiles, dtype=i32) * EXPERT_TM, jnp.where(inner, off, total)])
    idx = jnp.arange(n_items, dtype=i32)
    before = (keys[None, :] < keys[:, None]) | ((keys[None, :] == keys[:, None]) & (idx[None, :] < idx[:, None]))
    order = jnp.sum(before.astype(i32), axis=1)
    starts = jnp.sum(jnp.where(order[:, None] == idx[None, :], keys[:, None], 0), axis=0)
    stops = jnp.concatenate([starts[1:], jnp.full((1,), total, i32)])
    tile = starts // EXPERT_TM
    expert = jnp.sum((ends[None, :] <= starts[:, None]).astype(i32), axis=1)
    used = n_tiles + jnp.sum(inner.astype(i32))
    keep = jnp.minimum(idx, used - 1)
    pick = lambda a: a[keep]
    return (pos.reshape(n // MOVE_TILE, 1, 2 * MOVE_TILE), pick(tile), pick(jnp.minimum(expert, MOE_EXPERTS - 1)),
            pick(starts - tile * EXPERT_TM), pick(stops - tile * EXPERT_TM), used.reshape(1))


def _row_copy(src, src_row, dst, dst_row, sem):
    return pltpu.make_async_copy(src.at[pl.ds(src_row, 1)], dst.at[pl.ds(dst_row, 1)], sem)


def _dispatch_kernel(pos_ref, x_ref, xs_ref, sem):
    def copy(r, s):
        return _row_copy(x_ref, r, xs_ref, pos_ref[0, 0, s * MOVE_TILE + r], sem)

    def start(r, c):
        for s in range(2):
            copy(r, s).start()
        return c

    def wait(r, c):
        for s in range(2):
            copy(r, s).wait()
        return c

    lax.fori_loop(0, MOVE_TILE, start, 0, unroll=8)
    lax.fori_loop(0, MOVE_TILE, wait, 0, unroll=8)


def _dispatch_call(xn, pos):
    n = xn.shape[0]
    return pl.pallas_call(
        _dispatch_kernel,
        grid=(n // MOVE_TILE,),
        in_specs=[pl.BlockSpec((1, 1, 2 * MOVE_TILE), lambda i: (i, 0, 0), memory_space=pltpu.SMEM),
                  pl.BlockSpec((MOVE_TILE, D_MODEL), lambda i: (i, 0))],
        out_specs=pl.BlockSpec(memory_space=pl.ANY),
        out_shape=jax.ShapeDtypeStruct((2 * n, D_MODEL), xn.dtype),
        scratch_shapes=[pltpu.SemaphoreType.DMA(())],
        compiler_params=_cp(("arbitrary",)),
        name="moe_dispatch",
    )(pos, xn)


def _experts_kernel(it_ref, ie_ref, lo_ref, hi_ref, n_ref, xs_ref, wg_ref, wu_ref, wd_ref, os_ref,
                    wg_scr, wu_scr, wd_scr):
    i = pl.program_id(0)

    @pl.when(i < n_ref[0])
    def _():
        @pl.when((i == 0) | (ie_ref[i] != ie_ref[jnp.maximum(i - 1, 0)]))
        def _():
            wg_scr[...] = wg_ref[0].astype(bf16)
            wu_scr[...] = wu_ref[0].astype(bf16)
            wd_scr[...] = wd_ref[0].astype(bf16)

        x = xs_ref[...].astype(bf16)
        hg = _dot(x, wg_scr[...])
        hu = _dot(x, wu_scr[...])
        out = _dot((hg * _sigmoid(hg) * hu).astype(bf16), wd_scr[...])
        lo = lo_ref[i]
        row = lax.broadcasted_iota(jnp.int32, (EXPERT_TM, 1), 0)
        mine = (row >= lo) & (row < hi_ref[i])

        @pl.when(lo == 0)
        def _():
            os_ref[...] = jnp.where(mine, out, 0.0)

        @pl.when(lo != 0)
        def _():
            os_ref[...] = jnp.where(mine, out, os_ref[...])


def _experts_call(xs, item_tile, item_expert, item_lo, item_hi, n_items, wg, wu, wd):
    rows = lambda i, it, ie, lo, hi, n: (it[i], 0)
    wsel = lambda i, it, ie, lo, hi, n: (ie[i], 0, 0)
    return pl.pallas_call(
        _experts_kernel,
        grid_spec=pltpu.PrefetchScalarGridSpec(
            num_scalar_prefetch=5,
            grid=(item_tile.shape[0],),
            in_specs=[pl.BlockSpec((EXPERT_TM, D_MODEL), rows),
                      pl.BlockSpec((1, D_MODEL, D_EXPERT), wsel),
                      pl.BlockSpec((1, D_MODEL, D_EXPERT), wsel),
                      pl.BlockSpec((1, D_EXPERT, D_MODEL), wsel)],
            out_specs=pl.BlockSpec((EXPERT_TM, D_MODEL), rows),
            scratch_shapes=[pltpu.VMEM((D_MODEL, D_EXPERT), bf16), pltpu.VMEM((D_MODEL, D_EXPERT), bf16),
                            pltpu.VMEM((D_EXPERT, D_MODEL), bf16)],
        ),
        out_shape=jax.ShapeDtypeStruct(xs.shape, f32),
        compiler_params=_cp(("arbitrary",)),
        name="moe_experts",
    )(item_tile, item_expert, item_lo, item_hi, n_items, xs, wg, wu, wd)


def _combine_kernel(pos_ref, x_ref, route_ref, gn_ref, os_ref, *rest, emit_x, split):
    outs, (buf, sem) = rest[:-2], rest[-2:]

    def copy(r, s):
        return _row_copy(os_ref, pos_ref[0, 0, s * MOVE_TILE + r], buf.at[s], r, sem)

    def start(r, c):
        for s in range(2):
            copy(r, s).start()
        return c

    def wait(r, c):
        for s in range(2):
            copy(r, s).wait()
        return c

    lax.fori_loop(0, MOVE_TILE, start, 0, unroll=8)
    lax.fori_loop(0, MOVE_TILE, wait, 0, unroll=8)
    route = route_ref[...]
    x2 = x_ref[...] + route[:, 4:5] * buf[0] + route[:, 5:6] * buf[1]
    if emit_x:
        outs[0][...] = x2
        outs = outs[1:]
    xn = _rms(x2, gn_ref[...])
    if split:
        main_ref, tail_ref = outs
        is_main = pl.program_id(0) < N_MAIN // MOVE_TILE

        @pl.when(is_main)
        def _():
            main_ref[...] = xn.astype(main_ref.dtype)

        @pl.when(jnp.logical_not(is_main))
        def _():
            tail_ref[...] = xn.astype(tail_ref.dtype)
    else:
        outs[0][...] = xn.astype(outs[0].dtype)


def _combine_call(x1, route, pos, os_rows, gn, xn_dtype, *, emit_x, split):
    n = x1.shape[0]
    row = lambda i: (i, 0)
    blk = lambda m: pl.BlockSpec((MOVE_TILE, D_MODEL), m)
    nm = N_MAIN // MOVE_TILE
    out_specs, out_shape = [], []
    if emit_x:
        out_specs.append(blk(row))
        out_shape.append(jax.ShapeDtypeStruct((n, D_MODEL), f32))
    if split:
        out_specs += [blk(lambda i: (jnp.minimum(i, nm - 1), 0)), blk(lambda i: (jnp.maximum(i - nm, 0), 0))]
        out_shape += [jax.ShapeDtypeStruct((N_MAIN, D_MODEL), xn_dtype),
                      jax.ShapeDtypeStruct((n - N_MAIN, D_MODEL), xn_dtype)]
    else:
        out_specs.append(blk(row))
        out_shape.append(jax.ShapeDtypeStruct((n, D_MODEL), xn_dtype))
    return pl.pallas_call(
        functools.partial(_combine_kernel, emit_x=emit_x, split=split),
        grid=(n // MOVE_TILE,),
        in_specs=[pl.BlockSpec((1, 1, 2 * MOVE_TILE), lambda i: (i, 0, 0), memory_space=pltpu.SMEM),
                  blk(row), pl.BlockSpec((MOVE_TILE, LANES), row),
                  pl.BlockSpec((1, D_MODEL), lambda i: (0, 0)), pl.BlockSpec(memory_space=pl.ANY)],
        out_specs=out_specs,
        out_shape=out_shape,
        scratch_shapes=[pltpu.VMEM((2, MOVE_TILE, D_MODEL), f32), pltpu.SemaphoreType.DMA(())],
        compiler_params=_cp(("arbitrary",)),
        name="moe_combine",
    )(pos, x1, route, gn, os_rows)


S5_GROUP = 16
S5_GROUPS = D_MODEL // S5_GROUP
S5_STATE = 64
S5_CB = 128
S5_NB = D_MODEL // S5_CB
S5_GPB = S5_CB // S5_GROUP
S5_SB = S5_GPB * S5_STATE
S5_BC = S5_STATE * S5_GROUP


def _s5_disc_kernel(lr_ref, li_ref, ldt_ref, bre_ref, bim_ref,
                    abr_ref, abi_ref, ab2r_ref, ab2i_ref, bbr_ref, bbi_ref, abbr_ref, abbi_ref):
    lr, li = lr_ref[...], li_ref[...]
    dt = jnp.exp(ldt_ref[...])
    mag = jnp.exp(lr * dt)
    ang = li * dt
    ab_re, ab_im = mag * jnp.cos(ang), mag * jnp.sin(ang)
    nr, ni = ab_re - 1.0, ab_im
    den = lr * lr + li * li
    f_re = (nr * lr + ni * li) / den
    f_im = (ni * lr - nr * li) / den
    abr_ref[...] = ab_re
    abi_ref[...] = ab_im
    ab2r_ref[...] = ab_re * ab_re - ab_im * ab_im
    ab2i_ref[...] = 2.0 * (ab_re * ab_im)
    pi = lax.broadcasted_iota(jnp.int32, (S5_STATE, S5_BC), 0)
    ci = lax.broadcasted_iota(jnp.int32, (S5_STATE, S5_BC), 1)
    rep = jnp.where(ci // S5_GROUP == pi, 1.0, 0.0).astype(bf16)

    def expand(v):
        hi, mid, lo = _split3(v)
        return _dot(hi, rep) + _dot(mid, rep) + _dot(lo, rep)

    fr, fi, ar, ai = expand(f_re), expand(f_im), expand(ab_re), expand(ab_im)
    br, bi = bre_ref[...], bim_ref[...]
    bb_re = fr * br - fi * bi
    bb_im = fr * bi + fi * br
    bbr_ref[...] = bb_re
    bbi_ref[...] = bb_im
    abbr_ref[...] = ar * bb_re - ai * bb_im
    abbi_ref[...] = ar * bb_im + ai * bb_re


def _s5_weights(lam_re, lam_im, log_dt, b_re, b_im, c_re, c_im):
    st = jax.ShapeDtypeStruct((S5_GROUPS, S5_STATE), f32)
    bc = jax.ShapeDtypeStruct((S5_GROUPS, S5_BC), f32)
    ab_re, ab_im, ab2_re, ab2_im, bb_re, bb_im, abb_re, abb_im = pl.pallas_call(
        _s5_disc_kernel, out_shape=[st, st, st, st, bc, bc, bc, bc], name="s5_discretize",
    )(lam_re, lam_im, log_dt[:, None], b_re.reshape(S5_GROUPS, S5_BC), b_im.reshape(S5_GROUPS, S5_BC))
    eye = jnp.eye(S5_GPB, dtype=f32)

    def in_blocks(m):
        m = m.reshape(S5_NB, S5_GPB, S5_STATE, S5_GROUP)
        return jnp.einsum("jgpc,gh->jgchp", m, eye).reshape(S5_NB, S5_CB, S5_SB)

    def out_blocks(m):
        m = m.reshape(S5_NB, S5_GPB, S5_GROUP, S5_STATE)
        return jnp.einsum("jgcp,gh->jgphc", m, eye).reshape(S5_NB, S5_SB, S5_CB)

    bb2 = jnp.concatenate([
        jnp.concatenate([in_blocks(bb_re), in_blocks(bb_im)], axis=2),
        jnp.concatenate([in_blocks(abb_re), in_blocks(abb_im)], axis=2)], axis=1).astype(bf16)
    cc = jnp.concatenate([out_blocks(c_re), -out_blocks(c_im)], axis=1).astype(bf16)
    rows = [v.reshape(S5_NB, 1, S5_SB) for v in (ab_re, ab_im, ab2_re, ab2_im)]
    abv = jnp.concatenate(rows + [jnp.zeros((S5_NB, SUB - len(rows), S5_SB), f32)], axis=1)
    return bb2, cc, abv


def _s5_seq_kernel(x0_ref, x1_ref, x2_ref, x3_ref, halo0_ref, h0_ref, bb2_ref, cc_ref, abv_ref, y_ref, hout_ref,
                   xf_scr, xp_scr, bu_scr, hs_scr, yp_scr, h_scr, halo_scr):
    tb = pl.program_id(1)
    TL = x0_ref.shape[0]
    KB = TL // 2
    RB = BATCH * TL

    @pl.when(tb == 0)
    def _():
        h_scr[...] = h0_ref[0]
        halo_scr[...] = halo0_ref[...].astype(f32)

    for b, xb_ref in enumerate((x0_ref, x1_ref, x2_ref, x3_ref)):
        xf_scr[b * TL:(b + 1) * TL, :] = xb_ref[...].astype(f32)
    for b in range(BATCH):
        for p in range(2):
            xp_scr[pl.ds(2 * b + p, KB, stride=SUB), :] = xf_scr[pl.ds(b * TL + p, KB, stride=2), :]
    x = xp_scr[...]
    xc = jnp.concatenate([halo_scr[...], x], axis=0)
    odd = (lax.broadcasted_iota(jnp.int32, (RB + SUB, 1), 0) & 1) == 1
    xprev = jnp.where(odd, pltpu.roll(xc, 1, axis=0), pltpu.roll(xc, SUB - 1, axis=0))[SUB:]
    halo_scr[...] = x[RB - SUB:]
    lhs = jnp.concatenate([x, xprev], axis=1).astype(bf16)
    bu_scr[...] = _dot(lhs, bb2_ref[0])
    a2r = abv_ref[0, 2:3, :]
    a2i = abv_ref[0, 3:4, :]

    def step(k, carry):
        hr, hi = carry
        r0 = pl.multiple_of(k * SUB, SUB)
        bu = bu_scr[pl.ds(r0, SUB), :]
        nr = a2r * hr - a2i * hi + bu[:, :S5_SB]
        ni = a2r * hi + a2i * hr + bu[:, S5_SB:]
        hs_scr[pl.ds(r0, SUB), :S5_SB] = nr
        hs_scr[pl.ds(r0, SUB), S5_SB:] = ni
        return nr, ni

    hr, hi = lax.fori_loop(0, RB // SUB, step, (h_scr[:, :S5_SB], h_scr[:, S5_SB:]), unroll=4)
    h_scr[:, :S5_SB] = hr
    h_scr[:, S5_SB:] = hi
    yp_scr[...] = _dot(hs_scr[...].astype(bf16), cc_ref[0])
    for b in range(BATCH):
        for p in range(2):
            y_ref[b, pl.ds(p, KB, stride=2), :] = yp_scr[pl.ds(2 * b + p, KB, stride=SUB), :]

    @pl.when(tb == pl.num_programs(1) - 1)
    def _():
        hout_ref[0] = h_scr[...]


def _s5_seq_call(x, halo0, h0, bb2, cc, abv, *, row0, seq_len, tl):
    wsel = lambda j, t: (j, 0, 0)
    rb = BATCH * tl
    xspec = lambda b: pl.BlockSpec((tl, S5_CB), lambda j, t: ((row0 + b * seq_len) // tl + t, j))
    return pl.pallas_call(
        _s5_seq_kernel,
        grid=(S5_NB, seq_len // tl),
        in_specs=[xspec(b) for b in range(BATCH)]
        + [pl.BlockSpec((SUB, S5_CB), lambda j, t: (0, j)),
           pl.BlockSpec((1, SUB, 2 * S5_SB), wsel),
           pl.BlockSpec((1, 2 * S5_CB, 2 * S5_SB), wsel),
           pl.BlockSpec((1, 2 * S5_SB, S5_CB), wsel),
           pl.BlockSpec((1, SUB, S5_SB), wsel)],
        out_specs=[pl.BlockSpec((BATCH, tl, S5_CB), lambda j, t: (0, t, j)),
                   pl.BlockSpec((1, SUB, 2 * S5_SB), wsel)],
        out_shape=[jax.ShapeDtypeStruct((BATCH, seq_len, D_MODEL), f32),
                   jax.ShapeDtypeStruct((S5_NB, SUB, 2 * S5_SB), f32)],
        scratch_shapes=[pltpu.VMEM((rb, S5_CB), f32), pltpu.VMEM((rb, S5_CB), f32),
                        pltpu.VMEM((rb, 2 * S5_SB), f32), pltpu.VMEM((rb, 2 * S5_SB), f32),
                        pltpu.VMEM((rb, S5_CB), f32),
                        pltpu.VMEM((SUB, 2 * S5_SB), f32), pltpu.VMEM((SUB, S5_CB), f32)],
        compiler_params=_cp(("arbitrary", "arbitrary")),
        name=f"s5_seq_{seq_len}",
    )(x, x, x, x, halo0, h0, bb2, cc, abv)


def _s5_sample_kernel(x_ref, hre_ref, him_ref, bb2_ref, cc_ref, abv_ref, y_ref, ore_ref, oim_ref, hs_scr):
    nb = hre_ref.shape[0]
    bu = _dot(x_ref[...], bb2_ref[0, :S5_CB, :])
    ar = abv_ref[0, 0:1, :]
    ai = abv_ref[0, 1:2, :]
    hr, hi = hre_ref[...], him_ref[...]
    for t in range(DEC_SEQ):
        rows = slice(t * nb, (t + 1) * nb)
        hr, hi = (ar * hr - ai * hi + bu[rows, :S5_SB], ar * hi + ai * hr + bu[rows, S5_SB:])
        hs_scr[rows, :S5_SB] = hr
        hs_scr[rows, S5_SB:] = hi
    y_ref[...] = _dot(hs_scr[...].astype(bf16), cc_ref[0])
    ore_ref[...] = hr
    oim_ref[...] = hi


def _s5_sample_call(xt, h_re, h_im, bb2, cc, abv):
    n = xt.shape[0]
    nb = h_re.shape[0]
    wsel = lambda j: (j, 0, 0)
    st = pl.BlockSpec((nb, S5_SB), lambda j: (0, j))
    return pl.pallas_call(
        _s5_sample_kernel,
        grid=(S5_NB,),
        in_specs=[pl.BlockSpec((n, S5_CB), lambda j: (0, j)), st, st,
                  pl.BlockSpec((1, 2 * S5_CB, 2 * S5_SB), wsel),
                  pl.BlockSpec((1, 2 * S5_SB, S5_CB), wsel),
                  pl.BlockSpec((1, SUB, S5_SB), wsel)],
        out_specs=[pl.BlockSpec((n, S5_CB), lambda j: (0, j)), st, st],
        out_shape=[jax.ShapeDtypeStruct((n, D_MODEL), f32),
                   jax.ShapeDtypeStruct(h_re.shape, f32), jax.ShapeDtypeStruct(h_im.shape, f32)],
        scratch_shapes=[pltpu.VMEM((n, 2 * S5_SB), f32)],
        compiler_params=_cp(("arbitrary",)),
        name="s5_sample",
    )(xt, h_re, h_im, bb2, cc, abv)


GLA_CHUNK = 256
S5_TL = 128


def _moe_layer(layer, x1, xnf, route, plan, cnt, wg, wu, wd, gn, xn_dtype, *, emit_x, split):
    pos, item_tile, item_expert, item_lo, item_hi, n_items = _moe_plan(plan, cnt, x1.shape[0])
    xs = _dispatch_call(xnf, pos)
    os_rows = _experts_call(xs, item_tile, item_expert + layer * MOE_EXPERTS, item_lo, item_hi, n_items, wg, wu, wd)
    return _combine_call(x1, route, pos, os_rows, gn, xn_dtype, emit_x=emit_x, split=split)


def kernel(x_prompt, x_sample, state_gla, state_s5_re, state_s5_im, meta_tokens, norm_mix_g, norm_ffn_g, norm_final_g, gla_w_in, gla_w_a2, gla_b_a, gla_g_o, gla_w_o, s5_lambda_re, s5_lambda_im, s5_log_dt, s5_b_re, s5_b_im, s5_c_re, s5_c_im, s5_d, s5_w_glu, s5_b_glu, moe_w_rg, moe_b_rg, moe_w_re, moe_b_re, moe_w_gate, moe_w_up, moe_w_down):
    row = lambda v: v.reshape(1, -1)
    x0 = jnp.concatenate([
        x_prompt.reshape(N_MAIN, D_MODEL),
        jnp.tile(meta_tokens.astype(x_prompt.dtype), (BATCH, 1)),
        x_sample.reshape(N_SAMPLE, D_MODEL),
        jnp.zeros((N_ROWS - N_REAL, D_MODEL), x_prompt.dtype)], axis=0)
    wg = moe_w_gate.reshape(-1, D_MODEL, D_EXPERT)
    wu = moe_w_up.reshape(-1, D_MODEL, D_EXPERT)
    wd = moe_w_down.reshape(-1, D_EXPERT, D_MODEL)

    w_in = gla_w_in.reshape(D_MODEL, -1)
    wa1 = jnp.pad(w_in[:, GLA_QKVR:], ((0, 0), (0, LANES - GLA_RANK)))
    wa2 = jnp.pad(gla_w_a2.reshape(GLA_RANK, GLA_KEY), ((0, LANES - GLA_RANK), (0, 0)))
    xn, glog = _norm_gate_call(x0, row(norm_mix_g[0]), wa1, wa2, row(gla_b_a))
    proj = _proj_call(xn, w_in, GLA_QKVR)
    go = row(gla_g_o)
    og = jnp.zeros((N_ROWS, GLA_VAL), bf16)
    s_zero = jnp.zeros((BATCH, GLA_HEADS, GLA_DK, GLA_DV), f32)
    og, s_meta = _gla_seq_call(proj, glog, go, s_zero, og, row0=ROW_META, C=N_META, n_chunks=1, d=SUB)
    og, s_prompt = _gla_seq_call(proj, glog, go, s_meta, og, row0=0, C=GLA_CHUNK, n_chunks=SEQ // GLA_CHUNK, d=SUB)
    og, s_sample = _gla_sample_call(proj, glog, go, state_gla.reshape(DEC_BATCH, GLA_HEADS, GLA_DK, GLA_DV), og,
                                    row0=ROW_SAMPLE)
    wr, br = _router_weights(moe_w_rg[0], moe_b_rg[0], moe_w_re[0], moe_b_re[0])
    tile_row = lambda i: (i, 0)
    x1, xnf, route, plan, cnt = _mix_out_call(
        _gla_out_kernel, "gla_out", N_ROWS, [og, x0], [tile_row, tile_row],
        [gla_w_o.reshape(GLA_VAL, D_MODEL).astype(bf16), row(norm_ffn_g[0]), wr, br])
    x2, xn2 = _moe_layer(0, x1, xnf, route, plan, cnt, wg, wu, wd, row(norm_mix_g[1]), bf16, emit_x=True, split=False)

    bb2, cc, abv = _s5_weights(s5_lambda_re[0], s5_lambda_im[0], s5_log_dt[0], s5_b_re[0], s5_b_im[0],
                               s5_c_re[0], s5_c_im[0])
    y_meta, h_meta = _s5_seq_call(xn2, jnp.zeros((SUB, D_MODEL), bf16), jnp.zeros((S5_NB, SUB, 2 * S5_SB), f32),
                                  bb2, cc, abv, row0=ROW_META, seq_len=N_META, tl=N_META)
    halo = xn2[ROW_META:ROW_SAMPLE].reshape(BATCH, N_META, D_MODEL)[:, N_META - 2:].reshape(SUB, D_MODEL)
    y_main, h_main = _s5_seq_call(xn2, halo, h_meta, bb2, cc, abv, row0=0, seq_len=SEQ, tl=S5_TL)
    xt_sample = xn2[ROW_SAMPLE:N_REAL].reshape(DEC_BATCH, DEC_SEQ, D_MODEL).transpose(1, 0, 2).reshape(N_SAMPLE, D_MODEL)
    y_samp, s5r_s, s5i_s = _s5_sample_call(
        xt_sample, state_s5_re.reshape(DEC_BATCH, S5_GROUPS * S5_STATE),
        state_s5_im.reshape(DEC_BATCH, S5_GROUPS * S5_STATE), bb2, cc, abv)
    ys_tail = jnp.concatenate([
        y_meta.reshape(N_METAROWS, D_MODEL),
        y_samp.reshape(DEC_SEQ, DEC_BATCH, D_MODEL).transpose(1, 0, 2).reshape(N_SAMPLE, D_MODEL),
        jnp.zeros((N_ROWS - N_REAL, D_MODEL), f32)], axis=0)
    wr, br = _router_weights(moe_w_rg[1], moe_b_rg[1], moe_w_re[1], moe_b_re[1])
    nm = N_MAIN // MIX_TILE
    x3, xnf, route, plan, cnt = _mix_out_call(
        _s5_out_kernel, "s5_out", N_ROWS, [y_main.reshape(N_MAIN, D_MODEL), ys_tail, x2],
        [lambda i: (jnp.minimum(i, nm - 1), 0), lambda i: (jnp.maximum(i - nm, 0), 0), tile_row],
        [row(norm_mix_g[1]), row(s5_d), s5_w_glu.reshape(D_MODEL, D_MODEL).astype(bf16), row(s5_b_glu),
         row(norm_ffn_g[1]), wr, br])
    y_main_out, y_tail_out = _moe_layer(1, x3, xnf, route, plan, cnt, wg, wu, wd, row(norm_final_g), f32,
                                        emit_x=False, split=True)

    y_prompt = y_main_out.reshape(BATCH, SEQ, D_MODEL)
    y_sample = y_tail_out[N_METAROWS:N_METAROWS + N_SAMPLE].reshape(DEC_BATCH, DEC_SEQ, D_MODEL)
    hfin = h_main.reshape(S5_NB, BATCH, 2, 2, S5_GPB, S5_STATE)[:, :, 1]
    s5r_p = hfin[:, :, 0].transpose(1, 0, 2, 3).reshape(1, BATCH, S5_GROUPS, S5_STATE)
    s5i_p = hfin[:, :, 1].transpose(1, 0, 2, 3).reshape(1, BATCH, S5_GROUPS, S5_STATE)
    return (y_prompt, y_sample, s_prompt[None], s5r_p, s5i_p, s_sample[None],
            s5r_s.reshape(1, DEC_BATCH, S5_GROUPS, S5_STATE), s5i_s.reshape(1, DEC_BATCH, S5_GROUPS, S5_STATE))
```

```python
import functools

import jax
import jax.numpy as jnp
from jax import lax
from jax.experimental import pallas as pl
from jax.experimental.pallas import tpu as pltpu

f32 = jnp.float32
bf16 = jnp.bfloat16

D_MODEL = 2048
BATCH = 4
SEQ = 2048
DEC_BATCH = 128
DEC_SEQ = 4
N_META = 16
EPS = 1e-6
GLA_HEADS = 4
GLA_DK = 256
GLA_DV = 512
GLA_KEY = GLA_HEADS * GLA_DK
GLA_VAL = GLA_HEADS * GLA_DV
GLA_RANK = 16
GLA_TAU = 16.0
GLA_QKVR = 2 * GLA_KEY + 2 * GLA_VAL

N_MAIN = BATCH * SEQ
N_METAROWS = BATCH * N_META
N_SAMPLE = DEC_BATCH * DEC_SEQ
ROW_META = N_MAIN
ROW_SAMPLE = N_MAIN + N_METAROWS
N_REAL = ROW_SAMPLE + N_SAMPLE
ROW_TILE = 512
N_ROWS = -(-N_REAL // ROW_TILE) * ROW_TILE

VMEM_LIMIT = 56 * 1024 * 1024


def _cp(sem, vmem=VMEM_LIMIT):
    return pltpu.CompilerParams(dimension_semantics=sem, vmem_limit_bytes=vmem)


def _dot(a, b):
    return jnp.dot(a, b, preferred_element_type=f32)


def _dot_nt(a, b):
    return lax.dot_general(a, b, (((1,), (1,)), ((), ())), preferred_element_type=f32)


def _dot_tn(a, b):
    return lax.dot_general(a, b, (((0,), (0,)), ((), ())), preferred_element_type=f32)


def _sigmoid(x):
    return 1.0 / (1.0 + jnp.exp(-x))


def _split3(x):
    hi = x.astype(bf16)
    r1 = x - hi.astype(f32)
    mid = r1.astype(bf16)
    lo = (r1 - mid.astype(f32)).astype(bf16)
    return hi, mid, lo


def _cumsum_rows(g, C):
    if C <= 16:
        row = lax.broadcasted_iota(jnp.int32, (C, 1), 0)
        b = jnp.zeros_like(g)
        for s in range(C):
            b = b + jnp.where(row >= s, g[s:s + 1, :], 0.0)
        return b
    row = lax.broadcasted_iota(jnp.int32, (C, C), 0)
    col = lax.broadcasted_iota(jnp.int32, (C, C), 1)
    tri = jnp.where(row >= col, 1.0, 0.0).astype(bf16)
    hi, mid, lo = _split3(g)
    return _dot(tri, hi) + _dot(tri, mid) + _dot(tri, lo)


PAIR_LEVEL = 1000


def _gla_pair_code(C, d, tree=True):
    ti = lax.broadcasted_iota(jnp.int32, (C, C), 0)
    si = lax.broadcasted_iota(jnp.int32, (C, C), 1)
    code = jnp.where((ti // d == si // d) & (si <= ti), 1 + ti - si, 0)
    h = d
    while tree and h < C:
        tb = ti // h
        code = jnp.where(((tb % 2) == 1) & ((si // h) == tb - 1), PAIR_LEVEL + h, code)
        h *= 2
    return code


def _gla_scores(q, k, b, code, C, d, tree=True):
    row = lax.broadcasted_iota(jnp.int32, (C, 1), 0)
    scores = jnp.zeros((C, C), f32)
    for dl in range(d):
        ks = k if dl == 0 else pltpu.roll(k, dl, axis=0)
        bs = b if dl == 0 else pltpu.roll(b, dl, axis=0)
        term = q * ks * jnp.exp(jnp.minimum(b - bs, 0.0))
        colv = jnp.sum(term, axis=1, keepdims=True)
        scores = jnp.where(code == 1 + dl, colv, scores)
    z = b
    s = 1
    while tree and 2 * s < C:
        z = jnp.where((row & s) != 0, pltpu.roll(z, s, axis=0), z)
        s *= 2
        h = s
        if h < d:
            continue
        bnext = pltpu.roll(z, C - h, axis=0)
        qh = (q * jnp.exp(jnp.minimum(b - z, 0.0))).astype(bf16)
        kh = (k * jnp.exp(jnp.minimum(bnext - b, 0.0))).astype(bf16)
        scores = jnp.where(code == PAIR_LEVEL + h, _dot_nt(qh, kh), scores)
    return scores


def _gla_chunk(q, k, v, g, S, code, C, d, tree=True):
    b = _cumsum_rows(g, C)
    o = _dot((q * jnp.exp(b)).astype(bf16), S.astype(bf16))
    scores = _gla_scores(q, k, b, code, C, d, tree)
    o = o + _dot(scores.astype(bf16), v)
    b_last = b[C - 1:C, :]
    kd = (k * jnp.exp(b_last - b)).astype(bf16)
    if C == GLA_DK:
        eye = code == 1
    else:
        eye = (lax.broadcasted_iota(jnp.int32, (GLA_DK, GLA_DK), 0)
               == lax.broadcasted_iota(jnp.int32, (GLA_DK, GLA_DK), 1))
    dec_col = jnp.sum(jnp.where(eye, jnp.exp(b_last), 0.0), axis=1, keepdims=True)
    S_new = dec_col * S + _dot_tn(kd, v)
    return o, S_new


def _gla_head_epilogue(o, r, go):
    ms = jnp.mean(o * o, axis=1, keepdims=True)
    on = o * lax.rsqrt(ms + EPS) * go
    return on * (r * _sigmoid(r))


def _gla_seq_kernel(q_ref, k_ref, v_ref, r_ref, g_ref, go_ref, s0_ref, _og_in, og_ref, sout_ref, s_scr, *, C, d):
    c = pl.program_id(1)

    @pl.when(c == 0)
    def _():
        s_scr[...] = s0_ref[0]

    code = _gla_pair_code(C, d)

    def head(h, carry):
        ck = pl.ds(pl.multiple_of(h * GLA_DK, GLA_DK), GLA_DK)
        cv = pl.ds(pl.multiple_of(h * GLA_DV, GLA_DV), GLA_DV)
        q = q_ref[:, ck].astype(f32) * (GLA_DK ** -0.5)
        k = k_ref[:, ck].astype(f32)
        o, S_new = _gla_chunk(q, k, v_ref[:, cv], g_ref[:, ck], s_scr[h], code, C, d)
        s_scr[h] = S_new
        og_ref[:, cv] = _gla_head_epilogue(o, r_ref[:, cv].astype(f32), go_ref[:, cv]).astype(og_ref.dtype)
        return carry

    lax.fori_loop(0, GLA_HEADS, head, 0)

    @pl.when(c == pl.num_programs(1) - 1)
    def _():
        sout_ref[0] = s_scr[...]


def _gla_seq_call(proj, glog, go, s0, og_buf, *, row0, C, n_chunks, d):
    blk0 = row0 // C
    rows = lambda b, c: blk0 + b * n_chunks + c
    kern = functools.partial(_gla_seq_kernel, C=C, d=d)
    return pl.pallas_call(
        kern,
        grid=(BATCH, n_chunks),
        in_specs=[
            pl.BlockSpec((C, GLA_KEY), lambda b, c: (rows(b, c), 0)),
            pl.BlockSpec((C, GLA_KEY), lambda b, c: (rows(b, c), 1)),
            pl.BlockSpec((C, GLA_VAL), lambda b, c: (rows(b, c), 1)),
            pl.BlockSpec((C, GLA_VAL), lambda b, c: (rows(b, c), 2)),
            pl.BlockSpec((C, GLA_KEY), lambda b, c: (rows(b, c), 0)),
            pl.BlockSpec((1, GLA_VAL), lambda b, c: (0, 0)),
            pl.BlockSpec((1, GLA_HEADS, GLA_DK, GLA_DV), lambda b, c: (b, 0, 0, 0)),
            pl.BlockSpec(memory_space=pl.ANY),
        ],
        out_specs=[
            pl.BlockSpec((C, GLA_VAL), lambda b, c: (rows(b, c), 0)),
            pl.BlockSpec((1, GLA_HEADS, GLA_DK, GLA_DV), lambda b, c: (b, 0, 0, 0)),
        ],
        out_shape=[
            jax.ShapeDtypeStruct(og_buf.shape, og_buf.dtype),
            jax.ShapeDtypeStruct((BATCH, GLA_HEADS, GLA_DK, GLA_DV), f32),
        ],
        scratch_shapes=[pltpu.VMEM((GLA_HEADS, GLA_DK, GLA_DV), f32)],
        input_output_aliases={7: 0},
        compiler_params=_cp(("arbitrary", "arbitrary")),
        name=f"gla_seq_c{C}",
    )(proj, proj, proj, proj, glog, go, s0, og_buf)


SAMPLE_BB = 4
SAMPLE_C = SAMPLE_BB * DEC_SEQ


def _gla_sample_kernel(q_ref, k_ref, v_ref, r_ref, g_ref, go_ref, s0_ref, _og_in, og_ref, sout_ref):
    row = lax.broadcasted_iota(jnp.int32, (SAMPLE_C, 1), 0)

    code = _gla_pair_code(SAMPLE_C, DEC_SEQ, tree=False)

    def take(x, bb):
        sh = (SAMPLE_C - DEC_SEQ * bb) % SAMPLE_C
        return jnp.where(row < DEC_SEQ, pltpu.roll(x, sh, axis=0) if sh else x, 0.0)

    def head(h, carry):
        ck = pl.ds(pl.multiple_of(h * GLA_DK, GLA_DK), GLA_DK)
        cv = pl.ds(pl.multiple_of(h * GLA_DV, GLA_DV), GLA_DV)
        q_all = q_ref[:, ck].astype(f32) * (GLA_DK ** -0.5)
        k_all = k_ref[:, ck].astype(f32)
        v_all = v_ref[:, cv].astype(f32)
        r_all = r_ref[:, cv].astype(f32)
        g_all = g_ref[:, ck]
        go = go_ref[:, cv]
        acc = jnp.zeros((SAMPLE_C, GLA_DV), f32)
        for bb in range(SAMPLE_BB):
            o, S_new = _gla_chunk(take(q_all, bb), take(k_all, bb), take(v_all, bb).astype(bf16),
                                  take(g_all, bb), s0_ref[bb, h], code, SAMPLE_C, DEC_SEQ, tree=False)
            sout_ref[bb, h] = S_new
            y = _gla_head_epilogue(o, take(r_all, bb), go)
            acc = jnp.where(row // DEC_SEQ == bb, pltpu.roll(y, DEC_SEQ * bb, axis=0) if bb else y, acc)
        og_ref[:, cv] = acc.astype(og_ref.dtype)
        return carry

    lax.fori_loop(0, GLA_HEADS, head, 0)


def _gla_sample_call(proj, glog, go, s0, og_buf, *, row0):
    n_seq = s0.shape[0]
    blk0 = row0 // SAMPLE_C
    st_spec = pl.BlockSpec((SAMPLE_BB, GLA_HEADS, GLA_DK, GLA_DV), lambda i: (i, 0, 0, 0))
    return pl.pallas_call(
        _gla_sample_kernel,
        grid=(n_seq // SAMPLE_BB,),
        in_specs=[
            pl.BlockSpec((SAMPLE_C, GLA_KEY), lambda i: (blk0 + i, 0)),
            pl.BlockSpec((SAMPLE_C, GLA_KEY), lambda i: (blk0 + i, 1)),
            pl.BlockSpec((SAMPLE_C, GLA_VAL), lambda i: (blk0 + i, 1)),
            pl.BlockSpec((SAMPLE_C, GLA_VAL), lambda i: (blk0 + i, 2)),
            pl.BlockSpec((SAMPLE_C, GLA_KEY), lambda i: (blk0 + i, 0)),
            pl.BlockSpec((1, GLA_VAL), lambda i: (0, 0)),
            st_spec,
            pl.BlockSpec(memory_space=pl.ANY),
        ],
        out_specs=[pl.BlockSpec((SAMPLE_C, GLA_VAL), lambda i: (blk0 + i, 0)), st_spec],
        out_shape=[jax.ShapeDtypeStruct(og_buf.shape, og_buf.dtype), jax.ShapeDtypeStruct(s0.shape, f32)],
        input_output_aliases={7: 0},
        compiler_params=_cp(("arbitrary",)),
        name="gla_sample",
    )(proj, proj, proj, proj, glog, go, s0, og_buf)


LANES = 128
SUB = 8
MOE_GROUPS = 4
MOE_EPG = 8
MOE_EXPERTS = MOE_GROUPS * MOE_EPG
D_EXPERT = 256
ROUTE_E0 = MOE_GROUPS
MIX_TILE = 256
PROJ_NT = 1024


def _rms(x, g):
    r = lax.rsqrt(jnp.mean(x * x, axis=-1, keepdims=True) + EPS)
    return (x * r) * g


def _log_sigmoid(z):
    return jnp.minimum(z, 0.0) - jnp.log1p(jnp.exp(-jnp.abs(z)))


def _main_or_tail(tile, main_ref, tail_ref):
    return jnp.where(pl.program_id(0) < N_MAIN // tile, main_ref[...], tail_ref[...])


def _main_tail_maps(tile):
    nm = N_MAIN // tile
    return (lambda i: (jnp.minimum(i, nm - 1), 0)), (lambda i: (jnp.maximum(i - nm, 0), 0))


def _norm_gate_kernel(xm_ref, xt_ref, gn_ref, wa1_ref, wa2_ref, ba_ref, xn_ref, gl_ref):
    xnb = _rms(_main_or_tail(ROW_TILE, xm_ref, xt_ref), gn_ref[...]).astype(bf16)
    xn_ref[...] = xnb
    a = _dot(xnb, wa1_ref[...].astype(bf16))
    z = _dot(a.astype(bf16), wa2_ref[...].astype(bf16)) + ba_ref[...]
    gl_ref[...] = _log_sigmoid(z) * (1.0 / GLA_TAU)


def _norm_gate_call(x_main, x_tail, gn, wa1, wa2, ba):
    n = x_main.shape[0] + x_tail.shape[0]
    row = lambda i: (i, 0)
    fix = lambda i: (0, 0)
    main_map, tail_map = _main_tail_maps(ROW_TILE)
    return pl.pallas_call(
        _norm_gate_kernel,
        grid=(n // ROW_TILE,),
        in_specs=[pl.BlockSpec((ROW_TILE, D_MODEL), main_map), pl.BlockSpec((ROW_TILE, D_MODEL), tail_map),
                  pl.BlockSpec((1, D_MODEL), fix),
                  pl.BlockSpec((D_MODEL, LANES), fix), pl.BlockSpec((LANES, GLA_KEY), fix),
                  pl.BlockSpec((1, GLA_KEY), fix)],
        out_specs=[pl.BlockSpec((ROW_TILE, D_MODEL), row), pl.BlockSpec((ROW_TILE, GLA_KEY), row)],
        out_shape=[jax.ShapeDtypeStruct((n, D_MODEL), bf16), jax.ShapeDtypeStruct((n, GLA_KEY), f32)],
        compiler_params=_cp(("arbitrary",)),
        name="norm_gate",
    )(x_main, x_tail, gn, wa1, wa2, ba)


def _proj_kernel(xn_ref, w_ref, o_ref, wb_scr):
    @pl.when(pl.program_id(1) == 0)
    def _():
        wb_scr[...] = w_ref[...].astype(bf16)

    o_ref[...] = _dot(xn_ref[...], wb_scr[...]).astype(o_ref.dtype)


def _proj_call(xn, w, n_cols):
    n = xn.shape[0]
    return pl.pallas_call(
        _proj_kernel,
        grid=(n_cols // PROJ_NT, n // ROW_TILE),
        in_specs=[pl.BlockSpec((ROW_TILE, D_MODEL), lambda j, i: (i, 0)),
                  pl.BlockSpec((D_MODEL, PROJ_NT), lambda j, i: (0, j))],
        out_specs=pl.BlockSpec((ROW_TILE, PROJ_NT), lambda j, i: (i, j)),
        out_shape=jax.ShapeDtypeStruct((n, n_cols), bf16),
        scratch_shapes=[pltpu.VMEM((D_MODEL, PROJ_NT), bf16)],
        compiler_params=_cp(("arbitrary", "arbitrary")),
        name="gla_proj",
    )(xn, w)


def _route(xn, wr, br, cnt_ref):
    R = xn.shape[0]
    xh = xn.astype(bf16)
    xl = (xn - xh.astype(f32)).astype(bf16)
    wh = wr.astype(bf16)
    wl = (wr - wh.astype(f32)).astype(bf16)
    logits = _dot(xh, wh) + _dot(xl, wh) + _dot(xh, wl) + br
    lane_i = lax.broadcasted_iota(jnp.int32, (R, LANES), 1)
    lane = lane_i.astype(f32)
    neg = -jnp.inf
    big = float(LANES)
    is_g = lane_i < MOE_GROUPS
    lg = jnp.where(is_g, logits, neg)
    mg = jnp.max(lg, axis=1, keepdims=True)
    gidx = jnp.min(jnp.where(lg == mg, lane, big), axis=1, keepdims=True)
    ptop = 1.0 / jnp.sum(jnp.where(is_g, jnp.exp(logits - mg), 0.0), axis=1, keepdims=True)
    lo = ROUTE_E0 + MOE_EPG * gidx
    le = jnp.where((lane >= lo) & (lane < lo + MOE_EPG), logits, neg)
    v1 = jnp.max(le, axis=1, keepdims=True)
    i1 = jnp.min(jnp.where(le == v1, lane, big), axis=1, keepdims=True)
    le2 = jnp.where(lane == i1, neg, le)
    v2 = jnp.max(le2, axis=1, keepdims=True)
    i2 = jnp.min(jnp.where(le2 == v2, lane, big), axis=1, keepdims=True)
    s = jnp.exp(v2 - v1)
    w0 = ptop / (1.0 + s)
    w1 = ptop * s / (1.0 + s)
    oh = jnp.where((lane == i1) | (lane == i2), 1.0, 0.0)
    ri = lax.broadcasted_iota(jnp.int32, (R, R), 0)
    ci = lax.broadcasted_iota(jnp.int32, (R, R), 1)
    before = jnp.where(ri > ci, 1.0, 0.0).astype(bf16)
    tot = _dot(before, oh.astype(bf16)) + cnt_ref[...]
    rank0 = jnp.sum(jnp.where(lane == i1, tot, 0.0), axis=1, keepdims=True)
    rank1 = jnp.sum(jnp.where(lane == i2, tot, 0.0), axis=1, keepdims=True)
    cnt_ref[...] = cnt_ref[...] + jnp.sum(oh, axis=0, keepdims=True)
    vals = (i1 - ROUTE_E0, i2 - ROUTE_E0, rank0, rank1, w0, w1)
    slab = jnp.zeros((R, LANES), f32)
    for j, v in enumerate(vals):
        slab = jnp.where(lane_i == j, v, slab)
    sub = lax.broadcasted_iota(jnp.int32, (SUB, R), 0)
    plan = jnp.zeros((SUB, R), f32)
    for j, v in enumerate(vals[:4]):
        as_row = jnp.sum(jnp.where(ri == ci, v, 0.0), axis=0, keepdims=True)
        plan = jnp.where(sub == j, as_row, plan)
    return slab, plan


def _router_weights(w_rg, b_rg, w_re, b_re):
    w = jnp.concatenate([w_rg, jnp.moveaxis(w_re, 0, 1).reshape(D_MODEL, MOE_EXPERTS)], axis=1)
    b = jnp.concatenate([b_rg, b_re.reshape(MOE_EXPERTS)])
    pad = LANES - w.shape[1]
    return jnp.pad(w, ((0, 0), (0, pad))), jnp.pad(b, (0, pad))[None]


def _mix_out_tail(x1, gffn_ref, wr_ref, br_ref, x1_ref, xn_ref, route_ref, plan_ref, cnt_ref):
    @pl.when(pl.program_id(0) == 0)
    def _():
        cnt_ref[...] = jnp.zeros_like(cnt_ref)

    x1_ref[...] = x1
    xn = _rms(x1, gffn_ref[...])
    xn_ref[...] = xn
    route_ref[...], plan_ref[...] = _route(xn, wr_ref[...], br_ref[...], cnt_ref)


def _gla_out_kernel(og_ref, xm_ref, xt_ref, wo_ref, gffn_ref, wr_ref, br_ref, *outs):
    x1 = _main_or_tail(MIX_TILE, xm_ref, xt_ref) + _dot(og_ref[...], wo_ref[...])
    _mix_out_tail(x1, gffn_ref, wr_ref, br_ref, *outs)


def _gelu_tanh(x):
    return x * (0.5 * (1.0 + jnp.tanh(0.7978845608028654 * (x + 0.044715 * (x * x * x)))))


def _s5_out_kernel(ys_main_ref, ys_tail_ref, x_ref, gmix_ref, d_ref, wglu_ref, bglu_ref, gffn_ref, wr_ref, br_ref,
                   *outs):
    x = x_ref[...]
    u = _rms(x, gmix_ref[...])
    y = _gelu_tanh(_main_or_tail(MIX_TILE, ys_main_ref, ys_tail_ref) + d_ref[...] * u)
    z = _dot(y.astype(bf16), wglu_ref[...]) + bglu_ref[...]
    _mix_out_tail(x + y * _sigmoid(z), gffn_ref, wr_ref, br_ref, *outs)


def _mix_out_call(kern, name, n, row_ins, row_maps, fix_ins):
    row = lambda i: (i, 0)
    fix = lambda i: (0, 0)
    out_row = lambda w: pl.BlockSpec((MIX_TILE, w), row)
    nt = n // MIX_TILE
    return pl.pallas_call(
        kern,
        grid=(nt,),
        in_specs=[pl.BlockSpec((MIX_TILE, a.shape[1]), m) for a, m in zip(row_ins, row_maps)]
        + [pl.BlockSpec(a.shape, fix) for a in fix_ins],
        out_specs=[out_row(D_MODEL), out_row(D_MODEL), out_row(LANES), pl.BlockSpec((SUB, MIX_TILE), row),
                   pl.BlockSpec((1, LANES), fix)],
        out_shape=[jax.ShapeDtypeStruct((n, D_MODEL), f32), jax.ShapeDtypeStruct((n, D_MODEL), f32),
                   jax.ShapeDtypeStruct((n, LANES), f32), jax.ShapeDtypeStruct((nt * SUB, MIX_TILE), f32),
                   jax.ShapeDtypeStruct((1, LANES), f32)],
        compiler_params=_cp(("arbitrary",)),
        name=name,
    )(*row_ins, *fix_ins)


EXPERT_TM = 256
MOVE_TILE = MIX_TILE


def _moe_plan(plan, cnt, n):
    i32 = jnp.int32
    v = plan.reshape(n // MOVE_TILE, SUB, MOVE_TILE)[:, :4].astype(i32)
    counts = cnt[0, ROUTE_E0:ROUTE_E0 + MOE_EXPERTS].astype(i32)
    ends = jnp.cumsum(counts)
    off = ends - counts
    ids = jnp.arange(MOE_EXPERTS, dtype=i32)
    pos = jnp.sum(jnp.where(v[:, 0:2, :, None] == ids, off, 0), axis=-1) + v[:, 2:4]
    total = 2 * n
    n_tiles = total // EXPERT_TM
    n_items = n_tiles + MOE_EXPERTS
    inner = (counts > 0) & (off % EXPERT_TM != 0)
    keys = jnp.concatenate([jnp.arange(n_tiles, dtype=i32) * EXPERT_TM, jnp.where(inner, off, total)])
    idx = jnp.arange(n_items, dtype=i32)
    before = (keys[None, :] < keys[:, None]) | ((keys[None, :] == keys[:, None]) & (idx[None, :] < idx[:, None]))
    order = jnp.sum(before.astype(i32), axis=1)
    starts = jnp.sum(jnp.where(order[:, None] == idx[None, :], keys[:, None], 0), axis=0)
    stops = jnp.concatenate([starts[1:], jnp.full((1,), total, i32)])
    tile = starts // EXPERT_TM
    expert = jnp.sum((ends[None, :] <= starts[:, None]).astype(i32), axis=1)
    used = n_tiles + jnp.sum(inner.astype(i32))
    keep = jnp.minimum(idx, used - 1)
    pick = lambda a: a[keep]
    return (pos.reshape(n // MOVE_TILE, 1, 2 * MOVE_TILE), pick(tile), pick(jnp.minimum(expert, MOE_EXPERTS - 1)),
            pick(starts - tile * EXPERT_TM), pick(stops - tile * EXPERT_TM), used.reshape(1))


def _row_copy(src, src_row, dst, dst_row, sem):
    return pltpu.make_async_copy(src.at[pl.ds(src_row, 1)], dst.at[pl.ds(dst_row, 1)], sem)


def _dispatch_kernel(pos_ref, x_ref, xs_ref, sem):
    def copy(r, s):
        return _row_copy(x_ref, r, xs_ref, pos_ref[0, 0, s * MOVE_TILE + r], sem)

    def start(r, c):
        for s in range(2):
            copy(r, s).start(priority=s)
        return c

    def wait(r, c):
        for s in range(2):
            copy(r, s).wait()
        return c

    lax.fori_loop(0, MOVE_TILE, start, 0, unroll=8)
    lax.fori_loop(0, MOVE_TILE, wait, 0, unroll=8)


def _dispatch_call(xn, pos):
    n = xn.shape[0]
    return pl.pallas_call(
        _dispatch_kernel,
        grid=(n // MOVE_TILE,),
        in_specs=[pl.BlockSpec((1, 1, 2 * MOVE_TILE), lambda i: (i, 0, 0), memory_space=pltpu.SMEM),
                  pl.BlockSpec((MOVE_TILE, D_MODEL), lambda i: (i, 0))],
        out_specs=pl.BlockSpec(memory_space=pl.ANY),
        out_shape=jax.ShapeDtypeStruct((2 * n, D_MODEL), xn.dtype),
        scratch_shapes=[pltpu.SemaphoreType.DMA(())],
        compiler_params=_cp(("arbitrary",)),
        name="moe_dispatch",
    )(pos, xn)


def _experts_kernel(it_ref, ie_ref, lo_ref, hi_ref, n_ref, xs_ref, wg_ref, wu_ref, wd_ref, os_ref,
                    wg_scr, wu_scr, wd_scr):
    i = pl.program_id(0)

    @pl.when(i < n_ref[0])
    def _():
        @pl.when((i == 0) | (ie_ref[i] != ie_ref[jnp.maximum(i - 1, 0)]))
        def _():
            wg_scr[...] = wg_ref[0].astype(bf16)
            wu_scr[...] = wu_ref[0].astype(bf16)
            wd_scr[...] = wd_ref[0].astype(bf16)

        x = xs_ref[...].astype(bf16)
        hg = _dot(x, wg_scr[...])
        hu = _dot(x, wu_scr[...])
        out = _dot((hg * _sigmoid(hg) * hu).astype(bf16), wd_scr[...])
        lo = lo_ref[i]
        row = lax.broadcasted_iota(jnp.int32, (EXPERT_TM, 1), 0)
        mine = (row >= lo) & (row < hi_ref[i])

        @pl.when(lo == 0)
        def _():
            os_ref[...] = jnp.where(mine, out, 0.0)

        @pl.when(lo != 0)
        def _():
            os_ref[...] = jnp.where(mine, out, os_ref[...])


def _experts_call(xs, item_tile, item_expert, item_lo, item_hi, n_items, wg, wu, wd):
    rows = lambda i, it, ie, lo, hi, n: (it[i], 0)
    wsel = lambda i, it, ie, lo, hi, n: (ie[i], 0, 0)
    return pl.pallas_call(
        _experts_kernel,
        grid_spec=pltpu.PrefetchScalarGridSpec(
            num_scalar_prefetch=5,
            grid=(item_tile.shape[0],),
            in_specs=[pl.BlockSpec((EXPERT_TM, D_MODEL), rows),
                      pl.BlockSpec((1, D_MODEL, D_EXPERT), wsel),
                      pl.BlockSpec((1, D_MODEL, D_EXPERT), wsel),
                      pl.BlockSpec((1, D_EXPERT, D_MODEL), wsel)],
            out_specs=pl.BlockSpec((EXPERT_TM, D_MODEL), rows),
            scratch_shapes=[pltpu.VMEM((D_MODEL, D_EXPERT), bf16), pltpu.VMEM((D_MODEL, D_EXPERT), bf16),
                            pltpu.VMEM((D_EXPERT, D_MODEL), bf16)],
        ),
        out_shape=jax.ShapeDtypeStruct(xs.shape, f32),
        compiler_params=_cp(("arbitrary",)),
        name="moe_experts",
    )(item_tile, item_expert, item_lo, item_hi, n_items, xs, wg, wu, wd)


def _combine_kernel(pos_ref, pos_next_ref, x_ref, route_ref, gn_ref, os_ref, *rest, emit_x, split):
    outs, (buf, sem) = rest[:-2], rest[-2:]
    i = pl.program_id(0)
    half = i % 2

    def copy(p_ref, hf, r, s):
        return _row_copy(os_ref, p_ref[0, 0, s * MOVE_TILE + r], buf.at[hf, s], r, sem.at[hf])

    def start_all(p_ref, hf):
        def start(r, c):
            for s in range(2):
                copy(p_ref, hf, r, s).start(priority=s)
            return c

        lax.fori_loop(0, MOVE_TILE, start, 0, unroll=8)

    @pl.when(i == 0)
    def _():
        start_all(pos_ref, 0)

    @pl.when(i + 1 < pl.num_programs(0))
    def _():
        start_all(pos_next_ref, 1 - half)

    def wait(r, c):
        for s in range(2):
            copy(pos_ref, half, r, s).wait()
        return c

    lax.fori_loop(0, MOVE_TILE, wait, 0, unroll=8)
    route = route_ref[...]
    x2 = x_ref[...] + route[:, 4:5] * buf[half, 0] + route[:, 5:6] * buf[half, 1]
    if emit_x:
        outs[0][...] = x2
        outs = outs[1:]
    xn = _rms(x2, gn_ref[...])
    if split:
        main_ref, tail_ref = outs
        is_main = pl.program_id(0) < N_MAIN // MOVE_TILE

        @pl.when(is_main)
        def _():
            main_ref[...] = xn.astype(main_ref.dtype)

        @pl.when(jnp.logical_not(is_main))
        def _():
            tail_ref[...] = xn.astype(tail_ref.dtype)
    else:
        outs[0][...] = xn.astype(outs[0].dtype)


def _combine_call(x1, route, pos, os_rows, gn, xn_dtype, *, emit_x, split):
    n = x1.shape[0]
    row = lambda i: (i, 0)
    blk = lambda m: pl.BlockSpec((MOVE_TILE, D_MODEL), m)
    nm = N_MAIN // MOVE_TILE
    out_specs, out_shape = [], []
    if emit_x:
        out_specs.append(blk(row))
        out_shape.append(jax.ShapeDtypeStruct((n, D_MODEL), f32))
    if split:
        out_specs += [blk(lambda i: (jnp.minimum(i, nm - 1), 0)), blk(lambda i: (jnp.maximum(i - nm, 0), 0))]
        out_shape += [jax.ShapeDtypeStruct((N_MAIN, D_MODEL), xn_dtype),
                      jax.ShapeDtypeStruct((n - N_MAIN, D_MODEL), xn_dtype)]
    else:
        out_specs.append(blk(row))
        out_shape.append(jax.ShapeDtypeStruct((n, D_MODEL), xn_dtype))
    nt = n // MOVE_TILE
    pos_spec = lambda m: pl.BlockSpec((1, 1, 2 * MOVE_TILE), m, memory_space=pltpu.SMEM)
    return pl.pallas_call(
        functools.partial(_combine_kernel, emit_x=emit_x, split=split),
        grid=(nt,),
        in_specs=[pos_spec(lambda i: (i, 0, 0)), pos_spec(lambda i: (jnp.minimum(i + 1, nt - 1), 0, 0)),
                  blk(row), pl.BlockSpec((MOVE_TILE, LANES), row),
                  pl.BlockSpec((1, D_MODEL), lambda i: (0, 0)), pl.BlockSpec(memory_space=pl.ANY)],
        out_specs=out_specs,
        out_shape=out_shape,
        scratch_shapes=[pltpu.VMEM((2, 2, MOVE_TILE, D_MODEL), f32), pltpu.SemaphoreType.DMA((2,))],
        compiler_params=_cp(("arbitrary",)),
        name="moe_combine",
    )(pos, pos, x1, route, gn, os_rows)


S5_GROUP = 16
S5_GROUPS = D_MODEL // S5_GROUP
S5_STATE = 64
S5_CB = 128
S5_NB = D_MODEL // S5_CB
S5_GPB = S5_CB // S5_GROUP
S5_SB = S5_GPB * S5_STATE
S5_BC = S5_STATE * S5_GROUP
S5_CPS = 2


def _s5_disc_kernel(lr_ref, li_ref, ldt_ref, bre_ref, bim_ref,
                    abr_ref, abi_ref, ab2r_ref, ab2i_ref, bbr_ref, bbi_ref, abbr_ref, abbi_ref):
    lr, li = lr_ref[...], li_ref[...]
    dt = jnp.exp(ldt_ref[...])
    mag = jnp.exp(lr * dt)
    ang = li * dt
    ab_re, ab_im = mag * jnp.cos(ang), mag * jnp.sin(ang)
    nr, ni = ab_re - 1.0, ab_im
    den = lr * lr + li * li
    f_re = (nr * lr + ni * li) / den
    f_im = (ni * lr - nr * li) / den
    abr_ref[...] = ab_re
    abi_ref[...] = ab_im
    ab2r_ref[...] = ab_re * ab_re - ab_im * ab_im
    ab2i_ref[...] = 2.0 * (ab_re * ab_im)
    pi = lax.broadcasted_iota(jnp.int32, (S5_STATE, S5_BC), 0)
    ci = lax.broadcasted_iota(jnp.int32, (S5_STATE, S5_BC), 1)
    rep = jnp.where(ci // S5_GROUP == pi, 1.0, 0.0).astype(bf16)

    def expand(v):
        hi, mid, lo = _split3(v)
        return _dot(hi, rep) + _dot(mid, rep) + _dot(lo, rep)

    fr, fi, ar, ai = expand(f_re), expand(f_im), expand(ab_re), expand(ab_im)
    br, bi = bre_ref[...], bim_ref[...]
    bb_re = fr * br - fi * bi
    bb_im = fr * bi + fi * br
    bbr_ref[...] = bb_re
    bbi_ref[...] = bb_im
    abbr_ref[...] = ar * bb_re - ai * bb_im
    abbi_ref[...] = ar * bb_im + ai * bb_re


def _s5_weights(lam_re, lam_im, log_dt, b_re, b_im, c_re, c_im):
    st = jax.ShapeDtypeStruct((S5_GROUPS, S5_STATE), f32)
    bc = jax.ShapeDtypeStruct((S5_GROUPS, S5_BC), f32)
    ab_re, ab_im, ab2_re, ab2_im, bb_re, bb_im, abb_re, abb_im = pl.pallas_call(
        _s5_disc_kernel, out_shape=[st, st, st, st, bc, bc, bc, bc], name="s5_discretize",
    )(lam_re, lam_im, log_dt[:, None], b_re.reshape(S5_GROUPS, S5_BC), b_im.reshape(S5_GROUPS, S5_BC))
    eye = jnp.eye(S5_GPB, dtype=f32)

    def in_blocks(m):
        m = m.reshape(S5_NB, S5_GPB, S5_STATE, S5_GROUP)
        return jnp.einsum("jgpc,gh->jgchp", m, eye).reshape(S5_NB, S5_CB, S5_SB)

    def out_blocks(m):
        m = m.reshape(S5_NB, S5_GPB, S5_GROUP, S5_STATE)
        return jnp.einsum("jgcp,gh->jgphc", m, eye).reshape(S5_NB, S5_SB, S5_CB)

    bb2 = jnp.concatenate([
        jnp.concatenate([in_blocks(bb_re), in_blocks(bb_im)], axis=2),
        jnp.concatenate([in_blocks(abb_re), in_blocks(abb_im)], axis=2)], axis=1).astype(bf16)
    cc = jnp.concatenate([out_blocks(c_re), -out_blocks(c_im)], axis=1).astype(bf16)
    rows = [v.reshape(S5_NB, 1, S5_SB) for v in (ab_re, ab_im, ab2_re, ab2_im)]
    abv = jnp.concatenate(rows + [jnp.zeros((S5_NB, SUB - len(rows), S5_SB), f32)], axis=1)
    return bb2, cc, abv


def _s5_seq_kernel(x0_ref, x1_ref, x2_ref, x3_ref, halo0_ref, h0_ref, bb2_ref, cc_ref, abv_ref, y_ref, hout_ref,
                   xf_scr, xp_scr, bu_scr, hs_scr, yp_scr, yn_scr, h_scr, halo_scr):
    tb = pl.program_id(1)
    TL = x0_ref.shape[0]
    KB = TL // 2
    RB = BATCH * TL

    @pl.when(tb == 0)
    def _():
        h_scr[...] = h0_ref[...]
        halo_scr[...] = halo0_ref[...].astype(f32)

    chans = [slice(c * S5_CB, (c + 1) * S5_CB) for c in range(S5_CPS)]
    for c, ch in enumerate(chans):
        for b, xb_ref in enumerate((x0_ref, x1_ref, x2_ref, x3_ref)):
            xf_scr[c, b * TL:(b + 1) * TL, :] = xb_ref[:, ch].astype(f32)
        for b in range(BATCH):
            for p in range(2):
                xp_scr[c, pl.ds(2 * b + p, KB, stride=SUB), :] = xf_scr[c, pl.ds(b * TL + p, KB, stride=2), :]
    x = jnp.concatenate([xp_scr[c] for c in range(S5_CPS)], axis=1)
    xc = jnp.concatenate([halo_scr[...], x], axis=0)
    odd = (lax.broadcasted_iota(jnp.int32, (RB + SUB, 1), 0) & 1) == 1
    xprev = jnp.where(odd, pltpu.roll(xc, 1, axis=0), pltpu.roll(xc, SUB - 1, axis=0))[SUB:]
    halo_scr[...] = x[RB - SUB:]
    for c, ch in enumerate(chans):
        lhs = jnp.concatenate([x[:, ch], xprev[:, ch]], axis=1).astype(bf16)
        bu_scr[c] = _dot(lhs, bb2_ref[c])
    a2 = [(abv_ref[c, 2:3, :], abv_ref[c, 3:4, :]) for c in range(S5_CPS)]

    def step(k, carry):
        r0 = pl.multiple_of(k * SUB, SUB)
        out = []
        for c in range(S5_CPS):
            hr, hi = carry[2 * c], carry[2 * c + 1]
            a2r, a2i = a2[c]
            bu = bu_scr[c, pl.ds(r0, SUB), :]
            nr = a2r * hr - a2i * hi + bu[:, :S5_SB]
            ni = a2r * hi + a2i * hr + bu[:, S5_SB:]
            hs_scr[c, pl.ds(r0, SUB), :S5_SB] = nr
            hs_scr[c, pl.ds(r0, SUB), S5_SB:] = ni
            out += [nr, ni]
        return tuple(out)

    init = []
    for c in range(S5_CPS):
        init += [h_scr[c, :, :S5_SB], h_scr[c, :, S5_SB:]]
    fin = lax.fori_loop(0, RB // SUB, step, tuple(init), unroll=4)
    for c, ch in enumerate(chans):
        h_scr[c, :, :S5_SB] = fin[2 * c]
        h_scr[c, :, S5_SB:] = fin[2 * c + 1]
        yp_scr[c] = _dot(hs_scr[c].astype(bf16), cc_ref[c])
        for b in range(BATCH):
            for p in range(2):
                yn_scr[c, pl.ds(b * TL + p, KB, stride=2), :] = yp_scr[c, pl.ds(2 * b + p, KB, stride=SUB), :]
            y_ref[b, :, ch] = yn_scr[c, b * TL:(b + 1) * TL, :]

    @pl.when(tb == pl.num_programs(1) - 1)
    def _():
        hout_ref[...] = h_scr[...]


def _s5_seq_call(x, halo0, h0, bb2, cc, abv, *, row0, seq_len, tl):
    wsel = lambda j, t: (j, 0, 0)
    rb = BATCH * tl
    cw = S5_CPS * S5_CB
    xspec = lambda b: pl.BlockSpec((tl, cw), lambda j, t: ((row0 + b * seq_len) // tl + t, j))
    return pl.pallas_call(
        _s5_seq_kernel,
        grid=(S5_NB // S5_CPS, seq_len // tl),
        in_specs=[xspec(b) for b in range(BATCH)]
        + [pl.BlockSpec((SUB, cw), lambda j, t: (0, j)),
           pl.BlockSpec((S5_CPS, SUB, 2 * S5_SB), wsel),
           pl.BlockSpec((S5_CPS, 2 * S5_CB, 2 * S5_SB), wsel),
           pl.BlockSpec((S5_CPS, 2 * S5_SB, S5_CB), wsel),
           pl.BlockSpec((S5_CPS, SUB, S5_SB), wsel)],
        out_specs=[pl.BlockSpec((BATCH, tl, cw), lambda j, t: (0, t, j)),
                   pl.BlockSpec((S5_CPS, SUB, 2 * S5_SB), wsel)],
        out_shape=[jax.ShapeDtypeStruct((BATCH, seq_len, D_MODEL), f32),
                   jax.ShapeDtypeStruct((S5_NB, SUB, 2 * S5_SB), f32)],
        scratch_shapes=[pltpu.VMEM((S5_CPS, rb, S5_CB), f32), pltpu.VMEM((S5_CPS, rb, S5_CB), f32),
                        pltpu.VMEM((S5_CPS, rb, 2 * S5_SB), f32), pltpu.VMEM((S5_CPS, rb, 2 * S5_SB), f32),
                        pltpu.VMEM((S5_CPS, rb, S5_CB), f32), pltpu.VMEM((S5_CPS, rb, S5_CB), f32),
                        pltpu.VMEM((S5_CPS, SUB, 2 * S5_SB), f32), pltpu.VMEM((SUB, cw), f32)],
        compiler_params=_cp(("arbitrary", "arbitrary")),
        name=f"s5_seq_{seq_len}",
    )(x, x, x, x, halo0, h0, bb2, cc, abv)


def _s5_sample_kernel(x_ref, hre_ref, him_ref, bb2_ref, cc_ref, abv_ref, y_ref, ore_ref, oim_ref, hs_scr):
    nb = hre_ref.shape[0]
    bu = _dot(x_ref[...], bb2_ref[0, :S5_CB, :])
    ar = abv_ref[0, 0:1, :]
    ai = abv_ref[0, 1:2, :]
    hr, hi = hre_ref[...], him_ref[...]
    for t in range(DEC_SEQ):
        rows = slice(t * nb, (t + 1) * nb)
        hr, hi = (ar * hr - ai * hi + bu[rows, :S5_SB], ar * hi + ai * hr + bu[rows, S5_SB:])
        hs_scr[rows, :S5_SB] = hr
        hs_scr[rows, S5_SB:] = hi
    y_ref[...] = _dot(hs_scr[...].astype(bf16), cc_ref[0])
    ore_ref[...] = hr
    oim_ref[...] = hi


def _s5_sample_call(xt, h_re, h_im, bb2, cc, abv):
    n = xt.shape[0]
    nb = h_re.shape[0]
    wsel = lambda j: (j, 0, 0)
    st = pl.BlockSpec((nb, S5_SB), lambda j: (0, j))
    return pl.pallas_call(
        _s5_sample_kernel,
        grid=(S5_NB,),
        in_specs=[pl.BlockSpec((n, S5_CB), lambda j: (0, j)), st, st,
                  pl.BlockSpec((1, 2 * S5_CB, 2 * S5_SB), wsel),
                  pl.BlockSpec((1, 2 * S5_SB, S5_CB), wsel),
                  pl.BlockSpec((1, SUB, S5_SB), wsel)],
        out_specs=[pl.BlockSpec((n, S5_CB), lambda j: (0, j)), st, st],
        out_shape=[jax.ShapeDtypeStruct((n, D_MODEL), f32),
                   jax.ShapeDtypeStruct(h_re.shape, f32), jax.ShapeDtypeStruct(h_im.shape, f32)],
        scratch_shapes=[pltpu.VMEM((n, 2 * S5_SB), f32)],
        compiler_params=_cp(("arbitrary",)),
        name="s5_sample",
    )(xt, h_re, h_im, bb2, cc, abv)


GLA_CHUNK = 256
S5_TL = 256


def _moe_layer(layer, x1, xnf, route, plan, cnt, wg, wu, wd, gn, xn_dtype, *, emit_x, split):
    pos, item_tile, item_expert, item_lo, item_hi, n_items = _moe_plan(plan, cnt, x1.shape[0])
    xs = _dispatch_call(xnf, pos)
    os_rows = _experts_call(xs, item_tile, item_expert + layer * MOE_EXPERTS, item_lo, item_hi, n_items, wg, wu, wd)
    return _combine_call(x1, route, pos, os_rows, gn, xn_dtype, emit_x=emit_x, split=split)


def kernel(x_prompt, x_sample, state_gla, state_s5_re, state_s5_im, meta_tokens, norm_mix_g, norm_ffn_g, norm_final_g, gla_w_in, gla_w_a2, gla_b_a, gla_g_o, gla_w_o, s5_lambda_re, s5_lambda_im, s5_log_dt, s5_b_re, s5_b_im, s5_c_re, s5_c_im, s5_d, s5_w_glu, s5_b_glu, moe_w_rg, moe_b_rg, moe_w_re, moe_b_re, moe_w_gate, moe_w_up, moe_w_down):
    row = lambda v: v.reshape(1, -1)
    x_main = x_prompt.reshape(N_MAIN, D_MODEL)
    x_tail = jnp.concatenate([
        jnp.tile(meta_tokens.astype(x_prompt.dtype), (BATCH, 1)),
        x_sample.reshape(N_SAMPLE, D_MODEL),
        jnp.zeros((N_ROWS - N_REAL, D_MODEL), x_prompt.dtype)], axis=0)
    wg = moe_w_gate.reshape(-1, D_MODEL, D_EXPERT)
    wu = moe_w_up.reshape(-1, D_MODEL, D_EXPERT)
    wd = moe_w_down.reshape(-1, D_EXPERT, D_MODEL)

    w_in = gla_w_in.reshape(D_MODEL, -1)
    wa1 = jnp.pad(w_in[:, GLA_QKVR:], ((0, 0), (0, LANES - GLA_RANK)))
    wa2 = jnp.pad(gla_w_a2.reshape(GLA_RANK, GLA_KEY), ((0, LANES - GLA_RANK), (0, 0)))
    xn, glog = _norm_gate_call(x_main, x_tail, row(norm_mix_g[0]), wa1, wa2, row(gla_b_a))
    proj = _proj_call(xn, w_in, GLA_QKVR)
    go = row(gla_g_o)
    og = jnp.zeros((N_ROWS, GLA_VAL), bf16)
    s_zero = jnp.zeros((BATCH, GLA_HEADS, GLA_DK, GLA_DV), f32)
    og, s_meta = _gla_seq_call(proj, glog, go, s_zero, og, row0=ROW_META, C=N_META, n_chunks=1, d=SUB)
    og, s_prompt = _gla_seq_call(proj, glog, go, s_meta, og, row0=0, C=GLA_CHUNK, n_chunks=SEQ // GLA_CHUNK, d=SUB)
    og, s_sample = _gla_sample_call(proj, glog, go, state_gla.reshape(DEC_BATCH, GLA_HEADS, GLA_DK, GLA_DV), og,
                                    row0=ROW_SAMPLE)
    wr, br = _router_weights(moe_w_rg[0], moe_b_rg[0], moe_w_re[0], moe_b_re[0])
    tile_row = lambda i: (i, 0)
    x1, xnf, route, plan, cnt = _mix_out_call(
        _gla_out_kernel, "gla_out", N_ROWS, [og, x_main, x_tail], [tile_row, *_main_tail_maps(MIX_TILE)],
        [gla_w_o.reshape(GLA_VAL, D_MODEL).astype(bf16), row(norm_ffn_g[0]), wr, br])
    x2, xn2 = _moe_layer(0, x1, xnf, route, plan, cnt, wg, wu, wd, row(norm_mix_g[1]), bf16, emit_x=True, split=False)

    bb2, cc, abv = _s5_weights(s5_lambda_re[0], s5_lambda_im[0], s5_log_dt[0], s5_b_re[0], s5_b_im[0],
                               s5_c_re[0], s5_c_im[0])
    y_meta, h_meta = _s5_seq_call(xn2, jnp.zeros((SUB, D_MODEL), bf16), jnp.zeros((S5_NB, SUB, 2 * S5_SB), f32),
                                  bb2, cc, abv, row0=ROW_META, seq_len=N_META, tl=N_META)
    halo = xn2[ROW_META:ROW_SAMPLE].reshape(BATCH, N_META, D_MODEL)[:, N_META - 2:].reshape(SUB, D_MODEL)
    y_main, h_main = _s5_seq_call(xn2, halo, h_meta, bb2, cc, abv, row0=0, seq_len=SEQ, tl=S5_TL)
    xt_sample = xn2[ROW_SAMPLE:N_REAL].reshape(DEC_BATCH, DEC_SEQ, D_MODEL).transpose(1, 0, 2).reshape(N_SAMPLE, D_MODEL)
    y_samp, s5r_s, s5i_s = _s5_sample_call(
        xt_sample, state_s5_re.reshape(DEC_BATCH, S5_GROUPS * S5_STATE),
        state_s5_im.reshape(DEC_BATCH, S5_GROUPS * S5_STATE), bb2, cc, abv)
    ys_tail = jnp.concatenate([
        y_meta.reshape(N_METAROWS, D_MODEL),
        y_samp.reshape(DEC_SEQ, DEC_BATCH, D_MODEL).transpose(1, 0, 2).reshape(N_SAMPLE, D_MODEL),
        jnp.zeros((N_ROWS - N_REAL, D_MODEL), f32)], axis=0)
    wr, br = _router_weights(moe_w_rg[1], moe_b_rg[1], moe_w_re[1], moe_b_re[1])
    x3, xnf, route, plan, cnt = _mix_out_call(
        _s5_out_kernel, "s5_out", N_ROWS, [y_main.reshape(N_MAIN, D_MODEL), ys_tail, x2],
        [*_main_tail_maps(MIX_TILE), tile_row],
        [row(norm_mix_g[1]), row(s5_d), s5_w_glu.reshape(D_MODEL, D_MODEL).astype(bf16), row(s5_b_glu),
         row(norm_ffn_g[1]), wr, br])
    y_main_out, y_tail_out = _moe_layer(1, x3, xnf, route, plan, cnt, wg, wu, wd, row(norm_final_g), f32,
                                        emit_x=False, split=True)

    y_prompt = y_main_out.reshape(BATCH, SEQ, D_MODEL)
    y_sample = y_tail_out[N_METAROWS:N_METAROWS + N_SAMPLE].reshape(DEC_BATCH, DEC_SEQ, D_MODEL)
    hfin = h_main.reshape(S5_NB, BATCH, 2, 2, S5_GPB, S5_STATE)[:, :, 1]
    s5r_p = hfin[:, :, 0].transpose(1, 0, 2, 3).reshape(1, BATCH, S5_GROUPS, S5_STATE)
    s5i_p = hfin[:, :, 1].transpose(1, 0, 2, 3).reshape(1, BATCH, S5_GROUPS, S5_STATE)
    return (y_prompt, y_sample, s_prompt[None], s5r_p, s5i_p, s_sample[None],
            s5r_s.reshape(1, DEC_BATCH, S5_GROUPS, S5_STATE), s5i_s.reshape(1, DEC_BATCH, S5_GROUPS, S5_STATE))
```

```python
import functools

import jax
import jax.numpy as jnp
from jax import lax
from jax.experimental import pallas as pl
from jax.experimental.pallas import tpu as pltpu

f32 = jnp.float32
bf16 = jnp.bfloat16

D_MODEL = 2048
BATCH = 4
SEQ = 2048
DEC_BATCH = 128
DEC_SEQ = 4
N_META = 16
EPS = 1e-6
GLA_HEADS = 4
GLA_DK = 256
GLA_DV = 512
GLA_KEY = GLA_HEADS * GLA_DK
GLA_VAL = GLA_HEADS * GLA_DV
GLA_RANK = 16
GLA_TAU = 16.0
GLA_QKVR = 2 * GLA_KEY + 2 * GLA_VAL

N_MAIN = BATCH * SEQ
N_METAROWS = BATCH * N_META
N_SAMPLE = DEC_BATCH * DEC_SEQ
ROW_META = N_MAIN
ROW_SAMPLE = N_MAIN + N_METAROWS
N_REAL = ROW_SAMPLE + N_SAMPLE
ROW_TILE = 512
N_ROWS = -(-N_REAL // ROW_TILE) * ROW_TILE

VMEM_LIMIT = 56 * 1024 * 1024


def _cp(sem, vmem=VMEM_LIMIT):
    return pltpu.CompilerParams(dimension_semantics=sem, vmem_limit_bytes=vmem)


def _dot(a, b):
    return jnp.dot(a, b, preferred_element_type=f32)


def _dot_nt(a, b):
    return lax.dot_general(a, b, (((1,), (1,)), ((), ())), preferred_element_type=f32)


def _dot_tn(a, b):
    return lax.dot_general(a, b, (((0,), (0,)), ((), ())), preferred_element_type=f32)


def _sigmoid(x):
    return 1.0 / (1.0 + jnp.exp(-x))


def _split3(x):
    hi = x.astype(bf16)
    r1 = x - hi.astype(f32)
    mid = r1.astype(bf16)
    lo = (r1 - mid.astype(f32)).astype(bf16)
    return hi, mid, lo


def _cumsum_rows(g, C):
    if C <= 16:
        row = lax.broadcasted_iota(jnp.int32, (C, 1), 0)
        b = jnp.zeros_like(g)
        for s in range(C):
            b = b + jnp.where(row >= s, g[s:s + 1, :], 0.0)
        return b
    row = lax.broadcasted_iota(jnp.int32, (C, C), 0)
    col = lax.broadcasted_iota(jnp.int32, (C, C), 1)
    tri = jnp.where(row >= col, 1.0, 0.0).astype(bf16)
    hi, mid, lo = _split3(g)
    return _dot(tri, hi) + _dot(tri, mid) + _dot(tri, lo)


PAIR_LEVEL = 1000


def _gla_pair_code(C, d, tree=True):
    ti = lax.broadcasted_iota(jnp.int32, (C, C), 0)
    si = lax.broadcasted_iota(jnp.int32, (C, C), 1)
    code = jnp.where((ti // d == si // d) & (si <= ti), 1 + ti - si, 0)
    h = d
    while tree and h < C:
        tb = ti // h
        code = jnp.where(((tb % 2) == 1) & ((si // h) == tb - 1), PAIR_LEVEL + h, code)
        h *= 2
    return code


def _gla_scores(q, k, b, code, C, d, tree=True):
    row = lax.broadcasted_iota(jnp.int32, (C, 1), 0)
    scores = jnp.zeros((C, C), f32)
    for dl in range(d):
        ks = k if dl == 0 else pltpu.roll(k, dl, axis=0)
        bs = b if dl == 0 else pltpu.roll(b, dl, axis=0)
        term = q * ks * jnp.exp(jnp.minimum(b - bs, 0.0))
        colv = jnp.sum(term, axis=1, keepdims=True)
        scores = jnp.where(code == 1 + dl, colv, scores)
    z = b
    s = 1
    while tree and 2 * s < C:
        z = jnp.where((row & s) != 0, pltpu.roll(z, s, axis=0), z)
        s *= 2
        h = s
        if h < d:
            continue
        bnext = pltpu.roll(z, C - h, axis=0)
        qh = (q * jnp.exp(jnp.minimum(b - z, 0.0))).astype(bf16)
        kh = (k * jnp.exp(jnp.minimum(bnext - b, 0.0))).astype(bf16)
        scores = jnp.where(code == PAIR_LEVEL + h, _dot_nt(qh, kh), scores)
    return scores


def _gla_chunk(q, k, v, g, S, code, C, d, tree=True):
    b = _cumsum_rows(g, C)
    o = _dot((q * jnp.exp(b)).astype(bf16), S.astype(bf16))
    scores = _gla_scores(q, k, b, code, C, d, tree)
    o = o + _dot(scores.astype(bf16), v)
    b_last = b[C - 1:C, :]
    kd = (k * jnp.exp(b_last - b)).astype(bf16)
    if C == GLA_DK:
        eye = code == 1
    else:
        eye = (lax.broadcasted_iota(jnp.int32, (GLA_DK, GLA_DK), 0)
               == lax.broadcasted_iota(jnp.int32, (GLA_DK, GLA_DK), 1))
    dec_col = jnp.sum(jnp.where(eye, jnp.exp(b_last), 0.0), axis=1, keepdims=True)
    S_new = dec_col * S + _dot_tn(kd, v)
    return o, S_new


def _gla_head_epilogue(o, r, go):
    ms = jnp.mean(o * o, axis=1, keepdims=True)
    on = o * lax.rsqrt(ms + EPS) * go
    return on * (r * _sigmoid(r))


def _gla_seq_kernel(q_ref, k_ref, v_ref, r_ref, g_ref, go_ref, s0_ref, _og_in, og_ref, sout_ref, s_scr, *, C, d):
    c = pl.program_id(1)

    @pl.when(c == 0)
    def _():
        s_scr[...] = s0_ref[0]

    code = _gla_pair_code(C, d)

    def head(h, carry):
        ck = pl.ds(pl.multiple_of(h * GLA_DK, GLA_DK), GLA_DK)
        cv = pl.ds(pl.multiple_of(h * GLA_DV, GLA_DV), GLA_DV)
        q = q_ref[:, ck].astype(f32) * (GLA_DK ** -0.5)
        k = k_ref[:, ck].astype(f32)
        o, S_new = _gla_chunk(q, k, v_ref[:, cv], g_ref[:, ck], s_scr[h], code, C, d)
        s_scr[h] = S_new
        og_ref[:, cv] = _gla_head_epilogue(o, r_ref[:, cv].astype(f32), go_ref[:, cv]).astype(og_ref.dtype)
        return carry

    lax.fori_loop(0, GLA_HEADS, head, 0)

    @pl.when(c == pl.num_programs(1) - 1)
    def _():
        sout_ref[0] = s_scr[...]


def _gla_seq_call(proj, glog, go, s0, og_buf, *, row0, C, n_chunks, d):
    blk0 = row0 // C
    rows = lambda b, c: blk0 + b * n_chunks + c
    kern = functools.partial(_gla_seq_kernel, C=C, d=d)
    return pl.pallas_call(
        kern,
        grid=(BATCH, n_chunks),
        in_specs=[
            pl.BlockSpec((C, GLA_KEY), lambda b, c: (rows(b, c), 0)),
            pl.BlockSpec((C, GLA_KEY), lambda b, c: (rows(b, c), 1)),
            pl.BlockSpec((C, GLA_VAL), lambda b, c: (rows(b, c), 1)),
            pl.BlockSpec((C, GLA_VAL), lambda b, c: (rows(b, c), 2)),
            pl.BlockSpec((C, GLA_KEY), lambda b, c: (rows(b, c), 0)),
            pl.BlockSpec((1, GLA_VAL), lambda b, c: (0, 0)),
            pl.BlockSpec((1, GLA_HEADS, GLA_DK, GLA_DV), lambda b, c: (b, 0, 0, 0)),
            pl.BlockSpec(memory_space=pl.ANY),
        ],
        out_specs=[
            pl.BlockSpec((C, GLA_VAL), lambda b, c: (rows(b, c), 0)),
            pl.BlockSpec((1, GLA_HEADS, GLA_DK, GLA_DV), lambda b, c: (b, 0, 0, 0)),
        ],
        out_shape=[
            jax.ShapeDtypeStruct(og_buf.shape, og_buf.dtype),
            jax.ShapeDtypeStruct((BATCH, GLA_HEADS, GLA_DK, GLA_DV), f32),
        ],
        scratch_shapes=[pltpu.VMEM((GLA_HEADS, GLA_DK, GLA_DV), f32)],
        input_output_aliases={7: 0},
        compiler_params=_cp(("arbitrary", "arbitrary")),
        name=f"gla_seq_c{C}",
    )(proj, proj, proj, proj, glog, go, s0, og_buf)


SAMPLE_BB = 4
SAMPLE_C = SAMPLE_BB * DEC_SEQ


def _gla_sample_kernel(q_ref, k_ref, v_ref, r_ref, g_ref, go_ref, s0_ref, _og_in, og_ref, sout_ref):
    row = lax.broadcasted_iota(jnp.int32, (SAMPLE_C, 1), 0)

    code = _gla_pair_code(SAMPLE_C, DEC_SEQ, tree=False)

    def take(x, bb):
        sh = (SAMPLE_C - DEC_SEQ * bb) % SAMPLE_C
        return jnp.where(row < DEC_SEQ, pltpu.roll(x, sh, axis=0) if sh else x, 0.0)

    def head(h, carry):
        ck = pl.ds(pl.multiple_of(h * GLA_DK, GLA_DK), GLA_DK)
        cv = pl.ds(pl.multiple_of(h * GLA_DV, GLA_DV), GLA_DV)
        q_all = q_ref[:, ck].astype(f32) * (GLA_DK ** -0.5)
        k_all = k_ref[:, ck].astype(f32)
        v_all = v_ref[:, cv].astype(f32)
        r_all = r_ref[:, cv].astype(f32)
        g_all = g_ref[:, ck]
        go = go_ref[:, cv]
        acc = jnp.zeros((SAMPLE_C, GLA_DV), f32)
        for bb in range(SAMPLE_BB):
            o, S_new = _gla_chunk(take(q_all, bb), take(k_all, bb), take(v_all, bb).astype(bf16),
                                  take(g_all, bb), s0_ref[bb, h], code, SAMPLE_C, DEC_SEQ, tree=False)
            sout_ref[bb, h] = S_new
            y = _gla_head_epilogue(o, take(r_all, bb), go)
            acc = jnp.where(row // DEC_SEQ == bb, pltpu.roll(y, DEC_SEQ * bb, axis=0) if bb else y, acc)
        og_ref[:, cv] = acc.astype(og_ref.dtype)
        return carry

    lax.fori_loop(0, GLA_HEADS, head, 0)


def _gla_sample_call(proj, glog, go, s0, og_buf, *, row0):
    n_seq = s0.shape[0]
    blk0 = row0 // SAMPLE_C
    st_spec = pl.BlockSpec((SAMPLE_BB, GLA_HEADS, GLA_DK, GLA_DV), lambda i: (i, 0, 0, 0))
    return pl.pallas_call(
        _gla_sample_kernel,
        grid=(n_seq // SAMPLE_BB,),
        in_specs=[
            pl.BlockSpec((SAMPLE_C, GLA_KEY), lambda i: (blk0 + i, 0)),
            pl.BlockSpec((SAMPLE_C, GLA_KEY), lambda i: (blk0 + i, 1)),
            pl.BlockSpec((SAMPLE_C, GLA_VAL), lambda i: (blk0 + i, 1)),
            pl.BlockSpec((SAMPLE_C, GLA_VAL), lambda i: (blk0 + i, 2)),
            pl.BlockSpec((SAMPLE_C, GLA_KEY), lambda i: (blk0 + i, 0)),
            pl.BlockSpec((1, GLA_VAL), lambda i: (0, 0)),
            st_spec,
            pl.BlockSpec(memory_space=pl.ANY),
        ],
        out_specs=[pl.BlockSpec((SAMPLE_C, GLA_VAL), lambda i: (blk0 + i, 0)), st_spec],
        out_shape=[jax.ShapeDtypeStruct(og_buf.shape, og_buf.dtype), jax.ShapeDtypeStruct(s0.shape, f32)],
        input_output_aliases={7: 0},
        compiler_params=_cp(("arbitrary",)),
        name="gla_sample",
    )(proj, proj, proj, proj, glog, go, s0, og_buf)


LANES = 128
SUB = 8
MOE_GROUPS = 4
MOE_EPG = 8
MOE_EXPERTS = MOE_GROUPS * MOE_EPG
D_EXPERT = 256
ROUTE_E0 = MOE_GROUPS
MIX_TILE = 256
PROJ_NT = 1024


def _rms(x, g):
    r = lax.rsqrt(jnp.mean(x * x, axis=-1, keepdims=True) + EPS)
    return (x * r) * g


def _log_sigmoid(z):
    return jnp.minimum(z, 0.0) - jnp.log1p(jnp.exp(-jnp.abs(z)))


def _main_or_tail(tile, main_ref, tail_ref):
    return jnp.where(pl.program_id(0) < N_MAIN // tile, main_ref[...], tail_ref[...])


def _main_tail_maps(tile):
    nm = N_MAIN // tile
    return (lambda i: (jnp.minimum(i, nm - 1), 0)), (lambda i: (jnp.maximum(i - nm, 0), 0))


def _norm_gate_kernel(xm_ref, xt_ref, gn_ref, wa1_ref, wa2_ref, ba_ref, xn_ref, gl_ref):
    xnb = _rms(_main_or_tail(ROW_TILE, xm_ref, xt_ref), gn_ref[...]).astype(bf16)
    xn_ref[...] = xnb
    a = _dot(xnb, wa1_ref[...].astype(bf16))
    z = _dot(a.astype(bf16), wa2_ref[...].astype(bf16)) + ba_ref[...]
    gl_ref[...] = _log_sigmoid(z) * (1.0 / GLA_TAU)


def _norm_gate_call(x_main, x_tail, gn, wa1, wa2, ba):
    n = x_main.shape[0] + x_tail.shape[0]
    row = lambda i: (i, 0)
    fix = lambda i: (0, 0)
    main_map, tail_map = _main_tail_maps(ROW_TILE)
    return pl.pallas_call(
        _norm_gate_kernel,
        grid=(n // ROW_TILE,),
        in_specs=[pl.BlockSpec((ROW_TILE, D_MODEL), main_map), pl.BlockSpec((ROW_TILE, D_MODEL), tail_map),
                  pl.BlockSpec((1, D_MODEL), fix),
                  pl.BlockSpec((D_MODEL, LANES), fix), pl.BlockSpec((LANES, GLA_KEY), fix),
                  pl.BlockSpec((1, GLA_KEY), fix)],
        out_specs=[pl.BlockSpec((ROW_TILE, D_MODEL), row), pl.BlockSpec((ROW_TILE, GLA_KEY), row)],
        out_shape=[jax.ShapeDtypeStruct((n, D_MODEL), bf16), jax.ShapeDtypeStruct((n, GLA_KEY), f32)],
        compiler_params=_cp(("arbitrary",)),
        name="norm_gate",
    )(x_main, x_tail, gn, wa1, wa2, ba)


def _proj_kernel(xn_ref, w_ref, o_ref, wb_scr):
    @pl.when(pl.program_id(1) == 0)
    def _():
        wb_scr[...] = w_ref[0].astype(bf16)

    o_ref[...] = _dot_nt(xn_ref[...], wb_scr[...]).astype(o_ref.dtype)


def _proj_call(xn, wt, n_cols):
    n = xn.shape[0]
    return pl.pallas_call(
        _proj_kernel,
        grid=(n_cols // PROJ_NT, n // ROW_TILE),
        in_specs=[pl.BlockSpec((ROW_TILE, D_MODEL), lambda j, i: (i, 0)),
                  pl.BlockSpec((1, PROJ_NT, D_MODEL), lambda j, i: (0, j, 0))],
        out_specs=pl.BlockSpec((ROW_TILE, PROJ_NT), lambda j, i: (i, j)),
        out_shape=jax.ShapeDtypeStruct((n, n_cols), bf16),
        scratch_shapes=[pltpu.VMEM((PROJ_NT, D_MODEL), bf16)],
        compiler_params=_cp(("arbitrary", "arbitrary")),
        name="gla_proj",
    )(xn, wt)


def _route(xn, wr, br, cnt_ref):
    R = xn.shape[0]
    xh = xn.astype(bf16)
    xl = (xn - xh.astype(f32)).astype(bf16)
    wh = wr.astype(bf16)
    wl = (wr - wh.astype(f32)).astype(bf16)
    logits = _dot(xh, wh) + _dot(xl, wh) + _dot(xh, wl) + br
    lane_i = lax.broadcasted_iota(jnp.int32, (R, LANES), 1)
    lane = lane_i.astype(f32)
    neg = -jnp.inf
    big = float(LANES)
    is_g = lane_i < MOE_GROUPS
    lg = jnp.where(is_g, logits, neg)
    mg = jnp.max(lg, axis=1, keepdims=True)
    gidx = jnp.min(jnp.where(lg == mg, lane, big), axis=1, keepdims=True)
    ptop = 1.0 / jnp.sum(jnp.where(is_g, jnp.exp(logits - mg), 0.0), axis=1, keepdims=True)
    lo = ROUTE_E0 + MOE_EPG * gidx
    le = jnp.where((lane >= lo) & (lane < lo + MOE_EPG), logits, neg)
    v1 = jnp.max(le, axis=1, keepdims=True)
    i1 = jnp.min(jnp.where(le == v1, lane, big), axis=1, keepdims=True)
    le2 = jnp.where(lane == i1, neg, le)
    v2 = jnp.max(le2, axis=1, keepdims=True)
    i2 = jnp.min(jnp.where(le2 == v2, lane, big), axis=1, keepdims=True)
    s = jnp.exp(v2 - v1)
    w0 = ptop / (1.0 + s)
    w1 = ptop * s / (1.0 + s)
    oh = jnp.where((lane == i1) | (lane == i2), 1.0, 0.0)
    ri = lax.broadcasted_iota(jnp.int32, (R, R), 0)
    ci = lax.broadcasted_iota(jnp.int32, (R, R), 1)
    before = jnp.where(ri > ci, 1.0, 0.0).astype(bf16)
    tot = _dot(before, oh.astype(bf16)) + cnt_ref[...]
    rank0 = jnp.sum(jnp.where(lane == i1, tot, 0.0), axis=1, keepdims=True)
    rank1 = jnp.sum(jnp.where(lane == i2, tot, 0.0), axis=1, keepdims=True)
    cnt_ref[...] = cnt_ref[...] + jnp.sum(oh, axis=0, keepdims=True)
    vals = (i1 - ROUTE_E0, i2 - ROUTE_E0, rank0, rank1, w0, w1)
    slab = jnp.zeros((R, LANES), f32)
    for j, v in enumerate(vals):
        slab = jnp.where(lane_i == j, v, slab)
    sub = lax.broadcasted_iota(jnp.int32, (SUB, R), 0)
    plan = jnp.zeros((SUB, R), f32)
    for j, v in enumerate(vals[:4]):
        as_row = jnp.sum(jnp.where(ri == ci, v, 0.0), axis=0, keepdims=True)
        plan = jnp.where(sub == j, as_row, plan)
    return slab, plan


def _router_weights(w_rg, b_rg, w_re, b_re):
    w = jnp.concatenate([w_rg, jnp.moveaxis(w_re, 0, 1).reshape(D_MODEL, MOE_EXPERTS)], axis=1)
    b = jnp.concatenate([b_rg, b_re.reshape(MOE_EXPERTS)])
    pad = LANES - w.shape[1]
    return jnp.pad(w, ((0, 0), (0, pad))), jnp.pad(b, (0, pad))[None]


def _mix_out_tail(x1, gffn_ref, wr_ref, br_ref, x1_ref, xn_ref, route_ref, plan_ref, cnt_ref):
    @pl.when(pl.program_id(0) == 0)
    def _():
        cnt_ref[...] = jnp.zeros_like(cnt_ref)

    x1_ref[...] = x1
    xn = _rms(x1, gffn_ref[...])
    xn_ref[...] = xn
    route_ref[...], plan_ref[...] = _route(xn, wr_ref[...], br_ref[...], cnt_ref)


def _gla_out_kernel(og_ref, xm_ref, xt_ref, wo_ref, gffn_ref, wr_ref, br_ref, *outs):
    x1 = _main_or_tail(MIX_TILE, xm_ref, xt_ref) + _dot(og_ref[...], wo_ref[...])
    _mix_out_tail(x1, gffn_ref, wr_ref, br_ref, *outs)


def _gelu_tanh(x):
    return x * (0.5 * (1.0 + jnp.tanh(0.7978845608028654 * (x + 0.044715 * (x * x * x)))))


def _s5_out_kernel(ys_main_ref, ys_tail_ref, x_ref, gmix_ref, d_ref, wglu_ref, bglu_ref, gffn_ref, wr_ref, br_ref,
                   *outs):
    x = x_ref[...]
    u = _rms(x, gmix_ref[...])
    y = _gelu_tanh(_main_or_tail(MIX_TILE, ys_main_ref, ys_tail_ref) + d_ref[...] * u)
    z = _dot(y.astype(bf16), wglu_ref[...]) + bglu_ref[...]
    _mix_out_tail(x + y * _sigmoid(z), gffn_ref, wr_ref, br_ref, *outs)


def _mix_out_call(kern, name, n, row_ins, row_maps, fix_ins):
    row = lambda i: (i, 0)
    fix = lambda i: (0, 0)
    out_row = lambda w: pl.BlockSpec((MIX_TILE, w), row)
    nt = n // MIX_TILE
    return pl.pallas_call(
        kern,
        grid=(nt,),
        in_specs=[pl.BlockSpec((MIX_TILE, a.shape[1]), m) for a, m in zip(row_ins, row_maps)]
        + [pl.BlockSpec(a.shape, fix) for a in fix_ins],
        out_specs=[out_row(D_MODEL), out_row(D_MODEL), out_row(LANES), pl.BlockSpec((SUB, MIX_TILE), row),
                   pl.BlockSpec((1, LANES), fix)],
        out_shape=[jax.ShapeDtypeStruct((n, D_MODEL), f32), jax.ShapeDtypeStruct((n, D_MODEL), f32),
                   jax.ShapeDtypeStruct((n, LANES), f32), jax.ShapeDtypeStruct((nt * SUB, MIX_TILE), f32),
                   jax.ShapeDtypeStruct((1, LANES), f32)],
        compiler_params=_cp(("arbitrary",)),
        name=name,
    )(*row_ins, *fix_ins)


EXPERT_TM = 256
MOVE_TILE = MIX_TILE
COMBINE_CHUNK = 32


def _moe_plan(plan, cnt, n):
    i32 = jnp.int32
    v = plan.reshape(n // MOVE_TILE, SUB, MOVE_TILE)[:, :4].astype(i32)
    counts = cnt[0, ROUTE_E0:ROUTE_E0 + MOE_EXPERTS].astype(i32)
    ends = jnp.cumsum(counts)
    off = ends - counts
    ids = jnp.arange(MOE_EXPERTS, dtype=i32)
    pos = jnp.sum(jnp.where(v[:, 0:2, :, None] == ids, off, 0), axis=-1) + v[:, 2:4]
    total = 2 * n
    n_tiles = total // EXPERT_TM
    n_items = n_tiles + MOE_EXPERTS
    inner = (counts > 0) & (off % EXPERT_TM != 0)
    keys = jnp.concatenate([jnp.arange(n_tiles, dtype=i32) * EXPERT_TM, jnp.where(inner, off, total)])
    idx = jnp.arange(n_items, dtype=i32)
    before = (keys[None, :] < keys[:, None]) | ((keys[None, :] == keys[:, None]) & (idx[None, :] < idx[:, None]))
    order = jnp.sum(before.astype(i32), axis=1)
    starts = jnp.sum(jnp.where(order[:, None] == idx[None, :], keys[:, None], 0), axis=0)
    stops = jnp.concatenate([starts[1:], jnp.full((1,), total, i32)])
    tile = starts // EXPERT_TM
    expert = jnp.sum((ends[None, :] <= starts[:, None]).astype(i32), axis=1)
    used = n_tiles + jnp.sum(inner.astype(i32))
    keep = jnp.minimum(idx, used - 1)
    pick = lambda a: a[keep]
    return (pos.reshape(n // MOVE_TILE, 1, 2 * MOVE_TILE), pick(tile), pick(jnp.minimum(expert, MOE_EXPERTS - 1)),
            pick(starts - tile * EXPERT_TM), pick(stops - tile * EXPERT_TM), used.reshape(1))


def _row_copy(src, src_row, dst, dst_row, sem):
    return pltpu.make_async_copy(src.at[pl.ds(src_row, 1)], dst.at[pl.ds(dst_row, 1)], sem)


def _dispatch_kernel(pos_ref, x_ref, xs_ref, sem):
    def copy(r, s):
        return _row_copy(x_ref, r, xs_ref, pos_ref[0, 0, s * MOVE_TILE + r], sem)

    def start(r, c):
        for s in range(2):
            copy(r, s).start(priority=s)
        return c

    def wait(r, c):
        for s in range(2):
            copy(r, s).wait()
        return c

    lax.fori_loop(0, MOVE_TILE, start, 0, unroll=8)
    lax.fori_loop(0, MOVE_TILE, wait, 0, unroll=8)


def _dispatch_call(xn, pos):
    n = xn.shape[0]
    return pl.pallas_call(
        _dispatch_kernel,
        grid=(n // MOVE_TILE,),
        in_specs=[pl.BlockSpec((1, 1, 2 * MOVE_TILE), lambda i: (i, 0, 0), memory_space=pltpu.SMEM),
                  pl.BlockSpec((MOVE_TILE, D_MODEL), lambda i: (i, 0))],
        out_specs=pl.BlockSpec(memory_space=pl.ANY),
        out_shape=jax.ShapeDtypeStruct((2 * n, D_MODEL), xn.dtype),
        scratch_shapes=[pltpu.SemaphoreType.DMA(())],
        compiler_params=_cp(("arbitrary",)),
        name="moe_dispatch",
    )(pos, xn)


def _experts_kernel(it_ref, ie_ref, lo_ref, hi_ref, n_ref, xs_ref, wg_ref, wu_ref, wd_ref, os_ref,
                    wg_scr, wu_scr, wd_scr):
    i = pl.program_id(0)

    @pl.when(i < n_ref[0])
    def _():
        @pl.when((i == 0) | (ie_ref[i] != ie_ref[jnp.maximum(i - 1, 0)]))
        def _():
            wg_scr[...] = wg_ref[0].astype(bf16)
            wu_scr[...] = wu_ref[0].astype(bf16)
            wd_scr[...] = wd_ref[0].astype(bf16)

        x = xs_ref[...].astype(bf16)
        hg = _dot(x, wg_scr[...])
        hu = _dot(x, wu_scr[...])
        out = _dot((hg * _sigmoid(hg) * hu).astype(bf16), wd_scr[...])
        lo = lo_ref[i]
        row = lax.broadcasted_iota(jnp.int32, (EXPERT_TM, 1), 0)
        mine = (row >= lo) & (row < hi_ref[i])

        @pl.when(lo == 0)
        def _():
            os_ref[...] = jnp.where(mine, out, 0.0)

        @pl.when(lo != 0)
        def _():
            os_ref[...] = jnp.where(mine, out, os_ref[...])


def _experts_call(xs, item_tile, item_expert, item_lo, item_hi, n_items, wg, wu, wd):
    rows = lambda i, it, ie, lo, hi, n: (it[i], 0)
    wsel = lambda i, it, ie, lo, hi, n: (ie[i], 0, 0)
    return pl.pallas_call(
        _experts_kernel,
        grid_spec=pltpu.PrefetchScalarGridSpec(
            num_scalar_prefetch=5,
            grid=(item_tile.shape[0],),
            in_specs=[pl.BlockSpec((EXPERT_TM, D_MODEL), rows),
                      pl.BlockSpec((1, D_MODEL, D_EXPERT), wsel),
                      pl.BlockSpec((1, D_MODEL, D_EXPERT), wsel),
                      pl.BlockSpec((1, D_EXPERT, D_MODEL), wsel)],
            out_specs=pl.BlockSpec((EXPERT_TM, D_MODEL), rows),
            scratch_shapes=[pltpu.VMEM((D_MODEL, D_EXPERT), bf16), pltpu.VMEM((D_MODEL, D_EXPERT), bf16),
                            pltpu.VMEM((D_EXPERT, D_MODEL), bf16)],
        ),
        out_shape=jax.ShapeDtypeStruct(xs.shape, f32),
        compiler_params=_cp(("arbitrary",)),
        name="moe_experts",
    )(item_tile, item_expert, item_lo, item_hi, n_items, xs, wg, wu, wd)


def _combine_kernel(pos_ref, pos_next_ref, x_ref, route_ref, gn_ref, os_ref, *rest, emit_x, split):
    outs, (buf, sem, xn_scr) = rest[:-3], rest[-3:]
    i = pl.program_id(0)
    half = i % 2
    last = i == pl.num_programs(0) - 1

    def copy(p_ref, hf, r, s):
        return _row_copy(os_ref, p_ref[0, 0, s * MOVE_TILE + r], buf.at[hf, s], r, sem.at[hf])

    def loop_all(p_ref, hf, op):
        def body(r, c):
            for s in range(2):
                op(copy(p_ref, hf, r, s), s)
            return c

        lax.fori_loop(0, MOVE_TILE, body, 0, unroll=8)

    begin = lambda cp, s: cp.start(priority=s)
    finish = lambda cp, s: cp.wait()

    @pl.when(i == 0)
    def _():
        loop_all(pos_ref, 0, begin)

    loop_all(pos_ref, half, finish)
    xn_ref = xn_scr if split else outs[-1]
    for c in range(MOVE_TILE // COMBINE_CHUNK):
        rows = slice(c * COMBINE_CHUNK, (c + 1) * COMBINE_CHUNK)
        route = route_ref[rows, :]
        x2 = x_ref[rows, :] + route[:, 4:5] * buf[half, 0, rows, :] + route[:, 5:6] * buf[half, 1, rows, :]
        if emit_x:
            outs[0][rows, :] = x2
        xn_ref[rows, :] = _rms(x2, gn_ref[...]).astype(xn_ref.dtype)
        for r in range(rows.start, rows.stop):
            for s in range(2):
                begin(copy(pos_next_ref, 1 - half, r, s), s)

    @pl.when(last)
    def _():
        loop_all(pos_next_ref, 1 - half, finish)

    if split:
        main_ref, tail_ref = outs[-2:]
        is_main = i < N_MAIN // MOVE_TILE

        @pl.when(is_main)
        def _():
            main_ref[...] = xn_scr[...]

        @pl.when(jnp.logical_not(is_main))
        def _():
            tail_ref[...] = xn_scr[...]


def _combine_call(x1, route, pos, os_rows, gn, xn_dtype, *, emit_x, split):
    n = x1.shape[0]
    row = lambda i: (i, 0)
    blk = lambda m: pl.BlockSpec((MOVE_TILE, D_MODEL), m)
    nm = N_MAIN // MOVE_TILE
    out_specs, out_shape = [], []
    if emit_x:
        out_specs.append(blk(row))
        out_shape.append(jax.ShapeDtypeStruct((n, D_MODEL), f32))
    if split:
        out_specs += [blk(lambda i: (jnp.minimum(i, nm - 1), 0)), blk(lambda i: (jnp.maximum(i - nm, 0), 0))]
        out_shape += [jax.ShapeDtypeStruct((N_MAIN, D_MODEL), xn_dtype),
                      jax.ShapeDtypeStruct((n - N_MAIN, D_MODEL), xn_dtype)]
    else:
        out_specs.append(blk(row))
        out_shape.append(jax.ShapeDtypeStruct((n, D_MODEL), xn_dtype))
    nt = n // MOVE_TILE
    pos_spec = lambda m: pl.BlockSpec((1, 1, 2 * MOVE_TILE), m, memory_space=pltpu.SMEM)
    return pl.pallas_call(
        functools.partial(_combine_kernel, emit_x=emit_x, split=split),
        grid=(nt,),
        in_specs=[pos_spec(lambda i: (i, 0, 0)), pos_spec(lambda i: (jnp.minimum(i + 1, nt - 1), 0, 0)),
                  blk(row), pl.BlockSpec((MOVE_TILE, LANES), row),
                  pl.BlockSpec((1, D_MODEL), lambda i: (0, 0)), pl.BlockSpec(memory_space=pl.ANY)],
        out_specs=out_specs,
        out_shape=out_shape,
        scratch_shapes=[pltpu.VMEM((2, 2, MOVE_TILE, D_MODEL), f32), pltpu.SemaphoreType.DMA((2,)),
                        pltpu.VMEM((MOVE_TILE, D_MODEL), xn_dtype)],
        compiler_params=_cp(("arbitrary",)),
        name="moe_combine",
    )(pos, pos, x1, route, gn, os_rows)


S5_GROUP = 16
S5_GROUPS = D_MODEL // S5_GROUP
S5_STATE = 64
S5_CB = 128
S5_NB = D_MODEL // S5_CB
S5_GPB = S5_CB // S5_GROUP
S5_SB = S5_GPB * S5_STATE
S5_BC = S5_STATE * S5_GROUP
S5_CPS = 2


def _s5_disc_kernel(lr_ref, li_ref, ldt_ref, bre_ref, bim_ref,
                    abr_ref, abi_ref, ab2r_ref, ab2i_ref, bbr_ref, bbi_ref, abbr_ref, abbi_ref):
    lr, li = lr_ref[...], li_ref[...]
    dt = jnp.exp(ldt_ref[...])
    mag = jnp.exp(lr * dt)
    ang = li * dt
    ab_re, ab_im = mag * jnp.cos(ang), mag * jnp.sin(ang)
    nr, ni = ab_re - 1.0, ab_im
    den = lr * lr + li * li
    f_re = (nr * lr + ni * li) / den
    f_im = (ni * lr - nr * li) / den
    abr_ref[...] = ab_re
    abi_ref[...] = ab_im
    ab2r_ref[...] = ab_re * ab_re - ab_im * ab_im
    ab2i_ref[...] = 2.0 * (ab_re * ab_im)
    pi = lax.broadcasted_iota(jnp.int32, (S5_STATE, S5_BC), 0)
    ci = lax.broadcasted_iota(jnp.int32, (S5_STATE, S5_BC), 1)
    rep = jnp.where(ci // S5_GROUP == pi, 1.0, 0.0).astype(bf16)

    def expand(v):
        hi, mid, lo = _split3(v)
        return _dot(hi, rep) + _dot(mid, rep) + _dot(lo, rep)

    fr, fi, ar, ai = expand(f_re), expand(f_im), expand(ab_re), expand(ab_im)
    br, bi = bre_ref[...], bim_ref[...]
    bb_re = fr * br - fi * bi
    bb_im = fr * bi + fi * br
    bbr_ref[...] = bb_re
    bbi_ref[...] = bb_im
    abbr_ref[...] = ar * bb_re - ai * bb_im
    abbi_ref[...] = ar * bb_im + ai * bb_re


def _s5_weights(lam_re, lam_im, log_dt, b_re, b_im, c_re, c_im):
    st = jax.ShapeDtypeStruct((S5_GROUPS, S5_STATE), f32)
    bc = jax.ShapeDtypeStruct((S5_GROUPS, S5_BC), f32)
    ab_re, ab_im, ab2_re, ab2_im, bb_re, bb_im, abb_re, abb_im = pl.pallas_call(
        _s5_disc_kernel, out_shape=[st, st, st, st, bc, bc, bc, bc], name="s5_discretize",
    )(lam_re, lam_im, log_dt[:, None], b_re.reshape(S5_GROUPS, S5_BC), b_im.reshape(S5_GROUPS, S5_BC))
    eye = jnp.eye(S5_GPB, dtype=f32)

    def in_blocks(m):
        m = m.reshape(S5_NB, S5_GPB, S5_STATE, S5_GROUP)
        return jnp.einsum("jgpc,gh->jgchp", m, eye).reshape(S5_NB, S5_CB, S5_SB)

    def out_blocks(m):
        m = m.reshape(S5_NB, S5_GPB, S5_GROUP, S5_STATE)
        return jnp.einsum("jgcp,gh->jgphc", m, eye).reshape(S5_NB, S5_SB, S5_CB)

    bb2 = jnp.concatenate([
        jnp.concatenate([in_blocks(bb_re), in_blocks(bb_im)], axis=2),
        jnp.concatenate([in_blocks(abb_re), in_blocks(abb_im)], axis=2)], axis=1).astype(bf16)
    cc = jnp.concatenate([out_blocks(c_re), -out_blocks(c_im)], axis=1).astype(bf16)
    rows = [v.reshape(S5_NB, 1, S5_SB) for v in (ab_re, ab_im, ab2_re, ab2_im)]
    abv = jnp.concatenate(rows + [jnp.zeros((S5_NB, SUB - len(rows), S5_SB), f32)], axis=1)
    return bb2, cc, abv


def _s5_seq_kernel(x0_ref, x1_ref, x2_ref, x3_ref, halo0_ref, h0_ref, bb2_ref, cc_ref, abv_ref, y_ref, hout_ref,
                   xf_scr, xp_scr, bu_scr, hs_scr, yp_scr, yn_scr, h_scr, halo_scr):
    tb = pl.program_id(1)
    TL = x0_ref.shape[0]
    KB = TL // 2
    RB = BATCH * TL

    @pl.when(tb == 0)
    def _():
        h_scr[...] = h0_ref[...]
        halo_scr[...] = halo0_ref[...].astype(f32)

    chans = [slice(c * S5_CB, (c + 1) * S5_CB) for c in range(S5_CPS)]
    for c, ch in enumerate(chans):
        for b, xb_ref in enumerate((x0_ref, x1_ref, x2_ref, x3_ref)):
            xf_scr[c, b * TL:(b + 1) * TL, :] = xb_ref[:, ch].astype(f32)
        for b in range(BATCH):
            for p in range(2):
                xp_scr[c, pl.ds(2 * b + p, KB, stride=SUB), :] = xf_scr[c, pl.ds(b * TL + p, KB, stride=2), :]
    x = jnp.concatenate([xp_scr[c] for c in range(S5_CPS)], axis=1)
    xc = jnp.concatenate([halo_scr[...], x], axis=0)
    odd = (lax.broadcasted_iota(jnp.int32, (RB + SUB, 1), 0) & 1) == 1
    xprev = jnp.where(odd, pltpu.roll(xc, 1, axis=0), pltpu.roll(xc, SUB - 1, axis=0))[SUB:]
    halo_scr[...] = x[RB - SUB:]
    for c, ch in enumerate(chans):
        lhs = jnp.concatenate([x[:, ch], xprev[:, ch]], axis=1).astype(bf16)
        bu_scr[c] = _dot(lhs, bb2_ref[c])
    a2 = [(abv_ref[c, 2:3, :], abv_ref[c, 3:4, :]) for c in range(S5_CPS)]

    def step(k, carry):
        r0 = pl.multiple_of(k * SUB, SUB)
        out = []
        for c in range(S5_CPS):
            hr, hi = carry[2 * c], carry[2 * c + 1]
            a2r, a2i = a2[c]
            bu = bu_scr[c, pl.ds(r0, SUB), :]
            nr = a2r * hr - a2i * hi + bu[:, :S5_SB]
            ni = a2r * hi + a2i * hr + bu[:, S5_SB:]
            hs_scr[c, pl.ds(r0, SUB), :S5_SB] = nr
            hs_scr[c, pl.ds(r0, SUB), S5_SB:] = ni
            out += [nr, ni]
        return tuple(out)

    init = []
    for c in range(S5_CPS):
        init += [h_scr[c, :, :S5_SB], h_scr[c, :, S5_SB:]]
    fin = lax.fori_loop(0, RB // SUB, step, tuple(init), unroll=4)
    for c, ch in enumerate(chans):
        h_scr[c, :, :S5_SB] = fin[2 * c]
        h_scr[c, :, S5_SB:] = fin[2 * c + 1]
        yp_scr[c] = _dot(hs_scr[c].astype(bf16), cc_ref[c])
        for b in range(BATCH):
            for p in range(2):
                yn_scr[c, pl.ds(b * TL + p, KB, stride=2), :] = yp_scr[c, pl.ds(2 * b + p, KB, stride=SUB), :]
            y_ref[b, :, ch] = yn_scr[c, b * TL:(b + 1) * TL, :]

    @pl.when(tb == pl.num_programs(1) - 1)
    def _():
        hout_ref[...] = h_scr[...]


def _s5_seq_call(x, halo0, h0, bb2, cc, abv, *, row0, seq_len, tl):
    wsel = lambda j, t: (j, 0, 0)
    rb = BATCH * tl
    cw = S5_CPS * S5_CB
    xspec = lambda b: pl.BlockSpec((tl, cw), lambda j, t: ((row0 + b * seq_len) // tl + t, j))
    return pl.pallas_call(
        _s5_seq_kernel,
        grid=(S5_NB // S5_CPS, seq_len // tl),
        in_specs=[xspec(b) for b in range(BATCH)]
        + [pl.BlockSpec((SUB, cw), lambda j, t: (0, j)),
           pl.BlockSpec((S5_CPS, SUB, 2 * S5_SB), wsel),
           pl.BlockSpec((S5_CPS, 2 * S5_CB, 2 * S5_SB), wsel),
           pl.BlockSpec((S5_CPS, 2 * S5_SB, S5_CB), wsel),
           pl.BlockSpec((S5_CPS, SUB, S5_SB), wsel)],
        out_specs=[pl.BlockSpec((BATCH, tl, cw), lambda j, t: (0, t, j)),
                   pl.BlockSpec((S5_CPS, SUB, 2 * S5_SB), wsel)],
        out_shape=[jax.ShapeDtypeStruct((BATCH, seq_len, D_MODEL), f32),
                   jax.ShapeDtypeStruct((S5_NB, SUB, 2 * S5_SB), f32)],
        scratch_shapes=[pltpu.VMEM((S5_CPS, rb, S5_CB), f32), pltpu.VMEM((S5_CPS, rb, S5_CB), f32),
                        pltpu.VMEM((S5_CPS, rb, 2 * S5_SB), f32), pltpu.VMEM((S5_CPS, rb, 2 * S5_SB), f32),
                        pltpu.VMEM((S5_CPS, rb, S5_CB), f32), pltpu.VMEM((S5_CPS, rb, S5_CB), f32),
                        pltpu.VMEM((S5_CPS, SUB, 2 * S5_SB), f32), pltpu.VMEM((SUB, cw), f32)],
        compiler_params=_cp(("arbitrary", "arbitrary")),
        name=f"s5_seq_{seq_len}",
    )(x, x, x, x, halo0, h0, bb2, cc, abv)


def _s5_sample_kernel(x_ref, hre_ref, him_ref, bb2_ref, cc_ref, abv_ref, y_ref, ore_ref, oim_ref, hs_scr):
    nb = hre_ref.shape[0]
    bu = _dot(x_ref[...], bb2_ref[0, :S5_CB, :])
    ar = abv_ref[0, 0:1, :]
    ai = abv_ref[0, 1:2, :]
    hr, hi = hre_ref[...], him_ref[...]
    for t in range(DEC_SEQ):
        rows = slice(t * nb, (t + 1) * nb)
        hr, hi = (ar * hr - ai * hi + bu[rows, :S5_SB], ar * hi + ai * hr + bu[rows, S5_SB:])
        hs_scr[rows, :S5_SB] = hr
        hs_scr[rows, S5_SB:] = hi
    y_ref[...] = _dot(hs_scr[...].astype(bf16), cc_ref[0])
    ore_ref[...] = hr
    oim_ref[...] = hi


def _s5_sample_call(xt, h_re, h_im, bb2, cc, abv):
    n = xt.shape[0]
    nb = h_re.shape[0]
    wsel = lambda j: (j, 0, 0)
    st = pl.BlockSpec((nb, S5_SB), lambda j: (0, j))
    return pl.pallas_call(
        _s5_sample_kernel,
        grid=(S5_NB,),
        in_specs=[pl.BlockSpec((n, S5_CB), lambda j: (0, j)), st, st,
                  pl.BlockSpec((1, 2 * S5_CB, 2 * S5_SB), wsel),
                  pl.BlockSpec((1, 2 * S5_SB, S5_CB), wsel),
                  pl.BlockSpec((1, SUB, S5_SB), wsel)],
        out_specs=[pl.BlockSpec((n, S5_CB), lambda j: (0, j)), st, st],
        out_shape=[jax.ShapeDtypeStruct((n, D_MODEL), f32),
                   jax.ShapeDtypeStruct(h_re.shape, f32), jax.ShapeDtypeStruct(h_im.shape, f32)],
        scratch_shapes=[pltpu.VMEM((n, 2 * S5_SB), f32)],
        compiler_params=_cp(("arbitrary",)),
        name="s5_sample",
    )(xt, h_re, h_im, bb2, cc, abv)


GLA_CHUNK = 256
GLA_DIRECT = 2
S5_TL = 256


def _moe_layer(layer, x1, xnf, route, plan, cnt, wg, wu, wd, gn, xn_dtype, *, emit_x, split):
    pos, item_tile, item_expert, item_lo, item_hi, n_items = _moe_plan(plan, cnt, x1.shape[0])
    xs = _dispatch_call(xnf, pos)
    os_rows = _experts_call(xs, item_tile, item_expert + layer * MOE_EXPERTS, item_lo, item_hi, n_items, wg, wu, wd)
    return _combine_call(x1, route, pos, os_rows, gn, xn_dtype, emit_x=emit_x, split=split)


def kernel(x_prompt, x_sample, state_gla, state_s5_re, state_s5_im, meta_tokens, norm_mix_g, norm_ffn_g, norm_final_g, gla_w_in, gla_w_a2, gla_b_a, gla_g_o, gla_w_o, s5_lambda_re, s5_lambda_im, s5_log_dt, s5_b_re, s5_b_im, s5_c_re, s5_c_im, s5_d, s5_w_glu, s5_b_glu, moe_w_rg, moe_b_rg, moe_w_re, moe_b_re, moe_w_gate, moe_w_up, moe_w_down):
    row = lambda v: v.reshape(1, -1)
    x_main = x_prompt.reshape(N_MAIN, D_MODEL)
    x_tail = jnp.concatenate([
        jnp.tile(meta_tokens.astype(x_prompt.dtype), (BATCH, 1)),
        x_sample.reshape(N_SAMPLE, D_MODEL),
        jnp.zeros((N_ROWS - N_REAL, D_MODEL), x_prompt.dtype)], axis=0)
    wg = moe_w_gate.reshape(-1, D_MODEL, D_EXPERT)
    wu = moe_w_up.reshape(-1, D_MODEL, D_EXPERT)
    wd = moe_w_down.reshape(-1, D_EXPERT, D_MODEL)

    w_in = jnp.swapaxes(gla_w_in, 1, 2)
    wa1 = jnp.pad(gla_w_in[0, :, GLA_QKVR:], ((0, 0), (0, LANES - GLA_RANK)))
    wa2 = jnp.pad(gla_w_a2.reshape(GLA_RANK, GLA_KEY), ((0, LANES - GLA_RANK), (0, 0)))
    xn, glog = _norm_gate_call(x_main, x_tail, row(norm_mix_g[0]), wa1, wa2, row(gla_b_a))
    proj = _proj_call(xn, w_in, GLA_QKVR)
    go = row(gla_g_o)
    og = jnp.zeros((N_ROWS, GLA_VAL), bf16)
    s_zero = jnp.zeros((BATCH, GLA_HEADS, GLA_DK, GLA_DV), f32)
    og, s_meta = _gla_seq_call(proj, glog, go, s_zero, og, row0=ROW_META, C=N_META, n_chunks=1, d=GLA_DIRECT)
    og, s_prompt = _gla_seq_call(proj, glog, go, s_meta, og, row0=0, C=GLA_CHUNK, n_chunks=SEQ // GLA_CHUNK,
                                 d=GLA_DIRECT)
    og, s_sample = _gla_sample_call(proj, glog, go, state_gla.reshape(DEC_BATCH, GLA_HEADS, GLA_DK, GLA_DV), og,
                                    row0=ROW_SAMPLE)
    wr, br = _router_weights(moe_w_rg[0], moe_b_rg[0], moe_w_re[0], moe_b_re[0])
    tile_row = lambda i: (i, 0)
    x1, xnf, route, plan, cnt = _mix_out_call(
        _gla_out_kernel, "gla_out", N_ROWS, [og, x_main, x_tail], [tile_row, *_main_tail_maps(MIX_TILE)],
        [gla_w_o.reshape(GLA_VAL, D_MODEL).astype(bf16), row(norm_ffn_g[0]), wr, br])
    x2, xn2 = _moe_layer(0, x1, xnf, route, plan, cnt, wg, wu, wd, row(norm_mix_g[1]), bf16, emit_x=True, split=False)

    bb2, cc, abv = _s5_weights(s5_lambda_re[0], s5_lambda_im[0], s5_log_dt[0], s5_b_re[0], s5_b_im[0],
                               s5_c_re[0], s5_c_im[0])
    y_meta, h_meta = _s5_seq_call(xn2, jnp.zeros((SUB, D_MODEL), bf16), jnp.zeros((S5_NB, SUB, 2 * S5_SB), f32),
                                  bb2, cc, abv, row0=ROW_META, seq_len=N_META, tl=N_META)
    halo = xn2[ROW_META:ROW_SAMPLE].reshape(BATCH, N_META, D_MODEL)[:, N_META - 2:].reshape(SUB, D_MODEL)
    y_main, h_main = _s5_seq_call(xn2, halo, h_meta, bb2, cc, abv, row0=0, seq_len=SEQ, tl=S5_TL)
    xt_sample = xn2[ROW_SAMPLE:N_REAL].reshape(DEC_BATCH, DEC_SEQ, D_MODEL).transpose(1, 0, 2).reshape(N_SAMPLE, D_MODEL)
    y_samp, s5r_s, s5i_s = _s5_sample_call(
        xt_sample, state_s5_re.reshape(DEC_BATCH, S5_GROUPS * S5_STATE),
        state_s5_im.reshape(DEC_BATCH, S5_GROUPS * S5_STATE), bb2, cc, abv)
    ys_tail = jnp.concatenate([
        y_meta.reshape(N_METAROWS, D_MODEL),
        y_samp.reshape(DEC_SEQ, DEC_BATCH, D_MODEL).transpose(1, 0, 2).reshape(N_SAMPLE, D_MODEL),
        jnp.zeros((N_ROWS - N_REAL, D_MODEL), f32)], axis=0)
    wr, br = _router_weights(moe_w_rg[1], moe_b_rg[1], moe_w_re[1], moe_b_re[1])
    x3, xnf, route, plan, cnt = _mix_out_call(
        _s5_out_kernel, "s5_out", N_ROWS, [y_main.reshape(N_MAIN, D_MODEL), ys_tail, x2],
        [*_main_tail_maps(MIX_TILE), tile_row],
        [row(norm_mix_g[1]), row(s5_d), s5_w_glu.reshape(D_MODEL, D_MODEL).astype(bf16), row(s5_b_glu),
         row(norm_ffn_g[1]), wr, br])
    y_main_out, y_tail_out = _moe_layer(1, x3, xnf, route, plan, cnt, wg, wu, wd, row(norm_final_g), f32,
                                        emit_x=False, split=True)

    y_prompt = y_main_out.reshape(BATCH, SEQ, D_MODEL)
    y_sample = y_tail_out[N_METAROWS:N_METAROWS + N_SAMPLE].reshape(DEC_BATCH, DEC_SEQ, D_MODEL)
    hfin = h_main.reshape(S5_NB, BATCH, 2, 2, S5_GPB, S5_STATE)[:, :, 1]
    s5r_p = hfin[:, :, 0].transpose(1, 0, 2, 3).reshape(1, BATCH, S5_GROUPS, S5_STATE)
    s5i_p = hfin[:, :, 1].transpose(1, 0, 2, 3).reshape(1, BATCH, S5_GROUPS, S5_STATE)
    return (y_prompt, y_sample, s_prompt[None], s5r_p, s5i_p, s_sample[None],
            s5r_s.reshape(1, DEC_BATCH, S5_GROUPS, S5_STATE), s5i_s.reshape(1, DEC_BATCH, S5_GROUPS, S5_STATE))
```

```python
import functools

import jax
import jax.numpy as jnp
from jax import lax
from jax.experimental import pallas as pl
from jax.experimental.pallas import tpu as pltpu

f32 = jnp.float32
bf16 = jnp.bfloat16

D_MODEL = 2048
BATCH = 4
SEQ = 2048
DEC_BATCH = 128
DEC_SEQ = 4
N_META = 16
EPS = 1e-6
GLA_HEADS = 4
GLA_DK = 256
GLA_DV = 512
GLA_KEY = GLA_HEADS * GLA_DK
GLA_VAL = GLA_HEADS * GLA_DV
GLA_RANK = 16
GLA_TAU = 16.0
GLA_QKVR = 2 * GLA_KEY + 2 * GLA_VAL

N_MAIN = BATCH * SEQ
N_METAROWS = BATCH * N_META
N_SAMPLE = DEC_BATCH * DEC_SEQ
ROW_META = N_MAIN
ROW_SAMPLE = N_MAIN + N_METAROWS
N_REAL = ROW_SAMPLE + N_SAMPLE
ROW_TILE = 512
N_ROWS = -(-N_REAL // ROW_TILE) * ROW_TILE

VMEM_LIMIT = 56 * 1024 * 1024


def _cp(sem, vmem=VMEM_LIMIT):
    return pltpu.CompilerParams(dimension_semantics=sem, vmem_limit_bytes=vmem)


def _dot(a, b):
    return jnp.dot(a, b, preferred_element_type=f32)


def _dot_nt(a, b):
    return lax.dot_general(a, b, (((1,), (1,)), ((), ())), preferred_element_type=f32)


def _dot_tn(a, b):
    return lax.dot_general(a, b, (((0,), (0,)), ((), ())), preferred_element_type=f32)


def _sigmoid(x):
    return 1.0 / (1.0 + jnp.exp(-x))


def _split3(x):
    hi = x.astype(bf16)
    r1 = x - hi.astype(f32)
    mid = r1.astype(bf16)
    lo = (r1 - mid.astype(f32)).astype(bf16)
    return hi, mid, lo


def _cumsum_rows(g, C):
    if C <= 16:
        row = lax.broadcasted_iota(jnp.int32, (C, 1), 0)
        b = jnp.zeros_like(g)
        for s in range(C):
            b = b + jnp.where(row >= s, g[s:s + 1, :], 0.0)
        return b
    row = lax.broadcasted_iota(jnp.int32, (C, C), 0)
    col = lax.broadcasted_iota(jnp.int32, (C, C), 1)
    tri = jnp.where(row >= col, 1.0, 0.0).astype(bf16)
    hi, mid, lo = _split3(g)
    return _dot(tri, hi) + _dot(tri, mid) + _dot(tri, lo)


PAIR_LEVEL = 1000


def _gla_pair_code(C, d, tree=True):
    ti = lax.broadcasted_iota(jnp.int32, (C, C), 0)
    si = lax.broadcasted_iota(jnp.int32, (C, C), 1)
    code = jnp.where((ti // d == si // d) & (si <= ti), 1 + ti - si, 0)
    h = d
    while tree and h < C:
        tb = ti // h
        code = jnp.where(((tb % 2) == 1) & ((si // h) == tb - 1), PAIR_LEVEL + h, code)
        h *= 2
    return code


def _gla_scores(q, k, b, code, C, d, tree=True):
    row = lax.broadcasted_iota(jnp.int32, (C, 1), 0)
    scores = jnp.zeros((C, C), f32)
    for dl in range(d):
        ks = k if dl == 0 else pltpu.roll(k, dl, axis=0)
        bs = b if dl == 0 else pltpu.roll(b, dl, axis=0)
        term = q * ks * jnp.exp(jnp.minimum(b - bs, 0.0))
        colv = jnp.sum(term, axis=1, keepdims=True)
        scores = jnp.where(code == 1 + dl, colv, scores)
    z = b
    s = 1
    while tree and 2 * s < C:
        z = jnp.where((row & s) != 0, pltpu.roll(z, s, axis=0), z)
        s *= 2
        h = s
        if h < d:
            continue
        bnext = pltpu.roll(z, C - h, axis=0)
        qh = (q * jnp.exp(jnp.minimum(b - z, 0.0))).astype(bf16)
        kh = (k * jnp.exp(jnp.minimum(bnext - b, 0.0))).astype(bf16)
        scores = jnp.where(code == PAIR_LEVEL + h, _dot_nt(qh, kh), scores)
    return scores


def _gla_chunk(q, k, v, g, S, code, C, d, tree=True):
    b = _cumsum_rows(g, C)
    o = _dot((q * jnp.exp(b)).astype(bf16), S.astype(bf16))
    scores = _gla_scores(q, k, b, code, C, d, tree)
    o = o + _dot(scores.astype(bf16), v)
    b_last = b[C - 1:C, :]
    kd = (k * jnp.exp(b_last - b)).astype(bf16)
    if C == GLA_DK:
        eye = code == 1
    else:
        eye = (lax.broadcasted_iota(jnp.int32, (GLA_DK, GLA_DK), 0)
               == lax.broadcasted_iota(jnp.int32, (GLA_DK, GLA_DK), 1))
    dec_col = jnp.sum(jnp.where(eye, jnp.exp(b_last), 0.0), axis=1, keepdims=True)
    S_new = dec_col * S + _dot_tn(kd, v)
    return o, S_new


def _gla_head_epilogue(o, r, go):
    ms = jnp.mean(o * o, axis=1, keepdims=True)
    on = o * lax.rsqrt(ms + EPS) * go
    return on * (r * _sigmoid(r))


def _gla_seq_kernel(q_ref, k_ref, v_ref, r_ref, g_ref, go_ref, s0_ref, _og_in, og_ref, sout_ref, s_scr, *, C, d):
    c = pl.program_id(1)

    @pl.when(c == 0)
    def _():
        s_scr[...] = s0_ref[0]

    code = _gla_pair_code(C, d)

    def head(h, carry):
        ck = pl.ds(pl.multiple_of(h * GLA_DK, GLA_DK), GLA_DK)
        cv = pl.ds(pl.multiple_of(h * GLA_DV, GLA_DV), GLA_DV)
        q = q_ref[:, ck].astype(f32) * (GLA_DK ** -0.5)
        k = k_ref[:, ck].astype(f32)
        o, S_new = _gla_chunk(q, k, v_ref[:, cv], g_ref[:, ck], s_scr[h], code, C, d)
        s_scr[h] = S_new
        og_ref[:, cv] = _gla_head_epilogue(o, r_ref[:, cv].astype(f32), go_ref[:, cv]).astype(og_ref.dtype)
        return carry

    lax.fori_loop(0, GLA_HEADS, head, 0)

    @pl.when(c == pl.num_programs(1) - 1)
    def _():
        sout_ref[0] = s_scr[...]


def _gla_seq_call(proj, glog, go, s0, og_buf, *, row0, C, n_chunks, d):
    blk0 = row0 // C
    rows = lambda b, c: blk0 + b * n_chunks + c
    kern = functools.partial(_gla_seq_kernel, C=C, d=d)
    return pl.pallas_call(
        kern,
        grid=(BATCH, n_chunks),
        in_specs=[
            pl.BlockSpec((C, GLA_KEY), lambda b, c: (rows(b, c), 0)),
            pl.BlockSpec((C, GLA_KEY), lambda b, c: (rows(b, c), 1)),
            pl.BlockSpec((C, GLA_VAL), lambda b, c: (rows(b, c), 1)),
            pl.BlockSpec((C, GLA_VAL), lambda b, c: (rows(b, c), 2)),
            pl.BlockSpec((C, GLA_KEY), lambda b, c: (rows(b, c), 0)),
            pl.BlockSpec((1, GLA_VAL), lambda b, c: (0, 0)),
            pl.BlockSpec((1, GLA_HEADS, GLA_DK, GLA_DV), lambda b, c: (b, 0, 0, 0)),
            pl.BlockSpec(memory_space=pl.ANY),
        ],
        out_specs=[
            pl.BlockSpec((C, GLA_VAL), lambda b, c: (rows(b, c), 0)),
            pl.BlockSpec((1, GLA_HEADS, GLA_DK, GLA_DV), lambda b, c: (b, 0, 0, 0)),
        ],
        out_shape=[
            jax.ShapeDtypeStruct(og_buf.shape, og_buf.dtype),
            jax.ShapeDtypeStruct((BATCH, GLA_HEADS, GLA_DK, GLA_DV), f32),
        ],
        scratch_shapes=[pltpu.VMEM((GLA_HEADS, GLA_DK, GLA_DV), f32)],
        input_output_aliases={7: 0},
        compiler_params=_cp(("arbitrary", "arbitrary")),
        name=f"gla_seq_c{C}",
    )(proj, proj, proj, proj, glog, go, s0, og_buf)


SAMPLE_BB = 4
SAMPLE_C = SAMPLE_BB * DEC_SEQ


def _gla_sample_kernel(q_ref, k_ref, v_ref, r_ref, g_ref, go_ref, s0_ref, _og_in, og_ref, sout_ref):
    row = lax.broadcasted_iota(jnp.int32, (SAMPLE_C, 1), 0)

    code = _gla_pair_code(SAMPLE_C, DEC_SEQ, tree=False)

    def take(x, bb):
        sh = (SAMPLE_C - DEC_SEQ * bb) % SAMPLE_C
        return jnp.where(row < DEC_SEQ, pltpu.roll(x, sh, axis=0) if sh else x, 0.0)

    def head(h, carry):
        ck = pl.ds(pl.multiple_of(h * GLA_DK, GLA_DK), GLA_DK)
        cv = pl.ds(pl.multiple_of(h * GLA_DV, GLA_DV), GLA_DV)
        q_all = q_ref[:, ck].astype(f32) * (GLA_DK ** -0.5)
        k_all = k_ref[:, ck].astype(f32)
        v_all = v_ref[:, cv].astype(f32)
        r_all = r_ref[:, cv].astype(f32)
        g_all = g_ref[:, ck]
        go = go_ref[:, cv]
        acc = jnp.zeros((SAMPLE_C, GLA_DV), f32)
        for bb in range(SAMPLE_BB):
            o, S_new = _gla_chunk(take(q_all, bb), take(k_all, bb), take(v_all, bb).astype(bf16),
                                  take(g_all, bb), s0_ref[bb, h], code, SAMPLE_C, DEC_SEQ, tree=False)
            sout_ref[bb, h] = S_new
            y = _gla_head_epilogue(o, take(r_all, bb), go)
            acc = jnp.where(row // DEC_SEQ == bb, pltpu.roll(y, DEC_SEQ * bb, axis=0) if bb else y, acc)
        og_ref[:, cv] = acc.astype(og_ref.dtype)
        return carry

    lax.fori_loop(0, GLA_HEADS, head, 0)


def _gla_sample_call(proj, glog, go, s0, og_buf, *, row0):
    n_seq = s0.shape[0]
    blk0 = row0 // SAMPLE_C
    st_spec = pl.BlockSpec((SAMPLE_BB, GLA_HEADS, GLA_DK, GLA_DV), lambda i: (i, 0, 0, 0))
    return pl.pallas_call(
        _gla_sample_kernel,
        grid=(n_seq // SAMPLE_BB,),
        in_specs=[
            pl.BlockSpec((SAMPLE_C, GLA_KEY), lambda i: (blk0 + i, 0)),
            pl.BlockSpec((SAMPLE_C, GLA_KEY), lambda i: (blk0 + i, 1)),
            pl.BlockSpec((SAMPLE_C, GLA_VAL), lambda i: (blk0 + i, 1)),
            pl.BlockSpec((SAMPLE_C, GLA_VAL), lambda i: (blk0 + i, 2)),
            pl.BlockSpec((SAMPLE_C, GLA_KEY), lambda i: (blk0 + i, 0)),
            pl.BlockSpec((1, GLA_VAL), lambda i: (0, 0)),
            st_spec,
            pl.BlockSpec(memory_space=pl.ANY),
        ],
        out_specs=[pl.BlockSpec((SAMPLE_C, GLA_VAL), lambda i: (blk0 + i, 0)), st_spec],
        out_shape=[jax.ShapeDtypeStruct(og_buf.shape, og_buf.dtype), jax.ShapeDtypeStruct(s0.shape, f32)],
        input_output_aliases={7: 0},
        compiler_params=_cp(("arbitrary",)),
        name="gla_sample",
    )(proj, proj, proj, proj, glog, go, s0, og_buf)


LANES = 128
SUB = 8
MOE_GROUPS = 4
MOE_EPG = 8
MOE_EXPERTS = MOE_GROUPS * MOE_EPG
D_EXPERT = 256
ROUTE_E0 = MOE_GROUPS
MIX_TILE = 256
PROJ_NT = 1024


def _rms(x, g):
    r = lax.rsqrt(jnp.mean(x * x, axis=-1, keepdims=True) + EPS)
    return (x * r) * g


def _log_sigmoid(z):
    return jnp.minimum(z, 0.0) - jnp.log1p(jnp.exp(-jnp.abs(z)))


def _main_or_tail(tile, main_ref, tail_ref):
    return jnp.where(pl.program_id(0) < N_MAIN // tile, main_ref[...], tail_ref[...])


def _main_tail_maps(tile):
    nm = N_MAIN // tile
    return (lambda i: (jnp.minimum(i, nm - 1), 0)), (lambda i: (jnp.maximum(i - nm, 0), 0))


def _norm_gate_kernel(xm_ref, xt_ref, gn_ref, wa1_ref, wa2_ref, ba_ref, xn_ref, gl_ref):
    xnb = _rms(_main_or_tail(ROW_TILE, xm_ref, xt_ref), gn_ref[...]).astype(bf16)
    xn_ref[...] = xnb
    a = _dot(xnb, wa1_ref[...].astype(bf16))
    z = _dot(a.astype(bf16), wa2_ref[...].astype(bf16)) + ba_ref[...]
    gl_ref[...] = _log_sigmoid(z) * (1.0 / GLA_TAU)


def _norm_gate_call(x_main, x_tail, gn, wa1, wa2, ba):
    n = x_main.shape[0] + x_tail.shape[0]
    row = lambda i: (i, 0)
    fix = lambda i: (0, 0)
    main_map, tail_map = _main_tail_maps(ROW_TILE)
    return pl.pallas_call(
        _norm_gate_kernel,
        grid=(n // ROW_TILE,),
        in_specs=[pl.BlockSpec((ROW_TILE, D_MODEL), main_map), pl.BlockSpec((ROW_TILE, D_MODEL), tail_map),
                  pl.BlockSpec((1, D_MODEL), fix),
                  pl.BlockSpec((D_MODEL, LANES), fix), pl.BlockSpec((LANES, GLA_KEY), fix),
                  pl.BlockSpec((1, GLA_KEY), fix)],
        out_specs=[pl.BlockSpec((ROW_TILE, D_MODEL), row), pl.BlockSpec((ROW_TILE, GLA_KEY), row)],
        out_shape=[jax.ShapeDtypeStruct((n, D_MODEL), bf16), jax.ShapeDtypeStruct((n, GLA_KEY), f32)],
        compiler_params=_cp(("arbitrary",)),
        name="norm_gate",
    )(x_main, x_tail, gn, wa1, wa2, ba)


def _proj_kernel(xn_ref, w_ref, o_ref, wb_scr):
    @pl.when(pl.program_id(1) == 0)
    def _():
        wb_scr[...] = w_ref[0].astype(bf16)

    o_ref[...] = _dot_nt(xn_ref[...], wb_scr[...]).astype(o_ref.dtype)


def _proj_call(xn, wt, n_cols):
    n = xn.shape[0]
    return pl.pallas_call(
        _proj_kernel,
        grid=(n_cols // PROJ_NT, n // ROW_TILE),
        in_specs=[pl.BlockSpec((ROW_TILE, D_MODEL), lambda j, i: (i, 0)),
                  pl.BlockSpec((1, PROJ_NT, D_MODEL), lambda j, i: (0, j, 0))],
        out_specs=pl.BlockSpec((ROW_TILE, PROJ_NT), lambda j, i: (i, j)),
        out_shape=jax.ShapeDtypeStruct((n, n_cols), bf16),
        scratch_shapes=[pltpu.VMEM((PROJ_NT, D_MODEL), bf16)],
        compiler_params=_cp(("arbitrary", "arbitrary")),
        name="gla_proj",
    )(xn, wt)


def _route(xn, wr, br, cnt_ref):
    R = xn.shape[0]
    xh = xn.astype(bf16)
    xl = (xn - xh.astype(f32)).astype(bf16)
    wh = wr.astype(bf16)
    wl = (wr - wh.astype(f32)).astype(bf16)
    logits = _dot(xh, wh) + _dot(xl, wh) + _dot(xh, wl) + br
    lane_i = lax.broadcasted_iota(jnp.int32, (R, LANES), 1)
    lane = lane_i.astype(f32)
    neg = -jnp.inf
    big = float(LANES)
    is_g = lane_i < MOE_GROUPS
    lg = jnp.where(is_g, logits, neg)
    mg = jnp.max(lg, axis=1, keepdims=True)
    gidx = jnp.min(jnp.where(lg == mg, lane, big), axis=1, keepdims=True)
    ptop = 1.0 / jnp.sum(jnp.where(is_g, jnp.exp(logits - mg), 0.0), axis=1, keepdims=True)
    lo = ROUTE_E0 + MOE_EPG * gidx
    le = jnp.where((lane >= lo) & (lane < lo + MOE_EPG), logits, neg)
    v1 = jnp.max(le, axis=1, keepdims=True)
    i1 = jnp.min(jnp.where(le == v1, lane, big), axis=1, keepdims=True)
    le2 = jnp.where(lane == i1, neg, le)
    v2 = jnp.max(le2, axis=1, keepdims=True)
    i2 = jnp.min(jnp.where(le2 == v2, lane, big), axis=1, keepdims=True)
    s = jnp.exp(v2 - v1)
    w0 = ptop / (1.0 + s)
    w1 = ptop * s / (1.0 + s)
    oh = jnp.where((lane == i1) | (lane == i2), 1.0, 0.0)
    ri = lax.broadcasted_iota(jnp.int32, (R, R), 0)
    ci = lax.broadcasted_iota(jnp.int32, (R, R), 1)
    before = jnp.where(ri > ci, 1.0, 0.0).astype(bf16)
    tot = _dot(before, oh.astype(bf16)) + cnt_ref[...]
    rank0 = jnp.sum(jnp.where(lane == i1, tot, 0.0), axis=1, keepdims=True)
    rank1 = jnp.sum(jnp.where(lane == i2, tot, 0.0), axis=1, keepdims=True)
    cnt_ref[...] = cnt_ref[...] + jnp.sum(oh, axis=0, keepdims=True)
    vals = (i1 - ROUTE_E0, i2 - ROUTE_E0, rank0, rank1, w0, w1)
    slab = jnp.zeros((R, LANES), f32)
    for j, v in enumerate(vals):
        slab = jnp.where(lane_i == j, v, slab)
    sub = lax.broadcasted_iota(jnp.int32, (SUB, R), 0)
    plan = jnp.zeros((SUB, R), f32)
    for j, v in enumerate(vals[:4]):
        as_row = jnp.sum(jnp.where(ri == ci, v, 0.0), axis=0, keepdims=True)
        plan = jnp.where(sub == j, as_row, plan)
    return slab, plan


def _router_weights(w_rg, b_rg, w_re, b_re):
    w = jnp.concatenate([w_rg, jnp.moveaxis(w_re, 0, 1).reshape(D_MODEL, MOE_EXPERTS)], axis=1)
    b = jnp.concatenate([b_rg, b_re.reshape(MOE_EXPERTS)])
    pad = LANES - w.shape[1]
    return jnp.pad(w, ((0, 0), (0, pad))), jnp.pad(b, (0, pad))[None]


SLAB = D_MODEL // LANES


def _to_slabs(ref, row0, x):
    for j in range(SLAB):
        ref[pl.ds(row0 * SLAB + j, x.shape[0], stride=SLAB), :] = x[:, j * LANES:(j + 1) * LANES]


def _from_slabs(ref, row0, n_rows, lead=()):
    return jnp.concatenate(
        [ref[(*lead, pl.ds(row0 * SLAB + j, n_rows, stride=SLAB), slice(None))] for j in range(SLAB)], axis=1)


def _mix_out_tail(x1, gffn_ref, wr_ref, br_ref, x1_ref, xn_ref, route_ref, plan_ref, cnt_ref):
    @pl.when(pl.program_id(0) == 0)
    def _():
        cnt_ref[...] = jnp.zeros_like(cnt_ref)

    x1_ref[...] = x1
    xn = _rms(x1, gffn_ref[...])
    _to_slabs(xn_ref, 0, xn)
    route_ref[...], plan_ref[...] = _route(xn, wr_ref[...], br_ref[...], cnt_ref)


def _gla_out_kernel(og_ref, xm_ref, xt_ref, wo_ref, gffn_ref, wr_ref, br_ref, *outs):
    x1 = _main_or_tail(MIX_TILE, xm_ref, xt_ref) + _dot(og_ref[...], wo_ref[...])
    _mix_out_tail(x1, gffn_ref, wr_ref, br_ref, *outs)


def _gelu_tanh(x):
    return x * (0.5 * (1.0 + jnp.tanh(0.7978845608028654 * (x + 0.044715 * (x * x * x)))))


def _s5_out_kernel(ys_main_ref, ys_tail_ref, x_ref, gmix_ref, d_ref, wglu_ref, bglu_ref, gffn_ref, wr_ref, br_ref,
                   *outs):
    x = x_ref[...]
    u = _rms(x, gmix_ref[...])
    y = _gelu_tanh(_main_or_tail(MIX_TILE, ys_main_ref, ys_tail_ref) + d_ref[...] * u)
    z = _dot(y.astype(bf16), wglu_ref[...]) + bglu_ref[...]
    _mix_out_tail(x + y * _sigmoid(z), gffn_ref, wr_ref, br_ref, *outs)


def _mix_out_call(kern, name, n, row_ins, row_maps, fix_ins):
    row = lambda i: (i, 0)
    fix = lambda i: (0, 0)
    out_row = lambda w: pl.BlockSpec((MIX_TILE, w), row)
    nt = n // MIX_TILE
    return pl.pallas_call(
        kern,
        grid=(nt,),
        in_specs=[pl.BlockSpec((MIX_TILE, a.shape[1]), m) for a, m in zip(row_ins, row_maps)]
        + [pl.BlockSpec(a.shape, fix) for a in fix_ins],
        out_specs=[out_row(D_MODEL), pl.BlockSpec((MIX_TILE * SLAB, LANES), row), out_row(LANES),
                   pl.BlockSpec((SUB, MIX_TILE), row), pl.BlockSpec((1, LANES), fix)],
        out_shape=[jax.ShapeDtypeStruct((n, D_MODEL), f32), jax.ShapeDtypeStruct((n * SLAB, LANES), f32),
                   jax.ShapeDtypeStruct((n, LANES), f32), jax.ShapeDtypeStruct((nt * SUB, MIX_TILE), f32),
                   jax.ShapeDtypeStruct((1, LANES), f32)],
        compiler_params=_cp(("arbitrary",)),
        name=name,
    )(*row_ins, *fix_ins)


EXPERT_TM = 256
MOVE_TILE = MIX_TILE
COMBINE_CHUNK = 32


def _moe_plan(plan, cnt, n):
    i32 = jnp.int32
    v = plan.reshape(n // MOVE_TILE, SUB, MOVE_TILE)[:, :4].astype(i32)
    counts = cnt[0, ROUTE_E0:ROUTE_E0 + MOE_EXPERTS].astype(i32)
    ends = jnp.cumsum(counts)
    off = ends - counts
    ids = jnp.arange(MOE_EXPERTS, dtype=i32)
    pos = jnp.sum(jnp.where(v[:, 0:2, :, None] == ids, off, 0), axis=-1) + v[:, 2:4]
    total = 2 * n
    n_tiles = total // EXPERT_TM
    n_items = n_tiles + MOE_EXPERTS
    inner = (counts > 0) & (off % EXPERT_TM != 0)
    keys = jnp.concatenate([jnp.arange(n_tiles, dtype=i32) * EXPERT_TM, jnp.where(inner, off, total)])
    idx = jnp.arange(n_items, dtype=i32)
    before = (keys[None, :] < keys[:, None]) | ((keys[None, :] == keys[:, None]) & (idx[None, :] < idx[:, None]))
    order = jnp.sum(before.astype(i32), axis=1)
    starts = jnp.sum(jnp.where(order[:, None] == idx[None, :], keys[:, None], 0), axis=0)
    stops = jnp.concatenate([starts[1:], jnp.full((1,), total, i32)])
    tile = starts // EXPERT_TM
    expert = jnp.sum((ends[None, :] <= starts[:, None]).astype(i32), axis=1)
    used = n_tiles + jnp.sum(inner.astype(i32))
    keep = jnp.minimum(idx, used - 1)
    pick = lambda a: a[keep]
    return (pos.reshape(n // MOVE_TILE, 1, 2 * MOVE_TILE), pick(tile), pick(jnp.minimum(expert, MOE_EXPERTS - 1)),
            pick(starts - tile * EXPERT_TM), pick(stops - tile * EXPERT_TM), used.reshape(1))


def _row_copy(src, src_row, dst, dst_row, sem):
    slab = lambda r: pl.ds(pl.multiple_of(r * SLAB, SLAB), SLAB)
    return pltpu.make_async_copy(src.at[slab(src_row)], dst.at[slab(dst_row)], sem)


def _dispatch_kernel(pos_ref, x_ref, xs_ref, sem):
    def copy(r, s):
        return _row_copy(x_ref, r, xs_ref, pos_ref[0, 0, s * MOVE_TILE + r], sem)

    def start(r, c):
        for s in range(2):
            copy(r, s).start(priority=s)
        return c

    def wait(r, c):
        for s in range(2):
            copy(r, s).wait()
        return c

    lax.fori_loop(0, MOVE_TILE, start, 0, unroll=8)
    lax.fori_loop(0, MOVE_TILE, wait, 0, unroll=8)


def _dispatch_call(xn, pos):
    n = xn.shape[0] // SLAB
    return pl.pallas_call(
        _dispatch_kernel,
        grid=(n // MOVE_TILE,),
        in_specs=[pl.BlockSpec((1, 1, 2 * MOVE_TILE), lambda i: (i, 0, 0), memory_space=pltpu.SMEM),
                  pl.BlockSpec((MOVE_TILE * SLAB, LANES), lambda i: (i, 0))],
        out_specs=pl.BlockSpec(memory_space=pl.ANY),
        out_shape=jax.ShapeDtypeStruct((2 * n * SLAB, LANES), xn.dtype),
        scratch_shapes=[pltpu.SemaphoreType.DMA(())],
        compiler_params=_cp(("arbitrary",)),
        name="moe_dispatch",
    )(pos, xn)


def _experts_kernel(it_ref, ie_ref, lo_ref, hi_ref, n_ref, xs_ref, wg_ref, wu_ref, wd_ref, os_ref,
                    wg_scr, wu_scr, wd_scr, acc_scr):
    i = pl.program_id(0)

    @pl.when(i < n_ref[0])
    def _():
        @pl.when((i == 0) | (ie_ref[i] != ie_ref[jnp.maximum(i - 1, 0)]))
        def _():
            wg_scr[...] = wg_ref[0].astype(bf16)
            wu_scr[...] = wu_ref[0].astype(bf16)
            wd_scr[...] = wd_ref[0].astype(bf16)

        x = _from_slabs(xs_ref, 0, EXPERT_TM).astype(bf16)
        hg = _dot(x, wg_scr[...])
        hu = _dot(x, wu_scr[...])
        out = _dot((hg * _sigmoid(hg) * hu).astype(bf16), wd_scr[...])
        lo = lo_ref[i]
        row = lax.broadcasted_iota(jnp.int32, (EXPERT_TM, 1), 0)
        mine = (row >= lo) & (row < hi_ref[i])

        @pl.when(lo == 0)
        def _():
            acc_scr[...] = jnp.where(mine, out, 0.0)

        @pl.when(lo != 0)
        def _():
            acc_scr[...] = jnp.where(mine, out, acc_scr[...])

        _to_slabs(os_ref, 0, acc_scr[...])


def _experts_call(xs, item_tile, item_expert, item_lo, item_hi, n_items, wg, wu, wd):
    rows = lambda i, it, ie, lo, hi, n: (it[i], 0)
    wsel = lambda i, it, ie, lo, hi, n: (ie[i], 0, 0)
    return pl.pallas_call(
        _experts_kernel,
        grid_spec=pltpu.PrefetchScalarGridSpec(
            num_scalar_prefetch=5,
            grid=(item_tile.shape[0],),
            in_specs=[pl.BlockSpec((EXPERT_TM * SLAB, LANES), rows),
                      pl.BlockSpec((1, D_MODEL, D_EXPERT), wsel),
                      pl.BlockSpec((1, D_MODEL, D_EXPERT), wsel),
                      pl.BlockSpec((1, D_EXPERT, D_MODEL), wsel)],
            out_specs=pl.BlockSpec((EXPERT_TM * SLAB, LANES), rows),
            scratch_shapes=[pltpu.VMEM((D_MODEL, D_EXPERT), bf16), pltpu.VMEM((D_MODEL, D_EXPERT), bf16),
                            pltpu.VMEM((D_EXPERT, D_MODEL), bf16), pltpu.VMEM((EXPERT_TM, D_MODEL), f32)],
        ),
        out_shape=jax.ShapeDtypeStruct(xs.shape, f32),
        compiler_params=_cp(("arbitrary",)),
        name="moe_experts",
    )(item_tile, item_expert, item_lo, item_hi, n_items, xs, wg, wu, wd)


def _combine_kernel(pos_ref, pos_next_ref, x_ref, route_ref, gn_ref, os_ref, *rest, emit_x, split):
    outs, (buf, sem, xn_scr) = rest[:-3], rest[-3:]
    i = pl.program_id(0)
    half = i % 2
    last = i == pl.num_programs(0) - 1

    def copy(p_ref, hf, r, s):
        return _row_copy(os_ref, p_ref[0, 0, s * MOVE_TILE + r], buf.at[hf, s], r, sem.at[hf])

    def loop_all(p_ref, hf, op):
        def body(r, c):
            for s in range(2):
                op(copy(p_ref, hf, r, s), s)
            return c

        lax.fori_loop(0, MOVE_TILE, body, 0, unroll=8)

    begin = lambda cp, s: cp.start(priority=s)
    finish = lambda cp, s: cp.wait()

    @pl.when(i == 0)
    def _():
        loop_all(pos_ref, 0, begin)

    loop_all(pos_ref, half, finish)
    xn_ref = xn_scr if split else outs[-1]
    for c in range(MOVE_TILE // COMBINE_CHUNK):
        rows = slice(c * COMBINE_CHUNK, (c + 1) * COMBINE_CHUNK)
        route = route_ref[rows, :]
        o0 = _from_slabs(buf, rows.start, COMBINE_CHUNK, lead=(half, 0))
        o1 = _from_slabs(buf, rows.start, COMBINE_CHUNK, lead=(half, 1))
        x2 = x_ref[rows, :] + route[:, 4:5] * o0 + route[:, 5:6] * o1
        if emit_x:
            outs[0][rows, :] = x2
        xn_ref[rows, :] = _rms(x2, gn_ref[...]).astype(xn_ref.dtype)
        for r in range(rows.start, rows.stop):
            for s in range(2):
                begin(copy(pos_next_ref, 1 - half, r, s), s)

    @pl.when(last)
    def _():
        loop_all(pos_next_ref, 1 - half, finish)

    if split:
        main_ref, tail_ref = outs[-2:]
        is_main = i < N_MAIN // MOVE_TILE

        @pl.when(is_main)
        def _():
            main_ref[...] = xn_scr[...]

        @pl.when(jnp.logical_not(is_main))
        def _():
            tail_ref[...] = xn_scr[...]


def _combine_call(x1, route, pos, os_rows, gn, xn_dtype, *, emit_x, split):
    n = x1.shape[0]
    row = lambda i: (i, 0)
    blk = lambda m: pl.BlockSpec((MOVE_TILE, D_MODEL), m)
    nm = N_MAIN // MOVE_TILE
    out_specs, out_shape = [], []
    if emit_x:
        out_specs.append(blk(row))
        out_shape.append(jax.ShapeDtypeStruct((n, D_MODEL), f32))
    if split:
        out_specs += [blk(lambda i: (jnp.minimum(i, nm - 1), 0)), blk(lambda i: (jnp.maximum(i - nm, 0), 0))]
        out_shape += [jax.ShapeDtypeStruct((N_MAIN, D_MODEL), xn_dtype),
                      jax.ShapeDtypeStruct((n - N_MAIN, D_MODEL), xn_dtype)]
    else:
        out_specs.append(blk(row))
        out_shape.append(jax.ShapeDtypeStruct((n, D_MODEL), xn_dtype))
    nt = n // MOVE_TILE
    pos_spec = lambda m: pl.BlockSpec((1, 1, 2 * MOVE_TILE), m, memory_space=pltpu.SMEM)
    return pl.pallas_call(
        functools.partial(_combine_kernel, emit_x=emit_x, split=split),
        grid=(nt,),
        in_specs=[pos_spec(lambda i: (i, 0, 0)), pos_spec(lambda i: (jnp.minimum(i + 1, nt - 1), 0, 0)),
                  blk(row), pl.BlockSpec((MOVE_TILE, LANES), row),
                  pl.BlockSpec((1, D_MODEL), lambda i: (0, 0)), pl.BlockSpec(memory_space=pl.ANY)],
        out_specs=out_specs,
        out_shape=out_shape,
        scratch_shapes=[pltpu.VMEM((2, 2, MOVE_TILE * SLAB, LANES), f32), pltpu.SemaphoreType.DMA((2,)),
                        pltpu.VMEM((MOVE_TILE, D_MODEL), xn_dtype)],
        compiler_params=_cp(("arbitrary",)),
        name="moe_combine",
    )(pos, pos, x1, route, gn, os_rows)


S5_GROUP = 16
S5_GROUPS = D_MODEL // S5_GROUP
S5_STATE = 64
S5_CB = 128
S5_NB = D_MODEL // S5_CB
S5_GPB = S5_CB // S5_GROUP
S5_SB = S5_GPB * S5_STATE
S5_BC = S5_STATE * S5_GROUP
S5_CPS = 2


def _s5_disc_kernel(lr_ref, li_ref, ldt_ref, bre_ref, bim_ref,
                    abr_ref, abi_ref, ab2r_ref, ab2i_ref, bbr_ref, bbi_ref, abbr_ref, abbi_ref):
    lr, li = lr_ref[...], li_ref[...]
    dt = jnp.exp(ldt_ref[...])
    mag = jnp.exp(lr * dt)
    ang = li * dt
    ab_re, ab_im = mag * jnp.cos(ang), mag * jnp.sin(ang)
    nr, ni = ab_re - 1.0, ab_im
    den = lr * lr + li * li
    f_re = (nr * lr + ni * li) / den
    f_im = (ni * lr - nr * li) / den
    abr_ref[...] = ab_re
    abi_ref[...] = ab_im
    ab2r_ref[...] = ab_re * ab_re - ab_im * ab_im
    ab2i_ref[...] = 2.0 * (ab_re * ab_im)
    pi = lax.broadcasted_iota(jnp.int32, (S5_STATE, S5_BC), 0)
    ci = lax.broadcasted_iota(jnp.int32, (S5_STATE, S5_BC), 1)
    rep = jnp.where(ci // S5_GROUP == pi, 1.0, 0.0).astype(bf16)

    def expand(v):
        hi, mid, lo = _split3(v)
        return _dot(hi, rep) + _dot(mid, rep) + _dot(lo, rep)

    fr, fi, ar, ai = expand(f_re), expand(f_im), expand(ab_re), expand(ab_im)
    br, bi = bre_ref[...], bim_ref[...]
    bb_re = fr * br - fi * bi
    bb_im = fr * bi + fi * br
    bbr_ref[...] = bb_re
    bbi_ref[...] = bb_im
    abbr_ref[...] = ar * bb_re - ai * bb_im
    abbi_ref[...] = ar * bb_im + ai * bb_re


def _s5_weights(lam_re, lam_im, log_dt, b_re, b_im, c_re, c_im):
    st = jax.ShapeDtypeStruct((S5_GROUPS, S5_STATE), f32)
    bc = jax.ShapeDtypeStruct((S5_GROUPS, S5_BC), f32)
    ab_re, ab_im, ab2_re, ab2_im, bb_re, bb_im, abb_re, abb_im = pl.pallas_call(
        _s5_disc_kernel, out_shape=[st, st, st, st, bc, bc, bc, bc], name="s5_discretize",
    )(lam_re, lam_im, log_dt[:, None], b_re.reshape(S5_GROUPS, S5_BC), b_im.reshape(S5_GROUPS, S5_BC))
    eye = jnp.eye(S5_GPB, dtype=f32)

    def in_blocks(m):
        m = m.reshape(S5_NB, S5_GPB, S5_STATE, S5_GROUP)
        return jnp.einsum("jgpc,gh->jgchp", m, eye).reshape(S5_NB, S5_CB, S5_SB)

    def out_blocks(m):
        m = m.reshape(S5_NB, S5_GPB, S5_GROUP, S5_STATE)
        return jnp.einsum("jgcp,gh->jgphc", m, eye).reshape(S5_NB, S5_SB, S5_CB)

    bb2 = jnp.concatenate([
        jnp.concatenate([in_blocks(bb_re), in_blocks(bb_im)], axis=2),
        jnp.concatenate([in_blocks(abb_re), in_blocks(abb_im)], axis=2)], axis=1).astype(bf16)
    cc = jnp.concatenate([out_blocks(c_re), -out_blocks(c_im)], axis=1).astype(bf16)
    rows = [v.reshape(S5_NB, 1, S5_SB) for v in (ab_re, ab_im, ab2_re, ab2_im)]
    abv = jnp.concatenate(rows + [jnp.zeros((S5_NB, SUB - len(rows), S5_SB), f32)], axis=1)
    return bb2, cc, abv


def _s5_seq_kernel(x0_ref, x1_ref, x2_ref, x3_ref, halo0_ref, h0_ref, bb2_ref, cc_ref, abv_ref, y_ref, hout_ref,
                   xf_scr, xp_scr, bu_scr, hs_scr, yp_scr, yn_scr, h_scr, halo_scr):
    tb = pl.program_id(1)
    TL = x0_ref.shape[0]
    KB = TL // 2
    RB = BATCH * TL

    @pl.when(tb == 0)
    def _():
        h_scr[...] = h0_ref[...]
        halo_scr[...] = halo0_ref[...].astype(f32)

    chans = [slice(c * S5_CB, (c + 1) * S5_CB) for c in range(S5_CPS)]
    for c, ch in enumerate(chans):
        for b, xb_ref in enumerate((x0_ref, x1_ref, x2_ref, x3_ref)):
            xf_scr[c, b * TL:(b + 1) * TL, :] = xb_ref[:, ch].astype(f32)
        for b in range(BATCH):
            for p in range(2):
                xp_scr[c, pl.ds(2 * b + p, KB, stride=SUB), :] = xf_scr[c, pl.ds(b * TL + p, KB, stride=2), :]
    x = jnp.concatenate([xp_scr[c] for c in range(S5_CPS)], axis=1)
    xc = jnp.concatenate([halo_scr[...], x], axis=0)
    odd = (lax.broadcasted_iota(jnp.int32, (RB + SUB, 1), 0) & 1) == 1
    xprev = jnp.where(odd, pltpu.roll(xc, 1, axis=0), pltpu.roll(xc, SUB - 1, axis=0))[SUB:]
    halo_scr[...] = x[RB - SUB:]
    for c, ch in enumerate(chans):
        lhs = jnp.concatenate([x[:, ch], xprev[:, ch]], axis=1).astype(bf16)
        bu_scr[c] = _dot(lhs, bb2_ref[c])
    a2 = [(abv_ref[c, 2:3, :], abv_ref[c, 3:4, :]) for c in range(S5_CPS)]

    def step(k, carry):
        r0 = pl.multiple_of(k * SUB, SUB)
        out = []
        for c in range(S5_CPS):
            hr, hi = carry[2 * c], carry[2 * c + 1]
            a2r, a2i = a2[c]
            bu = bu_scr[c, pl.ds(r0, SUB), :]
            nr = a2r * hr - a2i * hi + bu[:, :S5_SB]
            ni = a2r * hi + a2i * hr + bu[:, S5_SB:]
            hs_scr[c, pl.ds(r0, SUB), :S5_SB] = nr
            hs_scr[c, pl.ds(r0, SUB), S5_SB:] = ni
            out += [nr, ni]
        return tuple(out)

    init = []
    for c in range(S5_CPS):
        init += [h_scr[c, :, :S5_SB], h_scr[c, :, S5_SB:]]
    fin = lax.fori_loop(0, RB // SUB, step, tuple(init), unroll=4)
    for c, ch in enumerate(chans):
        h_scr[c, :, :S5_SB] = fin[2 * c]
        h_scr[c, :, S5_SB:] = fin[2 * c + 1]
        yp_scr[c] = _dot(hs_scr[c].astype(bf16), cc_ref[c])
        for b in range(BATCH):
            for p in range(2):
                yn_scr[c, pl.ds(b * TL + p, KB, stride=2), :] = yp_scr[c, pl.ds(2 * b + p, KB, stride=SUB), :]
            y_ref[b, :, ch] = yn_scr[c, b * TL:(b + 1) * TL, :]

    @pl.when(tb == pl.num_programs(1) - 1)
    def _():
        hout_ref[...] = h_scr[...]


def _s5_seq_call(x, halo0, h0, bb2, cc, abv, *, row0, seq_len, tl):
    wsel = lambda j, t: (j, 0, 0)
    rb = BATCH * tl
    cw = S5_CPS * S5_CB
    xspec = lambda b: pl.BlockSpec((tl, cw), lambda j, t: ((row0 + b * seq_len) // tl + t, j))
    return pl.pallas_call(
        _s5_seq_kernel,
        grid=(S5_NB // S5_CPS, seq_len // tl),
        in_specs=[xspec(b) for b in range(BATCH)]
        + [pl.BlockSpec((SUB, cw), lambda j, t: (0, j)),
           pl.BlockSpec((S5_CPS, SUB, 2 * S5_SB), wsel),
           pl.BlockSpec((S5_CPS, 2 * S5_CB, 2 * S5_SB), wsel),
           pl.BlockSpec((S5_CPS, 2 * S5_SB, S5_CB), wsel),
           pl.BlockSpec((S5_CPS, SUB, S5_SB), wsel)],
        out_specs=[pl.BlockSpec((BATCH, tl, cw), lambda j, t: (0, t, j)),
                   pl.BlockSpec((S5_CPS, SUB, 2 * S5_SB), wsel)],
        out_shape=[jax.ShapeDtypeStruct((BATCH, seq_len, D_MODEL), f32),
                   jax.ShapeDtypeStruct((S5_NB, SUB, 2 * S5_SB), f32)],
        scratch_shapes=[pltpu.VMEM((S5_CPS, rb, S5_CB), f32), pltpu.VMEM((S5_CPS, rb, S5_CB), f32),
                        pltpu.VMEM((S5_CPS, rb, 2 * S5_SB), f32), pltpu.VMEM((S5_CPS, rb, 2 * S5_SB), f32),
                        pltpu.VMEM((S5_CPS, rb, S5_CB), f32), pltpu.VMEM((S5_CPS, rb, S5_CB), f32),
                        pltpu.VMEM((S5_CPS, SUB, 2 * S5_SB), f32), pltpu.VMEM((SUB, cw), f32)],
        compiler_params=_cp(("arbitrary", "arbitrary")),
        name=f"s5_seq_{seq_len}",
    )(x, x, x, x, halo0, h0, bb2, cc, abv)


def _s5_sample_kernel(x_ref, hre_ref, him_ref, bb2_ref, cc_ref, abv_ref, y_ref, ore_ref, oim_ref, hs_scr):
    nb = hre_ref.shape[0]
    bu = _dot(x_ref[...], bb2_ref[0, :S5_CB, :])
    ar = abv_ref[0, 0:1, :]
    ai = abv_ref[0, 1:2, :]
    hr, hi = hre_ref[...], him_ref[...]
    for t in range(DEC_SEQ):
        rows = slice(t * nb, (t + 1) * nb)
        hr, hi = (ar * hr - ai * hi + bu[rows, :S5_SB], ar * hi + ai * hr + bu[rows, S5_SB:])
        hs_scr[rows, :S5_SB] = hr
        hs_scr[rows, S5_SB:] = hi
    y_ref[...] = _dot(hs_scr[...].astype(bf16), cc_ref[0])
    ore_ref[...] = hr
    oim_ref[...] = hi


def _s5_sample_call(xt, h_re, h_im, bb2, cc, abv):
    n = xt.shape[0]
    nb = h_re.shape[0]
    wsel = lambda j: (j, 0, 0)
    st = pl.BlockSpec((nb, S5_SB), lambda j: (0, j))
    return pl.pallas_call(
        _s5_sample_kernel,
        grid=(S5_NB,),
        in_specs=[pl.BlockSpec((n, S5_CB), lambda j: (0, j)), st, st,
                  pl.BlockSpec((1, 2 * S5_CB, 2 * S5_SB), wsel),
                  pl.BlockSpec((1, 2 * S5_SB, S5_CB), wsel),
                  pl.BlockSpec((1, SUB, S5_SB), wsel)],
        out_specs=[pl.BlockSpec((n, S5_CB), lambda j: (0, j)), st, st],
        out_shape=[jax.ShapeDtypeStruct((n, D_MODEL), f32),
                   jax.ShapeDtypeStruct(h_re.shape, f32), jax.ShapeDtypeStruct(h_im.shape, f32)],
        scratch_shapes=[pltpu.VMEM((n, 2 * S5_SB), f32)],
        compiler_params=_cp(("arbitrary",)),
        name="s5_sample",
    )(xt, h_re, h_im, bb2, cc, abv)


GLA_CHUNK = 256
GLA_DIRECT = 2
S5_TL = 256


def _moe_layer(layer, x1, xnf, route, plan, cnt, wg, wu, wd, gn, xn_dtype, *, emit_x, split):
    pos, item_tile, item_expert, item_lo, item_hi, n_items = _moe_plan(plan, cnt, x1.shape[0])
    xs = _dispatch_call(xnf, pos)
    os_rows = _experts_call(xs, item_tile, item_expert + layer * MOE_EXPERTS, item_lo, item_hi, n_items, wg, wu, wd)
    return _combine_call(x1, route, pos, os_rows, gn, xn_dtype, emit_x=emit_x, split=split)


def kernel(x_prompt, x_sample, state_gla, state_s5_re, state_s5_im, meta_tokens, norm_mix_g, norm_ffn_g, norm_final_g, gla_w_in, gla_w_a2, gla_b_a, gla_g_o, gla_w_o, s5_lambda_re, s5_lambda_im, s5_log_dt, s5_b_re, s5_b_im, s5_c_re, s5_c_im, s5_d, s5_w_glu, s5_b_glu, moe_w_rg, moe_b_rg, moe_w_re, moe_b_re, moe_w_gate, moe_w_up, moe_w_down):
    row = lambda v: v.reshape(1, -1)
    x_main = x_prompt.reshape(N_MAIN, D_MODEL)
    x_tail = jnp.concatenate([
        jnp.tile(meta_tokens.astype(x_prompt.dtype), (BATCH, 1)),
        x_sample.reshape(N_SAMPLE, D_MODEL),
        jnp.zeros((N_ROWS - N_REAL, D_MODEL), x_prompt.dtype)], axis=0)
    wg = moe_w_gate.reshape(-1, D_MODEL, D_EXPERT)
    wu = moe_w_up.reshape(-1, D_MODEL, D_EXPERT)
    wd = moe_w_down.reshape(-1, D_EXPERT, D_MODEL)

    w_in = jnp.swapaxes(gla_w_in, 1, 2)
    wa1 = jnp.pad(gla_w_in[0, :, GLA_QKVR:], ((0, 0), (0, LANES - GLA_RANK)))
    wa2 = jnp.pad(gla_w_a2.reshape(GLA_RANK, GLA_KEY), ((0, LANES - GLA_RANK), (0, 0)))
    xn, glog = _norm_gate_call(x_main, x_tail, row(norm_mix_g[0]), wa1, wa2, row(gla_b_a))
    proj = _proj_call(xn, w_in, GLA_QKVR)
    go = row(gla_g_o)
    og = jnp.zeros((N_ROWS, GLA_VAL), bf16)
    s_zero = jnp.zeros((BATCH, GLA_HEADS, GLA_DK, GLA_DV), f32)
    og, s_meta = _gla_seq_call(proj, glog, go, s_zero, og, row0=ROW_META, C=N_META, n_chunks=1, d=GLA_DIRECT)
    og, s_prompt = _gla_seq_call(proj, glog, go, s_meta, og, row0=0, C=GLA_CHUNK, n_chunks=SEQ // GLA_CHUNK,
                                 d=GLA_DIRECT)
    og, s_sample = _gla_sample_call(proj, glog, go, state_gla.reshape(DEC_BATCH, GLA_HEADS, GLA_DK, GLA_DV), og,
                                    row0=ROW_SAMPLE)
    wr, br = _router_weights(moe_w_rg[0], moe_b_rg[0], moe_w_re[0], moe_b_re[0])
    tile_row = lambda i: (i, 0)
    x1, xnf, route, plan, cnt = _mix_out_call(
        _gla_out_kernel, "gla_out", N_ROWS, [og, x_main, x_tail], [tile_row, *_main_tail_maps(MIX_TILE)],
        [gla_w_o.reshape(GLA_VAL, D_MODEL).astype(bf16), row(norm_ffn_g[0]), wr, br])
    x2, xn2 = _moe_layer(0, x1, xnf, route, plan, cnt, wg, wu, wd, row(norm_mix_g[1]), bf16, emit_x=True, split=False)

    bb2, cc, abv = _s5_weights(s5_lambda_re[0], s5_lambda_im[0], s5_log_dt[0], s5_b_re[0], s5_b_im[0],
                               s5_c_re[0], s5_c_im[0])
    y_meta, h_meta = _s5_seq_call(xn2, jnp.zeros((SUB, D_MODEL), bf16), jnp.zeros((S5_NB, SUB, 2 * S5_SB), f32),
                                  bb2, cc, abv, row0=ROW_META, seq_len=N_META, tl=N_META)
    halo = xn2[ROW_META:ROW_SAMPLE].reshape(BATCH, N_META, D_MODEL)[:, N_META - 2:].reshape(SUB, D_MODEL)
    y_main, h_main = _s5_seq_call(xn2, halo, h_meta, bb2, cc, abv, row0=0, seq_len=SEQ, tl=S5_TL)
    xt_sample = xn2[ROW_SAMPLE:N_REAL].reshape(DEC_BATCH, DEC_SEQ, D_MODEL).transpose(1, 0, 2).reshape(N_SAMPLE, D_MODEL)
    y_samp, s5r_s, s5i_s = _s5_sample_call(
        xt_sample, state_s5_re.reshape(DEC_BATCH, S5_GROUPS * S5_STATE),
        state_s5_im.reshape(DEC_BATCH, S5_GROUPS * S5_STATE), bb2, cc, abv)
    ys_tail = jnp.concatenate([
        y_meta.reshape(N_METAROWS, D_MODEL),
        y_samp.reshape(DEC_SEQ, DEC_BATCH, D_MODEL).transpose(1, 0, 2).reshape(N_SAMPLE, D_MODEL),
        jnp.zeros((N_ROWS - N_REAL, D_MODEL), f32)], axis=0)
    wr, br = _router_weights(moe_w_rg[1], moe_b_rg[1], moe_w_re[1], moe_b_re[1])
    x3, xnf, route, plan, cnt = _mix_out_call(
        _s5_out_kernel, "s5_out", N_ROWS, [y_main.reshape(N_MAIN, D_MODEL), ys_tail, x2],
        [*_main_tail_maps(MIX_TILE), tile_row],
        [row(norm_mix_g[1]), row(s5_d), s5_w_glu.reshape(D_MODEL, D_MODEL).astype(bf16), row(s5_b_glu),
         row(norm_ffn_g[1]), wr, br])
    y_main_out, y_tail_out = _moe_layer(1, x3, xnf, route, plan, cnt, wg, wu, wd, row(norm_final_g), f32,
                                        emit_x=False, split=True)

    y_prompt = y_main_out.reshape(BATCH, SEQ, D_MODEL)
    y_sample = y_tail_out[N_METAROWS:N_METAROWS + N_SAMPLE].reshape(DEC_BATCH, DEC_SEQ, D_MODEL)
    hfin = h_main.reshape(S5_NB, BATCH, 2, 2, S5_GPB, S5_STATE)[:, :, 1]
    s5r_p = hfin[:, :, 0].transpose(1, 0, 2, 3).reshape(1, BATCH, S5_GROUPS, S5_STATE)
    s5i_p = hfin[:, :, 1].transpose(1, 0, 2, 3).reshape(1, BATCH, S5_GROUPS, S5_STATE)
    return (y_prompt, y_sample, s_prompt[None], s5r_p, s5i_p, s_sample[None],
            s5r_s.reshape(1, DEC_BATCH, S5_GROUPS, S5_STATE), s5i_s.reshape(1, DEC_BATCH, S5_GROUPS, S5_STATE))
```

```python
import functools

import jax
import jax.numpy as jnp
from jax import lax
from jax.experimental import pallas as pl
from jax.experimental.pallas import tpu as pltpu

f32 = jnp.float32
bf16 = jnp.bfloat16

D_MODEL = 2048
BATCH = 4
SEQ = 2048
DEC_BATCH = 128
DEC_SEQ = 4
N_META = 16
EPS = 1e-6
GLA_HEADS = 4
GLA_DK = 256
GLA_DV = 512
GLA_KEY = GLA_HEADS * GLA_DK
GLA_VAL = GLA_HEADS * GLA_DV
GLA_RANK = 16
GLA_TAU = 16.0
GLA_QKVR = 2 * GLA_KEY + 2 * GLA_VAL

N_MAIN = BATCH * SEQ
N_METAROWS = BATCH * N_META
N_SAMPLE = DEC_BATCH * DEC_SEQ
ROW_META = N_MAIN
ROW_SAMPLE = N_MAIN + N_METAROWS
N_REAL = ROW_SAMPLE + N_SAMPLE
ROW_TILE = 256
N_ROWS = -(-N_REAL // ROW_TILE) * ROW_TILE

VMEM_LIMIT = 56 * 1024 * 1024


def _cp(sem, vmem=VMEM_LIMIT):
    return pltpu.CompilerParams(dimension_semantics=sem, vmem_limit_bytes=vmem)


def _dot(a, b):
    return jnp.dot(a, b, preferred_element_type=f32)


def _dot_nt(a, b):
    return lax.dot_general(a, b, (((1,), (1,)), ((), ())), preferred_element_type=f32)


def _dot_tn(a, b):
    return lax.dot_general(a, b, (((0,), (0,)), ((), ())), preferred_element_type=f32)


def _sigmoid(x):
    return 1.0 / (1.0 + jnp.exp(-x))


def _split3(x):
    hi = x.astype(bf16)
    r1 = x - hi.astype(f32)
    mid = r1.astype(bf16)
    lo = (r1 - mid.astype(f32)).astype(bf16)
    return hi, mid, lo


def _cumsum_rows(g, C):
    if C <= 16:
        row = lax.broadcasted_iota(jnp.int32, (C, 1), 0)
        b = jnp.zeros_like(g)
        for s in range(C):
            b = b + jnp.where(row >= s, g[s:s + 1, :], 0.0)
        return b
    row = lax.broadcasted_iota(jnp.int32, (C, C), 0)
    col = lax.broadcasted_iota(jnp.int32, (C, C), 1)
    tri = jnp.where(row >= col, 1.0, 0.0).astype(bf16)
    hi, mid, lo = _split3(g)
    return _dot(tri, hi) + _dot(tri, mid) + _dot(tri, lo)


PAIR_LEVEL = 1000


def _gla_pair_code(C, d, tree=True):
    ti = lax.broadcasted_iota(jnp.int32, (C, C), 0)
    si = lax.broadcasted_iota(jnp.int32, (C, C), 1)
    code = jnp.where((ti // d == si // d) & (si <= ti), 1 + ti - si, 0)
    h = d
    while tree and h < C:
        tb = ti // h
        code = jnp.where(((tb % 2) == 1) & ((si // h) == tb - 1), PAIR_LEVEL + h, code)
        h *= 2
    return code


def _gla_scores(q, k, b, code, C, d, tree=True):
    row = lax.broadcasted_iota(jnp.int32, (C, 1), 0)
    scores = jnp.zeros((C, C), f32)
    for dl in range(d):
        ks = k if dl == 0 else pltpu.roll(k, dl, axis=0)
        bs = b if dl == 0 else pltpu.roll(b, dl, axis=0)
        term = q * ks * jnp.exp(jnp.minimum(b - bs, 0.0))
        colv = jnp.sum(term, axis=1, keepdims=True)
        scores = jnp.where(code == 1 + dl, colv, scores)
    z = b
    s = 1
    while tree and 2 * s < C:
        z = jnp.where((row & s) != 0, pltpu.roll(z, s, axis=0), z)
        s *= 2
        h = s
        if h < d:
            continue
        bnext = pltpu.roll(z, C - h, axis=0)
        qh = (q * jnp.exp(jnp.minimum(b - z, 0.0))).astype(bf16)
        kh = (k * jnp.exp(jnp.minimum(bnext - b, 0.0))).astype(bf16)
        scores = jnp.where(code == PAIR_LEVEL + h, _dot_nt(qh, kh), scores)
    return scores


def _gla_chunk(q, k, v, g, S, code, C, d, tree=True):
    b = _cumsum_rows(g, C)
    o = _dot((q * jnp.exp(b)).astype(bf16), S.astype(bf16))
    scores = _gla_scores(q, k, b, code, C, d, tree)
    o = o + _dot(scores.astype(bf16), v)
    b_last = b[C - 1:C, :]
    kd = (k * jnp.exp(b_last - b)).astype(bf16)
    if C == GLA_DK:
        eye = code == 1
    else:
        eye = (lax.broadcasted_iota(jnp.int32, (GLA_DK, GLA_DK), 0)
               == lax.broadcasted_iota(jnp.int32, (GLA_DK, GLA_DK), 1))
    dec_col = jnp.sum(jnp.where(eye, jnp.exp(b_last), 0.0), axis=1, keepdims=True)
    S_new = dec_col * S + _dot_tn(kd, v)
    return o, S_new


def _gla_head_epilogue(o, r, go):
    ms = jnp.mean(o * o, axis=1, keepdims=True)
    on = o * lax.rsqrt(ms + EPS) * go
    return on * (r * _sigmoid(r))


def _gla_seq_kernel(q_ref, k_ref, v_ref, r_ref, g_ref, go_ref, s0_ref, _og_in, og_ref, sout_ref, s_scr, *, C, d):
    c = pl.program_id(1)

    @pl.when(c == 0)
    def _():
        s_scr[...] = s0_ref[0]

    code = _gla_pair_code(C, d)

    def head(h, carry):
        ck = pl.ds(pl.multiple_of(h * GLA_DK, GLA_DK), GLA_DK)
        cv = pl.ds(pl.multiple_of(h * GLA_DV, GLA_DV), GLA_DV)
        q = q_ref[:, ck].astype(f32) * (GLA_DK ** -0.5)
        k = k_ref[:, ck].astype(f32)
        o, S_new = _gla_chunk(q, k, v_ref[:, cv], g_ref[:, ck], s_scr[h], code, C, d)
        s_scr[h] = S_new
        og_ref[:, cv] = _gla_head_epilogue(o, r_ref[:, cv].astype(f32), go_ref[:, cv]).astype(og_ref.dtype)
        return carry

    lax.fori_loop(0, GLA_HEADS, head, 0)

    @pl.when(c == pl.num_programs(1) - 1)
    def _():
        sout_ref[0] = s_scr[...]


def _gla_seq_call(proj, glog, go, s0, og_buf, *, row0, C, n_chunks, d):
    blk0 = row0 // C
    rows = lambda b, c: blk0 + b * n_chunks + c
    kern = functools.partial(_gla_seq_kernel, C=C, d=d)
    return pl.pallas_call(
        kern,
        grid=(BATCH, n_chunks),
        in_specs=[
            pl.BlockSpec((C, GLA_KEY), lambda b, c: (rows(b, c), 0)),
            pl.BlockSpec((C, GLA_KEY), lambda b, c: (rows(b, c), 1)),
            pl.BlockSpec((C, GLA_VAL), lambda b, c: (rows(b, c), 1)),
            pl.BlockSpec((C, GLA_VAL), lambda b, c: (rows(b, c), 2)),
            pl.BlockSpec((C, GLA_KEY), lambda b, c: (rows(b, c), 0)),
            pl.BlockSpec((1, GLA_VAL), lambda b, c: (0, 0)),
            pl.BlockSpec((1, GLA_HEADS, GLA_DK, GLA_DV), lambda b, c: (b, 0, 0, 0)),
            pl.BlockSpec(memory_space=pl.ANY),
        ],
        out_specs=[
            pl.BlockSpec((C, GLA_VAL), lambda b, c: (rows(b, c), 0)),
            pl.BlockSpec((1, GLA_HEADS, GLA_DK, GLA_DV), lambda b, c: (b, 0, 0, 0)),
        ],
        out_shape=[
            jax.ShapeDtypeStruct(og_buf.shape, og_buf.dtype),
            jax.ShapeDtypeStruct((BATCH, GLA_HEADS, GLA_DK, GLA_DV), f32),
        ],
        scratch_shapes=[pltpu.VMEM((GLA_HEADS, GLA_DK, GLA_DV), f32)],
        input_output_aliases={7: 0},
        compiler_params=_cp(("arbitrary", "arbitrary")),
        name=f"gla_seq_c{C}",
    )(proj, proj, proj, proj, glog, go, s0, og_buf)


SAMPLE_BB = 4
SAMPLE_C = SAMPLE_BB * DEC_SEQ


def _gla_sample_kernel(q_ref, k_ref, v_ref, r_ref, g_ref, go_ref, s0_ref, _og_in, og_ref, sout_ref):
    row = lax.broadcasted_iota(jnp.int32, (SAMPLE_C, 1), 0)

    code = _gla_pair_code(SAMPLE_C, DEC_SEQ, tree=False)

    def take(x, bb):
        sh = (SAMPLE_C - DEC_SEQ * bb) % SAMPLE_C
        return jnp.where(row < DEC_SEQ, pltpu.roll(x, sh, axis=0) if sh else x, 0.0)

    def head(h, carry):
        ck = pl.ds(pl.multiple_of(h * GLA_DK, GLA_DK), GLA_DK)
        cv = pl.ds(pl.multiple_of(h * GLA_DV, GLA_DV), GLA_DV)
        q_all = q_ref[:, ck].astype(f32) * (GLA_DK ** -0.5)
        k_all = k_ref[:, ck].astype(f32)
        v_all = v_ref[:, cv].astype(f32)
        r_all = r_ref[:, cv].astype(f32)
        g_all = g_ref[:, ck]
        go = go_ref[:, cv]
        acc = jnp.zeros((SAMPLE_C, GLA_DV), f32)
        for bb in range(SAMPLE_BB):
            o, S_new = _gla_chunk(take(q_all, bb), take(k_all, bb), take(v_all, bb).astype(bf16),
                                  take(g_all, bb), s0_ref[bb, h], code, SAMPLE_C, DEC_SEQ, tree=False)
            sout_ref[bb, h] = S_new
            y = _gla_head_epilogue(o, take(r_all, bb), go)
            acc = jnp.where(row // DEC_SEQ == bb, pltpu.roll(y, DEC_SEQ * bb, axis=0) if bb else y, acc)
        og_ref[:, cv] = acc.astype(og_ref.dtype)
        return carry

    lax.fori_loop(0, GLA_HEADS, head, 0)


def _gla_sample_call(proj, glog, go, s0, og_buf, *, row0):
    n_seq = s0.shape[0]
    blk0 = row0 // SAMPLE_C
    st_spec = pl.BlockSpec((SAMPLE_BB, GLA_HEADS, GLA_DK, GLA_DV), lambda i: (i, 0, 0, 0))
    return pl.pallas_call(
        _gla_sample_kernel,
        grid=(n_seq // SAMPLE_BB,),
        in_specs=[
            pl.BlockSpec((SAMPLE_C, GLA_KEY), lambda i: (blk0 + i, 0)),
            pl.BlockSpec((SAMPLE_C, GLA_KEY), lambda i: (blk0 + i, 1)),
            pl.BlockSpec((SAMPLE_C, GLA_VAL), lambda i: (blk0 + i, 1)),
            pl.BlockSpec((SAMPLE_C, GLA_VAL), lambda i: (blk0 + i, 2)),
            pl.BlockSpec((SAMPLE_C, GLA_KEY), lambda i: (blk0 + i, 0)),
            pl.BlockSpec((1, GLA_VAL), lambda i: (0, 0)),
            st_spec,
            pl.BlockSpec(memory_space=pl.ANY),
        ],
        out_specs=[pl.BlockSpec((SAMPLE_C, GLA_VAL), lambda i: (blk0 + i, 0)), st_spec],
        out_shape=[jax.ShapeDtypeStruct(og_buf.shape, og_buf.dtype), jax.ShapeDtypeStruct(s0.shape, f32)],
        input_output_aliases={7: 0},
        compiler_params=_cp(("arbitrary",)),
        name="gla_sample",
    )(proj, proj, proj, proj, glog, go, s0, og_buf)


LANES = 128
SUB = 8
MOE_GROUPS = 4
MOE_EPG = 8
MOE_EXPERTS = MOE_GROUPS * MOE_EPG
D_EXPERT = 256
ROUTE_E0 = MOE_GROUPS
MIX_TILE = 256
PROJ_NT = 1024
PROJ_MT = N_ROWS // 7


def _rms(x, g):
    r = lax.rsqrt(jnp.mean(x * x, axis=-1, keepdims=True) + EPS)
    return (x * r) * g


def _log_sigmoid(z):
    return jnp.minimum(z, 0.0) - jnp.log1p(jnp.exp(-jnp.abs(z)))


def _main_or_tail(tile, main_ref, tail_ref):
    return jnp.where(pl.program_id(0) < N_MAIN // tile, main_ref[...], tail_ref[...])


def _main_tail_maps(tile):
    nm = N_MAIN // tile
    return (lambda i: (jnp.minimum(i, nm - 1), 0)), (lambda i: (jnp.maximum(i - nm, 0), 0))


def _norm_gate_kernel(xm_ref, xt_ref, gn_ref, wa1_ref, wa2_ref, ba_ref, xn_ref, gl_ref):
    xnb = _rms(_main_or_tail(ROW_TILE, xm_ref, xt_ref), gn_ref[...]).astype(bf16)
    xn_ref[...] = xnb
    a = _dot(xnb, wa1_ref[...].astype(bf16))
    z = _dot(a.astype(bf16), wa2_ref[...].astype(bf16)) + ba_ref[...]
    gl_ref[...] = _log_sigmoid(z) * (1.0 / GLA_TAU)


def _norm_gate_call(x_main, x_tail, gn, wa1, wa2, ba):
    n = x_main.shape[0] + x_tail.shape[0]
    row = lambda i: (i, 0)
    fix = lambda i: (0, 0)
    main_map, tail_map = _main_tail_maps(ROW_TILE)
    return pl.pallas_call(
        _norm_gate_kernel,
        grid=(n // ROW_TILE,),
        in_specs=[pl.BlockSpec((ROW_TILE, D_MODEL), main_map), pl.BlockSpec((ROW_TILE, D_MODEL), tail_map),
                  pl.BlockSpec((1, D_MODEL), fix),
                  pl.BlockSpec((D_MODEL, LANES), fix), pl.BlockSpec((LANES, GLA_KEY), fix),
                  pl.BlockSpec((1, GLA_KEY), fix)],
        out_specs=[pl.BlockSpec((ROW_TILE, D_MODEL), row), pl.BlockSpec((ROW_TILE, GLA_KEY), row)],
        out_shape=[jax.ShapeDtypeStruct((n, D_MODEL), bf16), jax.ShapeDtypeStruct((n, GLA_KEY), f32)],
        compiler_params=_cp(("arbitrary",)),
        name="norm_gate",
    )(x_main, x_tail, gn, wa1, wa2, ba)


def _proj_kernel(xn_ref, w_ref, o_ref, wb_scr):
    @pl.when(pl.program_id(1) == 0)
    def _():
        wb_scr[...] = w_ref[0].astype(bf16)

    o_ref[...] = _dot_nt(xn_ref[...], wb_scr[...]).astype(o_ref.dtype)


def _proj_call(xn, wt, n_cols):
    n = xn.shape[0]
    return pl.pallas_call(
        _proj_kernel,
        grid=(n_cols // PROJ_NT, n // PROJ_MT),
        in_specs=[pl.BlockSpec((PROJ_MT, D_MODEL), lambda j, i: (i, 0)),
                  pl.BlockSpec((1, PROJ_NT, D_MODEL), lambda j, i: (0, j, 0))],
        out_specs=pl.BlockSpec((PROJ_MT, PROJ_NT), lambda j, i: (i, j)),
        out_shape=jax.ShapeDtypeStruct((n, n_cols), bf16),
        scratch_shapes=[pltpu.VMEM((PROJ_NT, D_MODEL), bf16)],
        compiler_params=_cp(("arbitrary", "arbitrary")),
        name="gla_proj",
    )(xn, wt)


def _route(xn, wr, br, cnt_ref):
    R = xn.shape[0]
    xh = xn.astype(bf16)
    xl = (xn - xh.astype(f32)).astype(bf16)
    wh = wr.astype(bf16)
    wl = (wr - wh.astype(f32)).astype(bf16)
    logits = _dot(xh, wh) + _dot(xl, wh) + _dot(xh, wl) + br
    lane_i = lax.broadcasted_iota(jnp.int32, (R, LANES), 1)
    lane = lane_i.astype(f32)
    neg = -jnp.inf
    big = float(LANES)
    is_g = lane_i < MOE_GROUPS
    lg = jnp.where(is_g, logits, neg)
    mg = jnp.max(lg, axis=1, keepdims=True)
    gidx = jnp.min(jnp.where(lg == mg, lane, big), axis=1, keepdims=True)
    ptop = 1.0 / jnp.sum(jnp.where(is_g, jnp.exp(logits - mg), 0.0), axis=1, keepdims=True)
    lo = ROUTE_E0 + MOE_EPG * gidx
    le = jnp.where((lane >= lo) & (lane < lo + MOE_EPG), logits, neg)
    v1 = jnp.max(le, axis=1, keepdims=True)
    i1 = jnp.min(jnp.where(le == v1, lane, big), axis=1, keepdims=True)
    le2 = jnp.where(lane == i1, neg, le)
    v2 = jnp.max(le2, axis=1, keepdims=True)
    i2 = jnp.min(jnp.where(le2 == v2, lane, big), axis=1, keepdims=True)
    s = jnp.exp(v2 - v1)
    w0 = ptop / (1.0 + s)
    w1 = ptop * s / (1.0 + s)
    oh = jnp.where((lane == i1) | (lane == i2), 1.0, 0.0)
    ri = lax.broadcasted_iota(jnp.int32, (R, R), 0)
    ci = lax.broadcasted_iota(jnp.int32, (R, R), 1)
    before = jnp.where(ri > ci, 1.0, 0.0).astype(bf16)
    tot = _dot(before, oh.astype(bf16)) + cnt_ref[...]
    rank0 = jnp.sum(jnp.where(lane == i1, tot, 0.0), axis=1, keepdims=True)
    rank1 = jnp.sum(jnp.where(lane == i2, tot, 0.0), axis=1, keepdims=True)
    cnt_ref[...] = cnt_ref[...] + jnp.sum(oh, axis=0, keepdims=True)
    vals = (i1 - ROUTE_E0, i2 - ROUTE_E0, rank0, rank1, w0, w1)
    slab = jnp.zeros((R, LANES), f32)
    for j, v in enumerate(vals):
        slab = jnp.where(lane_i == j, v, slab)
    sub = lax.broadcasted_iota(jnp.int32, (SUB, R), 0)
    plan = jnp.zeros((SUB, R), f32)
    for j, v in enumerate(vals[:4]):
        as_row = jnp.sum(jnp.where(ri == ci, v, 0.0), axis=0, keepdims=True)
        plan = jnp.where(sub == j, as_row, plan)
    return slab, plan


def _router_weights(w_rg, b_rg, w_re, b_re):
    w = jnp.concatenate([w_rg, jnp.moveaxis(w_re, 0, 1).reshape(D_MODEL, MOE_EXPERTS)], axis=1)
    b = jnp.concatenate([b_rg, b_re.reshape(MOE_EXPERTS)])
    pad = LANES - w.shape[1]
    return jnp.pad(w, ((0, 0), (0, pad))), jnp.pad(b, (0, pad))[None]


def _mix_out_tail(x1, gffn_ref, wr_ref, br_ref, x1_ref, xn_ref, route_ref, plan_ref, cnt_ref):
    @pl.when(pl.program_id(0) == 0)
    def _():
        cnt_ref[...] = jnp.zeros_like(cnt_ref)

    x1_ref[...] = x1
    xn = _rms(x1, gffn_ref[...])
    xn_ref[...] = xn
    route_ref[...], plan_ref[...] = _route(xn, wr_ref[...], br_ref[...], cnt_ref)


def _cast_once(w_ref, w_scr):
    @pl.when(pl.program_id(0) == 0)
    def _():
        w_scr[...] = w_ref[...].astype(bf16)


def _gla_out_kernel(og_ref, xm_ref, xt_ref, wo_ref, gffn_ref, wr_ref, br_ref, *rest):
    outs, w_scr = rest[:-1], rest[-1]
    _cast_once(wo_ref, w_scr)
    x1 = _main_or_tail(MIX_TILE, xm_ref, xt_ref) + _dot(og_ref[...], w_scr[...])
    _mix_out_tail(x1, gffn_ref, wr_ref, br_ref, *outs)


def _gelu_tanh(x):
    return x * (0.5 * (1.0 + jnp.tanh(0.7978845608028654 * (x + 0.044715 * (x * x * x)))))


def _s5_out_kernel(ys_main_ref, ys_tail_ref, x_ref, wglu_ref, gmix_ref, d_ref, bglu_ref, gffn_ref, wr_ref, br_ref,
                   *rest):
    outs, w_scr = rest[:-1], rest[-1]
    _cast_once(wglu_ref, w_scr)
    x = x_ref[...]
    u = _rms(x, gmix_ref[...])
    y = _gelu_tanh(_main_or_tail(MIX_TILE, ys_main_ref, ys_tail_ref) + d_ref[...] * u)
    z = _dot(y.astype(bf16), w_scr[...]) + bglu_ref[...]
    _mix_out_tail(x + y * _sigmoid(z), gffn_ref, wr_ref, br_ref, *outs)


def _mix_out_call(kern, name, n, row_ins, row_maps, fix_ins):
    row = lambda i: (i, 0)
    fix = lambda i: (0, 0)
    out_row = lambda w: pl.BlockSpec((MIX_TILE, w), row)
    nt = n // MIX_TILE
    return pl.pallas_call(
        kern,
        grid=(nt,),
        in_specs=[pl.BlockSpec((MIX_TILE, a.shape[1]), m) for a, m in zip(row_ins, row_maps)]
        + [pl.BlockSpec(a.shape, fix, pipeline_mode=pl.Buffered(1)) for a in fix_ins],
        out_specs=[out_row(D_MODEL), out_row(D_MODEL), out_row(LANES), pl.BlockSpec((SUB, MIX_TILE), row),
                   pl.BlockSpec((1, LANES), fix)],
        out_shape=[jax.ShapeDtypeStruct((n, D_MODEL), f32), jax.ShapeDtypeStruct((n, D_MODEL), f32),
                   jax.ShapeDtypeStruct((n, LANES), f32), jax.ShapeDtypeStruct((nt * SUB, MIX_TILE), f32),
                   jax.ShapeDtypeStruct((1, LANES), f32)],
        scratch_shapes=[pltpu.VMEM((D_MODEL, D_MODEL), bf16)],
        compiler_params=_cp(("arbitrary",)),
        name=name,
    )(*row_ins, *fix_ins)


EXPERT_TM = 256
MOVE_TILE = MIX_TILE
COMBINE_CHUNK = 32


def _moe_plan(plan, cnt, n):
    i32 = jnp.int32
    v = plan.reshape(n // MOVE_TILE, SUB, MOVE_TILE)[:, :4].astype(i32)
    counts = cnt[0, ROUTE_E0:ROUTE_E0 + MOE_EXPERTS].astype(i32)
    ends = jnp.cumsum(counts)
    off = ends - counts
    ids = jnp.arange(MOE_EXPERTS, dtype=i32)
    pos = jnp.sum(jnp.where(v[:, 0:2, :, None] == ids, off, 0), axis=-1) + v[:, 2:4]
    total = 2 * n
    n_tiles = total // EXPERT_TM
    n_items = n_tiles + MOE_EXPERTS
    inner = (counts > 0) & (off % EXPERT_TM != 0)
    keys = jnp.concatenate([jnp.arange(n_tiles, dtype=i32) * EXPERT_TM, jnp.where(inner, off, total)])
    idx = jnp.arange(n_items, dtype=i32)
    before = (keys[None, :] < keys[:, None]) | ((keys[None, :] == keys[:, None]) & (idx[None, :] < idx[:, None]))
    order = jnp.sum(before.astype(i32), axis=1)
    starts = jnp.sum(jnp.where(order[:, None] == idx[None, :], keys[:, None], 0), axis=0)
    stops = jnp.concatenate([starts[1:], jnp.full((1,), total, i32)])
    tile = starts // EXPERT_TM
    expert = jnp.sum((ends[None, :] <= starts[:, None]).astype(i32), axis=1)
    used = n_tiles + jnp.sum(inner.astype(i32))
    keep = jnp.minimum(idx, used - 1)
    pick = lambda a: a[keep]
    return (pos.reshape(n // MOVE_TILE, 1, 2 * MOVE_TILE), pick(tile), pick(jnp.minimum(expert, MOE_EXPERTS - 1)),
            pick(starts - tile * EXPERT_TM), pick(stops - tile * EXPERT_TM), used.reshape(1))


def _row_copy(src, src_row, dst, dst_row, sem):
    return pltpu.make_async_copy(src.at[pl.ds(src_row, 1)], dst.at[pl.ds(dst_row, 1)], sem)


def _dispatch_kernel(pos_ref, x_ref, xs_ref, sem):
    def copy(r, s):
        return _row_copy(x_ref, r, xs_ref, pos_ref[0, 0, s * MOVE_TILE + r], sem)

    def start(r, c):
        for s in range(2):
            copy(r, s).start(priority=s)
        return c

    def wait(r, c):
        for s in range(2):
            copy(r, s).wait()
        return c

    lax.fori_loop(0, MOVE_TILE, start, 0, unroll=8)
    lax.fori_loop(0, MOVE_TILE, wait, 0, unroll=8)


def _dispatch_call(xn, pos):
    n = xn.shape[0]
    return pl.pallas_call(
        _dispatch_kernel,
        grid=(n // MOVE_TILE,),
        in_specs=[pl.BlockSpec((1, 1, 2 * MOVE_TILE), lambda i: (i, 0, 0), memory_space=pltpu.SMEM),
                  pl.BlockSpec((MOVE_TILE, D_MODEL), lambda i: (i, 0))],
        out_specs=pl.BlockSpec(memory_space=pl.ANY),
        out_shape=jax.ShapeDtypeStruct((2 * n, D_MODEL), xn.dtype),
        scratch_shapes=[pltpu.SemaphoreType.DMA(())],
        compiler_params=_cp(("arbitrary",)),
        name="moe_dispatch",
    )(pos, xn)


def _experts_kernel(it_ref, ie_ref, lo_ref, hi_ref, n_ref, xs_ref, wg_ref, wu_ref, wd_ref, os_ref,
                    wg_scr, wu_scr, wd_scr):
    i = pl.program_id(0)

    @pl.when(i < n_ref[0])
    def _():
        @pl.when((i == 0) | (ie_ref[i] != ie_ref[jnp.maximum(i - 1, 0)]))
        def _():
            wg_scr[...] = wg_ref[0].astype(bf16)
            wu_scr[...] = wu_ref[0].astype(bf16)
            wd_scr[...] = wd_ref[0].astype(bf16)

        x = xs_ref[...].astype(bf16)
        hg = _dot(x, wg_scr[...])
        hu = _dot(x, wu_scr[...])
        out = _dot((hg * _sigmoid(hg) * hu).astype(bf16), wd_scr[...])
        lo = lo_ref[i]
        row = lax.broadcasted_iota(jnp.int32, (EXPERT_TM, 1), 0)
        mine = (row >= lo) & (row < hi_ref[i])

        @pl.when(lo == 0)
        def _():
            os_ref[...] = jnp.where(mine, out, 0.0)

        @pl.when(lo != 0)
        def _():
            os_ref[...] = jnp.where(mine, out, os_ref[...])


def _experts_call(xs, item_tile, item_expert, item_lo, item_hi, n_items, wg, wu, wd):
    rows = lambda i, it, ie, lo, hi, n: (it[i], 0)
    wsel = lambda i, it, ie, lo, hi, n: (ie[i], 0, 0)
    return pl.pallas_call(
        _experts_kernel,
        grid_spec=pltpu.PrefetchScalarGridSpec(
            num_scalar_prefetch=5,
            grid=(item_tile.shape[0],),
            in_specs=[pl.BlockSpec((EXPERT_TM, D_MODEL), rows),
                      pl.BlockSpec((1, D_MODEL, D_EXPERT), wsel),
                      pl.BlockSpec((1, D_MODEL, D_EXPERT), wsel),
                      pl.BlockSpec((1, D_EXPERT, D_MODEL), wsel)],
            out_specs=pl.BlockSpec((EXPERT_TM, D_MODEL), rows),
            scratch_shapes=[pltpu.VMEM((D_MODEL, D_EXPERT), bf16), pltpu.VMEM((D_MODEL, D_EXPERT), bf16),
                            pltpu.VMEM((D_EXPERT, D_MODEL), bf16)],
        ),
        out_shape=jax.ShapeDtypeStruct(xs.shape, f32),
        compiler_params=_cp(("arbitrary",)),
        name="moe_experts",
    )(item_tile, item_expert, item_lo, item_hi, n_items, xs, wg, wu, wd)


def _combine_kernel(pos_ref, pos_next_ref, x_ref, route_ref, gn_ref, os_ref, *rest, emit_x, split):
    outs, (buf, sem, xn_scr) = rest[:-3], rest[-3:]
    i = pl.program_id(0)
    half = i % 2
    last = i == pl.num_programs(0) - 1

    def copy(p_ref, hf, r, s):
        return _row_copy(os_ref, p_ref[0, 0, s * MOVE_TILE + r], buf.at[hf, s], r, sem.at[hf])

    def loop_all(p_ref, hf, op):
        def body(r, c):
            for s in range(2):
                op(copy(p_ref, hf, r, s), s)
            return c

        lax.fori_loop(0, MOVE_TILE, body, 0, unroll=8)

    begin = lambda cp, s: cp.start(priority=s)
    finish = lambda cp, s: cp.wait()

    @pl.when(i == 0)
    def _():
        loop_all(pos_ref, 0, begin)

    loop_all(pos_ref, half, finish)
    xn_ref = xn_scr if split else outs[-1]
    for c in range(MOVE_TILE // COMBINE_CHUNK):
        rows = slice(c * COMBINE_CHUNK, (c + 1) * COMBINE_CHUNK)
        route = route_ref[rows, :]
        x2 = x_ref[rows, :] + route[:, 4:5] * buf[half, 0, rows, :] + route[:, 5:6] * buf[half, 1, rows, :]
        if emit_x:
            outs[0][rows, :] = x2
        xn_ref[rows, :] = _rms(x2, gn_ref[...]).astype(xn_ref.dtype)
        for r in range(rows.start, rows.stop):
            for s in range(2):
                begin(copy(pos_next_ref, 1 - half, r, s), s)

    @pl.when(last)
    def _():
        loop_all(pos_next_ref, 1 - half, finish)

    if split:
        main_ref, tail_ref = outs[-2:]
        is_main = i < N_MAIN // MOVE_TILE

        @pl.when(is_main)
        def _():
            main_ref[...] = xn_scr[...]

        @pl.when(jnp.logical_not(is_main))
        def _():
            tail_ref[...] = xn_scr[...]


def _combine_call(x1, route, pos, os_rows, gn, xn_dtype, *, emit_x, split):
    n = x1.shape[0]
    row = lambda i: (i, 0)
    blk = lambda m: pl.BlockSpec((MOVE_TILE, D_MODEL), m)
    nm = N_MAIN // MOVE_TILE
    out_specs, out_shape = [], []
    if emit_x:
        out_specs.append(blk(row))
        out_shape.append(jax.ShapeDtypeStruct((n, D_MODEL), f32))
    if split:
        out_specs += [blk(lambda i: (jnp.minimum(i, nm - 1), 0)), blk(lambda i: (jnp.maximum(i - nm, 0), 0))]
        out_shape += [jax.ShapeDtypeStruct((N_MAIN, D_MODEL), xn_dtype),
                      jax.ShapeDtypeStruct((n - N_MAIN, D_MODEL), xn_dtype)]
    else:
        out_specs.append(blk(row))
        out_shape.append(jax.ShapeDtypeStruct((n, D_MODEL), xn_dtype))
    nt = n // MOVE_TILE
    pos_spec = lambda m: pl.BlockSpec((1, 1, 2 * MOVE_TILE), m, memory_space=pltpu.SMEM)
    return pl.pallas_call(
        functools.partial(_combine_kernel, emit_x=emit_x, split=split),
        grid=(nt,),
        in_specs=[pos_spec(lambda i: (i, 0, 0)), pos_spec(lambda i: (jnp.minimum(i + 1, nt - 1), 0, 0)),
                  blk(row), pl.BlockSpec((MOVE_TILE, LANES), row),
                  pl.BlockSpec((1, D_MODEL), lambda i: (0, 0)), pl.BlockSpec(memory_space=pl.ANY)],
        out_specs=out_specs,
        out_shape=out_shape,
        scratch_shapes=[pltpu.VMEM((2, 2, MOVE_TILE, D_MODEL), f32), pltpu.SemaphoreType.DMA((2,)),
                        pltpu.VMEM((MOVE_TILE, D_MODEL), xn_dtype)],
        compiler_params=_cp(("arbitrary",)),
        name="moe_combine",
    )(pos, pos, x1, route, gn, os_rows)


S5_GROUP = 16
S5_GROUPS = D_MODEL // S5_GROUP
S5_STATE = 64
S5_CB = 128
S5_NB = D_MODEL // S5_CB
S5_GPB = S5_CB // S5_GROUP
S5_SB = S5_GPB * S5_STATE
S5_BC = S5_STATE * S5_GROUP
S5_CPS = 2


def _s5_disc_kernel(lr_ref, li_ref, ldt_ref, bre_ref, bim_ref,
                    abr_ref, abi_ref, ab2r_ref, ab2i_ref, bbr_ref, bbi_ref, abbr_ref, abbi_ref):
    lr, li = lr_ref[...], li_ref[...]
    dt = jnp.exp(ldt_ref[...])
    mag = jnp.exp(lr * dt)
    ang = li * dt
    ab_re, ab_im = mag * jnp.cos(ang), mag * jnp.sin(ang)
    nr, ni = ab_re - 1.0, ab_im
    den = lr * lr + li * li
    f_re = (nr * lr + ni * li) / den
    f_im = (ni * lr - nr * li) / den
    abr_ref[...] = ab_re
    abi_ref[...] = ab_im
    ab2r_ref[...] = ab_re * ab_re - ab_im * ab_im
    ab2i_ref[...] = 2.0 * (ab_re * ab_im)
    pi = lax.broadcasted_iota(jnp.int32, (S5_STATE, S5_BC), 0)
    ci = lax.broadcasted_iota(jnp.int32, (S5_STATE, S5_BC), 1)
    rep = jnp.where(ci // S5_GROUP == pi, 1.0, 0.0).astype(bf16)

    def expand(v):
        hi, mid, lo = _split3(v)
        return _dot(hi, rep) + _dot(mid, rep) + _dot(lo, rep)

    fr, fi, ar, ai = expand(f_re), expand(f_im), expand(ab_re), expand(ab_im)
    br, bi = bre_ref[...], bim_ref[...]
    bb_re = fr * br - fi * bi
    bb_im = fr * bi + fi * br
    bbr_ref[...] = bb_re
    bbi_ref[...] = bb_im
    abbr_ref[...] = ar * bb_re - ai * bb_im
    abbi_ref[...] = ar * bb_im + ai * bb_re


def _s5_weights(lam_re, lam_im, log_dt, b_re, b_im, c_re, c_im):
    st = jax.ShapeDtypeStruct((S5_GROUPS, S5_STATE), f32)
    bc = jax.ShapeDtypeStruct((S5_GROUPS, S5_BC), f32)
    ab_re, ab_im, ab2_re, ab2_im, bb_re, bb_im, abb_re, abb_im = pl.pallas_call(
        _s5_disc_kernel, out_shape=[st, st, st, st, bc, bc, bc, bc], name="s5_discretize",
    )(lam_re, lam_im, log_dt[:, None], b_re.reshape(S5_GROUPS, S5_BC), b_im.reshape(S5_GROUPS, S5_BC))
    eye = jnp.eye(S5_GPB, dtype=f32)

    def in_blocks(m):
        m = m.reshape(S5_NB, S5_GPB, S5_STATE, S5_GROUP)
        return jnp.einsum("jgpc,gh->jgchp", m, eye).reshape(S5_NB, S5_CB, S5_SB)

    def out_blocks(m):
        m = m.reshape(S5_NB, S5_GPB, S5_GROUP, S5_STATE)
        return jnp.einsum("jgcp,gh->jgphc", m, eye).reshape(S5_NB, S5_SB, S5_CB)

    bb2 = jnp.concatenate([
        jnp.concatenate([in_blocks(bb_re), in_blocks(bb_im)], axis=2),
        jnp.concatenate([in_blocks(abb_re), in_blocks(abb_im)], axis=2)], axis=1).astype(bf16)
    cc = jnp.concatenate([out_blocks(c_re), -out_blocks(c_im)], axis=1).astype(bf16)
    rows = [v.reshape(S5_NB, 1, S5_SB) for v in (ab_re, ab_im, ab2_re, ab2_im)]
    abv = jnp.concatenate(rows + [jnp.zeros((S5_NB, SUB - len(rows), S5_SB), f32)], axis=1)
    return bb2, cc, abv


def _s5_seq_kernel(x0_ref, x1_ref, x2_ref, x3_ref, halo0_ref, h0_ref, bb2_ref, cc_ref, abv_ref, y_ref, hout_ref,
                   xf_scr, xp_scr, bu_scr, hs_scr, yp_scr, yn_scr, h_scr, halo_scr):
    tb = pl.program_id(1)
    TL = x0_ref.shape[0]
    KB = TL // 2
    RB = BATCH * TL

    @pl.when(tb == 0)
    def _():
        h_scr[...] = h0_ref[...]
        halo_scr[...] = halo0_ref[...].astype(f32)

    chans = [slice(c * S5_CB, (c + 1) * S5_CB) for c in range(S5_CPS)]
    for c, ch in enumerate(chans):
        for b, xb_ref in enumerate((x0_ref, x1_ref, x2_ref, x3_ref)):
            xf_scr[c, b * TL:(b + 1) * TL, :] = xb_ref[:, ch].astype(f32)
        for b in range(BATCH):
            for p in range(2):
                xp_scr[c, pl.ds(2 * b + p, KB, stride=SUB), :] = xf_scr[c, pl.ds(b * TL + p, KB, stride=2), :]
    x = jnp.concatenate([xp_scr[c] for c in range(S5_CPS)], axis=1)
    xc = jnp.concatenate([halo_scr[...], x], axis=0)
    odd = (lax.broadcasted_iota(jnp.int32, (RB + SUB, 1), 0) & 1) == 1
    xprev = jnp.where(odd, pltpu.roll(xc, 1, axis=0), pltpu.roll(xc, SUB - 1, axis=0))[SUB:]
    halo_scr[...] = x[RB - SUB:]
    for c, ch in enumerate(chans):
        lhs = jnp.concatenate([x[:, ch], xprev[:, ch]], axis=1).astype(bf16)
        bu_scr[c] = _dot(lhs, bb2_ref[c])
    a2 = [(abv_ref[c, 2:3, :], abv_ref[c, 3:4, :]) for c in range(S5_CPS)]

    def step(k, carry):
        r0 = pl.multiple_of(k * SUB, SUB)
        out = []
        for c in range(S5_CPS):
            hr, hi = carry[2 * c], carry[2 * c + 1]
            a2r, a2i = a2[c]
            bu = bu_scr[c, pl.ds(r0, SUB), :]
            nr = a2r * hr - a2i * hi + bu[:, :S5_SB]
            ni = a2r * hi + a2i * hr + bu[:, S5_SB:]
            hs_scr[c, pl.ds(r0, SUB), :S5_SB] = nr
            hs_scr[c, pl.ds(r0, SUB), S5_SB:] = ni
            out += [nr, ni]
        return tuple(out)

    init = []
    for c in range(S5_CPS):
        init += [h_scr[c, :, :S5_SB], h_scr[c, :, S5_SB:]]
    fin = lax.fori_loop(0, RB // SUB, step, tuple(init), unroll=4)
    for c, ch in enumerate(chans):
        h_scr[c, :, :S5_SB] = fin[2 * c]
        h_scr[c, :, S5_SB:] = fin[2 * c + 1]
        yp_scr[c] = _dot(hs_scr[c].astype(bf16), cc_ref[c])
        for b in range(BATCH):
            for p in range(2):
                yn_scr[c, pl.ds(b * TL + p, KB, stride=2), :] = yp_scr[c, pl.ds(2 * b + p, KB, stride=SUB), :]
            y_ref[b, :, ch] = yn_scr[c, b * TL:(b + 1) * TL, :]

    @pl.when(tb == pl.num_programs(1) - 1)
    def _():
        hout_ref[...] = h_scr[...]


def _s5_seq_call(x, halo0, h0, bb2, cc, abv, *, row0, seq_len, tl):
    wsel = lambda j, t: (j, 0, 0)
    rb = BATCH * tl
    cw = S5_CPS * S5_CB
    xspec = lambda b: pl.BlockSpec((tl, cw), lambda j, t: ((row0 + b * seq_len) // tl + t, j))
    return pl.pallas_call(
        _s5_seq_kernel,
        grid=(S5_NB // S5_CPS, seq_len // tl),
        in_specs=[xspec(b) for b in range(BATCH)]
        + [pl.BlockSpec((SUB, cw), lambda j, t: (0, j)),
           pl.BlockSpec((S5_CPS, SUB, 2 * S5_SB), wsel),
           pl.BlockSpec((S5_CPS, 2 * S5_CB, 2 * S5_SB), wsel),
           pl.BlockSpec((S5_CPS, 2 * S5_SB, S5_CB), wsel),
           pl.BlockSpec((S5_CPS, SUB, S5_SB), wsel)],
        out_specs=[pl.BlockSpec((BATCH, tl, cw), lambda j, t: (0, t, j)),
                   pl.BlockSpec((S5_CPS, SUB, 2 * S5_SB), wsel)],
        out_shape=[jax.ShapeDtypeStruct((BATCH, seq_len, D_MODEL), f32),
                   jax.ShapeDtypeStruct((S5_NB, SUB, 2 * S5_SB), f32)],
        scratch_shapes=[pltpu.VMEM((S5_CPS, rb, S5_CB), f32), pltpu.VMEM((S5_CPS, rb, S5_CB), f32),
                        pltpu.VMEM((S5_CPS, rb, 2 * S5_SB), f32), pltpu.VMEM((S5_CPS, rb, 2 * S5_SB), f32),
                        pltpu.VMEM((S5_CPS, rb, S5_CB), f32), pltpu.VMEM((S5_CPS, rb, S5_CB), f32),
                        pltpu.VMEM((S5_CPS, SUB, 2 * S5_SB), f32), pltpu.VMEM((SUB, cw), f32)],
        compiler_params=_cp(("arbitrary", "arbitrary")),
        name=f"s5_seq_{seq_len}",
    )(x, x, x, x, halo0, h0, bb2, cc, abv)


def _s5_sample_kernel(x_ref, hre_ref, him_ref, bb2_ref, cc_ref, abv_ref, y_ref, ore_ref, oim_ref, hs_scr):
    nb = hre_ref.shape[0]
    bu = _dot(x_ref[...], bb2_ref[0, :S5_CB, :])
    ar = abv_ref[0, 0:1, :]
    ai = abv_ref[0, 1:2, :]
    hr, hi = hre_ref[...], him_ref[...]
    for t in range(DEC_SEQ):
        rows = slice(t * nb, (t + 1) * nb)
        hr, hi = (ar * hr - ai * hi + bu[rows, :S5_SB], ar * hi + ai * hr + bu[rows, S5_SB:])
        hs_scr[rows, :S5_SB] = hr
        hs_scr[rows, S5_SB:] = hi
    y_ref[...] = _dot(hs_scr[...].astype(bf16), cc_ref[0])
    ore_ref[...] = hr
    oim_ref[...] = hi


def _s5_sample_call(xt, h_re, h_im, bb2, cc, abv):
    n = xt.shape[0]
    nb = h_re.shape[0]
    wsel = lambda j: (j, 0, 0)
    st = pl.BlockSpec((nb, S5_SB), lambda j: (0, j))
    return pl.pallas_call(
        _s5_sample_kernel,
        grid=(S5_NB,),
        in_specs=[pl.BlockSpec((n, S5_CB), lambda j: (0, j)), st, st,
                  pl.BlockSpec((1, 2 * S5_CB, 2 * S5_SB), wsel),
                  pl.BlockSpec((1, 2 * S5_SB, S5_CB), wsel),
                  pl.BlockSpec((1, SUB, S5_SB), wsel)],
        out_specs=[pl.BlockSpec((n, S5_CB), lambda j: (0, j)), st, st],
        out_shape=[jax.ShapeDtypeStruct((n, D_MODEL), f32),
                   jax.ShapeDtypeStruct(h_re.shape, f32), jax.ShapeDtypeStruct(h_im.shape, f32)],
        scratch_shapes=[pltpu.VMEM((n, 2 * S5_SB), f32)],
        compiler_params=_cp(("arbitrary",)),
        name="s5_sample",
    )(xt, h_re, h_im, bb2, cc, abv)


GLA_CHUNK = 256
GLA_DIRECT = 2
S5_TL = 256


def _moe_layer(layer, x1, xnf, route, plan, cnt, wg, wu, wd, gn, xn_dtype, *, emit_x, split):
    pos, item_tile, item_expert, item_lo, item_hi, n_items = _moe_plan(plan, cnt, x1.shape[0])
    xs = _dispatch_call(xnf, pos)
    os_rows = _experts_call(xs, item_tile, item_expert + layer * MOE_EXPERTS, item_lo, item_hi, n_items, wg, wu, wd)
    return _combine_call(x1, route, pos, os_rows, gn, xn_dtype, emit_x=emit_x, split=split)


def kernel(x_prompt, x_sample, state_gla, state_s5_re, state_s5_im, meta_tokens, norm_mix_g, norm_ffn_g, norm_final_g, gla_w_in, gla_w_a2, gla_b_a, gla_g_o, gla_w_o, s5_lambda_re, s5_lambda_im, s5_log_dt, s5_b_re, s5_b_im, s5_c_re, s5_c_im, s5_d, s5_w_glu, s5_b_glu, moe_w_rg, moe_b_rg, moe_w_re, moe_b_re, moe_w_gate, moe_w_up, moe_w_down):
    row = lambda v: v.reshape(1, -1)
    x_main = x_prompt.reshape(N_MAIN, D_MODEL)
    x_tail = jnp.concatenate([
        jnp.tile(meta_tokens.astype(x_prompt.dtype), (BATCH, 1)),
        x_sample.reshape(N_SAMPLE, D_MODEL),
        jnp.zeros((N_ROWS - N_REAL, D_MODEL), x_prompt.dtype)], axis=0)
    wg = moe_w_gate.reshape(-1, D_MODEL, D_EXPERT)
    wu = moe_w_up.reshape(-1, D_MODEL, D_EXPERT)
    wd = moe_w_down.reshape(-1, D_EXPERT, D_MODEL)

    w_in = jnp.swapaxes(gla_w_in, 1, 2)
    wa1 = jnp.pad(gla_w_in[0, :, GLA_QKVR:], ((0, 0), (0, LANES - GLA_RANK)))
    wa2 = jnp.pad(gla_w_a2.reshape(GLA_RANK, GLA_KEY), ((0, LANES - GLA_RANK), (0, 0)))
    xn, glog = _norm_gate_call(x_main, x_tail, row(norm_mix_g[0]), wa1, wa2, row(gla_b_a))
    proj = _proj_call(xn, w_in, GLA_QKVR)
    go = row(gla_g_o)
    og = jnp.zeros((N_ROWS, GLA_VAL), bf16)
    s_zero = jnp.zeros((BATCH, GLA_HEADS, GLA_DK, GLA_DV), f32)
    og, s_meta = _gla_seq_call(proj, glog, go, s_zero, og, row0=ROW_META, C=N_META, n_chunks=1, d=GLA_DIRECT)
    og, s_prompt = _gla_seq_call(proj, glog, go, s_meta, og, row0=0, C=GLA_CHUNK, n_chunks=SEQ // GLA_CHUNK,
                                 d=GLA_DIRECT)
    og, s_sample = _gla_sample_call(proj, glog, go, state_gla.reshape(DEC_BATCH, GLA_HEADS, GLA_DK, GLA_DV), og,
                                    row0=ROW_SAMPLE)
    wr, br = _router_weights(moe_w_rg[0], moe_b_rg[0], moe_w_re[0], moe_b_re[0])
    tile_row = lambda i: (i, 0)
    x1, xnf, route, plan, cnt = _mix_out_call(
        _gla_out_kernel, "gla_out", N_ROWS, [og, x_main, x_tail], [tile_row, *_main_tail_maps(MIX_TILE)],
        [gla_w_o.reshape(GLA_VAL, D_MODEL), row(norm_ffn_g[0]), wr, br])
    x2, xn2 = _moe_layer(0, x1, xnf, route, plan, cnt, wg, wu, wd, row(norm_mix_g[1]), bf16, emit_x=True, split=False)

    bb2, cc, abv = _s5_weights(s5_lambda_re[0], s5_lambda_im[0], s5_log_dt[0], s5_b_re[0], s5_b_im[0],
                               s5_c_re[0], s5_c_im[0])
    y_meta, h_meta = _s5_seq_call(xn2, jnp.zeros((SUB, D_MODEL), bf16), jnp.zeros((S5_NB, SUB, 2 * S5_SB), f32),
                                  bb2, cc, abv, row0=ROW_META, seq_len=N_META, tl=N_META)
    halo = xn2[ROW_META:ROW_SAMPLE].reshape(BATCH, N_META, D_MODEL)[:, N_META - 2:].reshape(SUB, D_MODEL)
    y_main, h_main = _s5_seq_call(xn2, halo, h_meta, bb2, cc, abv, row0=0, seq_len=SEQ, tl=S5_TL)
    xt_sample = xn2[ROW_SAMPLE:N_REAL].reshape(DEC_BATCH, DEC_SEQ, D_MODEL).transpose(1, 0, 2).reshape(N_SAMPLE, D_MODEL)
    y_samp, s5r_s, s5i_s = _s5_sample_call(
        xt_sample, state_s5_re.reshape(DEC_BATCH, S5_GROUPS * S5_STATE),
        state_s5_im.reshape(DEC_BATCH, S5_GROUPS * S5_STATE), bb2, cc, abv)
    ys_tail = jnp.concatenate([
        y_meta.reshape(N_METAROWS, D_MODEL),
        y_samp.reshape(DEC_SEQ, DEC_BATCH, D_MODEL).transpose(1, 0, 2).reshape(N_SAMPLE, D_MODEL),
        jnp.zeros((N_ROWS - N_REAL, D_MODEL), f32)], axis=0)
    wr, br = _router_weights(moe_w_rg[1], moe_b_rg[1], moe_w_re[1], moe_b_re[1])
    x3, xnf, route, plan, cnt = _mix_out_call(
        _s5_out_kernel, "s5_out", N_ROWS, [y_main.reshape(N_MAIN, D_MODEL), ys_tail, x2],
        [*_main_tail_maps(MIX_TILE), tile_row],
        [s5_w_glu.reshape(D_MODEL, D_MODEL), row(norm_mix_g[1]), row(s5_d), row(s5_b_glu),
         row(norm_ffn_g[1]), wr, br])
    y_main_out, y_tail_out = _moe_layer(1, x3, xnf, route, plan, cnt, wg, wu, wd, row(norm_final_g), f32,
                                        emit_x=False, split=True)

    y_prompt = y_main_out.reshape(BATCH, SEQ, D_MODEL)
    y_sample = y_tail_out[N_METAROWS:N_METAROWS + N_SAMPLE].reshape(DEC_BATCH, DEC_SEQ, D_MODEL)
    hfin = h_main.reshape(S5_NB, BATCH, 2, 2, S5_GPB, S5_STATE)[:, :, 1]
    s5r_p = hfin[:, :, 0].transpose(1, 0, 2, 3).reshape(1, BATCH, S5_GROUPS, S5_STATE)
    s5i_p = hfin[:, :, 1].transpose(1, 0, 2, 3).reshape(1, BATCH, S5_GROUPS, S5_STATE)
    return (y_prompt, y_sample, s_prompt[None], s5r_p, s5i_p, s_sample[None],
            s5r_s.reshape(1, DEC_BATCH, S5_GROUPS, S5_STATE), s5i_s.reshape(1, DEC_BATCH, S5_GROUPS, S5_STATE))
```

```python
import functools

import jax
import jax.numpy as jnp
from jax import lax
from jax.experimental import pallas as pl
from jax.experimental.pallas import tpu as pltpu

f32 = jnp.float32
bf16 = jnp.bfloat16

D_MODEL = 2048
BATCH = 4
SEQ = 2048
DEC_BATCH = 128
DEC_SEQ = 4
N_META = 16
EPS = 1e-6
GLA_HEADS = 4
GLA_DK = 256
GLA_DV = 512
GLA_KEY = GLA_HEADS * GLA_DK
GLA_VAL = GLA_HEADS * GLA_DV
GLA_RANK = 16
GLA_TAU = 16.0
GLA_QKVR = 2 * GLA_KEY + 2 * GLA_VAL

N_MAIN = BATCH * SEQ
N_METAROWS = BATCH * N_META
N_SAMPLE = DEC_BATCH * DEC_SEQ
ROW_META = N_MAIN
ROW_SAMPLE = N_MAIN + N_METAROWS
N_REAL = ROW_SAMPLE + N_SAMPLE
ROW_TILE = 256
N_ROWS = -(-N_REAL // ROW_TILE) * ROW_TILE

VMEM_LIMIT = 56 * 1024 * 1024


def _cp(sem, vmem=VMEM_LIMIT):
    return pltpu.CompilerParams(dimension_semantics=sem, vmem_limit_bytes=vmem)


def _dot(a, b):
    return jnp.dot(a, b, preferred_element_type=f32)


def _dot_nt(a, b):
    return lax.dot_general(a, b, (((1,), (1,)), ((), ())), preferred_element_type=f32)


def _dot_tn(a, b):
    return lax.dot_general(a, b, (((0,), (0,)), ((), ())), preferred_element_type=f32)


def _sigmoid(x):
    return 1.0 / (1.0 + jnp.exp(-x))


def _split3(x):
    hi = x.astype(bf16)
    r1 = x - hi.astype(f32)
    mid = r1.astype(bf16)
    lo = (r1 - mid.astype(f32)).astype(bf16)
    return hi, mid, lo


def _cumsum_rows(g, C):
    if C <= 16:
        row = lax.broadcasted_iota(jnp.int32, (C, 1), 0)
        b = jnp.zeros_like(g)
        for s in range(C):
            b = b + jnp.where(row >= s, g[s:s + 1, :], 0.0)
        return b
    row = lax.broadcasted_iota(jnp.int32, (C, C), 0)
    col = lax.broadcasted_iota(jnp.int32, (C, C), 1)
    tri = jnp.where(row >= col, 1.0, 0.0).astype(bf16)
    hi, mid, lo = _split3(g)
    return _dot(tri, hi) + _dot(tri, mid) + _dot(tri, lo)


PAIR_LEVEL = 1000


def _gla_pair_code(C, d, tree=True):
    ti = lax.broadcasted_iota(jnp.int32, (C, C), 0)
    si = lax.broadcasted_iota(jnp.int32, (C, C), 1)
    code = jnp.where((ti // d == si // d) & (si <= ti), 1 + ti - si, 0)
    h = d
    while tree and h < C:
        tb = ti // h
        code = jnp.where(((tb % 2) == 1) & ((si // h) == tb - 1), PAIR_LEVEL + h, code)
        h *= 2
    return code


def _gla_scores(q, k, b, code, C, d, tree=True):
    row = lax.broadcasted_iota(jnp.int32, (C, 1), 0)
    scores = jnp.zeros((C, C), f32)
    for dl in range(d):
        ks = k if dl == 0 else pltpu.roll(k, dl, axis=0)
        bs = b if dl == 0 else pltpu.roll(b, dl, axis=0)
        term = q * ks * jnp.exp(jnp.minimum(b - bs, 0.0))
        colv = jnp.sum(term, axis=1, keepdims=True)
        scores = jnp.where(code == 1 + dl, colv, scores)
    z = b
    s = 1
    while tree and 2 * s < C:
        z = jnp.where((row & s) != 0, pltpu.roll(z, s, axis=0), z)
        s *= 2
        h = s
        if h < d:
            continue
        bnext = pltpu.roll(z, C - h, axis=0)
        qh = (q * jnp.exp(jnp.minimum(b - z, 0.0))).astype(bf16)
        kh = (k * jnp.exp(jnp.minimum(bnext - b, 0.0))).astype(bf16)
        scores = jnp.where(code == PAIR_LEVEL + h, _dot_nt(qh, kh), scores)
    return scores


def _gla_chunk(q, k, v, g, S, code, C, d, tree=True):
    b = _cumsum_rows(g, C)
    o = _dot((q * jnp.exp(b)).astype(bf16), S.astype(bf16))
    scores = _gla_scores(q, k, b, code, C, d, tree)
    o = o + _dot(scores.astype(bf16), v)
    b_last = b[C - 1:C, :]
    kd = (k * jnp.exp(b_last - b)).astype(bf16)
    if C == GLA_DK:
        eye = code == 1
    else:
        eye = (lax.broadcasted_iota(jnp.int32, (GLA_DK, GLA_DK), 0)
               == lax.broadcasted_iota(jnp.int32, (GLA_DK, GLA_DK), 1))
    dec_col = jnp.sum(jnp.where(eye, jnp.exp(b_last), 0.0), axis=1, keepdims=True)
    S_new = dec_col * S + _dot_tn(kd, v)
    return o, S_new


def _gla_head_epilogue(o, r, go):
    ms = jnp.mean(o * o, axis=1, keepdims=True)
    on = o * lax.rsqrt(ms + EPS) * go
    return on * (r * _sigmoid(r))


def _gla_seq_kernel(q_ref, k_ref, v_ref, r_ref, g_ref, go_ref, s0_ref, _og_in, og_ref, sout_ref, s_scr, *, C, d):
    c = pl.program_id(1)

    @pl.when(c == 0)
    def _():
        s_scr[...] = s0_ref[0]

    code = _gla_pair_code(C, d)

    def head(h, carry):
        ck = pl.ds(pl.multiple_of(h * GLA_DK, GLA_DK), GLA_DK)
        cv = pl.ds(pl.multiple_of(h * GLA_DV, GLA_DV), GLA_DV)
        q = q_ref[:, ck].astype(f32) * (GLA_DK ** -0.5)
        k = k_ref[:, ck].astype(f32)
        o, S_new = _gla_chunk(q, k, v_ref[:, cv], g_ref[:, ck], s_scr[h], code, C, d)
        s_scr[h] = S_new
        og_ref[:, cv] = _gla_head_epilogue(o, r_ref[:, cv].astype(f32), go_ref[:, cv]).astype(og_ref.dtype)
        return carry

    lax.fori_loop(0, GLA_HEADS, head, 0)

    @pl.when(c == pl.num_programs(1) - 1)
    def _():
        sout_ref[0] = s_scr[...]


def _gla_seq_call(proj, glog, go, s0, og_buf, *, row0, C, n_chunks, d):
    blk0 = row0 // C
    rows = lambda b, c: blk0 + b * n_chunks + c
    kern = functools.partial(_gla_seq_kernel, C=C, d=d)
    return pl.pallas_call(
        kern,
        grid=(BATCH, n_chunks),
        in_specs=[
            pl.BlockSpec((C, GLA_KEY), lambda b, c: (rows(b, c), 0)),
            pl.BlockSpec((C, GLA_KEY), lambda b, c: (rows(b, c), 1)),
            pl.BlockSpec((C, GLA_VAL), lambda b, c: (rows(b, c), 1)),
            pl.BlockSpec((C, GLA_VAL), lambda b, c: (rows(b, c), 2)),
            pl.BlockSpec((C, GLA_KEY), lambda b, c: (rows(b, c), 0)),
            pl.BlockSpec((1, GLA_VAL), lambda b, c: (0, 0)),
            pl.BlockSpec((1, GLA_HEADS, GLA_DK, GLA_DV), lambda b, c: (b, 0, 0, 0)),
            pl.BlockSpec(memory_space=pl.ANY),
        ],
        out_specs=[
            pl.BlockSpec((C, GLA_VAL), lambda b, c: (rows(b, c), 0)),
            pl.BlockSpec((1, GLA_HEADS, GLA_DK, GLA_DV), lambda b, c: (b, 0, 0, 0)),
        ],
        out_shape=[
            jax.ShapeDtypeStruct(og_buf.shape, og_buf.dtype),
            jax.ShapeDtypeStruct((BATCH, GLA_HEADS, GLA_DK, GLA_DV), f32),
        ],
        scratch_shapes=[pltpu.VMEM((GLA_HEADS, GLA_DK, GLA_DV), f32)],
        input_output_aliases={7: 0},
        compiler_params=_cp(("arbitrary", "arbitrary")),
        name=f"gla_seq_c{C}",
    )(proj, proj, proj, proj, glog, go, s0, og_buf)


SAMPLE_BB = 4
SAMPLE_C = SAMPLE_BB * DEC_SEQ


def _gla_sample_kernel(q_ref, k_ref, v_ref, r_ref, g_ref, go_ref, s0_ref, _og_in, og_ref, sout_ref):
    row = lax.broadcasted_iota(jnp.int32, (SAMPLE_C, 1), 0)

    code = _gla_pair_code(SAMPLE_C, DEC_SEQ, tree=False)

    def take(x, bb):
        sh = (SAMPLE_C - DEC_SEQ * bb) % SAMPLE_C
        return jnp.where(row < DEC_SEQ, pltpu.roll(x, sh, axis=0) if sh else x, 0.0)

    def head(h, carry):
        ck = pl.ds(pl.multiple_of(h * GLA_DK, GLA_DK), GLA_DK)
        cv = pl.ds(pl.multiple_of(h * GLA_DV, GLA_DV), GLA_DV)
        q_all = q_ref[:, ck].astype(f32) * (GLA_DK ** -0.5)
        k_all = k_ref[:, ck].astype(f32)
        v_all = v_ref[:, cv].astype(f32)
        r_all = r_ref[:, cv].astype(f32)
        g_all = g_ref[:, ck]
        go = go_ref[:, cv]
        acc = jnp.zeros((SAMPLE_C, GLA_DV), f32)
        for bb in range(SAMPLE_BB):
            o, S_new = _gla_chunk(take(q_all, bb), take(k_all, bb), take(v_all, bb).astype(bf16),
                                  take(g_all, bb), s0_ref[bb, h], code, SAMPLE_C, DEC_SEQ, tree=False)
            sout_ref[bb, h] = S_new
            y = _gla_head_epilogue(o, take(r_all, bb), go)
            acc = jnp.where(row // DEC_SEQ == bb, pltpu.roll(y, DEC_SEQ * bb, axis=0) if bb else y, acc)
        og_ref[:, cv] = acc.astype(og_ref.dtype)
        return carry

    lax.fori_loop(0, GLA_HEADS, head, 0)


def _gla_sample_call(proj, glog, go, s0, og_buf, *, row0):
    n_seq = s0.shape[0]
    blk0 = row0 // SAMPLE_C
    st_spec = pl.BlockSpec((SAMPLE_BB, GLA_HEADS, GLA_DK, GLA_DV), lambda i: (i, 0, 0, 0))
    return pl.pallas_call(
        _gla_sample_kernel,
        grid=(n_seq // SAMPLE_BB,),
        in_specs=[
            pl.BlockSpec((SAMPLE_C, GLA_KEY), lambda i: (blk0 + i, 0)),
            pl.BlockSpec((SAMPLE_C, GLA_KEY), lambda i: (blk0 + i, 1)),
            pl.BlockSpec((SAMPLE_C, GLA_VAL), lambda i: (blk0 + i, 1)),
            pl.BlockSpec((SAMPLE_C, GLA_VAL), lambda i: (blk0 + i, 2)),
            pl.BlockSpec((SAMPLE_C, GLA_KEY), lambda i: (blk0 + i, 0)),
            pl.BlockSpec((1, GLA_VAL), lambda i: (0, 0)),
            st_spec,
            pl.BlockSpec(memory_space=pl.ANY),
        ],
        out_specs=[pl.BlockSpec((SAMPLE_C, GLA_VAL), lambda i: (blk0 + i, 0)), st_spec],
        out_shape=[jax.ShapeDtypeStruct(og_buf.shape, og_buf.dtype), jax.ShapeDtypeStruct(s0.shape, f32)],
        input_output_aliases={7: 0},
        compiler_params=_cp(("arbitrary",)),
        name="gla_sample",
    )(proj, proj, proj, proj, glog, go, s0, og_buf)


LANES = 128
SUB = 8
MOE_GROUPS = 4
MOE_EPG = 8
MOE_EXPERTS = MOE_GROUPS * MOE_EPG
D_EXPERT = 256
ROUTE_E0 = MOE_GROUPS
MIX_TILE = 256
PROJ_NT = 1024
PROJ_MT = N_ROWS // 7


def _rms(x, g):
    r = lax.rsqrt(jnp.mean(x * x, axis=-1, keepdims=True) + EPS)
    return (x * r) * g


def _log_sigmoid(z):
    return jnp.minimum(z, 0.0) - jnp.log1p(jnp.exp(-jnp.abs(z)))


def _main_or_tail(tile, main_ref, tail_ref):
    return jnp.where(pl.program_id(0) < N_MAIN // tile, main_ref[...], tail_ref[...])


def _main_tail_maps(tile):
    nm = N_MAIN // tile
    return (lambda i: (jnp.minimum(i, nm - 1), 0)), (lambda i: (jnp.maximum(i - nm, 0), 0))


def _norm_gate_kernel(xm_ref, xt_ref, gn_ref, wa1_ref, wa2_ref, ba_ref, xn_ref, gl_ref):
    xnb = _rms(_main_or_tail(ROW_TILE, xm_ref, xt_ref), gn_ref[...]).astype(bf16)
    xn_ref[...] = xnb
    a = _dot(xnb, wa1_ref[...].astype(bf16))
    z = _dot(a.astype(bf16), wa2_ref[...].astype(bf16)) + ba_ref[...]
    gl_ref[...] = _log_sigmoid(z) * (1.0 / GLA_TAU)


def _norm_gate_call(x_main, x_tail, gn, wa1, wa2, ba):
    n = x_main.shape[0] + x_tail.shape[0]
    row = lambda i: (i, 0)
    fix = lambda i: (0, 0)
    main_map, tail_map = _main_tail_maps(ROW_TILE)
    return pl.pallas_call(
        _norm_gate_kernel,
        grid=(n // ROW_TILE,),
        in_specs=[pl.BlockSpec((ROW_TILE, D_MODEL), main_map), pl.BlockSpec((ROW_TILE, D_MODEL), tail_map),
                  pl.BlockSpec((1, D_MODEL), fix),
                  pl.BlockSpec((D_MODEL, LANES), fix), pl.BlockSpec((LANES, GLA_KEY), fix),
                  pl.BlockSpec((1, GLA_KEY), fix)],
        out_specs=[pl.BlockSpec((ROW_TILE, D_MODEL), row), pl.BlockSpec((ROW_TILE, GLA_KEY), row)],
        out_shape=[jax.ShapeDtypeStruct((n, D_MODEL), bf16), jax.ShapeDtypeStruct((n, GLA_KEY), f32)],
        compiler_params=_cp(("arbitrary",)),
        name="norm_gate",
    )(x_main, x_tail, gn, wa1, wa2, ba)


def _proj_kernel(xn_ref, w_ref, o_ref, wb_scr):
    @pl.when(pl.program_id(1) == 0)
    def _():
        wb_scr[...] = w_ref[0].astype(bf16)

    o_ref[...] = _dot_nt(xn_ref[...], wb_scr[...]).astype(o_ref.dtype)


def _proj_call(xn, wt, n_cols):
    n = xn.shape[0]
    return pl.pallas_call(
        _proj_kernel,
        grid=(n_cols // PROJ_NT, n // PROJ_MT),
        in_specs=[pl.BlockSpec((PROJ_MT, D_MODEL), lambda j, i: (i, 0)),
                  pl.BlockSpec((1, PROJ_NT, D_MODEL), lambda j, i: (0, j, 0))],
        out_specs=pl.BlockSpec((PROJ_MT, PROJ_NT), lambda j, i: (i, j)),
        out_shape=jax.ShapeDtypeStruct((n, n_cols), bf16),
        scratch_shapes=[pltpu.VMEM((PROJ_NT, D_MODEL), bf16)],
        compiler_params=_cp(("arbitrary", "arbitrary")),
        name="gla_proj",
    )(xn, wt)


def _route(xn, wr, br, cnt_ref):
    R = xn.shape[0]
    xh = xn.astype(bf16)
    xl = (xn - xh.astype(f32)).astype(bf16)
    wh = wr.astype(bf16)
    wl = (wr - wh.astype(f32)).astype(bf16)
    logits = _dot(xh, wh) + _dot(xl, wh) + _dot(xh, wl) + br
    lane_i = lax.broadcasted_iota(jnp.int32, (R, LANES), 1)
    lane = lane_i.astype(f32)
    neg = -jnp.inf
    big = float(LANES)
    is_g = lane_i < MOE_GROUPS
    lg = jnp.where(is_g, logits, neg)
    mg = jnp.max(lg, axis=1, keepdims=True)
    gidx = jnp.min(jnp.where(lg == mg, lane, big), axis=1, keepdims=True)
    ptop = 1.0 / jnp.sum(jnp.where(is_g, jnp.exp(logits - mg), 0.0), axis=1, keepdims=True)
    lo = ROUTE_E0 + MOE_EPG * gidx
    le = jnp.where((lane >= lo) & (lane < lo + MOE_EPG), logits, neg)
    v1 = jnp.max(le, axis=1, keepdims=True)
    i1 = jnp.min(jnp.where(le == v1, lane, big), axis=1, keepdims=True)
    le2 = jnp.where(lane == i1, neg, le)
    v2 = jnp.max(le2, axis=1, keepdims=True)
    i2 = jnp.min(jnp.where(le2 == v2, lane, big), axis=1, keepdims=True)
    s = jnp.exp(v2 - v1)
    w0 = ptop / (1.0 + s)
    w1 = ptop * s / (1.0 + s)
    oh = jnp.where((lane == i1) | (lane == i2), 1.0, 0.0)
    ri = lax.broadcasted_iota(jnp.int32, (R, R), 0)
    ci = lax.broadcasted_iota(jnp.int32, (R, R), 1)
    before = jnp.where(ri > ci, 1.0, 0.0).astype(bf16)
    tot = _dot(before, oh.astype(bf16)) + cnt_ref[...]
    rank0 = jnp.sum(jnp.where(lane == i1, tot, 0.0), axis=1, keepdims=True)
    rank1 = jnp.sum(jnp.where(lane == i2, tot, 0.0), axis=1, keepdims=True)
    cnt_ref[...] = cnt_ref[...] + jnp.sum(oh, axis=0, keepdims=True)
    vals = (i1 - ROUTE_E0, i2 - ROUTE_E0, rank0, rank1, w0, w1)
    slab = jnp.zeros((R, LANES), f32)
    for j, v in enumerate(vals):
        slab = jnp.where(lane_i == j, v, slab)
    sub = lax.broadcasted_iota(jnp.int32, (SUB, R), 0)
    plan = jnp.zeros((SUB, R), f32)
    for j, v in enumerate(vals[:4]):
        as_row = jnp.sum(jnp.where(ri == ci, v, 0.0), axis=0, keepdims=True)
        plan = jnp.where(sub == j, as_row, plan)
    return slab, plan


def _router_weights(w_rg, b_rg, w_re, b_re):
    w = jnp.concatenate([w_rg, jnp.moveaxis(w_re, 0, 1).reshape(D_MODEL, MOE_EXPERTS)], axis=1)
    b = jnp.concatenate([b_rg, b_re.reshape(MOE_EXPERTS)])
    pad = LANES - w.shape[1]
    return jnp.pad(w, ((0, 0), (0, pad))), jnp.pad(b, (0, pad))[None]


def _mix_out_tail(x1, gffn_ref, wr_ref, br_ref, x1_ref, xn_ref, route_ref, plan_ref, cnt_ref):
    @pl.when(pl.program_id(0) == 0)
    def _():
        cnt_ref[...] = jnp.zeros_like(cnt_ref)

    x1_ref[...] = x1
    xn = _rms(x1, gffn_ref[...])
    xn_ref[...] = xn
    route_ref[...], plan_ref[...] = _route(xn, wr_ref[...], br_ref[...], cnt_ref)


def _cast_once(w_ref, w_scr):
    @pl.when(pl.program_id(0) == 0)
    def _():
        w_scr[...] = w_ref[...].astype(bf16)


def _gla_out_kernel(og_ref, xm_ref, xt_ref, wo_ref, gffn_ref, wr_ref, br_ref, *rest):
    outs, w_scr = rest[:-1], rest[-1]
    _cast_once(wo_ref, w_scr)
    x1 = _main_or_tail(MIX_TILE, xm_ref, xt_ref) + _dot(og_ref[...], w_scr[...])
    _mix_out_tail(x1, gffn_ref, wr_ref, br_ref, *outs)


def _gelu_tanh(x):
    return x * (0.5 * (1.0 + jnp.tanh(0.7978845608028654 * (x + 0.044715 * (x * x * x)))))


def _s5_out_kernel(ys_main_ref, ys_tail_ref, x_ref, wglu_ref, gmix_ref, d_ref, bglu_ref, gffn_ref, wr_ref, br_ref,
                   *rest):
    outs, w_scr = rest[:-1], rest[-1]
    _cast_once(wglu_ref, w_scr)
    x = x_ref[...]
    u = _rms(x, gmix_ref[...])
    y = _gelu_tanh(_main_or_tail(MIX_TILE, ys_main_ref, ys_tail_ref) + d_ref[...] * u)
    z = _dot(y.astype(bf16), w_scr[...]) + bglu_ref[...]
    _mix_out_tail(x + y * _sigmoid(z), gffn_ref, wr_ref, br_ref, *outs)


def _mix_out_call(kern, name, n, row_ins, row_maps, fix_ins):
    row = lambda i: (i, 0)
    fix = lambda i: (0, 0)
    out_row = lambda w: pl.BlockSpec((MIX_TILE, w), row)
    nt = n // MIX_TILE
    return pl.pallas_call(
        kern,
        grid=(nt,),
        in_specs=[pl.BlockSpec((MIX_TILE, a.shape[1]), m) for a, m in zip(row_ins, row_maps)]
        + [pl.BlockSpec(a.shape, fix, pipeline_mode=pl.Buffered(1)) for a in fix_ins],
        out_specs=[out_row(D_MODEL), out_row(D_MODEL), out_row(LANES), pl.BlockSpec((SUB, MIX_TILE), row),
                   pl.BlockSpec((1, LANES), fix)],
        out_shape=[jax.ShapeDtypeStruct((n, D_MODEL), f32), jax.ShapeDtypeStruct((n, D_MODEL), f32),
                   jax.ShapeDtypeStruct((n, LANES), f32), jax.ShapeDtypeStruct((nt * SUB, MIX_TILE), f32),
                   jax.ShapeDtypeStruct((1, LANES), f32)],
        scratch_shapes=[pltpu.VMEM((D_MODEL, D_MODEL), bf16)],
        compiler_params=_cp(("arbitrary",)),
        name=name,
    )(*row_ins, *fix_ins)


EXPERT_TM = 256
MOVE_TILE = MIX_TILE
COMBINE_CHUNK = 32


def _moe_plan(plan, cnt, n):
    i32 = jnp.int32
    v = plan.reshape(n // MOVE_TILE, SUB, MOVE_TILE)[:, :4].astype(i32)
    counts = cnt[0, ROUTE_E0:ROUTE_E0 + MOE_EXPERTS].astype(i32)
    ends = jnp.cumsum(counts)
    off = ends - counts
    ids = jnp.arange(MOE_EXPERTS, dtype=i32)
    pos = jnp.sum(jnp.where(v[:, 0:2, :, None] == ids, off, 0), axis=-1) + v[:, 2:4]
    total = 2 * n
    n_tiles = total // EXPERT_TM
    n_items = n_tiles + MOE_EXPERTS
    inner = (counts > 0) & (off % EXPERT_TM != 0)
    keys = jnp.concatenate([jnp.arange(n_tiles, dtype=i32) * EXPERT_TM, jnp.where(inner, off, total)])
    idx = jnp.arange(n_items, dtype=i32)
    before = (keys[None, :] < keys[:, None]) | ((keys[None, :] == keys[:, None]) & (idx[None, :] < idx[:, None]))
    order = jnp.sum(before.astype(i32), axis=1)
    starts = jnp.sum(jnp.where(order[:, None] == idx[None, :], keys[:, None], 0), axis=0)
    stops = jnp.concatenate([starts[1:], jnp.full((1,), total, i32)])
    tile = starts // EXPERT_TM
    expert = jnp.sum((ends[None, :] <= starts[:, None]).astype(i32), axis=1)
    expert = jnp.minimum(expert, MOE_EXPERTS - 1)
    used = n_tiles + jnp.sum(inner.astype(i32))
    first = (expert != jnp.concatenate([jnp.full((1,), -1, i32), expert[:-1]])) & (idx < used)
    parity = (jnp.cumsum(first.astype(i32)) - 1) % 2
    later = first[None, :] & (idx[None, :] > idx[:, None])
    nxt = jnp.min(jnp.where(later, idx[None, :], n_items), axis=1)
    next_expert = jnp.sum(jnp.where(idx[None, :] == nxt[:, None], expert[None, :], 0), axis=1)
    next_expert = jnp.where(nxt < n_items, next_expert, -1)
    keep = jnp.minimum(idx, used - 1)
    items = jnp.stack([tile, expert, starts - tile * EXPERT_TM, stops - tile * EXPERT_TM,
                       first.astype(i32), parity, next_expert, jnp.zeros_like(tile)])[:, keep]
    return pos.reshape(n // MOVE_TILE, 1, 2 * MOVE_TILE), items, used.reshape(1)


IT_TILE, IT_EXPERT, IT_LO, IT_HI, IT_FIRST, IT_PARITY, IT_NEXT = range(7)


def _row_copy(src, src_row, dst, dst_row, sem):
    return pltpu.make_async_copy(src.at[pl.ds(src_row, 1)], dst.at[pl.ds(dst_row, 1)], sem)


def _dispatch_kernel(pos_ref, x_ref, xs_ref, sem):
    def copy(r, s):
        return _row_copy(x_ref, r, xs_ref, pos_ref[0, 0, s * MOVE_TILE + r], sem)

    def start(r, c):
        for s in range(2):
            copy(r, s).start(priority=s)
        return c

    def wait(r, c):
        for s in range(2):
            copy(r, s).wait()
        return c

    lax.fori_loop(0, MOVE_TILE, start, 0, unroll=8)
    lax.fori_loop(0, MOVE_TILE, wait, 0, unroll=8)


def _dispatch_call(xn, pos):
    n = xn.shape[0]
    return pl.pallas_call(
        _dispatch_kernel,
        grid=(n // MOVE_TILE,),
        in_specs=[pl.BlockSpec((1, 1, 2 * MOVE_TILE), lambda i: (i, 0, 0), memory_space=pltpu.SMEM),
                  pl.BlockSpec((MOVE_TILE, D_MODEL), lambda i: (i, 0))],
        out_specs=pl.BlockSpec(memory_space=pl.ANY),
        out_shape=jax.ShapeDtypeStruct((2 * n, D_MODEL), xn.dtype),
        scratch_shapes=[pltpu.SemaphoreType.DMA(())],
        compiler_params=_cp(("arbitrary",)),
        name="moe_dispatch",
    )(pos, xn)


def _experts_kernel(items_ref, n_ref, layer_ref, xs_ref, wg_hbm, wu_hbm, wd_hbm, os_ref,
                    wg_buf, wu_buf, wd_buf, wg_scr, wu_scr, wd_scr, sem):
    i = pl.program_id(0)

    def weight_copies(expert, half):
        e = layer_ref[0] * MOE_EXPERTS + expert
        return [pltpu.make_async_copy(src.at[e], dst.at[half], sem.at[half])
                for src, dst in ((wg_hbm, wg_buf), (wu_hbm, wu_buf), (wd_hbm, wd_buf))]

    @pl.when(i < n_ref[0])
    def _():
        @pl.when(items_ref[IT_FIRST, i] == 1)
        def _():
            half = items_ref[IT_PARITY, i]
            own = weight_copies(items_ref[IT_EXPERT, i], half)

            @pl.when(i == 0)
            def _():
                for cp in own:
                    cp.start()

            for cp in own:
                cp.wait()
            wg_scr[...] = wg_buf[half].astype(bf16)
            wu_scr[...] = wu_buf[half].astype(bf16)
            wd_scr[...] = wd_buf[half].astype(bf16)
            nxt = items_ref[IT_NEXT, i]

            @pl.when(nxt >= 0)
            def _():
                for cp in weight_copies(nxt, 1 - half):
                    cp.start()

        x = xs_ref[...].astype(bf16)
        hg = _dot(x, wg_scr[...])
        hu = _dot(x, wu_scr[...])
        out = _dot((hg * _sigmoid(hg) * hu).astype(bf16), wd_scr[...])
        lo = items_ref[IT_LO, i]
        row = lax.broadcasted_iota(jnp.int32, (EXPERT_TM, 1), 0)
        mine = (row >= lo) & (row < items_ref[IT_HI, i])

        @pl.when(lo == 0)
        def _():
            os_ref[...] = jnp.where(mine, out, 0.0)

        @pl.when(lo != 0)
        def _():
            os_ref[...] = jnp.where(mine, out, os_ref[...])


def _experts_call(xs, items, n_items, layer, wg, wu, wd):
    rows = lambda i, items, n, layer: (items[IT_TILE, i], 0)
    hbm = pl.BlockSpec(memory_space=pl.ANY)
    return pl.pallas_call(
        _experts_kernel,
        grid_spec=pltpu.PrefetchScalarGridSpec(
            num_scalar_prefetch=3,
            grid=(items.shape[1],),
            in_specs=[pl.BlockSpec((EXPERT_TM, D_MODEL), rows), hbm, hbm, hbm],
            out_specs=pl.BlockSpec((EXPERT_TM, D_MODEL), rows),
            scratch_shapes=[pltpu.VMEM((2, D_MODEL, D_EXPERT), f32), pltpu.VMEM((2, D_MODEL, D_EXPERT), f32),
                            pltpu.VMEM((2, D_EXPERT, D_MODEL), f32),
                            pltpu.VMEM((D_MODEL, D_EXPERT), bf16), pltpu.VMEM((D_MODEL, D_EXPERT), bf16),
                            pltpu.VMEM((D_EXPERT, D_MODEL), bf16), pltpu.SemaphoreType.DMA((2,))],
        ),
        out_shape=jax.ShapeDtypeStruct(xs.shape, f32),
        compiler_params=_cp(("arbitrary",)),
        name="moe_experts",
    )(items, n_items, layer, xs, wg, wu, wd)


def _combine_kernel(pos_ref, pos_next_ref, x_ref, route_ref, gn_ref, os_ref, *rest, emit_x, split):
    outs, (buf, sem, xn_scr) = rest[:-3], rest[-3:]
    i = pl.program_id(0)
    half = i % 2
    last = i == pl.num_programs(0) - 1

    def copy(p_ref, hf, r, s):
        return _row_copy(os_ref, p_ref[0, 0, s * MOVE_TILE + r], buf.at[hf, s], r, sem.at[hf])

    def loop_all(p_ref, hf, op):
        def body(r, c):
            for s in range(2):
                op(copy(p_ref, hf, r, s), s)
            return c

        lax.fori_loop(0, MOVE_TILE, body, 0, unroll=8)

    begin = lambda cp, s: cp.start(priority=s)
    finish = lambda cp, s: cp.wait()

    @pl.when(i == 0)
    def _():
        loop_all(pos_ref, 0, begin)

    loop_all(pos_ref, half, finish)
    xn_ref = xn_scr if split else outs[-1]
    for c in range(MOVE_TILE // COMBINE_CHUNK):
        rows = slice(c * COMBINE_CHUNK, (c + 1) * COMBINE_CHUNK)
        route = route_ref[rows, :]
        x2 = x_ref[rows, :] + route[:, 4:5] * buf[half, 0, rows, :] + route[:, 5:6] * buf[half, 1, rows, :]
        if emit_x:
            outs[0][rows, :] = x2
        xn_ref[rows, :] = _rms(x2, gn_ref[...]).astype(xn_ref.dtype)
        for r in range(rows.start, rows.stop):
            for s in range(2):
                begin(copy(pos_next_ref, 1 - half, r, s), s)

    @pl.when(last)
    def _():
        loop_all(pos_next_ref, 1 - half, finish)

    if split:
        main_ref, tail_ref = outs[-2:]
        is_main = i < N_MAIN // MOVE_TILE

        @pl.when(is_main)
        def _():
            main_ref[...] = xn_scr[...]

        @pl.when(jnp.logical_not(is_main))
        def _():
            tail_ref[...] = xn_scr[...]


def _combine_call(x1, route, pos, os_rows, gn, xn_dtype, *, emit_x, split):
    n = x1.shape[0]
    row = lambda i: (i, 0)
    blk = lambda m: pl.BlockSpec((MOVE_TILE, D_MODEL), m)
    nm = N_MAIN // MOVE_TILE
    out_specs, out_shape = [], []
    if emit_x:
        out_specs.append(blk(row))
        out_shape.append(jax.ShapeDtypeStruct((n, D_MODEL), f32))
    if split:
        out_specs += [blk(lambda i: (jnp.minimum(i, nm - 1), 0)), blk(lambda i: (jnp.maximum(i - nm, 0), 0))]
        out_shape += [jax.ShapeDtypeStruct((N_MAIN, D_MODEL), xn_dtype),
                      jax.ShapeDtypeStruct((n - N_MAIN, D_MODEL), xn_dtype)]
    else:
        out_specs.append(blk(row))
        out_shape.append(jax.ShapeDtypeStruct((n, D_MODEL), xn_dtype))
    nt = n // MOVE_TILE
    pos_spec = lambda m: pl.BlockSpec((1, 1, 2 * MOVE_TILE), m, memory_space=pltpu.SMEM)
    return pl.pallas_call(
        functools.partial(_combine_kernel, emit_x=emit_x, split=split),
        grid=(nt,),
        in_specs=[pos_spec(lambda i: (i, 0, 0)), pos_spec(lambda i: (jnp.minimum(i + 1, nt - 1), 0, 0)),
                  blk(row), pl.BlockSpec((MOVE_TILE, LANES), row),
                  pl.BlockSpec((1, D_MODEL), lambda i: (0, 0)), pl.BlockSpec(memory_space=pl.ANY)],
        out_specs=out_specs,
        out_shape=out_shape,
        scratch_shapes=[pltpu.VMEM((2, 2, MOVE_TILE, D_MODEL), f32), pltpu.SemaphoreType.DMA((2,)),
                        pltpu.VMEM((MOVE_TILE, D_MODEL), xn_dtype)],
        compiler_params=_cp(("arbitrary",)),
        name="moe_combine",
    )(pos, pos, x1, route, gn, os_rows)


S5_GROUP = 16
S5_GROUPS = D_MODEL // S5_GROUP
S5_STATE = 64
S5_CB = 128
S5_NB = D_MODEL // S5_CB
S5_GPB = S5_CB // S5_GROUP
S5_SB = S5_GPB * S5_STATE
S5_BC = S5_STATE * S5_GROUP
S5_CPS = 2


def _s5_disc_kernel(lr_ref, li_ref, ldt_ref, bre_ref, bim_ref,
                    abr_ref, abi_ref, ab2r_ref, ab2i_ref, bbr_ref, bbi_ref, abbr_ref, abbi_ref):
    lr, li = lr_ref[...], li_ref[...]
    dt = jnp.exp(ldt_ref[...])
    mag = jnp.exp(lr * dt)
    ang = li * dt
    ab_re, ab_im = mag * jnp.cos(ang), mag * jnp.sin(ang)
    nr, ni = ab_re - 1.0, ab_im
    den = lr * lr + li * li
    f_re = (nr * lr + ni * li) / den
    f_im = (ni * lr - nr * li) / den
    abr_ref[...] = ab_re
    abi_ref[...] = ab_im
    ab2r_ref[...] = ab_re * ab_re - ab_im * ab_im
    ab2i_ref[...] = 2.0 * (ab_re * ab_im)
    pi = lax.broadcasted_iota(jnp.int32, (S5_STATE, S5_BC), 0)
    ci = lax.broadcasted_iota(jnp.int32, (S5_STATE, S5_BC), 1)
    rep = jnp.where(ci // S5_GROUP == pi, 1.0, 0.0).astype(bf16)

    def expand(v):
        hi, mid, lo = _split3(v)
        return _dot(hi, rep) + _dot(mid, rep) + _dot(lo, rep)

    fr, fi, ar, ai = expand(f_re), expand(f_im), expand(ab_re), expand(ab_im)
    br, bi = bre_ref[...], bim_ref[...]
    bb_re = fr * br - fi * bi
    bb_im = fr * bi + fi * br
    bbr_ref[...] = bb_re
    bbi_ref[...] = bb_im
    abbr_ref[...] = ar * bb_re - ai * bb_im
    abbi_ref[...] = ar * bb_im + ai * bb_re


def _s5_weights(lam_re, lam_im, log_dt, b_re, b_im, c_re, c_im):
    st = jax.ShapeDtypeStruct((S5_GROUPS, S5_STATE), f32)
    bc = jax.ShapeDtypeStruct((S5_GROUPS, S5_BC), f32)
    ab_re, ab_im, ab2_re, ab2_im, bb_re, bb_im, abb_re, abb_im = pl.pallas_call(
        _s5_disc_kernel, out_shape=[st, st, st, st, bc, bc, bc, bc], name="s5_discretize",
    )(lam_re, lam_im, log_dt[:, None], b_re.reshape(S5_GROUPS, S5_BC), b_im.reshape(S5_GROUPS, S5_BC))
    eye = jnp.eye(S5_GPB, dtype=f32)

    def in_blocks(m):
        m = m.reshape(S5_NB, S5_GPB, S5_STATE, S5_GROUP)
        return jnp.einsum("jgpc,gh->jgchp", m, eye).reshape(S5_NB, S5_CB, S5_SB)

    def out_blocks(m):
        m = m.reshape(S5_NB, S5_GPB, S5_GROUP, S5_STATE)
        return jnp.einsum("jgcp,gh->jgphc", m, eye).reshape(S5_NB, S5_SB, S5_CB)

    bb2 = jnp.concatenate([
        jnp.concatenate([in_blocks(bb_re), in_blocks(bb_im)], axis=2),
        jnp.concatenate([in_blocks(abb_re), in_blocks(abb_im)], axis=2)], axis=1).astype(bf16)
    cc = jnp.concatenate([out_blocks(c_re), -out_blocks(c_im)], axis=1).astype(bf16)
    rows = [v.reshape(S5_NB, 1, S5_SB) for v in (ab_re, ab_im, ab2_re, ab2_im)]
    abv = jnp.concatenate(rows + [jnp.zeros((S5_NB, SUB - len(rows), S5_SB), f32)], axis=1)
    return bb2, cc, abv


def _s5_seq_kernel(x0_ref, x1_ref, x2_ref, x3_ref, halo0_ref, h0_ref, bb2_ref, cc_ref, abv_ref, y_ref, hout_ref,
                   xf_scr, xp_scr, bu_scr, hs_scr, yp_scr, yn_scr, h_scr, halo_scr):
    tb = pl.program_id(1)
    TL = x0_ref.shape[0]
    KB = TL // 2
    RB = BATCH * TL

    @pl.when(tb == 0)
    def _():
        h_scr[...] = h0_ref[...]
        halo_scr[...] = halo0_ref[...].astype(f32)

    chans = [slice(c * S5_CB, (c + 1) * S5_CB) for c in range(S5_CPS)]
    for c, ch in enumerate(chans):
        for b, xb_ref in enumerate((x0_ref, x1_ref, x2_ref, x3_ref)):
            xf_scr[c, b * TL:(b + 1) * TL, :] = xb_ref[:, ch].astype(f32)
        for b in range(BATCH):
            for p in range(2):
                xp_scr[c, pl.ds(2 * b + p, KB, stride=SUB), :] = xf_scr[c, pl.ds(b * TL + p, KB, stride=2), :]
    x = jnp.concatenate([xp_scr[c] for c in range(S5_CPS)], axis=1)
    xc = jnp.concatenate([halo_scr[...], x], axis=0)
    odd = (lax.broadcasted_iota(jnp.int32, (RB + SUB, 1), 0) & 1) == 1
    xprev = jnp.where(odd, pltpu.roll(xc, 1, axis=0), pltpu.roll(xc, SUB - 1, axis=0))[SUB:]
    halo_scr[...] = x[RB - SUB:]
    for c, ch in enumerate(chans):
        lhs = jnp.concatenate([x[:, ch], xprev[:, ch]], axis=1).astype(bf16)
        bu_scr[c] = _dot(lhs, bb2_ref[c])
    a2 = [(abv_ref[c, 2:3, :], abv_ref[c, 3:4, :]) for c in range(S5_CPS)]

    def step(k, carry):
        r0 = pl.multiple_of(k * SUB, SUB)
        out = []
        for c in range(S5_CPS):
            hr, hi = carry[2 * c], carry[2 * c + 1]
            a2r, a2i = a2[c]
            bu = bu_scr[c, pl.ds(r0, SUB), :]
            nr = a2r * hr - a2i * hi + bu[:, :S5_SB]
            ni = a2r * hi + a2i * hr + bu[:, S5_SB:]
            hs_scr[c, pl.ds(r0, SUB), :S5_SB] = nr
            hs_scr[c, pl.ds(r0, SUB), S5_SB:] = ni
            out += [nr, ni]
        return tuple(out)

    init = []
    for c in range(S5_CPS):
        init += [h_scr[c, :, :S5_SB], h_scr[c, :, S5_SB:]]
    fin = lax.fori_loop(0, RB // SUB, step, tuple(init), unroll=4)
    for c, ch in enumerate(chans):
        h_scr[c, :, :S5_SB] = fin[2 * c]
        h_scr[c, :, S5_SB:] = fin[2 * c + 1]
        yp_scr[c] = _dot(hs_scr[c].astype(bf16), cc_ref[c])
        for b in range(BATCH):
            for p in range(2):
                yn_scr[c, pl.ds(b * TL + p, KB, stride=2), :] = yp_scr[c, pl.ds(2 * b + p, KB, stride=SUB), :]
            y_ref[b, :, ch] = yn_scr[c, b * TL:(b + 1) * TL, :]

    @pl.when(tb == pl.num_programs(1) - 1)
    def _():
        hout_ref[...] = h_scr[...]


def _s5_seq_call(x, halo0, h0, bb2, cc, abv, *, row0, seq_len, tl):
    wsel = lambda j, t: (j, 0, 0)
    rb = BATCH * tl
    cw = S5_CPS * S5_CB
    xspec = lambda b: pl.BlockSpec((tl, cw), lambda j, t: ((row0 + b * seq_len) // tl + t, j))
    return pl.pallas_call(
        _s5_seq_kernel,
        grid=(S5_NB // S5_CPS, seq_len // tl),
        in_specs=[xspec(b) for b in range(BATCH)]
        + [pl.BlockSpec((SUB, cw), lambda j, t: (0, j)),
           pl.BlockSpec((S5_CPS, SUB, 2 * S5_SB), wsel),
           pl.BlockSpec((S5_CPS, 2 * S5_CB, 2 * S5_SB), wsel),
           pl.BlockSpec((S5_CPS, 2 * S5_SB, S5_CB), wsel),
           pl.BlockSpec((S5_CPS, SUB, S5_SB), wsel)],
        out_specs=[pl.BlockSpec((BATCH, tl, cw), lambda j, t: (0, t, j)),
                   pl.BlockSpec((S5_CPS, SUB, 2 * S5_SB), wsel)],
        out_shape=[jax.ShapeDtypeStruct((BATCH, seq_len, D_MODEL), f32),
                   jax.ShapeDtypeStruct((S5_NB, SUB, 2 * S5_SB), f32)],
        scratch_shapes=[pltpu.VMEM((S5_CPS, rb, S5_CB), f32), pltpu.VMEM((S5_CPS, rb, S5_CB), f32),
                        pltpu.VMEM((S5_CPS, rb, 2 * S5_SB), f32), pltpu.VMEM((S5_CPS, rb, 2 * S5_SB), f32),
                        pltpu.VMEM((S5_CPS, rb, S5_CB), f32), pltpu.VMEM((S5_CPS, rb, S5_CB), f32),
                        pltpu.VMEM((S5_CPS, SUB, 2 * S5_SB), f32), pltpu.VMEM((SUB, cw), f32)],
        compiler_params=_cp(("arbitrary", "arbitrary")),
        name=f"s5_seq_{seq_len}",
    )(x, x, x, x, halo0, h0, bb2, cc, abv)


def _s5_sample_kernel(x_ref, hre_ref, him_ref, bb2_ref, cc_ref, abv_ref, y_ref, ore_ref, oim_ref, hs_scr):
    nb = hre_ref.shape[0]
    bu = _dot(x_ref[...], bb2_ref[0, :S5_CB, :])
    ar = abv_ref[0, 0:1, :]
    ai = abv_ref[0, 1:2, :]
    hr, hi = hre_ref[...], him_ref[...]
    for t in range(DEC_SEQ):
        rows = slice(t * nb, (t + 1) * nb)
        hr, hi = (ar * hr - ai * hi + bu[rows, :S5_SB], ar * hi + ai * hr + bu[rows, S5_SB:])
        hs_scr[rows, :S5_SB] = hr
        hs_scr[rows, S5_SB:] = hi
    y_ref[...] = _dot(hs_scr[...].astype(bf16), cc_ref[0])
    ore_ref[...] = hr
    oim_ref[...] = hi


def _s5_sample_call(xt, h_re, h_im, bb2, cc, abv):
    n = xt.shape[0]
    nb = h_re.shape[0]
    wsel = lambda j: (j, 0, 0)
    st = pl.BlockSpec((nb, S5_SB), lambda j: (0, j))
    return pl.pallas_call(
        _s5_sample_kernel,
        grid=(S5_NB,),
        in_specs=[pl.BlockSpec((n, S5_CB), lambda j: (0, j)), st, st,
                  pl.BlockSpec((1, 2 * S5_CB, 2 * S5_SB), wsel),
                  pl.BlockSpec((1, 2 * S5_SB, S5_CB), wsel),
                  pl.BlockSpec((1, SUB, S5_SB), wsel)],
        out_specs=[pl.BlockSpec((n, S5_CB), lambda j: (0, j)), st, st],
        out_shape=[jax.ShapeDtypeStruct((n, D_MODEL), f32),
                   jax.ShapeDtypeStruct(h_re.shape, f32), jax.ShapeDtypeStruct(h_im.shape, f32)],
        scratch_shapes=[pltpu.VMEM((n, 2 * S5_SB), f32)],
        compiler_params=_cp(("arbitrary",)),
        name="s5_sample",
    )(xt, h_re, h_im, bb2, cc, abv)


GLA_CHUNK = 256
GLA_DIRECT = 2
S5_TL = 256


def _moe_layer(layer, x1, xnf, route, plan, cnt, wg, wu, wd, gn, xn_dtype, *, emit_x, split):
    pos, items, n_items = _moe_plan(plan, cnt, x1.shape[0])
    xs = _dispatch_call(xnf, pos)
    os_rows = _experts_call(xs, items, n_items, jnp.full((1,), layer, jnp.int32), wg, wu, wd)
    return _combine_call(x1, route, pos, os_rows, gn, xn_dtype, emit_x=emit_x, split=split)


def kernel(x_prompt, x_sample, state_gla, state_s5_re, state_s5_im, meta_tokens, norm_mix_g, norm_ffn_g, norm_final_g, gla_w_in, gla_w_a2, gla_b_a, gla_g_o, gla_w_o, s5_lambda_re, s5_lambda_im, s5_log_dt, s5_b_re, s5_b_im, s5_c_re, s5_c_im, s5_d, s5_w_glu, s5_b_glu, moe_w_rg, moe_b_rg, moe_w_re, moe_b_re, moe_w_gate, moe_w_up, moe_w_down):
    row = lambda v: v.reshape(1, -1)
    x_main = x_prompt.reshape(N_MAIN, D_MODEL)
    x_tail = jnp.concatenate([
        jnp.tile(meta_tokens.astype(x_prompt.dtype), (BATCH, 1)),
        x_sample.reshape(N_SAMPLE, D_MODEL),
        jnp.zeros((N_ROWS - N_REAL, D_MODEL), x_prompt.dtype)], axis=0)
    wg = moe_w_gate.reshape(-1, D_MODEL, D_EXPERT)
    wu = moe_w_up.reshape(-1, D_MODEL, D_EXPERT)
    wd = moe_w_down.reshape(-1, D_EXPERT, D_MODEL)

    w_in = jnp.swapaxes(gla_w_in, 1, 2)
    wa1 = jnp.pad(gla_w_in[0, :, GLA_QKVR:], ((0, 0), (0, LANES - GLA_RANK)))
    wa2 = jnp.pad(gla_w_a2.reshape(GLA_RANK, GLA_KEY), ((0, LANES - GLA_RANK), (0, 0)))
    xn, glog = _norm_gate_call(x_main, x_tail, row(norm_mix_g[0]), wa1, wa2, row(gla_b_a))
    proj = _proj_call(xn, w_in, GLA_QKVR)
    go = row(gla_g_o)
    og = jnp.zeros((N_ROWS, GLA_VAL), bf16)
    s_zero = jnp.zeros((BATCH, GLA_HEADS, GLA_DK, GLA_DV), f32)
    og, s_meta = _gla_seq_call(proj, glog, go, s_zero, og, row0=ROW_META, C=N_META, n_chunks=1, d=GLA_DIRECT)
    og, s_prompt = _gla_seq_call(proj, glog, go, s_meta, og, row0=0, C=GLA_CHUNK, n_chunks=SEQ // GLA_CHUNK,
                                 d=GLA_DIRECT)
    og, s_sample = _gla_sample_call(proj, glog, go, state_gla.reshape(DEC_BATCH, GLA_HEADS, GLA_DK, GLA_DV), og,
                                    row0=ROW_SAMPLE)
    wr, br = _router_weights(moe_w_rg[0], moe_b_rg[0], moe_w_re[0], moe_b_re[0])
    tile_row = lambda i: (i, 0)
    x1, xnf, route, plan, cnt = _mix_out_call(
        _gla_out_kernel, "gla_out", N_ROWS, [og, x_main, x_tail], [tile_row, *_main_tail_maps(MIX_TILE)],
        [gla_w_o.reshape(GLA_VAL, D_MODEL), row(norm_ffn_g[0]), wr, br])
    x2, xn2 = _moe_layer(0, x1, xnf, route, plan, cnt, wg, wu, wd, row(norm_mix_g[1]), bf16, emit_x=True, split=False)

    bb2, cc, abv = _s5_weights(s5_lambda_re[0], s5_lambda_im[0], s5_log_dt[0], s5_b_re[0], s5_b_im[0],
                               s5_c_re[0], s5_c_im[0])
    y_meta, h_meta = _s5_seq_call(xn2, jnp.zeros((SUB, D_MODEL), bf16), jnp.zeros((S5_NB, SUB, 2 * S5_SB), f32),
                                  bb2, cc, abv, row0=ROW_META, seq_len=N_META, tl=N_META)
    halo = xn2[ROW_META:ROW_SAMPLE].reshape(BATCH, N_META, D_MODEL)[:, N_META - 2:].reshape(SUB, D_MODEL)
    y_main, h_main = _s5_seq_call(xn2, halo, h_meta, bb2, cc, abv, row0=0, seq_len=SEQ, tl=S5_TL)
    xt_sample = xn2[ROW_SAMPLE:N_REAL].reshape(DEC_BATCH, DEC_SEQ, D_MODEL).transpose(1, 0, 2).reshape(N_SAMPLE, D_MODEL)
    y_samp, s5r_s, s5i_s = _s5_sample_call(
        xt_sample, state_s5_re.reshape(DEC_BATCH, S5_GROUPS * S5_STATE),
        state_s5_im.reshape(DEC_BATCH, S5_GROUPS * S5_STATE), bb2, cc, abv)
    ys_tail = jnp.concatenate([
        y_meta.reshape(N_METAROWS, D_MODEL),
        y_samp.reshape(DEC_SEQ, DEC_BATCH, D_MODEL).transpose(1, 0, 2).reshape(N_SAMPLE, D_MODEL),
        jnp.zeros((N_ROWS - N_REAL, D_MODEL), f32)], axis=0)
    wr, br = _router_weights(moe_w_rg[1], moe_b_rg[1], moe_w_re[1], moe_b_re[1])
    x3, xnf, route, plan, cnt = _mix_out_call(
        _s5_out_kernel, "s5_out", N_ROWS, [y_main.reshape(N_MAIN, D_MODEL), ys_tail, x2],
        [*_main_tail_maps(MIX_TILE), tile_row],
        [s5_w_glu.reshape(D_MODEL, D_MODEL), row(norm_mix_g[1]), row(s5_d), row(s5_b_glu),
         row(norm_ffn_g[1]), wr, br])
    y_main_out, y_tail_out = _moe_layer(1, x3, xnf, route, plan, cnt, wg, wu, wd, row(norm_final_g), f32,
                                        emit_x=False, split=True)

    y_prompt = y_main_out.reshape(BATCH, SEQ, D_MODEL)
    y_sample = y_tail_out[N_METAROWS:N_METAROWS + N_SAMPLE].reshape(DEC_BATCH, DEC_SEQ, D_MODEL)
    hfin = h_main.reshape(S5_NB, BATCH, 2, 2, S5_GPB, S5_STATE)[:, :, 1]
    s5r_p = hfin[:, :, 0].transpose(1, 0, 2, 3).reshape(1, BATCH, S5_GROUPS, S5_STATE)
    s5i_p = hfin[:, :, 1].transpose(1, 0, 2, 3).reshape(1, BATCH, S5_GROUPS, S5_STATE)
    return (y_prompt, y_sample, s_prompt[None], s5r_p, s5i_p, s_sample[None],
            s5r_s.reshape(1, DEC_BATCH, S5_GROUPS, S5_STATE), s5i_s.reshape(1, DEC_BATCH, S5_GROUPS, S5_STATE))
```

```python
import functools

import jax
import jax.numpy as jnp
from jax import lax
from jax.experimental import pallas as pl
from jax.experimental.pallas import tpu as pltpu

f32 = jnp.float32
bf16 = jnp.bfloat16

D_MODEL = 2048
BATCH = 4
SEQ = 2048
DEC_BATCH = 128
DEC_SEQ = 4
N_META = 16
EPS = 1e-6
GLA_HEADS = 4
GLA_DK = 256
GLA_DV = 512
GLA_KEY = GLA_HEADS * GLA_DK
GLA_VAL = GLA_HEADS * GLA_DV
GLA_RANK = 16
GLA_TAU = 16.0
GLA_QKVR = 2 * GLA_KEY + 2 * GLA_VAL

N_MAIN = BATCH * SEQ
N_METAROWS = BATCH * N_META
N_SAMPLE = DEC_BATCH * DEC_SEQ
ROW_META = N_MAIN
ROW_SAMPLE = N_MAIN + N_METAROWS
N_REAL = ROW_SAMPLE + N_SAMPLE
ROW_TILE = 256
N_ROWS = -(-N_REAL // ROW_TILE) * ROW_TILE

VMEM_LIMIT = 56 * 1024 * 1024


def _cp(sem, vmem=VMEM_LIMIT):
    return pltpu.CompilerParams(dimension_semantics=sem, vmem_limit_bytes=vmem)


def _dot(a, b):
    return jnp.dot(a, b, preferred_element_type=f32)


def _dot_nt(a, b):
    return lax.dot_general(a, b, (((1,), (1,)), ((), ())), preferred_element_type=f32)


def _dot_tn(a, b):
    return lax.dot_general(a, b, (((0,), (0,)), ((), ())), preferred_element_type=f32)


def _sigmoid(x):
    return 1.0 / (1.0 + jnp.exp(-x))


def _split3(x):
    hi = x.astype(bf16)
    r1 = x - hi.astype(f32)
    mid = r1.astype(bf16)
    lo = (r1 - mid.astype(f32)).astype(bf16)
    return hi, mid, lo


def _cumsum_rows(g, C):
    if C <= 16:
        row = lax.broadcasted_iota(jnp.int32, (C, 1), 0)
        b = jnp.zeros_like(g)
        for s in range(C):
            b = b + jnp.where(row >= s, g[s:s + 1, :], 0.0)
        return b
    row = lax.broadcasted_iota(jnp.int32, (C, C), 0)
    col = lax.broadcasted_iota(jnp.int32, (C, C), 1)
    tri = jnp.where(row >= col, 1.0, 0.0).astype(bf16)
    hi, mid, lo = _split3(g)
    return _dot(tri, hi) + _dot(tri, mid) + _dot(tri, lo)


PAIR_LEVEL = 1000


def _gla_pair_code(C, d, tree=True):
    ti = lax.broadcasted_iota(jnp.int32, (C, C), 0)
    si = lax.broadcasted_iota(jnp.int32, (C, C), 1)
    code = jnp.where((ti // d == si // d) & (si <= ti), 1 + ti - si, 0)
    h = d
    while tree and h < C:
        tb = ti // h
        code = jnp.where(((tb % 2) == 1) & ((si // h) == tb - 1), PAIR_LEVEL + h, code)
        h *= 2
    return code


def _gla_scores(q, k, b, code, C, d, tree=True):
    row = lax.broadcasted_iota(jnp.int32, (C, 1), 0)
    scores = jnp.zeros((C, C), f32)
    for dl in range(d):
        ks = k if dl == 0 else pltpu.roll(k, dl, axis=0)
        bs = b if dl == 0 else pltpu.roll(b, dl, axis=0)
        term = q * ks * jnp.exp(jnp.minimum(b - bs, 0.0))
        colv = jnp.sum(term, axis=1, keepdims=True)
        scores = jnp.where(code == 1 + dl, colv, scores)
    z = b
    s = 1
    while tree and 2 * s < C:
        z = jnp.where((row & s) != 0, pltpu.roll(z, s, axis=0), z)
        s *= 2
        h = s
        if h < d:
            continue
        bnext = pltpu.roll(z, C - h, axis=0)
        qh = (q * jnp.exp(jnp.minimum(b - z, 0.0))).astype(bf16)
        kh = (k * jnp.exp(jnp.minimum(bnext - b, 0.0))).astype(bf16)
        scores = jnp.where(code == PAIR_LEVEL + h, _dot_nt(qh, kh), scores)
    return scores


def _gla_chunk(q, k, v, g, S, code, C, d, tree=True):
    b = _cumsum_rows(g, C)
    o = _dot((q * jnp.exp(b)).astype(bf16), S.astype(bf16))
    scores = _gla_scores(q, k, b, code, C, d, tree)
    o = o + _dot(scores.astype(bf16), v)
    b_last = b[C - 1:C, :]
    kd = (k * jnp.exp(b_last - b)).astype(bf16)
    if C == GLA_DK:
        eye = code == 1
    else:
        eye = (lax.broadcasted_iota(jnp.int32, (GLA_DK, GLA_DK), 0)
               == lax.broadcasted_iota(jnp.int32, (GLA_DK, GLA_DK), 1))
    dec_col = jnp.sum(jnp.where(eye, jnp.exp(b_last), 0.0), axis=1, keepdims=True)
    S_new = dec_col * S + _dot_tn(kd, v)
    return o, S_new


def _gla_head_epilogue(o, r, go):
    ms = jnp.mean(o * o, axis=1, keepdims=True)
    on = o * lax.rsqrt(ms + EPS) * go
    return on * (r * _sigmoid(r))


def _gla_seq_kernel(q_ref, k_ref, v_ref, r_ref, g_ref, go_ref, s0_ref, _og_in, og_ref, sout_ref, s_scr, *, C, d):
    c = pl.program_id(1)

    @pl.when(c == 0)
    def _():
        s_scr[...] = s0_ref[0]

    code = _gla_pair_code(C, d)

    def head(h, carry):
        ck = pl.ds(pl.multiple_of(h * GLA_DK, GLA_DK), GLA_DK)
        cv = pl.ds(pl.multiple_of(h * GLA_DV, GLA_DV), GLA_DV)
        q = q_ref[:, ck].astype(f32) * (GLA_DK ** -0.5)
        k = k_ref[:, ck].astype(f32)
        o, S_new = _gla_chunk(q, k, v_ref[:, cv], g_ref[:, ck], s_scr[h], code, C, d)
        s_scr[h] = S_new
        og_ref[:, cv] = _gla_head_epilogue(o, r_ref[:, cv].astype(f32), go_ref[:, cv]).astype(og_ref.dtype)
        return carry

    lax.fori_loop(0, GLA_HEADS, head, 0)

    @pl.when(c == pl.num_programs(1) - 1)
    def _():
        sout_ref[0] = s_scr[...]


def _gla_seq_call(proj, glog, go, s0, og_buf, *, row0, C, n_chunks, d):
    blk0 = row0 // C
    rows = lambda b, c: blk0 + b * n_chunks + c
    kern = functools.partial(_gla_seq_kernel, C=C, d=d)
    return pl.pallas_call(
        kern,
        grid=(BATCH, n_chunks),
        in_specs=[
            pl.BlockSpec((C, GLA_KEY), lambda b, c: (rows(b, c), 0)),
            pl.BlockSpec((C, GLA_KEY), lambda b, c: (rows(b, c), 1)),
            pl.BlockSpec((C, GLA_VAL), lambda b, c: (rows(b, c), 1)),
            pl.BlockSpec((C, GLA_VAL), lambda b, c: (rows(b, c), 2)),
            pl.BlockSpec((C, GLA_KEY), lambda b, c: (rows(b, c), 0)),
            pl.BlockSpec((1, GLA_VAL), lambda b, c: (0, 0)),
            pl.BlockSpec((1, GLA_HEADS, GLA_DK, GLA_DV), lambda b, c: (b, 0, 0, 0)),
            pl.BlockSpec(memory_space=pl.ANY),
        ],
        out_specs=[
            pl.BlockSpec((C, GLA_VAL), lambda b, c: (rows(b, c), 0)),
            pl.BlockSpec((1, GLA_HEADS, GLA_DK, GLA_DV), lambda b, c: (b, 0, 0, 0)),
        ],
        out_shape=[
            jax.ShapeDtypeStruct(og_buf.shape, og_buf.dtype),
            jax.ShapeDtypeStruct((BATCH, GLA_HEADS, GLA_DK, GLA_DV), f32),
        ],
        scratch_shapes=[pltpu.VMEM((GLA_HEADS, GLA_DK, GLA_DV), f32)],
        input_output_aliases={7: 0},
        compiler_params=_cp(("arbitrary", "arbitrary")),
        name=f"gla_seq_c{C}",
    )(proj, proj, proj, proj, glog, go, s0, og_buf)


SAMPLE_BB = 4
SAMPLE_C = SAMPLE_BB * DEC_SEQ


def _gla_sample_kernel(q_ref, k_ref, v_ref, r_ref, g_ref, go_ref, s0_ref, _og_in, og_ref, sout_ref):
    row = lax.broadcasted_iota(jnp.int32, (SAMPLE_C, 1), 0)

    code = _gla_pair_code(SAMPLE_C, DEC_SEQ, tree=False)

    def take(x, bb):
        sh = (SAMPLE_C - DEC_SEQ * bb) % SAMPLE_C
        return jnp.where(row < DEC_SEQ, pltpu.roll(x, sh, axis=0) if sh else x, 0.0)

    def head(h, carry):
        ck = pl.ds(pl.multiple_of(h * GLA_DK, GLA_DK), GLA_DK)
        cv = pl.ds(pl.multiple_of(h * GLA_DV, GLA_DV), GLA_DV)
        q_all = q_ref[:, ck].astype(f32) * (GLA_DK ** -0.5)
        k_all = k_ref[:, ck].astype(f32)
        v_all = v_ref[:, cv].astype(f32)
        r_all = r_ref[:, cv].astype(f32)
        g_all = g_ref[:, ck]
        go = go_ref[:, cv]
        acc = jnp.zeros((SAMPLE_C, GLA_DV), f32)
        for bb in range(SAMPLE_BB):
            o, S_new = _gla_chunk(take(q_all, bb), take(k_all, bb), take(v_all, bb).astype(bf16),
                                  take(g_all, bb), s0_ref[bb, h], code, SAMPLE_C, DEC_SEQ, tree=False)
            sout_ref[bb, h] = S_new
            y = _gla_head_epilogue(o, take(r_all, bb), go)
            acc = jnp.where(row // DEC_SEQ == bb, pltpu.roll(y, DEC_SEQ * bb, axis=0) if bb else y, acc)
        og_ref[:, cv] = acc.astype(og_ref.dtype)
        return carry

    lax.fori_loop(0, GLA_HEADS, head, 0)


def _gla_sample_call(proj, glog, go, s0, og_buf, *, row0):
    n_seq = s0.shape[0]
    blk0 = row0 // SAMPLE_C
    st_spec = pl.BlockSpec((SAMPLE_BB, GLA_HEADS, GLA_DK, GLA_DV), lambda i: (i, 0, 0, 0))
    return pl.pallas_call(
        _gla_sample_kernel,
        grid=(n_seq // SAMPLE_BB,),
        in_specs=[
            pl.BlockSpec((SAMPLE_C, GLA_KEY), lambda i: (blk0 + i, 0)),
            pl.BlockSpec((SAMPLE_C, GLA_KEY), lambda i: (blk0 + i, 1)),
            pl.BlockSpec((SAMPLE_C, GLA_VAL), lambda i: (blk0 + i, 1)),
            pl.BlockSpec((SAMPLE_C, GLA_VAL), lambda i: (blk0 + i, 2)),
            pl.BlockSpec((SAMPLE_C, GLA_KEY), lambda i: (blk0 + i, 0)),
            pl.BlockSpec((1, GLA_VAL), lambda i: (0, 0)),
            st_spec,
            pl.BlockSpec(memory_space=pl.ANY),
        ],
        out_specs=[pl.BlockSpec((SAMPLE_C, GLA_VAL), lambda i: (blk0 + i, 0)), st_spec],
        out_shape=[jax.ShapeDtypeStruct(og_buf.shape, og_buf.dtype), jax.ShapeDtypeStruct(s0.shape, f32)],
        input_output_aliases={7: 0},
        compiler_params=_cp(("arbitrary",)),
        name="gla_sample",
    )(proj, proj, proj, proj, glog, go, s0, og_buf)


LANES = 128
SUB = 8
MOE_GROUPS = 4
MOE_EPG = 8
MOE_EXPERTS = MOE_GROUPS * MOE_EPG
D_EXPERT = 256
ROUTE_E0 = MOE_GROUPS
MIX_TILE = 256
PROJ_NT = 1024
PROJ_MT = N_ROWS // 7


def _rms(x, g):
    r = lax.rsqrt(jnp.mean(x * x, axis=-1, keepdims=True) + EPS)
    return (x * r) * g


def _log_sigmoid(z):
    return jnp.minimum(z, 0.0) - jnp.log1p(jnp.exp(-jnp.abs(z)))


def _main_or_tail(tile, main_ref, tail_ref):
    return jnp.where(pl.program_id(0) < N_MAIN // tile, main_ref[...], tail_ref[...])


def _main_tail_maps(tile):
    nm = N_MAIN // tile
    return (lambda i: (jnp.minimum(i, nm - 1), 0)), (lambda i: (jnp.maximum(i - nm, 0), 0))


def _norm_gate_kernel(xm_ref, xt_ref, gn_ref, wa1_ref, wa2_ref, ba_ref, xn_ref, gl_ref):
    xnb = _rms(_main_or_tail(ROW_TILE, xm_ref, xt_ref), gn_ref[...]).astype(bf16)
    xn_ref[...] = xnb
    a = _dot(xnb, wa1_ref[...].astype(bf16))
    z = _dot(a.astype(bf16), wa2_ref[...].astype(bf16)) + ba_ref[...]
    gl_ref[...] = _log_sigmoid(z) * (1.0 / GLA_TAU)


def _norm_gate_call(x_main, x_tail, gn, wa1, wa2, ba):
    n = x_main.shape[0] + x_tail.shape[0]
    row = lambda i: (i, 0)
    fix = lambda i: (0, 0)
    main_map, tail_map = _main_tail_maps(ROW_TILE)
    return pl.pallas_call(
        _norm_gate_kernel,
        grid=(n // ROW_TILE,),
        in_specs=[pl.BlockSpec((ROW_TILE, D_MODEL), main_map), pl.BlockSpec((ROW_TILE, D_MODEL), tail_map),
                  pl.BlockSpec((1, D_MODEL), fix),
                  pl.BlockSpec((D_MODEL, LANES), fix), pl.BlockSpec((LANES, GLA_KEY), fix),
                  pl.BlockSpec((1, GLA_KEY), fix)],
        out_specs=[pl.BlockSpec((ROW_TILE, D_MODEL), row), pl.BlockSpec((ROW_TILE, GLA_KEY), row)],
        out_shape=[jax.ShapeDtypeStruct((n, D_MODEL), bf16), jax.ShapeDtypeStruct((n, GLA_KEY), f32)],
        compiler_params=_cp(("arbitrary",)),
        name="norm_gate",
    )(x_main, x_tail, gn, wa1, wa2, ba)


def _proj_kernel(xn_ref, w_ref, o_ref, wb_scr):
    @pl.when(pl.program_id(1) == 0)
    def _():
        wb_scr[...] = w_ref[0].astype(bf16)

    o_ref[...] = _dot_nt(xn_ref[...], wb_scr[...]).astype(o_ref.dtype)


def _proj_call(xn, wt, n_cols):
    n = xn.shape[0]
    return pl.pallas_call(
        _proj_kernel,
        grid=(n_cols // PROJ_NT, n // PROJ_MT),
        in_specs=[pl.BlockSpec((PROJ_MT, D_MODEL), lambda j, i: (i, 0)),
                  pl.BlockSpec((1, PROJ_NT, D_MODEL), lambda j, i: (0, j, 0))],
        out_specs=pl.BlockSpec((PROJ_MT, PROJ_NT), lambda j, i: (i, j)),
        out_shape=jax.ShapeDtypeStruct((n, n_cols), bf16),
        scratch_shapes=[pltpu.VMEM((PROJ_NT, D_MODEL), bf16)],
        compiler_params=_cp(("arbitrary", "arbitrary")),
        name="gla_proj",
    )(xn, wt)


def _route(xn, wr, br, cnt_ref):
    R = xn.shape[0]
    xh = xn.astype(bf16)
    xl = (xn - xh.astype(f32)).astype(bf16)
    wh = wr.astype(bf16)
    wl = (wr - wh.astype(f32)).astype(bf16)
    logits = _dot(xh, wh) + _dot(xl, wh) + _dot(xh, wl) + br
    lane_i = lax.broadcasted_iota(jnp.int32, (R, LANES), 1)
    lane = lane_i.astype(f32)
    neg = -jnp.inf
    big = float(LANES)
    is_g = lane_i < MOE_GROUPS
    lg = jnp.where(is_g, logits, neg)
    mg = jnp.max(lg, axis=1, keepdims=True)
    gidx = jnp.min(jnp.where(lg == mg, lane, big), axis=1, keepdims=True)
    ptop = 1.0 / jnp.sum(jnp.where(is_g, jnp.exp(logits - mg), 0.0), axis=1, keepdims=True)
    lo = ROUTE_E0 + MOE_EPG * gidx
    le = jnp.where((lane >= lo) & (lane < lo + MOE_EPG), logits, neg)
    v1 = jnp.max(le, axis=1, keepdims=True)
    i1 = jnp.min(jnp.where(le == v1, lane, big), axis=1, keepdims=True)
    le2 = jnp.where(lane == i1, neg, le)
    v2 = jnp.max(le2, axis=1, keepdims=True)
    i2 = jnp.min(jnp.where(le2 == v2, lane, big), axis=1, keepdims=True)
    s = jnp.exp(v2 - v1)
    w0 = ptop / (1.0 + s)
    w1 = ptop * s / (1.0 + s)
    oh = jnp.where((lane == i1) | (lane == i2), 1.0, 0.0)
    ri = lax.broadcasted_iota(jnp.int32, (R, R), 0)
    ci = lax.broadcasted_iota(jnp.int32, (R, R), 1)
    before = jnp.where(ri > ci, 1.0, 0.0).astype(bf16)
    tot = _dot(before, oh.astype(bf16)) + cnt_ref[...]
    rank0 = jnp.sum(jnp.where(lane == i1, tot, 0.0), axis=1, keepdims=True)
    rank1 = jnp.sum(jnp.where(lane == i2, tot, 0.0), axis=1, keepdims=True)
    cnt_ref[...] = cnt_ref[...] + jnp.sum(oh, axis=0, keepdims=True)
    vals = (i1 - ROUTE_E0, i2 - ROUTE_E0, rank0, rank1, w0, w1)
    slab = jnp.zeros((R, LANES), f32)
    for j, v in enumerate(vals):
        slab = jnp.where(lane_i == j, v, slab)
    sub = lax.broadcasted_iota(jnp.int32, (SUB, R), 0)
    plan = jnp.zeros((SUB, R), f32)
    for j, v in enumerate(vals[:4]):
        as_row = jnp.sum(jnp.where(ri == ci, v, 0.0), axis=0, keepdims=True)
        plan = jnp.where(sub == j, as_row, plan)
    return slab, plan


def _router_weights(w_rg, b_rg, w_re, b_re):
    w = jnp.concatenate([w_rg, jnp.moveaxis(w_re, 0, 1).reshape(D_MODEL, MOE_EXPERTS)], axis=1)
    b = jnp.concatenate([b_rg, b_re.reshape(MOE_EXPERTS)])
    pad = LANES - w.shape[1]
    return jnp.pad(w, ((0, 0), (0, pad))), jnp.pad(b, (0, pad))[None]


def _mix_out_tail(x1, gffn_ref, wr_ref, br_ref, x1_ref, xn_ref, route_ref, plan_ref, cnt_ref):
    @pl.when(pl.program_id(0) == 0)
    def _():
        cnt_ref[...] = jnp.zeros_like(cnt_ref)

    x1_ref[...] = x1
    xn = _rms(x1, gffn_ref[...])
    xn_ref[...] = xn
    route_ref[...], plan_ref[...] = _route(xn, wr_ref[...], br_ref[...], cnt_ref)


def _cast_once(w_ref, w_scr):
    @pl.when(pl.program_id(0) == 0)
    def _():
        w_scr[...] = w_ref[...].astype(bf16)


def _gla_out_kernel(og_ref, xm_ref, xt_ref, wo_ref, gffn_ref, wr_ref, br_ref, *rest):
    outs, w_scr = rest[:-1], rest[-1]
    _cast_once(wo_ref, w_scr)
    x1 = _main_or_tail(MIX_TILE, xm_ref, xt_ref) + _dot(og_ref[...], w_scr[...])
    _mix_out_tail(x1, gffn_ref, wr_ref, br_ref, *outs)


def _gelu_tanh(x):
    return x * (0.5 * (1.0 + jnp.tanh(0.7978845608028654 * (x + 0.044715 * (x * x * x)))))


def _s5_out_kernel(ys_main_ref, ys_tail_ref, x_ref, wglu_ref, gmix_ref, d_ref, bglu_ref, gffn_ref, wr_ref, br_ref,
                   *rest):
    outs, w_scr = rest[:-1], rest[-1]
    _cast_once(wglu_ref, w_scr)
    x = x_ref[...]
    u = _rms(x, gmix_ref[...])
    y = _gelu_tanh(_main_or_tail(MIX_TILE, ys_main_ref, ys_tail_ref) + d_ref[...] * u)
    z = _dot(y.astype(bf16), w_scr[...]) + bglu_ref[...]
    _mix_out_tail(x + y * _sigmoid(z), gffn_ref, wr_ref, br_ref, *outs)


def _mix_out_call(kern, name, n, row_ins, row_maps, fix_ins):
    row = lambda i: (i, 0)
    fix = lambda i: (0, 0)
    out_row = lambda w: pl.BlockSpec((MIX_TILE, w), row)
    nt = n // MIX_TILE
    return pl.pallas_call(
        kern,
        grid=(nt,),
        in_specs=[pl.BlockSpec((MIX_TILE, a.shape[1]), m) for a, m in zip(row_ins, row_maps)]
        + [pl.BlockSpec(a.shape, fix, pipeline_mode=pl.Buffered(1)) for a in fix_ins],
        out_specs=[out_row(D_MODEL), out_row(D_MODEL), out_row(LANES), pl.BlockSpec((SUB, MIX_TILE), row),
                   pl.BlockSpec((1, LANES), fix)],
        out_shape=[jax.ShapeDtypeStruct((n, D_MODEL), f32), jax.ShapeDtypeStruct((n, D_MODEL), f32),
                   jax.ShapeDtypeStruct((n, LANES), f32), jax.ShapeDtypeStruct((nt * SUB, MIX_TILE), f32),
                   jax.ShapeDtypeStruct((1, LANES), f32)],
        scratch_shapes=[pltpu.VMEM((D_MODEL, D_MODEL), bf16)],
        compiler_params=_cp(("arbitrary",)),
        name=name,
    )(*row_ins, *fix_ins)


EXPERT_TM = 256
MOVE_TILE = MIX_TILE
COMBINE_CHUNK = 32


def _moe_plan(plan, cnt, n):
    i32 = jnp.int32
    v = plan.reshape(n // MOVE_TILE, SUB, MOVE_TILE)[:, :4].astype(i32)
    counts = cnt[0, ROUTE_E0:ROUTE_E0 + MOE_EXPERTS].astype(i32)
    ends = jnp.cumsum(counts)
    off = ends - counts
    ids = jnp.arange(MOE_EXPERTS, dtype=i32)
    pos = jnp.sum(jnp.where(v[:, 0:2, :, None] == ids, off, 0), axis=-1) + v[:, 2:4]
    total = 2 * n
    n_tiles = total // EXPERT_TM
    n_items = n_tiles + MOE_EXPERTS
    inner = (counts > 0) & (off % EXPERT_TM != 0)
    keys = jnp.concatenate([jnp.arange(n_tiles, dtype=i32) * EXPERT_TM, jnp.where(inner, off, total)])
    idx = jnp.arange(n_items, dtype=i32)
    before = (keys[None, :] < keys[:, None]) | ((keys[None, :] == keys[:, None]) & (idx[None, :] < idx[:, None]))
    order = jnp.sum(before.astype(i32), axis=1)
    starts = jnp.sum(jnp.where(order[:, None] == idx[None, :], keys[:, None], 0), axis=0)
    stops = jnp.concatenate([starts[1:], jnp.full((1,), total, i32)])
    tile = starts // EXPERT_TM
    expert = jnp.sum((ends[None, :] <= starts[:, None]).astype(i32), axis=1)
    expert = jnp.minimum(expert, MOE_EXPERTS - 1)
    used = n_tiles + jnp.sum(inner.astype(i32))
    first = (expert != jnp.concatenate([jnp.full((1,), -1, i32), expert[:-1]])) & (idx < used)
    parity = (jnp.cumsum(first.astype(i32)) - 1) % 2
    later = first[None, :] & (idx[None, :] > idx[:, None])
    nxt = jnp.min(jnp.where(later, idx[None, :], n_items), axis=1)
    next_expert = jnp.sum(jnp.where(idx[None, :] == nxt[:, None], expert[None, :], 0), axis=1)
    next_expert = jnp.where(nxt < n_items, next_expert, -1)
    keep = jnp.minimum(idx, used - 1)
    items = jnp.stack([tile, expert, starts - tile * EXPERT_TM, stops - tile * EXPERT_TM,
                       first.astype(i32), parity, next_expert, jnp.zeros_like(tile)])[:, keep]
    return pos.reshape(n // MOVE_TILE, 1, 2 * MOVE_TILE), items, used.reshape(1)


IT_TILE, IT_EXPERT, IT_LO, IT_HI, IT_FIRST, IT_PARITY, IT_NEXT = range(7)


def _row_copy(src, src_row, dst, dst_row, sem):
    return pltpu.make_async_copy(src.at[pl.ds(src_row, 1)], dst.at[pl.ds(dst_row, 1)], sem)


def _source_rows(pos, n):
    token = jnp.arange(n, dtype=jnp.int32).reshape(n // MOVE_TILE, 1, MOVE_TILE)
    token = jnp.broadcast_to(token, (n // MOVE_TILE, 2, MOVE_TILE)).reshape(-1)
    src = jnp.zeros((2 * n,), jnp.int32).at[pos.reshape(-1)].set(token, unique_indices=True)
    return src.reshape(2 * n // EXPERT_TM, 1, EXPERT_TM)


def _experts_kernel(items_ref, n_ref, layer_ref, src_ref, src_next_ref, x_hbm, wg_hbm, wu_hbm, wd_hbm, os_ref,
                    x_buf, wg_buf, wu_buf, wd_buf, wg_scr, wu_scr, wd_scr, xsem, sem):
    i = pl.program_id(0)
    n_tiles = x_hbm.shape[0] * 2 // EXPERT_TM

    def gather(s_ref, half, op):
        def body(r, c):
            op(_row_copy(x_hbm, s_ref[0, 0, r], x_buf.at[half], r, xsem.at[half]))
            return c

        lax.fori_loop(0, EXPERT_TM, body, 0, unroll=8)

    def weight_copies(expert, half):
        e = layer_ref[0] * MOE_EXPERTS + expert
        return [pltpu.make_async_copy(src.at[e], dst.at[half], sem.at[half])
                for src, dst in ((wg_hbm, wg_buf), (wu_hbm, wu_buf), (wd_hbm, wd_buf))]

    @pl.when(i < n_ref[0])
    def _():
        @pl.when(items_ref[IT_FIRST, i] == 1)
        def _():
            half = items_ref[IT_PARITY, i]
            own = weight_copies(items_ref[IT_EXPERT, i], half)

            @pl.when(i == 0)
            def _():
                for cp in own:
                    cp.start()

            for cp in own:
                cp.wait()
            wg_scr[...] = wg_buf[half].astype(bf16)
            wu_scr[...] = wu_buf[half].astype(bf16)
            wd_scr[...] = wd_buf[half].astype(bf16)
            nxt = items_ref[IT_NEXT, i]

            @pl.when(nxt >= 0)
            def _():
                for cp in weight_copies(nxt, 1 - half):
                    cp.start()

        tile = items_ref[IT_TILE, i]
        xhalf = tile % 2

        @pl.when(items_ref[IT_LO, i] == 0)
        def _():
            @pl.when(i == 0)
            def _():
                gather(src_ref, 0, lambda cp: cp.start())

            @pl.when(tile + 1 < n_tiles)
            def _():
                gather(src_next_ref, 1 - xhalf, lambda cp: cp.start())

            gather(src_ref, xhalf, lambda cp: cp.wait())

        x = x_buf[xhalf].astype(bf16)
        hg = _dot(x, wg_scr[...])
        hu = _dot(x, wu_scr[...])
        out = _dot((hg * _sigmoid(hg) * hu).astype(bf16), wd_scr[...])
        lo = items_ref[IT_LO, i]
        row = lax.broadcasted_iota(jnp.int32, (EXPERT_TM, 1), 0)
        mine = (row >= lo) & (row < items_ref[IT_HI, i])

        @pl.when(lo == 0)
        def _():
            os_ref[...] = jnp.where(mine, out, 0.0)

        @pl.when(lo != 0)
        def _():
            os_ref[...] = jnp.where(mine, out, os_ref[...])


def _experts_call(xn, src, items, n_items, layer, wg, wu, wd):
    n_tiles = src.shape[0]
    rows = lambda i, items, n, layer: (items[IT_TILE, i], 0)
    src_spec = lambda m: pl.BlockSpec((1, 1, EXPERT_TM), m, memory_space=pltpu.SMEM)
    hbm = pl.BlockSpec(memory_space=pl.ANY)
    return pl.pallas_call(
        _experts_kernel,
        grid_spec=pltpu.PrefetchScalarGridSpec(
            num_scalar_prefetch=3,
            grid=(items.shape[1],),
            in_specs=[src_spec(lambda i, items, n, layer: (items[IT_TILE, i], 0, 0)),
                      src_spec(lambda i, items, n, layer: (jnp.minimum(items[IT_TILE, i] + 1, n_tiles - 1), 0, 0)),
                      hbm, hbm, hbm, hbm],
            out_specs=pl.BlockSpec((EXPERT_TM, D_MODEL), rows),
            scratch_shapes=[pltpu.VMEM((2, EXPERT_TM, D_MODEL), f32),
                            pltpu.VMEM((2, D_MODEL, D_EXPERT), f32), pltpu.VMEM((2, D_MODEL, D_EXPERT), f32),
                            pltpu.VMEM((2, D_EXPERT, D_MODEL), f32),
                            pltpu.VMEM((D_MODEL, D_EXPERT), bf16), pltpu.VMEM((D_MODEL, D_EXPERT), bf16),
                            pltpu.VMEM((D_EXPERT, D_MODEL), bf16),
                            pltpu.SemaphoreType.DMA((2,)), pltpu.SemaphoreType.DMA((2,))],
        ),
        out_shape=jax.ShapeDtypeStruct((2 * xn.shape[0], D_MODEL), f32),
        compiler_params=_cp(("arbitrary",)),
        name="moe_experts",
    )(items, n_items, layer, src, src, xn, wg, wu, wd)


def _combine_kernel(pos_ref, pos_next_ref, x_ref, route_ref, gn_ref, os_ref, *rest, emit_x, split):
    outs, (buf, sem, xn_scr) = rest[:-3], rest[-3:]
    i = pl.program_id(0)
    half = i % 2
    last = i == pl.num_programs(0) - 1

    def copy(p_ref, hf, r, s):
        return _row_copy(os_ref, p_ref[0, 0, s * MOVE_TILE + r], buf.at[hf, s], r, sem.at[hf])

    def loop_all(p_ref, hf, op):
        def body(r, c):
            for s in range(2):
                op(copy(p_ref, hf, r, s), s)
            return c

        lax.fori_loop(0, MOVE_TILE, body, 0, unroll=8)

    begin = lambda cp, s: cp.start(priority=s)
    finish = lambda cp, s: cp.wait()

    @pl.when(i == 0)
    def _():
        loop_all(pos_ref, 0, begin)

    loop_all(pos_ref, half, finish)
    xn_ref = xn_scr if split else outs[-1]
    for c in range(MOVE_TILE // COMBINE_CHUNK):
        rows = slice(c * COMBINE_CHUNK, (c + 1) * COMBINE_CHUNK)
        route = route_ref[rows, :]
        x2 = x_ref[rows, :] + route[:, 4:5] * buf[half, 0, rows, :] + route[:, 5:6] * buf[half, 1, rows, :]
        if emit_x:
            outs[0][rows, :] = x2
        xn_ref[rows, :] = _rms(x2, gn_ref[...]).astype(xn_ref.dtype)
        for r in range(rows.start, rows.stop):
            for s in range(2):
                begin(copy(pos_next_ref, 1 - half, r, s), s)

    @pl.when(last)
    def _():
        loop_all(pos_next_ref, 1 - half, finish)

    if split:
        main_ref, tail_ref = outs[-2:]
        is_main = i < N_MAIN // MOVE_TILE

        @pl.when(is_main)
        def _():
            main_ref[...] = xn_scr[...]

        @pl.when(jnp.logical_not(is_main))
        def _():
            tail_ref[...] = xn_scr[...]


def _combine_call(x1, route, pos, os_rows, gn, xn_dtype, *, emit_x, split):
    n = x1.shape[0]
    row = lambda i: (i, 0)
    blk = lambda m: pl.BlockSpec((MOVE_TILE, D_MODEL), m)
    nm = N_MAIN // MOVE_TILE
    out_specs, out_shape = [], []
    if emit_x:
        out_specs.append(blk(row))
        out_shape.append(jax.ShapeDtypeStruct((n, D_MODEL), f32))
    if split:
        out_specs += [blk(lambda i: (jnp.minimum(i, nm - 1), 0)), blk(lambda i: (jnp.maximum(i - nm, 0), 0))]
        out_shape += [jax.ShapeDtypeStruct((N_MAIN, D_MODEL), xn_dtype),
                      jax.ShapeDtypeStruct((n - N_MAIN, D_MODEL), xn_dtype)]
    else:
        out_specs.append(blk(row))
        out_shape.append(jax.ShapeDtypeStruct((n, D_MODEL), xn_dtype))
    nt = n // MOVE_TILE
    pos_spec = lambda m: pl.BlockSpec((1, 1, 2 * MOVE_TILE), m, memory_space=pltpu.SMEM)
    return pl.pallas_call(
        functools.partial(_combine_kernel, emit_x=emit_x, split=split),
        grid=(nt,),
        in_specs=[pos_spec(lambda i: (i, 0, 0)), pos_spec(lambda i: (jnp.minimum(i + 1, nt - 1), 0, 0)),
                  blk(row), pl.BlockSpec((MOVE_TILE, LANES), row),
                  pl.BlockSpec((1, D_MODEL), lambda i: (0, 0)), pl.BlockSpec(memory_space=pl.ANY)],
        out_specs=out_specs,
        out_shape=out_shape,
        scratch_shapes=[pltpu.VMEM((2, 2, MOVE_TILE, D_MODEL), f32), pltpu.SemaphoreType.DMA((2,)),
                        pltpu.VMEM((MOVE_TILE, D_MODEL), xn_dtype)],
        compiler_params=_cp(("arbitrary",)),
        name="moe_combine",
    )(pos, pos, x1, route, gn, os_rows)


S5_GROUP = 16
S5_GROUPS = D_MODEL // S5_GROUP
S5_STATE = 64
S5_CB = 128
S5_NB = D_MODEL // S5_CB
S5_GPB = S5_CB // S5_GROUP
S5_SB = S5_GPB * S5_STATE
S5_BC = S5_STATE * S5_GROUP
S5_CPS = 2


def _s5_disc_kernel(lr_ref, li_ref, ldt_ref, bre_ref, bim_ref,
                    abr_ref, abi_ref, ab2r_ref, ab2i_ref, bbr_ref, bbi_ref, abbr_ref, abbi_ref):
    lr, li = lr_ref[...], li_ref[...]
    dt = jnp.exp(ldt_ref[...])
    mag = jnp.exp(lr * dt)
    ang = li * dt
    ab_re, ab_im = mag * jnp.cos(ang), mag * jnp.sin(ang)
    nr, ni = ab_re - 1.0, ab_im
    den = lr * lr + li * li
    f_re = (nr * lr + ni * li) / den
    f_im = (ni * lr - nr * li) / den
    abr_ref[...] = ab_re
    abi_ref[...] = ab_im
    ab2r_ref[...] = ab_re * ab_re - ab_im * ab_im
    ab2i_ref[...] = 2.0 * (ab_re * ab_im)
    pi = lax.broadcasted_iota(jnp.int32, (S5_STATE, S5_BC), 0)
    ci = lax.broadcasted_iota(jnp.int32, (S5_STATE, S5_BC), 1)
    rep = jnp.where(ci // S5_GROUP == pi, 1.0, 0.0).astype(bf16)

    def expand(v):
        hi, mid, lo = _split3(v)
        return _dot(hi, rep) + _dot(mid, rep) + _dot(lo, rep)

    fr, fi, ar, ai = expand(f_re), expand(f_im), expand(ab_re), expand(ab_im)
    br, bi = bre_ref[...], bim_ref[...]
    bb_re = fr * br - fi * bi
    bb_im = fr * bi + fi * br
    bbr_ref[...] = bb_re
    bbi_ref[...] = bb_im
    abbr_ref[...] = ar * bb_re - ai * bb_im
    abbi_ref[...] = ar * bb_im + ai * bb_re


def _s5_weights(lam_re, lam_im, log_dt, b_re, b_im, c_re, c_im):
    st = jax.ShapeDtypeStruct((S5_GROUPS, S5_STATE), f32)
    bc = jax.ShapeDtypeStruct((S5_GROUPS, S5_BC), f32)
    ab_re, ab_im, ab2_re, ab2_im, bb_re, bb_im, abb_re, abb_im = pl.pallas_call(
        _s5_disc_kernel, out_shape=[st, st, st, st, bc, bc, bc, bc], name="s5_discretize",
    )(lam_re, lam_im, log_dt[:, None], b_re.reshape(S5_GROUPS, S5_BC), b_im.reshape(S5_GROUPS, S5_BC))
    eye = jnp.eye(S5_GPB, dtype=f32)

    def in_blocks(m):
        m = m.reshape(S5_NB, S5_GPB, S5_STATE, S5_GROUP)
        return jnp.einsum("jgpc,gh->jgchp", m, eye).reshape(S5_NB, S5_CB, S5_SB)

    def out_blocks(m):
        m = m.reshape(S5_NB, S5_GPB, S5_GROUP, S5_STATE)
        return jnp.einsum("jgcp,gh->jgphc", m, eye).reshape(S5_NB, S5_SB, S5_CB)

    bb2 = jnp.concatenate([
        jnp.concatenate([in_blocks(bb_re), in_blocks(bb_im)], axis=2),
        jnp.concatenate([in_blocks(abb_re), in_blocks(abb_im)], axis=2)], axis=1).astype(bf16)
    cc = jnp.concatenate([out_blocks(c_re), -out_blocks(c_im)], axis=1).astype(bf16)
    rows = [v.reshape(S5_NB, 1, S5_SB) for v in (ab_re, ab_im, ab2_re, ab2_im)]
    abv = jnp.concatenate(rows + [jnp.zeros((S5_NB, SUB - len(rows), S5_SB), f32)], axis=1)
    return bb2, cc, abv


def _s5_seq_kernel(x0_ref, x1_ref, x2_ref, x3_ref, halo0_ref, h0_ref, bb2_ref, cc_ref, abv_ref, y_ref, hout_ref,
                   xf_scr, xp_scr, bu_scr, hs_scr, yp_scr, yn_scr, h_scr, halo_scr):
    tb = pl.program_id(1)
    TL = x0_ref.shape[0]
    KB = TL // 2
    RB = BATCH * TL

    @pl.when(tb == 0)
    def _():
        h_scr[...] = h0_ref[...]
        halo_scr[...] = halo0_ref[...].astype(f32)

    chans = [slice(c * S5_CB, (c + 1) * S5_CB) for c in range(S5_CPS)]
    for c, ch in enumerate(chans):
        for b, xb_ref in enumerate((x0_ref, x1_ref, x2_ref, x3_ref)):
            xf_scr[c, b * TL:(b + 1) * TL, :] = xb_ref[:, ch].astype(f32)
        for b in range(BATCH):
            for p in range(2):
                xp_scr[c, pl.ds(2 * b + p, KB, stride=SUB), :] = xf_scr[c, pl.ds(b * TL + p, KB, stride=2), :]
    x = jnp.concatenate([xp_scr[c] for c in range(S5_CPS)], axis=1)
    xc = jnp.concatenate([halo_scr[...], x], axis=0)
    odd = (lax.broadcasted_iota(jnp.int32, (RB + SUB, 1), 0) & 1) == 1
    xprev = jnp.where(odd, pltpu.roll(xc, 1, axis=0), pltpu.roll(xc, SUB - 1, axis=0))[SUB:]
    halo_scr[...] = x[RB - SUB:]
    for c, ch in enumerate(chans):
        lhs = jnp.concatenate([x[:, ch], xprev[:, ch]], axis=1).astype(bf16)
        bu_scr[c] = _dot(lhs, bb2_ref[c])
    a2 = [(abv_ref[c, 2:3, :], abv_ref[c, 3:4, :]) for c in range(S5_CPS)]

    def step(k, carry):
        r0 = pl.multiple_of(k * SUB, SUB)
        out = []
        for c in range(S5_CPS):
            hr, hi = carry[2 * c], carry[2 * c + 1]
            a2r, a2i = a2[c]
            bu = bu_scr[c, pl.ds(r0, SUB), :]
            nr = a2r * hr - a2i * hi + bu[:, :S5_SB]
            ni = a2r * hi + a2i * hr + bu[:, S5_SB:]
            hs_scr[c, pl.ds(r0, SUB), :S5_SB] = nr
            hs_scr[c, pl.ds(r0, SUB), S5_SB:] = ni
            out += [nr, ni]
        return tuple(out)

    init = []
    for c in range(S5_CPS):
        init += [h_scr[c, :, :S5_SB], h_scr[c, :, S5_SB:]]
    fin = lax.fori_loop(0, RB // SUB, step, tuple(init), unroll=4)
    for c, ch in enumerate(chans):
        h_scr[c, :, :S5_SB] = fin[2 * c]
        h_scr[c, :, S5_SB:] = fin[2 * c + 1]
        yp_scr[c] = _dot(hs_scr[c].astype(bf16), cc_ref[c])
        for b in range(BATCH):
            for p in range(2):
                yn_scr[c, pl.ds(b * TL + p, KB, stride=2), :] = yp_scr[c, pl.ds(2 * b + p, KB, stride=SUB), :]
            y_ref[b, :, ch] = yn_scr[c, b * TL:(b + 1) * TL, :]

    @pl.when(tb == pl.num_programs(1) - 1)
    def _():
        hout_ref[...] = h_scr[...]


def _s5_seq_call(x, halo0, h0, bb2, cc, abv, *, row0, seq_len, tl):
    wsel = lambda j, t: (j, 0, 0)
    rb = BATCH * tl
    cw = S5_CPS * S5_CB
    xspec = lambda b: pl.BlockSpec((tl, cw), lambda j, t: ((row0 + b * seq_len) // tl + t, j))
    return pl.pallas_call(
        _s5_seq_kernel,
        grid=(S5_NB // S5_CPS, seq_len // tl),
        in_specs=[xspec(b) for b in range(BATCH)]
        + [pl.BlockSpec((SUB, cw), lambda j, t: (0, j)),
           pl.BlockSpec((S5_CPS, SUB, 2 * S5_SB), wsel),
           pl.BlockSpec((S5_CPS, 2 * S5_CB, 2 * S5_SB), wsel),
           pl.BlockSpec((S5_CPS, 2 * S5_SB, S5_CB), wsel),
           pl.BlockSpec((S5_CPS, SUB, S5_SB), wsel)],
        out_specs=[pl.BlockSpec((BATCH, tl, cw), lambda j, t: (0, t, j)),
                   pl.BlockSpec((S5_CPS, SUB, 2 * S5_SB), wsel)],
        out_shape=[jax.ShapeDtypeStruct((BATCH, seq_len, D_MODEL), f32),
                   jax.ShapeDtypeStruct((S5_NB, SUB, 2 * S5_SB), f32)],
        scratch_shapes=[pltpu.VMEM((S5_CPS, rb, S5_CB), f32), pltpu.VMEM((S5_CPS, rb, S5_CB), f32),
                        pltpu.VMEM((S5_CPS, rb, 2 * S5_SB), f32), pltpu.VMEM((S5_CPS, rb, 2 * S5_SB), f32),
                        pltpu.VMEM((S5_CPS, rb, S5_CB), f32), pltpu.VMEM((S5_CPS, rb, S5_CB), f32),
                        pltpu.VMEM((S5_CPS, SUB, 2 * S5_SB), f32), pltpu.VMEM((SUB, cw), f32)],
        compiler_params=_cp(("arbitrary", "arbitrary")),
        name=f"s5_seq_{seq_len}",
    )(x, x, x, x, halo0, h0, bb2, cc, abv)


def _s5_sample_kernel(x_ref, hre_ref, him_ref, bb2_ref, cc_ref, abv_ref, y_ref, ore_ref, oim_ref, hs_scr):
    nb = hre_ref.shape[0]
    bu = _dot(x_ref[...], bb2_ref[0, :S5_CB, :])
    ar = abv_ref[0, 0:1, :]
    ai = abv_ref[0, 1:2, :]
    hr, hi = hre_ref[...], him_ref[...]
    for t in range(DEC_SEQ):
        rows = slice(t * nb, (t + 1) * nb)
        hr, hi = (ar * hr - ai * hi + bu[rows, :S5_SB], ar * hi + ai * hr + bu[rows, S5_SB:])
        hs_scr[rows, :S5_SB] = hr
        hs_scr[rows, S5_SB:] = hi
    y_ref[...] = _dot(hs_scr[...].astype(bf16), cc_ref[0])
    ore_ref[...] = hr
    oim_ref[...] = hi


def _s5_sample_call(xt, h_re, h_im, bb2, cc, abv):
    n = xt.shape[0]
    nb = h_re.shape[0]
    wsel = lambda j: (j, 0, 0)
    st = pl.BlockSpec((nb, S5_SB), lambda j: (0, j))
    return pl.pallas_call(
        _s5_sample_kernel,
        grid=(S5_NB,),
        in_specs=[pl.BlockSpec((n, S5_CB), lambda j: (0, j)), st, st,
                  pl.BlockSpec((1, 2 * S5_CB, 2 * S5_SB), wsel),
                  pl.BlockSpec((1, 2 * S5_SB, S5_CB), wsel),
                  pl.BlockSpec((1, SUB, S5_SB), wsel)],
        out_specs=[pl.BlockSpec((n, S5_CB), lambda j: (0, j)), st, st],
        out_shape=[jax.ShapeDtypeStruct((n, D_MODEL), f32),
                   jax.ShapeDtypeStruct(h_re.shape, f32), jax.ShapeDtypeStruct(h_im.shape, f32)],
        scratch_shapes=[pltpu.VMEM((n, 2 * S5_SB), f32)],
        compiler_params=_cp(("arbitrary",)),
        name="s5_sample",
    )(xt, h_re, h_im, bb2, cc, abv)


GLA_CHUNK = 256
GLA_DIRECT = 2
S5_TL = 256


def _moe_layer(layer, x1, xnf, route, plan, cnt, wg, wu, wd, gn, xn_dtype, *, emit_x, split):
    pos, items, n_items = _moe_plan(plan, cnt, x1.shape[0])
    src = _source_rows(pos, x1.shape[0])
    os_rows = _experts_call(xnf, src, items, n_items, jnp.full((1,), layer, jnp.int32), wg, wu, wd)
    return _combine_call(x1, route, pos, os_rows, gn, xn_dtype, emit_x=emit_x, split=split)


def kernel(x_prompt, x_sample, state_gla, state_s5_re, state_s5_im, meta_tokens, norm_mix_g, norm_ffn_g, norm_final_g, gla_w_in, gla_w_a2, gla_b_a, gla_g_o, gla_w_o, s5_lambda_re, s5_lambda_im, s5_log_dt, s5_b_re, s5_b_im, s5_c_re, s5_c_im, s5_d, s5_w_glu, s5_b_glu, moe_w_rg, moe_b_rg, moe_w_re, moe_b_re, moe_w_gate, moe_w_up, moe_w_down):
    row = lambda v: v.reshape(1, -1)
    x_main = x_prompt.reshape(N_MAIN, D_MODEL)
    x_tail = jnp.concatenate([
        jnp.tile(meta_tokens.astype(x_prompt.dtype), (BATCH, 1)),
        x_sample.reshape(N_SAMPLE, D_MODEL),
        jnp.zeros((N_ROWS - N_REAL, D_MODEL), x_prompt.dtype)], axis=0)
    wg = moe_w_gate.reshape(-1, D_MODEL, D_EXPERT)
    wu = moe_w_up.reshape(-1, D_MODEL, D_EXPERT)
    wd = moe_w_down.reshape(-1, D_EXPERT, D_MODEL)

    w_in = jnp.swapaxes(gla_w_in, 1, 2)
    wa1 = jnp.pad(gla_w_in[0, :, GLA_QKVR:], ((0, 0), (0, LANES - GLA_RANK)))
    wa2 = jnp.pad(gla_w_a2.reshape(GLA_RANK, GLA_KEY), ((0, LANES - GLA_RANK), (0, 0)))
    xn, glog = _norm_gate_call(x_main, x_tail, row(norm_mix_g[0]), wa1, wa2, row(gla_b_a))
    proj = _proj_call(xn, w_in, GLA_QKVR)
    go = row(gla_g_o)
    og = jnp.zeros((N_ROWS, GLA_VAL), bf16)
    s_zero = jnp.zeros((BATCH, GLA_HEADS, GLA_DK, GLA_DV), f32)
    og, s_meta = _gla_seq_call(proj, glog, go, s_zero, og, row0=ROW_META, C=N_META, n_chunks=1, d=GLA_DIRECT)
    og, s_prompt = _gla_seq_call(proj, glog, go, s_meta, og, row0=0, C=GLA_CHUNK, n_chunks=SEQ // GLA_CHUNK,
                                 d=GLA_DIRECT)
    og, s_sample = _gla_sample_call(proj, glog, go, state_gla.reshape(DEC_BATCH, GLA_HEADS, GLA_DK, GLA_DV), og,
                                    row0=ROW_SAMPLE)
    wr, br = _router_weights(moe_w_rg[0], moe_b_rg[0], moe_w_re[0], moe_b_re[0])
    tile_row = lambda i: (i, 0)
    x1, xnf, route, plan, cnt = _mix_out_call(
        _gla_out_kernel, "gla_out", N_ROWS, [og, x_main, x_tail], [tile_row, *_main_tail_maps(MIX_TILE)],
        [gla_w_o.reshape(GLA_VAL, D_MODEL), row(norm_ffn_g[0]), wr, br])
    x2, xn2 = _moe_layer(0, x1, xnf, route, plan, cnt, wg, wu, wd, row(norm_mix_g[1]), bf16, emit_x=True, split=False)

    bb2, cc, abv = _s5_weights(s5_lambda_re[0], s5_lambda_im[0], s5_log_dt[0], s5_b_re[0], s5_b_im[0],
                               s5_c_re[0], s5_c_im[0])
    y_meta, h_meta = _s5_seq_call(xn2, jnp.zeros((SUB, D_MODEL), bf16), jnp.zeros((S5_NB, SUB, 2 * S5_SB), f32),
                                  bb2, cc, abv, row0=ROW_META, seq_len=N_META, tl=N_META)
    halo = xn2[ROW_META:ROW_SAMPLE].reshape(BATCH, N_META, D_MODEL)[:, N_META - 2:].reshape(SUB, D_MODEL)
    y_main, h_main = _s5_seq_call(xn2, halo, h_meta, bb2, cc, abv, row0=0, seq_len=SEQ, tl=S5_TL)
    xt_sample = xn2[ROW_SAMPLE:N_REAL].reshape(DEC_BATCH, DEC_SEQ, D_MODEL).transpose(1, 0, 2).reshape(N_SAMPLE, D_MODEL)
    y_samp, s5r_s, s5i_s = _s5_sample_call(
        xt_sample, state_s5_re.reshape(DEC_BATCH, S5_GROUPS * S5_STATE),
        state_s5_im.reshape(DEC_BATCH, S5_GROUPS * S5_STATE), bb2, cc, abv)
    ys_tail = jnp.concatenate([
        y_meta.reshape(N_METAROWS, D_MODEL),
        y_samp.reshape(DEC_SEQ, DEC_BATCH, D_MODEL).transpose(1, 0, 2).reshape(N_SAMPLE, D_MODEL),
        jnp.zeros((N_ROWS - N_REAL, D_MODEL), f32)], axis=0)
    wr, br = _router_weights(moe_w_rg[1], moe_b_rg[1], moe_w_re[1], moe_b_re[1])
    x3, xnf, route, plan, cnt = _mix_out_call(
        _s5_out_kernel, "s5_out", N_ROWS, [y_main.reshape(N_MAIN, D_MODEL), ys_tail, x2],
        [*_main_tail_maps(MIX_TILE), tile_row],
        [s5_w_glu.reshape(D_MODEL, D_MODEL), row(norm_mix_g[1]), row(s5_d), row(s5_b_glu),
         row(norm_ffn_g[1]), wr, br])
    y_main_out, y_tail_out = _moe_layer(1, x3, xnf, route, plan, cnt, wg, wu, wd, row(norm_final_g), f32,
                                        emit_x=False, split=True)

    y_prompt = y_main_out.reshape(BATCH, SEQ, D_MODEL)
    y_sample = y_tail_out[N_METAROWS:N_METAROWS + N_SAMPLE].reshape(DEC_BATCH, DEC_SEQ, D_MODEL)
    hfin = h_main.reshape(S5_NB, BATCH, 2, 2, S5_GPB, S5_STATE)[:, :, 1]
    s5r_p = hfin[:, :, 0].transpose(1, 0, 2, 3).reshape(1, BATCH, S5_GROUPS, S5_STATE)
    s5i_p = hfin[:, :, 1].transpose(1, 0, 2, 3).reshape(1, BATCH, S5_GROUPS, S5_STATE)
    return (y_prompt, y_sample, s_prompt[None], s5r_p, s5i_p, s_sample[None],
            s5r_s.reshape(1, DEC_BATCH, S5_GROUPS, S5_STATE), s5i_s.reshape(1, DEC_BATCH, S5_GROUPS, S5_STATE))
```

```python
import functools

import jax
import jax.numpy as jnp
from jax import lax
from jax.experimental import pallas as pl
from jax.experimental.pallas import tpu as pltpu

f32 = jnp.float32
bf16 = jnp.bfloat16

D_MODEL = 2048
BATCH = 4
SEQ = 2048
DEC_BATCH = 128
DEC_SEQ = 4
N_META = 16
EPS = 1e-6
GLA_HEADS = 4
GLA_DK = 256
GLA_DV = 512
GLA_KEY = GLA_HEADS * GLA_DK
GLA_VAL = GLA_HEADS * GLA_DV
GLA_RANK = 16
GLA_TAU = 16.0
GLA_QKVR = 2 * GLA_KEY + 2 * GLA_VAL

N_MAIN = BATCH * SEQ
N_METAROWS = BATCH * N_META
N_SAMPLE = DEC_BATCH * DEC_SEQ
ROW_META = N_MAIN
ROW_SAMPLE = N_MAIN + N_METAROWS
N_REAL = ROW_SAMPLE + N_SAMPLE
ROW_TILE = 256
N_ROWS = -(-N_REAL // ROW_TILE) * ROW_TILE

VMEM_LIMIT = 56 * 1024 * 1024


def _cp(sem, vmem=VMEM_LIMIT):
    return pltpu.CompilerParams(dimension_semantics=sem, vmem_limit_bytes=vmem)


def _dot(a, b):
    return jnp.dot(a, b, preferred_element_type=f32)


def _dot_nt(a, b):
    return lax.dot_general(a, b, (((1,), (1,)), ((), ())), preferred_element_type=f32)


def _dot_tn(a, b):
    return lax.dot_general(a, b, (((0,), (0,)), ((), ())), preferred_element_type=f32)


def _sigmoid(x):
    return 1.0 / (1.0 + jnp.exp(-x))


def _split3(x):
    hi = x.astype(bf16)
    r1 = x - hi.astype(f32)
    mid = r1.astype(bf16)
    lo = (r1 - mid.astype(f32)).astype(bf16)
    return hi, mid, lo


def _cumsum_rows(g, C):
    if C <= 16:
        row = lax.broadcasted_iota(jnp.int32, (C, 1), 0)
        b = jnp.zeros_like(g)
        for s in range(C):
            b = b + jnp.where(row >= s, g[s:s + 1, :], 0.0)
        return b
    row = lax.broadcasted_iota(jnp.int32, (C, C), 0)
    col = lax.broadcasted_iota(jnp.int32, (C, C), 1)
    tri = jnp.where(row >= col, 1.0, 0.0).astype(bf16)
    hi, mid, lo = _split3(g)
    return _dot(tri, hi) + _dot(tri, mid) + _dot(tri, lo)


PAIR_LEVEL = 1000


def _gla_pair_code(C, d, tree=True):
    ti = lax.broadcasted_iota(jnp.int32, (C, C), 0)
    si = lax.broadcasted_iota(jnp.int32, (C, C), 1)
    code = jnp.where((ti // d == si // d) & (si <= ti), 1 + ti - si, 0)
    h = d
    while tree and h < C:
        tb = ti // h
        code = jnp.where(((tb % 2) == 1) & ((si // h) == tb - 1), PAIR_LEVEL + h, code)
        h *= 2
    return code


def _gla_scores(q, k, b, code, C, d, tree=True):
    row = lax.broadcasted_iota(jnp.int32, (C, 1), 0)
    scores = jnp.zeros((C, C), f32)
    for dl in range(d):
        ks = k if dl == 0 else pltpu.roll(k, dl, axis=0)
        bs = b if dl == 0 else pltpu.roll(b, dl, axis=0)
        term = q * ks * jnp.exp(jnp.minimum(b - bs, 0.0))
        colv = jnp.sum(term, axis=1, keepdims=True)
        scores = jnp.where(code == 1 + dl, colv, scores)
    z = b
    s = 1
    while tree and 2 * s < C:
        z = jnp.where((row & s) != 0, pltpu.roll(z, s, axis=0), z)
        s *= 2
        h = s
        if h < d:
            continue
        bnext = pltpu.roll(z, C - h, axis=0)
        qh = (q * jnp.exp(jnp.minimum(b - z, 0.0))).astype(bf16)
        kh = (k * jnp.exp(jnp.minimum(bnext - b, 0.0))).astype(bf16)
        scores = jnp.where(code == PAIR_LEVEL + h, _dot_nt(qh, kh), scores)
    return scores


def _gla_chunk(q, k, v, g, S, code, C, d, tree=True):
    b = _cumsum_rows(g, C)
    o = _dot((q * jnp.exp(b)).astype(bf16), S.astype(bf16))
    scores = _gla_scores(q, k, b, code, C, d, tree)
    o = o + _dot(scores.astype(bf16), v)
    b_last = b[C - 1:C, :]
    kd = (k * jnp.exp(b_last - b)).astype(bf16)
    if C == GLA_DK:
        eye = code == 1
    else:
        eye = (lax.broadcasted_iota(jnp.int32, (GLA_DK, GLA_DK), 0)
               == lax.broadcasted_iota(jnp.int32, (GLA_DK, GLA_DK), 1))
    dec_col = jnp.sum(jnp.where(eye, jnp.exp(b_last), 0.0), axis=1, keepdims=True)
    S_new = dec_col * S + _dot_tn(kd, v)
    return o, S_new


def _gla_head_epilogue(o, r, go):
    ms = jnp.mean(o * o, axis=1, keepdims=True)
    on = o * lax.rsqrt(ms + EPS) * go
    return on * (r * _sigmoid(r))


def _gla_seq_kernel(q_ref, k_ref, v_ref, r_ref, g_ref, go_ref, s0_ref, _og_in, og_ref, sout_ref, s_scr, *, C, d):
    c = pl.program_id(1)

    @pl.when(c == 0)
    def _():
        s_scr[...] = s0_ref[0]

    code = _gla_pair_code(C, d)

    def head(h, carry):
        ck = pl.ds(pl.multiple_of(h * GLA_DK, GLA_DK), GLA_DK)
        cv = pl.ds(pl.multiple_of(h * GLA_DV, GLA_DV), GLA_DV)
        q = q_ref[:, ck].astype(f32) * (GLA_DK ** -0.5)
        k = k_ref[:, ck].astype(f32)
        o, S_new = _gla_chunk(q, k, v_ref[:, cv], g_ref[:, ck], s_scr[h], code, C, d)
        s_scr[h] = S_new
        og_ref[:, cv] = _gla_head_epilogue(o, r_ref[:, cv].astype(f32), go_ref[:, cv]).astype(og_ref.dtype)
        return carry

    lax.fori_loop(0, GLA_HEADS, head, 0)

    @pl.when(c == pl.num_programs(1) - 1)
    def _():
        sout_ref[0] = s_scr[...]


def _gla_seq_call(proj, glog, go, s0, og_buf, *, row0, C, n_chunks, d):
    blk0 = row0 // C
    rows = lambda b, c: blk0 + b * n_chunks + c
    kern = functools.partial(_gla_seq_kernel, C=C, d=d)
    return pl.pallas_call(
        kern,
        grid=(BATCH, n_chunks),
        in_specs=[
            pl.BlockSpec((C, GLA_KEY), lambda b, c: (rows(b, c), 0)),
            pl.BlockSpec((C, GLA_KEY), lambda b, c: (rows(b, c), 1)),
            pl.BlockSpec((C, GLA_VAL), lambda b, c: (rows(b, c), 1)),
            pl.BlockSpec((C, GLA_VAL), lambda b, c: (rows(b, c), 2)),
            pl.BlockSpec((C, GLA_KEY), lambda b, c: (rows(b, c), 0)),
            pl.BlockSpec((1, GLA_VAL), lambda b, c: (0, 0)),
            pl.BlockSpec((1, GLA_HEADS, GLA_DK, GLA_DV), lambda b, c: (b, 0, 0, 0)),
            pl.BlockSpec(memory_space=pl.ANY),
        ],
        out_specs=[
            pl.BlockSpec((C, GLA_VAL), lambda b, c: (rows(b, c), 0)),
            pl.BlockSpec((1, GLA_HEADS, GLA_DK, GLA_DV), lambda b, c: (b, 0, 0, 0)),
        ],
        out_shape=[
            jax.ShapeDtypeStruct(og_buf.shape, og_buf.dtype),
            jax.ShapeDtypeStruct((BATCH, GLA_HEADS, GLA_DK, GLA_DV), f32),
        ],
        scratch_shapes=[pltpu.VMEM((GLA_HEADS, GLA_DK, GLA_DV), f32)],
        input_output_aliases={7: 0},
        compiler_params=_cp(("arbitrary", "arbitrary")),
        name=f"gla_seq_c{C}",
    )(proj, proj, proj, proj, glog, go, s0, og_buf)


SAMPLE_BB = 4
SAMPLE_C = SAMPLE_BB * DEC_SEQ


def _gla_sample_kernel(q_ref, k_ref, v_ref, r_ref, g_ref, go_ref, s0_ref, _og_in, og_ref, sout_ref):
    row = lax.broadcasted_iota(jnp.int32, (SAMPLE_C, 1), 0)

    code = _gla_pair_code(SAMPLE_C, DEC_SEQ, tree=False)

    def take(x, bb):
        sh = (SAMPLE_C - DEC_SEQ * bb) % SAMPLE_C
        return jnp.where(row < DEC_SEQ, pltpu.roll(x, sh, axis=0) if sh else x, 0.0)

    def head(h, carry):
        ck = pl.ds(pl.multiple_of(h * GLA_DK, GLA_DK), GLA_DK)
        cv = pl.ds(pl.multiple_of(h * GLA_DV, GLA_DV), GLA_DV)
        q_all = q_ref[:, ck].astype(f32) * (GLA_DK ** -0.5)
        k_all = k_ref[:, ck].astype(f32)
        v_all = v_ref[:, cv].astype(f32)
        r_all = r_ref[:, cv].astype(f32)
        g_all = g_ref[:, ck]
        go = go_ref[:, cv]
        acc = jnp.zeros((SAMPLE_C, GLA_DV), f32)
        for bb in range(SAMPLE_BB):
            o, S_new = _gla_chunk(take(q_all, bb), take(k_all, bb), take(v_all, bb).astype(bf16),
                                  take(g_all, bb), s0_ref[bb, h], code, SAMPLE_C, DEC_SEQ, tree=False)
            sout_ref[bb, h] = S_new
            y = _gla_head_epilogue(o, take(r_all, bb), go)
            acc = jnp.where(row // DEC_SEQ == bb, pltpu.roll(y, DEC_SEQ * bb, axis=0) if bb else y, acc)
        og_ref[:, cv] = acc.astype(og_ref.dtype)
        return carry

    lax.fori_loop(0, GLA_HEADS, head, 0)


def _gla_sample_call(proj, glog, go, s0, og_buf, *, row0):
    n_seq = s0.shape[0]
    blk0 = row0 // SAMPLE_C
    st_spec = pl.BlockSpec((SAMPLE_BB, GLA_HEADS, GLA_DK, GLA_DV), lambda i: (i, 0, 0, 0))
    return pl.pallas_call(
        _gla_sample_kernel,
        grid=(n_seq // SAMPLE_BB,),
        in_specs=[
            pl.BlockSpec((SAMPLE_C, GLA_KEY), lambda i: (blk0 + i, 0)),
            pl.BlockSpec((SAMPLE_C, GLA_KEY), lambda i: (blk0 + i, 1)),
            pl.BlockSpec((SAMPLE_C, GLA_VAL), lambda i: (blk0 + i, 1)),
            pl.BlockSpec((SAMPLE_C, GLA_VAL), lambda i: (blk0 + i, 2)),
            pl.BlockSpec((SAMPLE_C, GLA_KEY), lambda i: (blk0 + i, 0)),
            pl.BlockSpec((1, GLA_VAL), lambda i: (0, 0)),
            st_spec,
            pl.BlockSpec(memory_space=pl.ANY),
        ],
        out_specs=[pl.BlockSpec((SAMPLE_C, GLA_VAL), lambda i: (blk0 + i, 0)), st_spec],
        out_shape=[jax.ShapeDtypeStruct(og_buf.shape, og_buf.dtype), jax.ShapeDtypeStruct(s0.shape, f32)],
        input_output_aliases={7: 0},
        compiler_params=_cp(("arbitrary",)),
        name="gla_sample",
    )(proj, proj, proj, proj, glog, go, s0, og_buf)


LANES = 128
SUB = 8
MOE_GROUPS = 4
MOE_EPG = 8
MOE_EXPERTS = MOE_GROUPS * MOE_EPG
D_EXPERT = 256
ROUTE_E0 = MOE_GROUPS
MIX_TILE = 256
PROJ_NT = 1024
PROJ_MT = N_ROWS // 7


def _rms(x, g):
    r = lax.rsqrt(jnp.mean(x * x, axis=-1, keepdims=True) + EPS)
    return (x * r) * g


def _log_sigmoid(z):
    return jnp.minimum(z, 0.0) - jnp.log1p(jnp.exp(-jnp.abs(z)))


def _main_or_tail(tile, main_ref, tail_ref):
    return jnp.where(pl.program_id(0) < N_MAIN // tile, main_ref[...], tail_ref[...])


def _main_tail_maps(tile):
    nm = N_MAIN // tile
    return (lambda i: (jnp.minimum(i, nm - 1), 0)), (lambda i: (jnp.maximum(i - nm, 0), 0))


def _norm_gate_kernel(xm_ref, xt_ref, gn_ref, wa1_ref, wa2_ref, ba_ref, xn_ref, gl_ref):
    xnb = _rms(_main_or_tail(ROW_TILE, xm_ref, xt_ref), gn_ref[...]).astype(bf16)
    xn_ref[...] = xnb
    a = _dot(xnb, wa1_ref[...].astype(bf16))
    z = _dot(a.astype(bf16), wa2_ref[...].astype(bf16)) + ba_ref[...]
    gl_ref[...] = _log_sigmoid(z) * (1.0 / GLA_TAU)


def _norm_gate_call(x_main, x_tail, gn, wa1, wa2, ba):
    n = x_main.shape[0] + x_tail.shape[0]
    row = lambda i: (i, 0)
    fix = lambda i: (0, 0)
    main_map, tail_map = _main_tail_maps(ROW_TILE)
    return pl.pallas_call(
        _norm_gate_kernel,
        grid=(n // ROW_TILE,),
        in_specs=[pl.BlockSpec((ROW_TILE, D_MODEL), main_map), pl.BlockSpec((ROW_TILE, D_MODEL), tail_map),
                  pl.BlockSpec((1, D_MODEL), fix),
                  pl.BlockSpec((D_MODEL, LANES), fix), pl.BlockSpec((LANES, GLA_KEY), fix),
                  pl.BlockSpec((1, GLA_KEY), fix)],
        out_specs=[pl.BlockSpec((ROW_TILE, D_MODEL), row), pl.BlockSpec((ROW_TILE, GLA_KEY), row)],
        out_shape=[jax.ShapeDtypeStruct((n, D_MODEL), bf16), jax.ShapeDtypeStruct((n, GLA_KEY), f32)],
        compiler_params=_cp(("arbitrary",)),
        name="norm_gate",
    )(x_main, x_tail, gn, wa1, wa2, ba)


def _proj_kernel(xn_ref, w_ref, o_ref, wb_scr):
    @pl.when(pl.program_id(1) == 0)
    def _():
        wb_scr[...] = w_ref[0].astype(bf16)

    o_ref[...] = _dot_nt(xn_ref[...], wb_scr[...]).astype(o_ref.dtype)


def _proj_call(xn, wt, n_cols):
    n = xn.shape[0]
    return pl.pallas_call(
        _proj_kernel,
        grid=(n_cols // PROJ_NT, n // PROJ_MT),
        in_specs=[pl.BlockSpec((PROJ_MT, D_MODEL), lambda j, i: (i, 0)),
                  pl.BlockSpec((1, PROJ_NT, D_MODEL), lambda j, i: (0, j, 0))],
        out_specs=pl.BlockSpec((PROJ_MT, PROJ_NT), lambda j, i: (i, j)),
        out_shape=jax.ShapeDtypeStruct((n, n_cols), bf16),
        scratch_shapes=[pltpu.VMEM((PROJ_NT, D_MODEL), bf16)],
        compiler_params=_cp(("arbitrary", "arbitrary")),
        name="gla_proj",
    )(xn, wt)


def _route(xn, wr, br, cnt_ref):
    R = xn.shape[0]
    xh = xn.astype(bf16)
    xl = (xn - xh.astype(f32)).astype(bf16)
    wh = wr.astype(bf16)
    wl = (wr - wh.astype(f32)).astype(bf16)
    logits = _dot(xh, wh) + _dot(xl, wh) + _dot(xh, wl) + br
    lane_i = lax.broadcasted_iota(jnp.int32, (R, LANES), 1)
    lane = lane_i.astype(f32)
    neg = -jnp.inf
    big = float(LANES)
    is_g = lane_i < MOE_GROUPS
    lg = jnp.where(is_g, logits, neg)
    mg = jnp.max(lg, axis=1, keepdims=True)
    gidx = jnp.min(jnp.where(lg == mg, lane, big), axis=1, keepdims=True)
    ptop = 1.0 / jnp.sum(jnp.where(is_g, jnp.exp(logits - mg), 0.0), axis=1, keepdims=True)
    lo = ROUTE_E0 + MOE_EPG * gidx
    le = jnp.where((lane >= lo) & (lane < lo + MOE_EPG), logits, neg)
    v1 = jnp.max(le, axis=1, keepdims=True)
    i1 = jnp.min(jnp.where(le == v1, lane, big), axis=1, keepdims=True)
    le2 = jnp.where(lane == i1, neg, le)
    v2 = jnp.max(le2, axis=1, keepdims=True)
    i2 = jnp.min(jnp.where(le2 == v2, lane, big), axis=1, keepdims=True)
    s = jnp.exp(v2 - v1)
    w0 = ptop / (1.0 + s)
    w1 = ptop * s / (1.0 + s)
    oh = jnp.where((lane == i1) | (lane == i2), 1.0, 0.0)
    ri = lax.broadcasted_iota(jnp.int32, (R, R), 0)
    ci = lax.broadcasted_iota(jnp.int32, (R, R), 1)
    before = jnp.where(ri > ci, 1.0, 0.0).astype(bf16)
    tot = _dot(before, oh.astype(bf16)) + cnt_ref[...]
    rank0 = jnp.sum(jnp.where(lane == i1, tot, 0.0), axis=1, keepdims=True)
    rank1 = jnp.sum(jnp.where(lane == i2, tot, 0.0), axis=1, keepdims=True)
    cnt_ref[...] = cnt_ref[...] + jnp.sum(oh, axis=0, keepdims=True)
    vals = (i1 - ROUTE_E0, i2 - ROUTE_E0, rank0, rank1, w0, w1)
    slab = jnp.zeros((R, LANES), f32)
    for j, v in enumerate(vals):
        slab = jnp.where(lane_i == j, v, slab)
    sub = lax.broadcasted_iota(jnp.int32, (SUB, R), 0)
    plan = jnp.zeros((SUB, R), f32)
    for j, v in enumerate(vals[:4]):
        as_row = jnp.sum(jnp.where(ri == ci, v, 0.0), axis=0, keepdims=True)
        plan = jnp.where(sub == j, as_row, plan)
    return slab, plan


def _router_weights(w_rg, b_rg, w_re, b_re):
    w = jnp.concatenate([w_rg, jnp.moveaxis(w_re, 0, 1).reshape(D_MODEL, MOE_EXPERTS)], axis=1)
    b = jnp.concatenate([b_rg, b_re.reshape(MOE_EXPERTS)])
    pad = LANES - w.shape[1]
    return jnp.pad(w, ((0, 0), (0, pad))), jnp.pad(b, (0, pad))[None]


def _mix_out_tail(x1, gffn_ref, wr_ref, br_ref, x1_ref, xn_ref, route_ref, plan_ref, cnt_ref):
    @pl.when(pl.program_id(0) == 0)
    def _():
        cnt_ref[...] = jnp.zeros_like(cnt_ref)

    x1_ref[...] = x1
    xn = _rms(x1, gffn_ref[...])
    xn_ref[...] = xn
    route_ref[...], plan_ref[...] = _route(xn, wr_ref[...], br_ref[...], cnt_ref)


def _cast_once(w_ref, w_scr):
    @pl.when(pl.program_id(0) == 0)
    def _():
        w_scr[...] = w_ref[...].astype(bf16)


def _gla_out_kernel(og_ref, xm_ref, xt_ref, wo_ref, gffn_ref, wr_ref, br_ref, *rest):
    outs, w_scr = rest[:-1], rest[-1]
    _cast_once(wo_ref, w_scr)
    x1 = _main_or_tail(MIX_TILE, xm_ref, xt_ref) + _dot(og_ref[...], w_scr[...])
    _mix_out_tail(x1, gffn_ref, wr_ref, br_ref, *outs)


def _gelu_tanh(x):
    return x * (0.5 * (1.0 + jnp.tanh(0.7978845608028654 * (x + 0.044715 * (x * x * x)))))


def _s5_out_kernel(ys_main_ref, ys_tail_ref, x_ref, wglu_ref, gmix_ref, d_ref, bglu_ref, gffn_ref, wr_ref, br_ref,
                   *rest):
    outs, w_scr = rest[:-1], rest[-1]
    _cast_once(wglu_ref, w_scr)
    x = x_ref[...]
    u = _rms(x, gmix_ref[...])
    y = _gelu_tanh(_main_or_tail(MIX_TILE, ys_main_ref, ys_tail_ref) + d_ref[...] * u)
    z = _dot(y.astype(bf16), w_scr[...]) + bglu_ref[...]
    _mix_out_tail(x + y * _sigmoid(z), gffn_ref, wr_ref, br_ref, *outs)


def _mix_out_call(kern, name, n, row_ins, row_maps, fix_ins):
    row = lambda i: (i, 0)
    fix = lambda i: (0, 0)
    out_row = lambda w: pl.BlockSpec((MIX_TILE, w), row)
    nt = n // MIX_TILE
    return pl.pallas_call(
        kern,
        grid=(nt,),
        in_specs=[pl.BlockSpec((MIX_TILE, a.shape[1]), m) for a, m in zip(row_ins, row_maps)]
        + [pl.BlockSpec(a.shape, fix, pipeline_mode=pl.Buffered(1)) for a in fix_ins],
        out_specs=[out_row(D_MODEL), out_row(D_MODEL), out_row(LANES), pl.BlockSpec((SUB, MIX_TILE), row),
                   pl.BlockSpec((1, LANES), fix)],
        out_shape=[jax.ShapeDtypeStruct((n, D_MODEL), f32), jax.ShapeDtypeStruct((n, D_MODEL), f32),
                   jax.ShapeDtypeStruct((n, LANES), f32), jax.ShapeDtypeStruct((nt * SUB, MIX_TILE), f32),
                   jax.ShapeDtypeStruct((1, LANES), f32)],
        scratch_shapes=[pltpu.VMEM((D_MODEL, D_MODEL), bf16)],
        compiler_params=_cp(("arbitrary",)),
        name=name,
    )(*row_ins, *fix_ins)


EXPERT_TM = 256
MOVE_TILE = MIX_TILE
COMBINE_CHUNK = 32


def _moe_plan(plan, cnt, n):
    i32 = jnp.int32
    v = plan.reshape(n // MOVE_TILE, SUB, MOVE_TILE)[:, :4].astype(i32)
    counts = cnt[0, ROUTE_E0:ROUTE_E0 + MOE_EXPERTS].astype(i32)
    ends = jnp.cumsum(counts)
    off = ends - counts
    ids = jnp.arange(MOE_EXPERTS, dtype=i32)
    pos = jnp.sum(jnp.where(v[:, 0:2, :, None] == ids, off, 0), axis=-1) + v[:, 2:4]
    total = 2 * n
    n_tiles = total // EXPERT_TM
    n_items = n_tiles + MOE_EXPERTS
    inner = (counts > 0) & (off % EXPERT_TM != 0)
    keys = jnp.concatenate([jnp.arange(n_tiles, dtype=i32) * EXPERT_TM, jnp.where(inner, off, total)])
    idx = jnp.arange(n_items, dtype=i32)
    before = (keys[None, :] < keys[:, None]) | ((keys[None, :] == keys[:, None]) & (idx[None, :] < idx[:, None]))
    order = jnp.sum(before.astype(i32), axis=1)
    starts = jnp.sum(jnp.where(order[:, None] == idx[None, :], keys[:, None], 0), axis=0)
    stops = jnp.concatenate([starts[1:], jnp.full((1,), total, i32)])
    tile = starts // EXPERT_TM
    expert = jnp.sum((ends[None, :] <= starts[:, None]).astype(i32), axis=1)
    expert = jnp.minimum(expert, MOE_EXPERTS - 1)
    used = n_tiles + jnp.sum(inner.astype(i32))
    first = (expert != jnp.concatenate([jnp.full((1,), -1, i32), expert[:-1]])) & (idx < used)
    parity = (jnp.cumsum(first.astype(i32)) - 1) % 2
    later = first[None, :] & (idx[None, :] > idx[:, None])
    nxt = jnp.min(jnp.where(later, idx[None, :], n_items), axis=1)
    next_expert = jnp.sum(jnp.where(idx[None, :] == nxt[:, None], expert[None, :], 0), axis=1)
    next_expert = jnp.where(nxt < n_items, next_expert, -1)
    keep = jnp.minimum(idx, used - 1)
    items = jnp.stack([tile, expert, starts - tile * EXPERT_TM, stops - tile * EXPERT_TM,
                       first.astype(i32), parity, next_expert, jnp.zeros_like(tile)])[:, keep]
    return pos.reshape(n // MOVE_TILE, 1, 2 * MOVE_TILE), items, used.reshape(1)


IT_TILE, IT_EXPERT, IT_LO, IT_HI, IT_FIRST, IT_PARITY, IT_NEXT = range(7)


def _row_copy(src, src_row, dst, dst_row, sem):
    return pltpu.make_async_copy(src.at[pl.ds(src_row, 1)], dst.at[pl.ds(dst_row, 1)], sem)


def _source_rows_kernel(pos_ref, src_ref):
    base = pl.program_id(0) * MOVE_TILE

    def body(r, c):
        for s in range(2):
            src_ref[pos_ref[0, 0, s * MOVE_TILE + r]] = base + r
        return c

    lax.fori_loop(0, MOVE_TILE, body, 0, unroll=8)


def _source_rows(pos, n):
    src = pl.pallas_call(
        _source_rows_kernel,
        grid=(n // MOVE_TILE,),
        in_specs=[pl.BlockSpec((1, 1, 2 * MOVE_TILE), lambda i: (i, 0, 0), memory_space=pltpu.SMEM)],
        out_specs=pl.BlockSpec(memory_space=pltpu.SMEM),
        out_shape=jax.ShapeDtypeStruct((2 * n,), jnp.int32),
        compiler_params=_cp(("arbitrary",)),
        name="moe_source_rows",
    )(pos)
    return src.reshape(2 * n // EXPERT_TM, 1, EXPERT_TM)


def _experts_kernel(items_ref, n_ref, layer_ref, src_ref, src_next_ref, x_hbm, wg_hbm, wu_hbm, wd_hbm, os_ref,
                    x_buf, wg_buf, wu_buf, wd_buf, wg_scr, wu_scr, wd_scr, xsem, sem):
    i = pl.program_id(0)
    n_tiles = x_hbm.shape[0] * 2 // EXPERT_TM

    def gather(s_ref, half, op):
        def body(r, c):
            op(_row_copy(x_hbm, s_ref[0, 0, r], x_buf.at[half], r, xsem.at[half]))
            return c

        lax.fori_loop(0, EXPERT_TM, body, 0, unroll=8)

    def weight_copies(expert, half):
        e = layer_ref[0] * MOE_EXPERTS + expert
        return [pltpu.make_async_copy(src.at[e], dst.at[half], sem.at[half])
                for src, dst in ((wg_hbm, wg_buf), (wu_hbm, wu_buf), (wd_hbm, wd_buf))]

    @pl.when(i < n_ref[0])
    def _():
        @pl.when(items_ref[IT_FIRST, i] == 1)
        def _():
            half = items_ref[IT_PARITY, i]
            own = weight_copies(items_ref[IT_EXPERT, i], half)

            @pl.when(i == 0)
            def _():
                for cp in own:
                    cp.start()

            for cp in own:
                cp.wait()
            wg_scr[...] = wg_buf[half].astype(bf16)
            wu_scr[...] = wu_buf[half].astype(bf16)
            wd_scr[...] = wd_buf[half].astype(bf16)
            nxt = items_ref[IT_NEXT, i]

            @pl.when(nxt >= 0)
            def _():
                for cp in weight_copies(nxt, 1 - half):
                    cp.start()

        tile = items_ref[IT_TILE, i]
        xhalf = tile % 2

        lo = items_ref[IT_LO, i]

        @pl.when(lo == 0)
        def _():
            @pl.when(i == 0)
            def _():
                gather(src_ref, 0, lambda cp: cp.start())

            gather(src_ref, xhalf, lambda cp: cp.wait())

        x = x_buf[xhalf].astype(bf16)
        hg = _dot(x, wg_scr[...])
        hu = _dot(x, wu_scr[...])
        out = _dot((hg * _sigmoid(hg) * hu).astype(bf16), wd_scr[...])
        prefetch = (lo == 0) & (tile + 1 < n_tiles)
        for r in range(EXPERT_TM):
            @pl.when(prefetch)
            def _():
                _row_copy(x_hbm, src_next_ref[0, 0, r], x_buf.at[1 - xhalf], r, xsem.at[1 - xhalf]).start()
        row = lax.broadcasted_iota(jnp.int32, (EXPERT_TM, 1), 0)
        mine = (row >= lo) & (row < items_ref[IT_HI, i])

        @pl.when(lo == 0)
        def _():
            os_ref[...] = jnp.where(mine, out, 0.0)

        @pl.when(lo != 0)
        def _():
            os_ref[...] = jnp.where(mine, out, os_ref[...])


def _experts_call(xn, src, items, n_items, layer, wg, wu, wd):
    n_tiles = src.shape[0]
    rows = lambda i, items, n, layer: (items[IT_TILE, i], 0)
    src_spec = lambda m: pl.BlockSpec((1, 1, EXPERT_TM), m, memory_space=pltpu.SMEM)
    hbm = pl.BlockSpec(memory_space=pl.ANY)
    return pl.pallas_call(
        _experts_kernel,
        grid_spec=pltpu.PrefetchScalarGridSpec(
            num_scalar_prefetch=3,
            grid=(items.shape[1],),
            in_specs=[src_spec(lambda i, items, n, layer: (items[IT_TILE, i], 0, 0)),
                      src_spec(lambda i, items, n, layer: (jnp.minimum(items[IT_TILE, i] + 1, n_tiles - 1), 0, 0)),
                      hbm, hbm, hbm, hbm],
            out_specs=pl.BlockSpec((EXPERT_TM, D_MODEL), rows),
            scratch_shapes=[pltpu.VMEM((2, EXPERT_TM, D_MODEL), f32),
                            pltpu.VMEM((2, D_MODEL, D_EXPERT), f32), pltpu.VMEM((2, D_MODEL, D_EXPERT), f32),
                            pltpu.VMEM((2, D_EXPERT, D_MODEL), f32),
                            pltpu.VMEM((D_MODEL, D_EXPERT), bf16), pltpu.VMEM((D_MODEL, D_EXPERT), bf16),
                            pltpu.VMEM((D_EXPERT, D_MODEL), bf16),
                            pltpu.SemaphoreType.DMA((2,)), pltpu.SemaphoreType.DMA((2,))],
        ),
        out_shape=jax.ShapeDtypeStruct((2 * xn.shape[0], D_MODEL), f32),
        compiler_params=_cp(("arbitrary",)),
        name="moe_experts",
    )(items, n_items, layer, src, src, xn, wg, wu, wd)


def _combine_kernel(pos_ref, pos_next_ref, x_ref, route_ref, gn_ref, os_ref, *rest, emit_x, split):
    outs, (buf, sem, xn_scr) = rest[:-3], rest[-3:]
    i = pl.program_id(0)
    half = i % 2
    last = i == pl.num_programs(0) - 1

    def copy(p_ref, hf, r, s):
        return _row_copy(os_ref, p_ref[0, 0, s * MOVE_TILE + r], buf.at[hf, s], r, sem.at[hf])

    def loop_all(p_ref, hf, op):
        def body(r, c):
            for s in range(2):
                op(copy(p_ref, hf, r, s), s)
            return c

        lax.fori_loop(0, MOVE_TILE, body, 0, unroll=8)

    begin = lambda cp, s: cp.start(priority=s)
    finish = lambda cp, s: cp.wait()

    @pl.when(i == 0)
    def _():
        loop_all(pos_ref, 0, begin)

    loop_all(pos_ref, half, finish)
    xn_ref = xn_scr if split else outs[-1]
    for c in range(MOVE_TILE // COMBINE_CHUNK):
        rows = slice(c * COMBINE_CHUNK, (c + 1) * COMBINE_CHUNK)
        route = route_ref[rows, :]
        x2 = x_ref[rows, :] + route[:, 4:5] * buf[half, 0, rows, :] + route[:, 5:6] * buf[half, 1, rows, :]
        if emit_x:
            outs[0][rows, :] = x2
        xn_ref[rows, :] = _rms(x2, gn_ref[...]).astype(xn_ref.dtype)
        for r in range(rows.start, rows.stop):
            for s in range(2):
                begin(copy(pos_next_ref, 1 - half, r, s), s)

    @pl.when(last)
    def _():
        loop_all(pos_next_ref, 1 - half, finish)

    if split:
        main_ref, tail_ref = outs[-2:]
        is_main = i < N_MAIN // MOVE_TILE

        @pl.when(is_main)
        def _():
            main_ref[...] = xn_scr[...]

        @pl.when(jnp.logical_not(is_main))
        def _():
            tail_ref[...] = xn_scr[...]


def _combine_call(x1, route, pos, os_rows, gn, xn_dtype, *, emit_x, split):
    n = x1.shape[0]
    row = lambda i: (i, 0)
    blk = lambda m: pl.BlockSpec((MOVE_TILE, D_MODEL), m)
    nm = N_MAIN // MOVE_TILE
    out_specs, out_shape = [], []
    if emit_x:
        out_specs.append(blk(row))
        out_shape.append(jax.ShapeDtypeStruct((n, D_MODEL), f32))
    if split:
        out_specs += [blk(lambda i: (jnp.minimum(i, nm - 1), 0)), blk(lambda i: (jnp.maximum(i - nm, 0), 0))]
        out_shape += [jax.ShapeDtypeStruct((N_MAIN, D_MODEL), xn_dtype),
                      jax.ShapeDtypeStruct((n - N_MAIN, D_MODEL), xn_dtype)]
    else:
        out_specs.append(blk(row))
        out_shape.append(jax.ShapeDtypeStruct((n, D_MODEL), xn_dtype))
    nt = n // MOVE_TILE
    pos_spec = lambda m: pl.BlockSpec((1, 1, 2 * MOVE_TILE), m, memory_space=pltpu.SMEM)
    return pl.pallas_call(
        functools.partial(_combine_kernel, emit_x=emit_x, split=split),
        grid=(nt,),
        in_specs=[pos_spec(lambda i: (i, 0, 0)), pos_spec(lambda i: (jnp.minimum(i + 1, nt - 1), 0, 0)),
                  blk(row), pl.BlockSpec((MOVE_TILE, LANES), row),
                  pl.BlockSpec((1, D_MODEL), lambda i: (0, 0)), pl.BlockSpec(memory_space=pl.ANY)],
        out_specs=out_specs,
        out_shape=out_shape,
        scratch_shapes=[pltpu.VMEM((2, 2, MOVE_TILE, D_MODEL), f32), pltpu.SemaphoreType.DMA((2,)),
                        pltpu.VMEM((MOVE_TILE, D_MODEL), xn_dtype)],
        compiler_params=_cp(("arbitrary",)),
        name="moe_combine",
    )(pos, pos, x1, route, gn, os_rows)


S5_GROUP = 16
S5_GROUPS = D_MODEL // S5_GROUP
S5_STATE = 64
S5_CB = 128
S5_NB = D_MODEL // S5_CB
S5_GPB = S5_CB // S5_GROUP
S5_SB = S5_GPB * S5_STATE
S5_BC = S5_STATE * S5_GROUP
S5_CPS = 2


def _s5_disc_kernel(lr_ref, li_ref, ldt_ref, bre_ref, bim_ref,
                    abr_ref, abi_ref, ab2r_ref, ab2i_ref, bbr_ref, bbi_ref, abbr_ref, abbi_ref):
    lr, li = lr_ref[...], li_ref[...]
    dt = jnp.exp(ldt_ref[...])
    mag = jnp.exp(lr * dt)
    ang = li * dt
    ab_re, ab_im = mag * jnp.cos(ang), mag * jnp.sin(ang)
    nr, ni = ab_re - 1.0, ab_im
    den = lr * lr + li * li
    f_re = (nr * lr + ni * li) / den
    f_im = (ni * lr - nr * li) / den
    abr_ref[...] = ab_re
    abi_ref[...] = ab_im
    ab2r_ref[...] = ab_re * ab_re - ab_im * ab_im
    ab2i_ref[...] = 2.0 * (ab_re * ab_im)
    pi = lax.broadcasted_iota(jnp.int32, (S5_STATE, S5_BC), 0)
    ci = lax.broadcasted_iota(jnp.int32, (S5_STATE, S5_BC), 1)
    rep = jnp.where(ci // S5_GROUP == pi, 1.0, 0.0).astype(bf16)

    def expand(v):
        hi, mid, lo = _split3(v)
        return _dot(hi, rep) + _dot(mid, rep) + _dot(lo, rep)

    fr, fi, ar, ai = expand(f_re), expand(f_im), expand(ab_re), expand(ab_im)
    br, bi = bre_ref[...], bim_ref[...]
    bb_re = fr * br - fi * bi
    bb_im = fr * bi + fi * br
    bbr_ref[...] = bb_re
    bbi_ref[...] = bb_im
    abbr_ref[...] = ar * bb_re - ai * bb_im
    abbi_ref[...] = ar * bb_im + ai * bb_re


def _s5_weights(lam_re, lam_im, log_dt, b_re, b_im, c_re, c_im):
    st = jax.ShapeDtypeStruct((S5_GROUPS, S5_STATE), f32)
    bc = jax.ShapeDtypeStruct((S5_GROUPS, S5_BC), f32)
    ab_re, ab_im, ab2_re, ab2_im, bb_re, bb_im, abb_re, abb_im = pl.pallas_call(
        _s5_disc_kernel, out_shape=[st, st, st, st, bc, bc, bc, bc], name="s5_discretize",
    )(lam_re, lam_im, log_dt[:, None], b_re.reshape(S5_GROUPS, S5_BC), b_im.reshape(S5_GROUPS, S5_BC))
    eye = jnp.eye(S5_GPB, dtype=f32)

    def in_blocks(m):
        m = m.reshape(S5_NB, S5_GPB, S5_STATE, S5_GROUP)
        return jnp.einsum("jgpc,gh->jgchp", m, eye).reshape(S5_NB, S5_CB, S5_SB)

    def out_blocks(m):
        m = m.reshape(S5_NB, S5_GPB, S5_GROUP, S5_STATE)
        return jnp.einsum("jgcp,gh->jgphc", m, eye).reshape(S5_NB, S5_SB, S5_CB)

    bb2 = jnp.concatenate([
        jnp.concatenate([in_blocks(bb_re), in_blocks(bb_im)], axis=2),
        jnp.concatenate([in_blocks(abb_re), in_blocks(abb_im)], axis=2)], axis=1).astype(bf16)
    cc = jnp.concatenate([out_blocks(c_re), -out_blocks(c_im)], axis=1).astype(bf16)
    rows = [v.reshape(S5_NB, 1, S5_SB) for v in (ab_re, ab_im, ab2_re, ab2_im)]
    abv = jnp.concatenate(rows + [jnp.zeros((S5_NB, SUB - len(rows), S5_SB), f32)], axis=1)
    return bb2, cc, abv


def _s5_seq_kernel(x0_ref, x1_ref, x2_ref, x3_ref, halo0_ref, h0_ref, bb2_ref, cc_ref, abv_ref, y_ref, hout_ref,
                   xf_scr, xp_scr, bu_scr, hs_scr, yp_scr, yn_scr, h_scr, halo_scr):
    tb = pl.program_id(1)
    TL = x0_ref.shape[0]
    KB = TL // 2
    RB = BATCH * TL

    @pl.when(tb == 0)
    def _():
        h_scr[...] = h0_ref[...]
        halo_scr[...] = halo0_ref[...].astype(f32)

    chans = [slice(c * S5_CB, (c + 1) * S5_CB) for c in range(S5_CPS)]
    for c, ch in enumerate(chans):
        for b, xb_ref in enumerate((x0_ref, x1_ref, x2_ref, x3_ref)):
            xf_scr[c, b * TL:(b + 1) * TL, :] = xb_ref[:, ch].astype(f32)
        for b in range(BATCH):
            for p in range(2):
                xp_scr[c, pl.ds(2 * b + p, KB, stride=SUB), :] = xf_scr[c, pl.ds(b * TL + p, KB, stride=2), :]
    x = jnp.concatenate([xp_scr[c] for c in range(S5_CPS)], axis=1)
    xc = jnp.concatenate([halo_scr[...], x], axis=0)
    odd = (lax.broadcasted_iota(jnp.int32, (RB + SUB, 1), 0) & 1) == 1
    xprev = jnp.where(odd, pltpu.roll(xc, 1, axis=0), pltpu.roll(xc, SUB - 1, axis=0))[SUB:]
    halo_scr[...] = x[RB - SUB:]
    for c, ch in enumerate(chans):
        lhs = jnp.concatenate([x[:, ch], xprev[:, ch]], axis=1).astype(bf16)
        bu_scr[c] = _dot(lhs, bb2_ref[c])
    a2 = [(abv_ref[c, 2:3, :], abv_ref[c, 3:4, :]) for c in range(S5_CPS)]

    def step(k, carry):
        r0 = pl.multiple_of(k * SUB, SUB)
        out = []
        for c in range(S5_CPS):
            hr, hi = carry[2 * c], carry[2 * c + 1]
            a2r, a2i = a2[c]
            bu = bu_scr[c, pl.ds(r0, SUB), :]
            nr = a2r * hr - a2i * hi + bu[:, :S5_SB]
            ni = a2r * hi + a2i * hr + bu[:, S5_SB:]
            hs_scr[c, pl.ds(r0, SUB), :S5_SB] = nr
            hs_scr[c, pl.ds(r0, SUB), S5_SB:] = ni
            out += [nr, ni]
        return tuple(out)

    init = []
    for c in range(S5_CPS):
        init += [h_scr[c, :, :S5_SB], h_scr[c, :, S5_SB:]]
    fin = lax.fori_loop(0, RB // SUB, step, tuple(init), unroll=4)
    for c, ch in enumerate(chans):
        h_scr[c, :, :S5_SB] = fin[2 * c]
        h_scr[c, :, S5_SB:] = fin[2 * c + 1]
        yp_scr[c] = _dot(hs_scr[c].astype(bf16), cc_ref[c])
        for b in range(BATCH):
            for p in range(2):
                yn_scr[c, pl.ds(b * TL + p, KB, stride=2), :] = yp_scr[c, pl.ds(2 * b + p, KB, stride=SUB), :]
            y_ref[b, :, ch] = yn_scr[c, b * TL:(b + 1) * TL, :]

    @pl.when(tb == pl.num_programs(1) - 1)
    def _():
        hout_ref[...] = h_scr[...]


def _s5_seq_call(x, halo0, h0, bb2, cc, abv, *, row0, seq_len, tl):
    wsel = lambda j, t: (j, 0, 0)
    rb = BATCH * tl
    cw = S5_CPS * S5_CB
    xspec = lambda b: pl.BlockSpec((tl, cw), lambda j, t: ((row0 + b * seq_len) // tl + t, j))
    return pl.pallas_call(
        _s5_seq_kernel,
        grid=(S5_NB // S5_CPS, seq_len // tl),
        in_specs=[xspec(b) for b in range(BATCH)]
        + [pl.BlockSpec((SUB, cw), lambda j, t: (0, j)),
           pl.BlockSpec((S5_CPS, SUB, 2 * S5_SB), wsel),
           pl.BlockSpec((S5_CPS, 2 * S5_CB, 2 * S5_SB), wsel),
           pl.BlockSpec((S5_CPS, 2 * S5_SB, S5_CB), wsel),
           pl.BlockSpec((S5_CPS, SUB, S5_SB), wsel)],
        out_specs=[pl.BlockSpec((BATCH, tl, cw), lambda j, t: (0, t, j)),
                   pl.BlockSpec((S5_CPS, SUB, 2 * S5_SB), wsel)],
        out_shape=[jax.ShapeDtypeStruct((BATCH, seq_len, D_MODEL), f32),
                   jax.ShapeDtypeStruct((S5_NB, SUB, 2 * S5_SB), f32)],
        scratch_shapes=[pltpu.VMEM((S5_CPS, rb, S5_CB), f32), pltpu.VMEM((S5_CPS, rb, S5_CB), f32),
                        pltpu.VMEM((S5_CPS, rb, 2 * S5_SB), f32), pltpu.VMEM((S5_CPS, rb, 2 * S5_SB), f32),
                        pltpu.VMEM((S5_CPS, rb, S5_CB), f32), pltpu.VMEM((S5_CPS, rb, S5_CB), f32),
                        pltpu.VMEM((S5_CPS, SUB, 2 * S5_SB), f32), pltpu.VMEM((SUB, cw), f32)],
        compiler_params=_cp(("arbitrary", "arbitrary")),
        name=f"s5_seq_{seq_len}",
    )(x, x, x, x, halo0, h0, bb2, cc, abv)


def _s5_sample_kernel(x_ref, hre_ref, him_ref, bb2_ref, cc_ref, abv_ref, y_ref, ore_ref, oim_ref, hs_scr):
    nb = hre_ref.shape[0]
    bu = _dot(x_ref[...], bb2_ref[0, :S5_CB, :])
    ar = abv_ref[0, 0:1, :]
    ai = abv_ref[0, 1:2, :]
    hr, hi = hre_ref[...], him_ref[...]
    for t in range(DEC_SEQ):
        rows = slice(t * nb, (t + 1) * nb)
        hr, hi = (ar * hr - ai * hi + bu[rows, :S5_SB], ar * hi + ai * hr + bu[rows, S5_SB:])
        hs_scr[rows, :S5_SB] = hr
        hs_scr[rows, S5_SB:] = hi
    y_ref[...] = _dot(hs_scr[...].astype(bf16), cc_ref[0])
    ore_ref[...] = hr
    oim_ref[...] = hi


def _s5_sample_call(xt, h_re, h_im, bb2, cc, abv):
    n = xt.shape[0]
    nb = h_re.shape[0]
    wsel = lambda j: (j, 0, 0)
    st = pl.BlockSpec((nb, S5_SB), lambda j: (0, j))
    return pl.pallas_call(
        _s5_sample_kernel,
        grid=(S5_NB,),
        in_specs=[pl.BlockSpec((n, S5_CB), lambda j: (0, j)), st, st,
                  pl.BlockSpec((1, 2 * S5_CB, 2 * S5_SB), wsel),
                  pl.BlockSpec((1, 2 * S5_SB, S5_CB), wsel),
                  pl.BlockSpec((1, SUB, S5_SB), wsel)],
        out_specs=[pl.BlockSpec((n, S5_CB), lambda j: (0, j)), st, st],
        out_shape=[jax.ShapeDtypeStruct((n, D_MODEL), f32),
                   jax.ShapeDtypeStruct(h_re.shape, f32), jax.ShapeDtypeStruct(h_im.shape, f32)],
        scratch_shapes=[pltpu.VMEM((n, 2 * S5_SB), f32)],
        compiler_params=_cp(("arbitrary",)),
        name="s5_sample",
    )(xt, h_re, h_im, bb2, cc, abv)


GLA_CHUNK = 256
GLA_DIRECT = 2
S5_TL = 256


def _moe_layer(layer, x1, xnf, route, plan, cnt, wg, wu, wd, gn, xn_dtype, *, emit_x, split):
    pos, items, n_items = _moe_plan(plan, cnt, x1.shape[0])
    src = _source_rows(pos, x1.shape[0])
    os_rows = _experts_call(xnf, src, items, n_items, jnp.full((1,), layer, jnp.int32), wg, wu, wd)
    return _combine_call(x1, route, pos, os_rows, gn, xn_dtype, emit_x=emit_x, split=split)


def kernel(x_prompt, x_sample, state_gla, state_s5_re, state_s5_im, meta_tokens, norm_mix_g, norm_ffn_g, norm_final_g, gla_w_in, gla_w_a2, gla_b_a, gla_g_o, gla_w_o, s5_lambda_re, s5_lambda_im, s5_log_dt, s5_b_re, s5_b_im, s5_c_re, s5_c_im, s5_d, s5_w_glu, s5_b_glu, moe_w_rg, moe_b_rg, moe_w_re, moe_b_re, moe_w_gate, moe_w_up, moe_w_down):
    row = lambda v: v.reshape(1, -1)
    x_main = x_prompt.reshape(N_MAIN, D_MODEL)
    x_tail = jnp.concatenate([
        jnp.tile(meta_tokens.astype(x_prompt.dtype), (BATCH, 1)),
        x_sample.reshape(N_SAMPLE, D_MODEL),
        jnp.zeros((N_ROWS - N_REAL, D_MODEL), x_prompt.dtype)], axis=0)
    wg = moe_w_gate.reshape(-1, D_MODEL, D_EXPERT)
    wu = moe_w_up.reshape(-1, D_MODEL, D_EXPERT)
    wd = moe_w_down.reshape(-1, D_EXPERT, D_MODEL)

    w_in = jnp.swapaxes(gla_w_in, 1, 2)
    wa1 = jnp.pad(gla_w_in[0, :, GLA_QKVR:], ((0, 0), (0, LANES - GLA_RANK)))
    wa2 = jnp.pad(gla_w_a2.reshape(GLA_RANK, GLA_KEY), ((0, LANES - GLA_RANK), (0, 0)))
    xn, glog = _norm_gate_call(x_main, x_tail, row(norm_mix_g[0]), wa1, wa2, row(gla_b_a))
    proj = _proj_call(xn, w_in, GLA_QKVR)
    go = row(gla_g_o)
    og = jnp.zeros((N_ROWS, GLA_VAL), bf16)
    s_zero = jnp.zeros((BATCH, GLA_HEADS, GLA_DK, GLA_DV), f32)
    og, s_meta = _gla_seq_call(proj, glog, go, s_zero, og, row0=ROW_META, C=N_META, n_chunks=1, d=GLA_DIRECT)
    og, s_prompt = _gla_seq_call(proj, glog, go, s_meta, og, row0=0, C=GLA_CHUNK, n_chunks=SEQ // GLA_CHUNK,
                                 d=GLA_DIRECT)
    og, s_sample = _gla_sample_call(proj, glog, go, state_gla.reshape(DEC_BATCH, GLA_HEADS, GLA_DK, GLA_DV), og,
                                    row0=ROW_SAMPLE)
    wr, br = _router_weights(moe_w_rg[0], moe_b_rg[0], moe_w_re[0], moe_b_re[0])
    tile_row = lambda i: (i, 0)
    x1, xnf, route, plan, cnt = _mix_out_call(
        _gla_out_kernel, "gla_out", N_ROWS, [og, x_main, x_tail], [tile_row, *_main_tail_maps(MIX_TILE)],
        [gla_w_o.reshape(GLA_VAL, D_MODEL), row(norm_ffn_g[0]), wr, br])
    x2, xn2 = _moe_layer(0, x1, xnf, route, plan, cnt, wg, wu, wd, row(norm_mix_g[1]), bf16, emit_x=True, split=False)

    bb2, cc, abv = _s5_weights(s5_lambda_re[0], s5_lambda_im[0], s5_log_dt[0], s5_b_re[0], s5_b_im[0],
                               s5_c_re[0], s5_c_im[0])
    y_meta, h_meta = _s5_seq_call(xn2, jnp.zeros((SUB, D_MODEL), bf16), jnp.zeros((S5_NB, SUB, 2 * S5_SB), f32),
                                  bb2, cc, abv, row0=ROW_META, seq_len=N_META, tl=N_META)
    halo = xn2[ROW_META:ROW_SAMPLE].reshape(BATCH, N_META, D_MODEL)[:, N_META - 2:].reshape(SUB, D_MODEL)
    y_main, h_main = _s5_seq_call(xn2, halo, h_meta, bb2, cc, abv, row0=0, seq_len=SEQ, tl=S5_TL)
    xt_sample = xn2[ROW_SAMPLE:N_REAL].reshape(DEC_BATCH, DEC_SEQ, D_MODEL).transpose(1, 0, 2).reshape(N_SAMPLE, D_MODEL)
    y_samp, s5r_s, s5i_s = _s5_sample_call(
        xt_sample, state_s5_re.reshape(DEC_BATCH, S5_GROUPS * S5_STATE),
        state_s5_im.reshape(DEC_BATCH, S5_GROUPS * S5_STATE), bb2, cc, abv)
    ys_tail = jnp.concatenate([
        y_meta.reshape(N_METAROWS, D_MODEL),
        y_samp.reshape(DEC_SEQ, DEC_BATCH, D_MODEL).transpose(1, 0, 2).reshape(N_SAMPLE, D_MODEL),
        jnp.zeros((N_ROWS - N_REAL, D_MODEL), f32)], axis=0)
    wr, br = _router_weights(moe_w_rg[1], moe_b_rg[1], moe_w_re[1], moe_b_re[1])
    x3, xnf, route, plan, cnt = _mix_out_call(
        _s5_out_kernel, "s5_out", N_ROWS, [y_main.reshape(N_MAIN, D_MODEL), ys_tail, x2],
        [*_main_tail_maps(MIX_TILE), tile_row],
        [s5_w_glu.reshape(D_MODEL, D_MODEL), row(norm_mix_g[1]), row(s5_d), row(s5_b_glu),
         row(norm_ffn_g[1]), wr, br])
    y_main_out, y_tail_out = _moe_layer(1, x3, xnf, route, plan, cnt, wg, wu, wd, row(norm_final_g), f32,
                                        emit_x=False, split=True)

    y_prompt = y_main_out.reshape(BATCH, SEQ, D_MODEL)
    y_sample = y_tail_out[N_METAROWS:N_METAROWS + N_SAMPLE].reshape(DEC_BATCH, DEC_SEQ, D_MODEL)
    hfin = h_main.reshape(S5_NB, BATCH, 2, 2, S5_GPB, S5_STATE)[:, :, 1]
    s5r_p = hfin[:, :, 0].transpose(1, 0, 2, 3).reshape(1, BATCH, S5_GROUPS, S5_STATE)
    s5i_p = hfin[:, :, 1].transpose(1, 0, 2, 3).reshape(1, BATCH, S5_GROUPS, S5_STATE)
    return (y_prompt, y_sample, s_prompt[None], s5r_p, s5i_p, s_sample[None],
            s5r_s.reshape(1, DEC_BATCH, S5_GROUPS, S5_STATE), s5i_s.reshape(1, DEC_BATCH, S5_GROUPS, S5_STATE))
```

```python
import functools

import jax
import jax.numpy as jnp
from jax import lax
from jax.experimental import pallas as pl
from jax.experimental.pallas import tpu as pltpu

f32 = jnp.float32
bf16 = jnp.bfloat16

D_MODEL = 2048
BATCH = 4
SEQ = 2048
DEC_BATCH = 128
DEC_SEQ = 4
N_META = 16
EPS = 1e-6
GLA_HEADS = 4
GLA_DK = 256
GLA_DV = 512
GLA_KEY = GLA_HEADS * GLA_DK
GLA_VAL = GLA_HEADS * GLA_DV
GLA_RANK = 16
GLA_TAU = 16.0
GLA_QKVR = 2 * GLA_KEY + 2 * GLA_VAL

N_MAIN = BATCH * SEQ
N_METAROWS = BATCH * N_META
N_SAMPLE = DEC_BATCH * DEC_SEQ
ROW_META = N_MAIN
ROW_SAMPLE = N_MAIN + N_METAROWS
N_REAL = ROW_SAMPLE + N_SAMPLE
ROW_TILE = 256
N_ROWS = -(-N_REAL // ROW_TILE) * ROW_TILE

VMEM_LIMIT = 56 * 1024 * 1024


def _cp(sem, vmem=VMEM_LIMIT):
    return pltpu.CompilerParams(dimension_semantics=sem, vmem_limit_bytes=vmem)


def _dot(a, b):
    return jnp.dot(a, b, preferred_element_type=f32)


def _dot_nt(a, b):
    return lax.dot_general(a, b, (((1,), (1,)), ((), ())), preferred_element_type=f32)


def _dot_tn(a, b):
    return lax.dot_general(a, b, (((0,), (0,)), ((), ())), preferred_element_type=f32)


def _sigmoid(x):
    return 1.0 / (1.0 + jnp.exp(-x))


def _split3(x):
    hi = x.astype(bf16)
    r1 = x - hi.astype(f32)
    mid = r1.astype(bf16)
    lo = (r1 - mid.astype(f32)).astype(bf16)
    return hi, mid, lo


def _cumsum_rows(g, C):
    if C <= 16:
        row = lax.broadcasted_iota(jnp.int32, (C, 1), 0)
        b = jnp.zeros_like(g)
        for s in range(C):
            b = b + jnp.where(row >= s, g[s:s + 1, :], 0.0)
        return b
    row = lax.broadcasted_iota(jnp.int32, (C, C), 0)
    col = lax.broadcasted_iota(jnp.int32, (C, C), 1)
    tri = jnp.where(row >= col, 1.0, 0.0).astype(bf16)
    hi, mid, lo = _split3(g)
    return _dot(tri, hi) + _dot(tri, mid) + _dot(tri, lo)


PAIR_LEVEL = 1000


def _gla_pair_code(C, d, tree=True):
    ti = lax.broadcasted_iota(jnp.int32, (C, C), 0)
    si = lax.broadcasted_iota(jnp.int32, (C, C), 1)
    code = jnp.where((ti // d == si // d) & (si <= ti), 1 + ti - si, 0)
    h = d
    while tree and h < C:
        tb = ti // h
        code = jnp.where(((tb % 2) == 1) & ((si // h) == tb - 1), PAIR_LEVEL + h, code)
        h *= 2
    return code


def _gla_scores(q, k, b, code, C, d, tree=True):
    row = lax.broadcasted_iota(jnp.int32, (C, 1), 0)
    scores = jnp.zeros((C, C), f32)
    for dl in range(d):
        ks = k if dl == 0 else pltpu.roll(k, dl, axis=0)
        bs = b if dl == 0 else pltpu.roll(b, dl, axis=0)
        term = q * ks * jnp.exp(jnp.minimum(b - bs, 0.0))
        colv = jnp.sum(term, axis=1, keepdims=True)
        scores = jnp.where(code == 1 + dl, colv, scores)
    z = b
    s = 1
    while tree and 2 * s < C:
        z = jnp.where((row & s) != 0, pltpu.roll(z, s, axis=0), z)
        s *= 2
        h = s
        if h < d:
            continue
        bnext = pltpu.roll(z, C - h, axis=0)
        qh = (q * jnp.exp(jnp.minimum(b - z, 0.0))).astype(bf16)
        kh = (k * jnp.exp(jnp.minimum(bnext - b, 0.0))).astype(bf16)
        scores = jnp.where(code == PAIR_LEVEL + h, _dot_nt(qh, kh), scores)
    return scores


def _gla_chunk(q, k, v, g, S, code, C, d, tree=True):
    b = _cumsum_rows(g, C)
    o = _dot((q * jnp.exp(b)).astype(bf16), S.astype(bf16))
    scores = _gla_scores(q, k, b, code, C, d, tree)
    o = o + _dot(scores.astype(bf16), v)
    b_last = b[C - 1:C, :]
    kd = (k * jnp.exp(b_last - b)).astype(bf16)
    if C == GLA_DK:
        eye = code == 1
    else:
        eye = (lax.broadcasted_iota(jnp.int32, (GLA_DK, GLA_DK), 0)
               == lax.broadcasted_iota(jnp.int32, (GLA_DK, GLA_DK), 1))
    dec_col = jnp.sum(jnp.where(eye, jnp.exp(b_last), 0.0), axis=1, keepdims=True)
    S_new = dec_col * S + _dot_tn(kd, v)
    return o, S_new


def _gla_head_epilogue(o, r, go):
    ms = jnp.mean(o * o, axis=1, keepdims=True)
    on = o * lax.rsqrt(ms + EPS) * go
    return on * (r * _sigmoid(r))


def _gla_seq_kernel(q_ref, k_ref, v_ref, r_ref, g_ref, go_ref, s0_ref, _og_in, og_ref, sout_ref, s_scr, *, C, d):
    c = pl.program_id(1)

    @pl.when(c == 0)
    def _():
        s_scr[...] = s0_ref[0]

    code = _gla_pair_code(C, d)

    def head(h, carry):
        ck = pl.ds(pl.multiple_of(h * GLA_DK, GLA_DK), GLA_DK)
        cv = pl.ds(pl.multiple_of(h * GLA_DV, GLA_DV), GLA_DV)
        q = q_ref[:, ck].astype(f32) * (GLA_DK ** -0.5)
        k = k_ref[:, ck].astype(f32)
        o, S_new = _gla_chunk(q, k, v_ref[:, cv], g_ref[:, ck], s_scr[h], code, C, d)
        s_scr[h] = S_new
        og_ref[:, cv] = _gla_head_epilogue(o, r_ref[:, cv].astype(f32), go_ref[:, cv]).astype(og_ref.dtype)
        return carry

    lax.fori_loop(0, GLA_HEADS, head, 0)

    @pl.when(c == pl.num_programs(1) - 1)
    def _():
        sout_ref[0] = s_scr[...]


def _gla_seq_call(proj, glog, go, s0, og_buf, *, row0, C, n_chunks, d):
    blk0 = row0 // C
    rows = lambda b, c: blk0 + b * n_chunks + c
    kern = functools.partial(_gla_seq_kernel, C=C, d=d)
    return pl.pallas_call(
        kern,
        grid=(BATCH, n_chunks),
        in_specs=[
            pl.BlockSpec((C, GLA_KEY), lambda b, c: (rows(b, c), 0)),
            pl.BlockSpec((C, GLA_KEY), lambda b, c: (rows(b, c), 1)),
            pl.BlockSpec((C, GLA_VAL), lambda b, c: (rows(b, c), 1)),
            pl.BlockSpec((C, GLA_VAL), lambda b, c: (rows(b, c), 2)),
            pl.BlockSpec((C, GLA_KEY), lambda b, c: (rows(b, c), 0)),
            pl.BlockSpec((1, GLA_VAL), lambda b, c: (0, 0)),
            pl.BlockSpec((1, GLA_HEADS, GLA_DK, GLA_DV), lambda b, c: (b, 0, 0, 0)),
            pl.BlockSpec(memory_space=pl.ANY),
        ],
        out_specs=[
            pl.BlockSpec((C, GLA_VAL), lambda b, c: (rows(b, c), 0)),
            pl.BlockSpec((1, GLA_HEADS, GLA_DK, GLA_DV), lambda b, c: (b, 0, 0, 0)),
        ],
        out_shape=[
            jax.ShapeDtypeStruct(og_buf.shape, og_buf.dtype),
            jax.ShapeDtypeStruct((BATCH, GLA_HEADS, GLA_DK, GLA_DV), f32),
        ],
        scratch_shapes=[pltpu.VMEM((GLA_HEADS, GLA_DK, GLA_DV), f32)],
        input_output_aliases={7: 0},
        compiler_params=_cp(("arbitrary", "arbitrary")),
        name=f"gla_seq_c{C}",
    )(proj, proj, proj, proj, glog, go, s0, og_buf)


SAMPLE_BB = 4
SAMPLE_C = SAMPLE_BB * DEC_SEQ


def _gla_sample_kernel(q_ref, k_ref, v_ref, r_ref, g_ref, go_ref, s0_ref, _og_in, og_ref, sout_ref):
    row = lax.broadcasted_iota(jnp.int32, (SAMPLE_C, 1), 0)

    code = _gla_pair_code(SAMPLE_C, DEC_SEQ, tree=False)

    def take(x, bb):
        sh = (SAMPLE_C - DEC_SEQ * bb) % SAMPLE_C
        return jnp.where(row < DEC_SEQ, pltpu.roll(x, sh, axis=0) if sh else x, 0.0)

    def head(h, carry):
        ck = pl.ds(pl.multiple_of(h * GLA_DK, GLA_DK), GLA_DK)
        cv = pl.ds(pl.multiple_of(h * GLA_DV, GLA_DV), GLA_DV)
        q_all = q_ref[:, ck].astype(f32) * (GLA_DK ** -0.5)
        k_all = k_ref[:, ck].astype(f32)
        v_all = v_ref[:, cv].astype(f32)
        r_all = r_ref[:, cv].astype(f32)
        g_all = g_ref[:, ck]
        go = go_ref[:, cv]
        acc = jnp.zeros((SAMPLE_C, GLA_DV), f32)
        for bb in range(SAMPLE_BB):
            o, S_new = _gla_chunk(take(q_all, bb), take(k_all, bb), take(v_all, bb).astype(bf16),
                                  take(g_all, bb), s0_ref[bb, h], code, SAMPLE_C, DEC_SEQ, tree=False)
            sout_ref[bb, h] = S_new
            y = _gla_head_epilogue(o, take(r_all, bb), go)
            acc = jnp.where(row // DEC_SEQ == bb, pltpu.roll(y, DEC_SEQ * bb, axis=0) if bb else y, acc)
        og_ref[:, cv] = acc.astype(og_ref.dtype)
        return carry

    lax.fori_loop(0, GLA_HEADS, head, 0)


def _gla_sample_call(proj, glog, go, s0, og_buf, *, row0):
    n_seq = s0.shape[0]
    blk0 = row0 // SAMPLE_C
    st_spec = pl.BlockSpec((SAMPLE_BB, GLA_HEADS, GLA_DK, GLA_DV), lambda i: (i, 0, 0, 0))
    return pl.pallas_call(
        _gla_sample_kernel,
        grid=(n_seq // SAMPLE_BB,),
        in_specs=[
            pl.BlockSpec((SAMPLE_C, GLA_KEY), lambda i: (blk0 + i, 0)),
            pl.BlockSpec((SAMPLE_C, GLA_KEY), lambda i: (blk0 + i, 1)),
            pl.BlockSpec((SAMPLE_C, GLA_VAL), lambda i: (blk0 + i, 1)),
            pl.BlockSpec((SAMPLE_C, GLA_VAL), lambda i: (blk0 + i, 2)),
            pl.BlockSpec((SAMPLE_C, GLA_KEY), lambda i: (blk0 + i, 0)),
            pl.BlockSpec((1, GLA_VAL), lambda i: (0, 0)),
            st_spec,
            pl.BlockSpec(memory_space=pl.ANY),
        ],
        out_specs=[pl.BlockSpec((SAMPLE_C, GLA_VAL), lambda i: (blk0 + i, 0)), st_spec],
        out_shape=[jax.ShapeDtypeStruct(og_buf.shape, og_buf.dtype), jax.ShapeDtypeStruct(s0.shape, f32)],
        input_output_aliases={7: 0},
        compiler_params=_cp(("arbitrary",)),
        name="gla_sample",
    )(proj, proj, proj, proj, glog, go, s0, og_buf)


LANES = 128
SUB = 8
MOE_GROUPS = 4
MOE_EPG = 8
MOE_EXPERTS = MOE_GROUPS * MOE_EPG
D_EXPERT = 256
ROUTE_E0 = MOE_GROUPS
MIX_TILE = 256
PROJ_NT = 1024
PROJ_MT = N_ROWS // 7


def _rms(x, g):
    r = lax.rsqrt(jnp.mean(x * x, axis=-1, keepdims=True) + EPS)
    return (x * r) * g


def _log_sigmoid(z):
    return jnp.minimum(z, 0.0) - jnp.log1p(jnp.exp(-jnp.abs(z)))


def _main_or_tail(tile, main_ref, tail_ref):
    return jnp.where(pl.program_id(0) < N_MAIN // tile, main_ref[...], tail_ref[...])


def _main_tail_maps(tile):
    nm = N_MAIN // tile
    return (lambda i: (jnp.minimum(i, nm - 1), 0)), (lambda i: (jnp.maximum(i - nm, 0), 0))


def _norm_gate_kernel(xm_ref, xt_ref, gn_ref, wa1_ref, wa2_ref, ba_ref, xn_ref, gl_ref):
    xnb = _rms(_main_or_tail(ROW_TILE, xm_ref, xt_ref), gn_ref[...]).astype(bf16)
    xn_ref[...] = xnb
    a = _dot(xnb, wa1_ref[...].astype(bf16))
    z = _dot(a.astype(bf16), wa2_ref[...].astype(bf16)) + ba_ref[...]
    gl_ref[...] = _log_sigmoid(z) * (1.0 / GLA_TAU)


def _norm_gate_call(x_main, x_tail, gn, wa1, wa2, ba):
    n = x_main.shape[0] + x_tail.shape[0]
    row = lambda i: (i, 0)
    fix = lambda i: (0, 0)
    main_map, tail_map = _main_tail_maps(ROW_TILE)
    return pl.pallas_call(
        _norm_gate_kernel,
        grid=(n // ROW_TILE,),
        in_specs=[pl.BlockSpec((ROW_TILE, D_MODEL), main_map), pl.BlockSpec((ROW_TILE, D_MODEL), tail_map),
                  pl.BlockSpec((1, D_MODEL), fix),
                  pl.BlockSpec((D_MODEL, LANES), fix), pl.BlockSpec((LANES, GLA_KEY), fix),
                  pl.BlockSpec((1, GLA_KEY), fix)],
        out_specs=[pl.BlockSpec((ROW_TILE, D_MODEL), row), pl.BlockSpec((ROW_TILE, GLA_KEY), row)],
        out_shape=[jax.ShapeDtypeStruct((n, D_MODEL), bf16), jax.ShapeDtypeStruct((n, GLA_KEY), f32)],
        compiler_params=_cp(("arbitrary",)),
        name="norm_gate",
    )(x_main, x_tail, gn, wa1, wa2, ba)


def _proj_kernel(xn_ref, w_ref, o_ref, wb_scr):
    @pl.when(pl.program_id(1) == 0)
    def _():
        wb_scr[...] = w_ref[0].astype(bf16)

    o_ref[...] = _dot_nt(xn_ref[...], wb_scr[...]).astype(o_ref.dtype)


def _proj_call(xn, wt, n_cols):
    n = xn.shape[0]
    return pl.pallas_call(
        _proj_kernel,
        grid=(n_cols // PROJ_NT, n // PROJ_MT),
        in_specs=[pl.BlockSpec((PROJ_MT, D_MODEL), lambda j, i: (i, 0)),
                  pl.BlockSpec((1, PROJ_NT, D_MODEL), lambda j, i: (0, j, 0))],
        out_specs=pl.BlockSpec((PROJ_MT, PROJ_NT), lambda j, i: (i, j)),
        out_shape=jax.ShapeDtypeStruct((n, n_cols), bf16),
        scratch_shapes=[pltpu.VMEM((PROJ_NT, D_MODEL), bf16)],
        compiler_params=_cp(("arbitrary", "arbitrary")),
        name="gla_proj",
    )(xn, wt)


def _route(xn, wr, br, cnt_ref):
    R = xn.shape[0]
    xh = xn.astype(bf16)
    xl = (xn - xh.astype(f32)).astype(bf16)
    wh = wr.astype(bf16)
    wl = (wr - wh.astype(f32)).astype(bf16)
    logits = _dot(xh, wh) + _dot(xl, wh) + _dot(xh, wl) + br
    lane_i = lax.broadcasted_iota(jnp.int32, (R, LANES), 1)
    lane = lane_i.astype(f32)
    neg = -jnp.inf
    big = float(LANES)
    is_g = lane_i < MOE_GROUPS
    lg = jnp.where(is_g, logits, neg)
    mg = jnp.max(lg, axis=1, keepdims=True)
    gidx = jnp.min(jnp.where(lg == mg, lane, big), axis=1, keepdims=True)
    ptop = 1.0 / jnp.sum(jnp.where(is_g, jnp.exp(logits - mg), 0.0), axis=1, keepdims=True)
    lo = ROUTE_E0 + MOE_EPG * gidx
    le = jnp.where((lane >= lo) & (lane < lo + MOE_EPG), logits, neg)
    v1 = jnp.max(le, axis=1, keepdims=True)
    i1 = jnp.min(jnp.where(le == v1, lane, big), axis=1, keepdims=True)
    le2 = jnp.where(lane == i1, neg, le)
    v2 = jnp.max(le2, axis=1, keepdims=True)
    i2 = jnp.min(jnp.where(le2 == v2, lane, big), axis=1, keepdims=True)
    s = jnp.exp(v2 - v1)
    w0 = ptop / (1.0 + s)
    w1 = ptop * s / (1.0 + s)
    oh = jnp.where((lane == i1) | (lane == i2), 1.0, 0.0)
    ri = lax.broadcasted_iota(jnp.int32, (R, R), 0)
    ci = lax.broadcasted_iota(jnp.int32, (R, R), 1)
    before = jnp.where(ri > ci, 1.0, 0.0).astype(bf16)
    tot = _dot(before, oh.astype(bf16)) + cnt_ref[...]
    rank0 = jnp.sum(jnp.where(lane == i1, tot, 0.0), axis=1, keepdims=True)
    rank1 = jnp.sum(jnp.where(lane == i2, tot, 0.0), axis=1, keepdims=True)
    cnt_ref[...] = cnt_ref[...] + jnp.sum(oh, axis=0, keepdims=True)
    vals = (i1 - ROUTE_E0, i2 - ROUTE_E0, rank0, rank1, w0, w1)
    slab = jnp.zeros((R, LANES), f32)
    for j, v in enumerate(vals):
        slab = jnp.where(lane_i == j, v, slab)
    sub = lax.broadcasted_iota(jnp.int32, (SUB, R), 0)
    plan = jnp.zeros((SUB, R), f32)
    for j, v in enumerate(vals[:4]):
        as_row = jnp.sum(jnp.where(ri == ci, v, 0.0), axis=0, keepdims=True)
        plan = jnp.where(sub == j, as_row, plan)
    return slab, plan


def _router_weights(w_rg, b_rg, w_re, b_re):
    w = jnp.concatenate([w_rg, jnp.moveaxis(w_re, 0, 1).reshape(D_MODEL, MOE_EXPERTS)], axis=1)
    b = jnp.concatenate([b_rg, b_re.reshape(MOE_EXPERTS)])
    pad = LANES - w.shape[1]
    return jnp.pad(w, ((0, 0), (0, pad))), jnp.pad(b, (0, pad))[None]


def _mix_out_tail(x1, gffn_ref, wr_ref, br_ref, x1_ref, xn_ref, route_ref, plan_ref, cnt_ref):
    @pl.when(pl.program_id(0) == 0)
    def _():
        cnt_ref[...] = jnp.zeros_like(cnt_ref)

    x1_ref[...] = x1
    xn = _rms(x1, gffn_ref[...])
    xn_ref[...] = xn
    route_ref[...], plan_ref[...] = _route(xn, wr_ref[...], br_ref[...], cnt_ref)


def _cast_once(w_ref, w_scr):
    @pl.when(pl.program_id(0) == 0)
    def _():
        w_scr[...] = w_ref[...].astype(bf16)


def _gla_out_kernel(og_ref, xm_ref, xt_ref, wo_ref, gffn_ref, wr_ref, br_ref, *rest):
    outs, w_scr = rest[:-1], rest[-1]
    _cast_once(wo_ref, w_scr)
    x1 = _main_or_tail(MIX_TILE, xm_ref, xt_ref) + _dot(og_ref[...], w_scr[...])
    _mix_out_tail(x1, gffn_ref, wr_ref, br_ref, *outs)


def _gelu_tanh(x):
    return x * (0.5 * (1.0 + jnp.tanh(0.7978845608028654 * (x + 0.044715 * (x * x * x)))))


def _s5_out_kernel(ys_main_ref, ys_tail_ref, x_ref, wglu_ref, gmix_ref, d_ref, bglu_ref, gffn_ref, wr_ref, br_ref,
                   *rest):
    outs, w_scr = rest[:-1], rest[-1]
    _cast_once(wglu_ref, w_scr)
    x = x_ref[...]
    u = _rms(x, gmix_ref[...])
    y = _gelu_tanh(_main_or_tail(MIX_TILE, ys_main_ref, ys_tail_ref) + d_ref[...] * u)
    z = _dot(y.astype(bf16), w_scr[...]) + bglu_ref[...]
    _mix_out_tail(x + y * _sigmoid(z), gffn_ref, wr_ref, br_ref, *outs)


def _mix_out_call(kern, name, n, row_ins, row_maps, fix_ins):
    row = lambda i: (i, 0)
    fix = lambda i: (0, 0)
    out_row = lambda w: pl.BlockSpec((MIX_TILE, w), row)
    nt = n // MIX_TILE
    return pl.pallas_call(
        kern,
        grid=(nt,),
        in_specs=[pl.BlockSpec((MIX_TILE, a.shape[1]), m) for a, m in zip(row_ins, row_maps)]
        + [pl.BlockSpec(a.shape, fix, pipeline_mode=pl.Buffered(1)) for a in fix_ins],
        out_specs=[out_row(D_MODEL), out_row(D_MODEL), out_row(LANES), pl.BlockSpec((SUB, MIX_TILE), row),
                   pl.BlockSpec((1, LANES), fix)],
        out_shape=[jax.ShapeDtypeStruct((n, D_MODEL), f32), jax.ShapeDtypeStruct((n, D_MODEL), f32),
                   jax.ShapeDtypeStruct((n, LANES), f32), jax.ShapeDtypeStruct((nt * SUB, MIX_TILE), f32),
                   jax.ShapeDtypeStruct((1, LANES), f32)],
        scratch_shapes=[pltpu.VMEM((D_MODEL, D_MODEL), bf16)],
        compiler_params=_cp(("arbitrary",)),
        name=name,
    )(*row_ins, *fix_ins)


EXPERT_TM = 256
MOVE_TILE = MIX_TILE
COMBINE_CHUNK = 32


def _moe_plan(plan, cnt, n):
    i32 = jnp.int32
    v = plan.reshape(n // MOVE_TILE, SUB, MOVE_TILE)[:, :4].astype(i32)
    counts = cnt[0, ROUTE_E0:ROUTE_E0 + MOE_EXPERTS].astype(i32)
    ends = jnp.cumsum(counts)
    off = ends - counts
    ids = jnp.arange(MOE_EXPERTS, dtype=i32)
    pos = jnp.sum(jnp.where(v[:, 0:2, :, None] == ids, off, 0), axis=-1) + v[:, 2:4]
    total = 2 * n
    n_tiles = total // EXPERT_TM
    n_items = n_tiles + MOE_EXPERTS
    inner = (counts > 0) & (off % EXPERT_TM != 0)
    keys = jnp.concatenate([jnp.arange(n_tiles, dtype=i32) * EXPERT_TM, jnp.where(inner, off, total)])
    idx = jnp.arange(n_items, dtype=i32)
    before = (keys[None, :] < keys[:, None]) | ((keys[None, :] == keys[:, None]) & (idx[None, :] < idx[:, None]))
    order = jnp.sum(before.astype(i32), axis=1)
    starts = jnp.sum(jnp.where(order[:, None] == idx[None, :], keys[:, None], 0), axis=0)
    stops = jnp.concatenate([starts[1:], jnp.full((1,), total, i32)])
    tile = starts // EXPERT_TM
    expert = jnp.sum((ends[None, :] <= starts[:, None]).astype(i32), axis=1)
    expert = jnp.minimum(expert, MOE_EXPERTS - 1)
    used = n_tiles + jnp.sum(inner.astype(i32))
    first = (expert != jnp.concatenate([jnp.full((1,), -1, i32), expert[:-1]])) & (idx < used)
    parity = (jnp.cumsum(first.astype(i32)) - 1) % 2
    later = first[None, :] & (idx[None, :] > idx[:, None])
    nxt = jnp.min(jnp.where(later, idx[None, :], n_items), axis=1)
    next_expert = jnp.sum(jnp.where(idx[None, :] == nxt[:, None], expert[None, :], 0), axis=1)
    next_expert = jnp.where(nxt < n_items, next_expert, -1)
    keep = jnp.minimum(idx, used - 1)
    items = jnp.stack([tile, expert, starts - tile * EXPERT_TM, stops - tile * EXPERT_TM,
                       first.astype(i32), parity, next_expert, jnp.zeros_like(tile)])[:, keep]
    return pos.reshape(n // MOVE_TILE, 1, 2 * MOVE_TILE), items, used.reshape(1)


IT_TILE, IT_EXPERT, IT_LO, IT_HI, IT_FIRST, IT_PARITY, IT_NEXT = range(7)


def _row_copy(src, src_row, dst, dst_row, sem):
    return pltpu.make_async_copy(src.at[pl.ds(src_row, 1)], dst.at[pl.ds(dst_row, 1)], sem)


def _source_rows_kernel(pos_ref, src_ref):
    base = pl.program_id(0) * MOVE_TILE

    def body(r, c):
        for s in range(2):
            src_ref[pos_ref[0, 0, s * MOVE_TILE + r]] = base + r
        return c

    lax.fori_loop(0, MOVE_TILE, body, 0, unroll=8)


def _source_rows(pos, n):
    src = pl.pallas_call(
        _source_rows_kernel,
        grid=(n // MOVE_TILE,),
        in_specs=[pl.BlockSpec((1, 1, 2 * MOVE_TILE), lambda i: (i, 0, 0), memory_space=pltpu.SMEM)],
        out_specs=pl.BlockSpec(memory_space=pltpu.SMEM),
        out_shape=jax.ShapeDtypeStruct((2 * n,), jnp.int32),
        compiler_params=_cp(("arbitrary",)),
        name="moe_source_rows",
    )(pos)
    return src.reshape(2 * n // EXPERT_TM, 1, EXPERT_TM)


def _experts_kernel(items_ref, n_ref, layer_ref, src_ref, src_next_ref, x_hbm, wg_hbm, wu_hbm, wd_hbm, os_ref,
                    x_buf, wg_buf, wu_buf, wd_buf, wg_scr, wu_scr, wd_scr, xsem, sem):
    i = pl.program_id(0)
    n_tiles = x_hbm.shape[0] * 2 // EXPERT_TM

    def gather(s_ref, half, op):
        def body(r, c):
            op(_row_copy(x_hbm, s_ref[0, 0, r], x_buf.at[half], r, xsem.at[half]))
            return c

        lax.fori_loop(0, EXPERT_TM, body, 0, unroll=8)

    def weight_copies(expert, half):
        e = layer_ref[0] * MOE_EXPERTS + expert
        return [pltpu.make_async_copy(src.at[e], dst.at[half], sem.at[half])
                for src, dst in ((wg_hbm, wg_buf), (wu_hbm, wu_buf), (wd_hbm, wd_buf))]

    @pl.when(i < n_ref[0])
    def _():
        @pl.when(items_ref[IT_FIRST, i] == 1)
        def _():
            half = items_ref[IT_PARITY, i]
            own = weight_copies(items_ref[IT_EXPERT, i], half)

            @pl.when(i == 0)
            def _():
                for cp in own:
                    cp.start()

            for cp in own:
                cp.wait()
            wg_scr[...] = wg_buf[half].astype(bf16)
            wu_scr[...] = wu_buf[half].astype(bf16)
            wd_scr[...] = wd_buf[half].astype(bf16)
            nxt = items_ref[IT_NEXT, i]

            @pl.when(nxt >= 0)
            def _():
                for cp in weight_copies(nxt, 1 - half):
                    cp.start()

        tile = items_ref[IT_TILE, i]
        xhalf = tile % 2

        lo = items_ref[IT_LO, i]

        @pl.when(lo == 0)
        def _():
            @pl.when(i == 0)
            def _():
                gather(src_ref, 0, lambda cp: cp.start())

            gather(src_ref, xhalf, lambda cp: cp.wait())

        x = x_buf[xhalf].astype(bf16)
        hg = _dot(x, wg_scr[...])
        hu = _dot(x, wu_scr[...])
        out = _dot((hg * _sigmoid(hg) * hu).astype(bf16), wd_scr[...])
        prefetch = (lo == 0) & (tile + 1 < n_tiles)
        for r in range(EXPERT_TM):
            @pl.when(prefetch)
            def _():
                _row_copy(x_hbm, src_next_ref[0, 0, r], x_buf.at[1 - xhalf], r, xsem.at[1 - xhalf]).start()
        row = lax.broadcasted_iota(jnp.int32, (EXPERT_TM, 1), 0)
        mine = (row >= lo) & (row < items_ref[IT_HI, i])

        @pl.when(lo == 0)
        def _():
            os_ref[...] = jnp.where(mine, out, 0.0)

        @pl.when(lo != 0)
        def _():
            os_ref[...] = jnp.where(mine, out, os_ref[...])


def _experts_call(xn, src, items, n_items, layer, wg, wu, wd):
    n_tiles = src.shape[0]
    rows = lambda i, items, n, layer: (items[IT_TILE, i], 0)
    src_spec = lambda m: pl.BlockSpec((1, 1, EXPERT_TM), m, memory_space=pltpu.SMEM)
    hbm = pl.BlockSpec(memory_space=pl.ANY)
    return pl.pallas_call(
        _experts_kernel,
        grid_spec=pltpu.PrefetchScalarGridSpec(
            num_scalar_prefetch=3,
            grid=(items.shape[1],),
            in_specs=[src_spec(lambda i, items, n, layer: (items[IT_TILE, i], 0, 0)),
                      src_spec(lambda i, items, n, layer: (jnp.minimum(items[IT_TILE, i] + 1, n_tiles - 1), 0, 0)),
                      hbm, hbm, hbm, hbm],
            out_specs=pl.BlockSpec((EXPERT_TM, D_MODEL), rows),
            scratch_shapes=[pltpu.VMEM((2, EXPERT_TM, D_MODEL), f32),
                            pltpu.VMEM((2, D_MODEL, D_EXPERT), f32), pltpu.VMEM((2, D_MODEL, D_EXPERT), f32),
                            pltpu.VMEM((2, D_EXPERT, D_MODEL), f32),
                            pltpu.VMEM((D_MODEL, D_EXPERT), bf16), pltpu.VMEM((D_MODEL, D_EXPERT), bf16),
                            pltpu.VMEM((D_EXPERT, D_MODEL), bf16),
                            pltpu.SemaphoreType.DMA((2,)), pltpu.SemaphoreType.DMA((2,))],
        ),
        out_shape=jax.ShapeDtypeStruct((2 * xn.shape[0], D_MODEL), f32),
        compiler_params=_cp(("arbitrary",)),
        name="moe_experts",
    )(items, n_items, layer, src, src, xn, wg, wu, wd)


def _combine_kernel(pos_ref, pos_next_ref, x_ref, route_ref, gn_ref, os_ref, *rest, emit_x, split):
    outs, (buf, sem, xn_scr) = rest[:-3], rest[-3:]
    i = pl.program_id(0)
    half = i % 2
    last = i == pl.num_programs(0) - 1

    def copy(p_ref, hf, r, s):
        return _row_copy(os_ref, p_ref[0, 0, s * MOVE_TILE + r], buf.at[hf, s], r, sem.at[hf])

    def loop_all(p_ref, hf, op):
        def body(r, c):
            for s in range(2):
                op(copy(p_ref, hf, r, s), s)
            return c

        lax.fori_loop(0, MOVE_TILE, body, 0, unroll=8)

    begin = lambda cp, s: cp.start(priority=s)
    finish = lambda cp, s: cp.wait()

    @pl.when(i == 0)
    def _():
        loop_all(pos_ref, 0, begin)

    loop_all(pos_ref, half, finish)
    xn_ref = xn_scr if split else outs[-1]
    for c in range(MOVE_TILE // COMBINE_CHUNK):
        rows = slice(c * COMBINE_CHUNK, (c + 1) * COMBINE_CHUNK)
        route = route_ref[rows, :]
        x2 = x_ref[rows, :] + route[:, 4:5] * buf[half, 0, rows, :] + route[:, 5:6] * buf[half, 1, rows, :]
        if emit_x:
            outs[0][rows, :] = x2
        xn_ref[rows, :] = _rms(x2, gn_ref[...]).astype(xn_ref.dtype)
        for r in range(rows.start, rows.stop):
            for s in range(2):
                begin(copy(pos_next_ref, 1 - half, r, s), s)

    @pl.when(last)
    def _():
        loop_all(pos_next_ref, 1 - half, finish)

    if split:
        main_ref, tail_ref = outs[-2:]
        is_main = i < N_MAIN // MOVE_TILE

        @pl.when(is_main)
        def _():
            main_ref[...] = xn_scr[...]

        @pl.when(jnp.logical_not(is_main))
        def _():
            tail_ref[...] = xn_scr[...]


def _combine_call(x1, route, pos, os_rows, gn, xn_dtype, *, emit_x, split):
    n = x1.shape[0]
    row = lambda i: (i, 0)
    blk = lambda m: pl.BlockSpec((MOVE_TILE, D_MODEL), m)
    nm = N_MAIN // MOVE_TILE
    out_specs, out_shape = [], []
    if emit_x:
        out_specs.append(blk(row))
        out_shape.append(jax.ShapeDtypeStruct((n, D_MODEL), f32))
    if split:
        out_specs += [blk(lambda i: (jnp.minimum(i, nm - 1), 0)), blk(lambda i: (jnp.maximum(i - nm, 0), 0))]
        out_shape += [jax.ShapeDtypeStruct((N_MAIN, D_MODEL), xn_dtype),
                      jax.ShapeDtypeStruct((n - N_MAIN, D_MODEL), xn_dtype)]
    else:
        out_specs.append(blk(row))
        out_shape.append(jax.ShapeDtypeStruct((n, D_MODEL), xn_dtype))
    nt = n // MOVE_TILE
    pos_spec = lambda m: pl.BlockSpec((1, 1, 2 * MOVE_TILE), m, memory_space=pltpu.SMEM)
    return pl.pallas_call(
        functools.partial(_combine_kernel, emit_x=emit_x, split=split),
        grid=(nt,),
        in_specs=[pos_spec(lambda i: (i, 0, 0)), pos_spec(lambda i: (jnp.minimum(i + 1, nt - 1), 0, 0)),
                  blk(row), pl.BlockSpec((MOVE_TILE, LANES), row),
                  pl.BlockSpec((1, D_MODEL), lambda i: (0, 0)), pl.BlockSpec(memory_space=pl.ANY)],
        out_specs=out_specs,
        out_shape=out_shape,
        scratch_shapes=[pltpu.VMEM((2, 2, MOVE_TILE, D_MODEL), f32), pltpu.SemaphoreType.DMA((2,)),
                        pltpu.VMEM((MOVE_TILE, D_MODEL), xn_dtype)],
        compiler_params=_cp(("arbitrary",)),
        name="moe_combine",
    )(pos, pos, x1, route, gn, os_rows)


S5_GROUP = 16
S5_GROUPS = D_MODEL // S5_GROUP
S5_STATE = 64
S5_CB = 128
S5_NB = D_MODEL // S5_CB
S5_GPB = S5_CB // S5_GROUP
S5_SB = S5_GPB * S5_STATE
S5_BC = S5_STATE * S5_GROUP
S5_CPS = 2


def _dot_exact01(x, sel):
    hi, mid, lo = _split3(x)
    return _dot(hi, sel) + _dot(mid, sel) + _dot(lo, sel)


def _s5_prep_kernel(lr_ref, li_ref, ldt_ref, bre_ref, bim_ref, cre_ref, cim_ref, bb2t_ref, cct_ref, abv_ref):
    lr, li = lr_ref[...], li_ref[...]
    dt = jnp.exp(ldt_ref[...])
    mag = jnp.exp(lr * dt)
    ang = li * dt
    ab_re, ab_im = mag * jnp.cos(ang), mag * jnp.sin(ang)
    nr, ni = ab_re - 1.0, ab_im
    den = lr * lr + li * li
    f_re = (nr * lr + ni * li) / den
    f_im = (ni * lr - nr * li) / den
    ab2_re = ab_re * ab_re - ab_im * ab_im
    ab2_im = 2.0 * (ab_re * ab_im)

    def iota(shape, axis):
        return lax.broadcasted_iota(jnp.int32, shape, axis)

    def as_col(v):
        rep = jnp.concatenate([jnp.broadcast_to(v[g:g + 1, :], (S5_STATE, S5_STATE)) for g in range(S5_GPB)], axis=0)
        pick = iota((S5_SB, S5_STATE), 1) == iota((S5_SB, S5_STATE), 0) % S5_STATE
        return jnp.sum(jnp.where(pick, rep, 0.0), axis=1, keepdims=True)

    tile_c = jnp.where(iota((S5_GROUP, S5_CB), 1) % S5_GROUP == iota((S5_GROUP, S5_CB), 0), 1.0, 0.0).astype(bf16)
    tile_p = jnp.where(iota((S5_STATE, S5_SB), 1) % S5_STATE == iota((S5_STATE, S5_SB), 0), 1.0, 0.0).astype(bf16)
    own_c = iota((S5_SB, S5_CB), 1) // S5_GROUP == iota((S5_SB, S5_CB), 0) // S5_STATE
    own_p = iota((S5_CB, S5_SB), 1) // S5_STATE == iota((S5_CB, S5_SB), 0) // S5_GROUP

    fr, fi, ar, ai = as_col(f_re), as_col(f_im), as_col(ab_re), as_col(ab_im)
    br, bi = bre_ref[...], bim_ref[...]
    bb_re = fr * br - fi * bi
    bb_im = fr * bi + fi * br
    abb_re = ar * bb_re - ai * bb_im
    abb_im = ar * bb_im + ai * bb_re
    spread_c = lambda m: jnp.where(own_c, _dot_exact01(m, tile_c), 0.0)
    spread_p = lambda m: jnp.where(own_p, _dot_exact01(m, tile_p), 0.0)
    bb2t_ref[0] = jnp.concatenate([
        jnp.concatenate([spread_c(bb_re), spread_c(abb_re)], axis=1),
        jnp.concatenate([spread_c(bb_im), spread_c(abb_im)], axis=1)], axis=0).astype(bf16)
    cct_ref[0] = jnp.concatenate([spread_p(cre_ref[...]), -spread_p(cim_ref[...])], axis=1).astype(bf16)
    grp = iota((S5_GPB, S5_SB), 1) // S5_STATE == iota((S5_GPB, S5_SB), 0)
    as_row = lambda v: jnp.sum(jnp.where(grp, _dot_exact01(v, tile_p), 0.0), axis=0, keepdims=True)
    rows = [as_row(v) for v in (ab_re, ab_im, ab2_re, ab2_im)]
    abv_ref[0] = jnp.concatenate(rows + [jnp.zeros((SUB - len(rows), S5_SB), f32)], axis=0)


def _s5_weights(lam_re, lam_im, log_dt, b_re, b_im, c_re, c_im):
    blk = lambda r, c: pl.BlockSpec((r, c), lambda j: (j, 0))
    out = lambda r, c: pl.BlockSpec((1, r, c), lambda j: (j, 0, 0))
    return pl.pallas_call(
        _s5_prep_kernel,
        grid=(S5_NB,),
        in_specs=[blk(S5_GPB, S5_STATE), blk(S5_GPB, S5_STATE), blk(S5_GPB, 1),
                  blk(S5_SB, S5_GROUP), blk(S5_SB, S5_GROUP), blk(S5_CB, S5_STATE), blk(S5_CB, S5_STATE)],
        out_specs=[out(2 * S5_SB, 2 * S5_CB), out(S5_CB, 2 * S5_SB), out(SUB, S5_SB)],
        out_shape=[jax.ShapeDtypeStruct((S5_NB, 2 * S5_SB, 2 * S5_CB), bf16),
                   jax.ShapeDtypeStruct((S5_NB, S5_CB, 2 * S5_SB), bf16),
                   jax.ShapeDtypeStruct((S5_NB, SUB, S5_SB), f32)],
        compiler_params=_cp(("arbitrary",)),
        name="s5_prep",
    )(lam_re, lam_im, log_dt[:, None], b_re.reshape(-1, S5_GROUP), b_im.reshape(-1, S5_GROUP),
      c_re.reshape(-1, S5_STATE), c_im.reshape(-1, S5_STATE))


def _s5_seq_kernel(x0_ref, x1_ref, x2_ref, x3_ref, halo0_ref, h0_ref, bb2_ref, cc_ref, abv_ref, y_ref, hout_ref,
                   xf_scr, xp_scr, bu_scr, hs_scr, yp_scr, yn_scr, h_scr, halo_scr):
    tb = pl.program_id(1)
    TL = x0_ref.shape[0]
    KB = TL // 2
    RB = BATCH * TL

    @pl.when(tb == 0)
    def _():
        h_scr[...] = h0_ref[...]
        halo_scr[...] = halo0_ref[...].astype(f32)

    chans = [slice(c * S5_CB, (c + 1) * S5_CB) for c in range(S5_CPS)]
    for c, ch in enumerate(chans):
        for b, xb_ref in enumerate((x0_ref, x1_ref, x2_ref, x3_ref)):
            xf_scr[c, b * TL:(b + 1) * TL, :] = xb_ref[:, ch].astype(f32)
        for b in range(BATCH):
            for p in range(2):
                xp_scr[c, pl.ds(2 * b + p, KB, stride=SUB), :] = xf_scr[c, pl.ds(b * TL + p, KB, stride=2), :]
    x = jnp.concatenate([xp_scr[c] for c in range(S5_CPS)], axis=1)
    xc = jnp.concatenate([halo_scr[...], x], axis=0)
    odd = (lax.broadcasted_iota(jnp.int32, (RB + SUB, 1), 0) & 1) == 1
    xprev = jnp.where(odd, pltpu.roll(xc, 1, axis=0), pltpu.roll(xc, SUB - 1, axis=0))[SUB:]
    halo_scr[...] = x[RB - SUB:]
    for c, ch in enumerate(chans):
        lhs = jnp.concatenate([x[:, ch], xprev[:, ch]], axis=1).astype(bf16)
        bu_scr[c] = _dot_nt(lhs, bb2_ref[c])
    a2 = [(abv_ref[c, 2:3, :], abv_ref[c, 3:4, :]) for c in range(S5_CPS)]

    def step(k, carry):
        r0 = pl.multiple_of(k * SUB, SUB)
        out = []
        for c in range(S5_CPS):
            hr, hi = carry[2 * c], carry[2 * c + 1]
            a2r, a2i = a2[c]
            bu = bu_scr[c, pl.ds(r0, SUB), :]
            nr = a2r * hr - a2i * hi + bu[:, :S5_SB]
            ni = a2r * hi + a2i * hr + bu[:, S5_SB:]
            hs_scr[c, pl.ds(r0, SUB), :S5_SB] = nr
            hs_scr[c, pl.ds(r0, SUB), S5_SB:] = ni
            out += [nr, ni]
        return tuple(out)

    init = []
    for c in range(S5_CPS):
        init += [h_scr[c, :, :S5_SB], h_scr[c, :, S5_SB:]]
    fin = lax.fori_loop(0, RB // SUB, step, tuple(init), unroll=4)
    for c, ch in enumerate(chans):
        h_scr[c, :, :S5_SB] = fin[2 * c]
        h_scr[c, :, S5_SB:] = fin[2 * c + 1]
        yp_scr[c] = _dot_nt(hs_scr[c].astype(bf16), cc_ref[c])
        for b in range(BATCH):
            for p in range(2):
                yn_scr[c, pl.ds(b * TL + p, KB, stride=2), :] = yp_scr[c, pl.ds(2 * b + p, KB, stride=SUB), :]
            y_ref[b, :, ch] = yn_scr[c, b * TL:(b + 1) * TL, :]

    @pl.when(tb == pl.num_programs(1) - 1)
    def _():
        hout_ref[...] = h_scr[...]


def _s5_seq_call(x, halo0, h0, bb2, cc, abv, *, row0, seq_len, tl):
    wsel = lambda j, t: (j, 0, 0)
    rb = BATCH * tl
    cw = S5_CPS * S5_CB
    xspec = lambda b: pl.BlockSpec((tl, cw), lambda j, t: ((row0 + b * seq_len) // tl + t, j))
    return pl.pallas_call(
        _s5_seq_kernel,
        grid=(S5_NB // S5_CPS, seq_len // tl),
        in_specs=[xspec(b) for b in range(BATCH)]
        + [pl.BlockSpec((SUB, cw), lambda j, t: (0, j)),
           pl.BlockSpec((S5_CPS, SUB, 2 * S5_SB), wsel),
           pl.BlockSpec((S5_CPS, 2 * S5_SB, 2 * S5_CB), wsel),
           pl.BlockSpec((S5_CPS, S5_CB, 2 * S5_SB), wsel),
           pl.BlockSpec((S5_CPS, SUB, S5_SB), wsel)],
        out_specs=[pl.BlockSpec((BATCH, tl, cw), lambda j, t: (0, t, j)),
                   pl.BlockSpec((S5_CPS, SUB, 2 * S5_SB), wsel)],
        out_shape=[jax.ShapeDtypeStruct((BATCH, seq_len, D_MODEL), f32),
                   jax.ShapeDtypeStruct((S5_NB, SUB, 2 * S5_SB), f32)],
        scratch_shapes=[pltpu.VMEM((S5_CPS, rb, S5_CB), f32), pltpu.VMEM((S5_CPS, rb, S5_CB), f32),
                        pltpu.VMEM((S5_CPS, rb, 2 * S5_SB), f32), pltpu.VMEM((S5_CPS, rb, 2 * S5_SB), f32),
                        pltpu.VMEM((S5_CPS, rb, S5_CB), f32), pltpu.VMEM((S5_CPS, rb, S5_CB), f32),
                        pltpu.VMEM((S5_CPS, SUB, 2 * S5_SB), f32), pltpu.VMEM((SUB, cw), f32)],
        compiler_params=_cp(("arbitrary", "arbitrary")),
        name=f"s5_seq_{seq_len}",
    )(x, x, x, x, halo0, h0, bb2, cc, abv)


def _s5_sample_kernel(x_ref, hre_ref, him_ref, bb2_ref, cc_ref, abv_ref, y_ref, ore_ref, oim_ref, hs_scr):
    nb = hre_ref.shape[0]
    bu = _dot_nt(x_ref[...], bb2_ref[0, :, :S5_CB])
    ar = abv_ref[0, 0:1, :]
    ai = abv_ref[0, 1:2, :]
    hr, hi = hre_ref[...], him_ref[...]
    for t in range(DEC_SEQ):
        rows = slice(t * nb, (t + 1) * nb)
        hr, hi = (ar * hr - ai * hi + bu[rows, :S5_SB], ar * hi + ai * hr + bu[rows, S5_SB:])
        hs_scr[rows, :S5_SB] = hr
        hs_scr[rows, S5_SB:] = hi
    y_ref[...] = _dot_nt(hs_scr[...].astype(bf16), cc_ref[0])
    ore_ref[...] = hr
    oim_ref[...] = hi


def _s5_sample_call(xt, h_re, h_im, bb2, cc, abv):
    n = xt.shape[0]
    nb = h_re.shape[0]
    wsel = lambda j: (j, 0, 0)
    st = pl.BlockSpec((nb, S5_SB), lambda j: (0, j))
    return pl.pallas_call(
        _s5_sample_kernel,
        grid=(S5_NB,),
        in_specs=[pl.BlockSpec((n, S5_CB), lambda j: (0, j)), st, st,
                  pl.BlockSpec((1, 2 * S5_SB, 2 * S5_CB), wsel),
                  pl.BlockSpec((1, S5_CB, 2 * S5_SB), wsel),
                  pl.BlockSpec((1, SUB, S5_SB), wsel)],
        out_specs=[pl.BlockSpec((n, S5_CB), lambda j: (0, j)), st, st],
        out_shape=[jax.ShapeDtypeStruct((n, D_MODEL), f32),
                   jax.ShapeDtypeStruct(h_re.shape, f32), jax.ShapeDtypeStruct(h_im.shape, f32)],
        scratch_shapes=[pltpu.VMEM((n, 2 * S5_SB), f32)],
        compiler_params=_cp(("arbitrary",)),
        name="s5_sample",
    )(xt, h_re, h_im, bb2, cc, abv)


GLA_CHUNK = 256
GLA_DIRECT = 2
S5_TL = 256


def _moe_layer(layer, x1, xnf, route, plan, cnt, wg, wu, wd, gn, xn_dtype, *, emit_x, split):
    pos, items, n_items = _moe_plan(plan, cnt, x1.shape[0])
    src = _source_rows(pos, x1.shape[0])
    os_rows = _experts_call(xnf, src, items, n_items, jnp.full((1,), layer, jnp.int32), wg, wu, wd)
    return _combine_call(x1, route, pos, os_rows, gn, xn_dtype, emit_x=emit_x, split=split)


def kernel(x_prompt, x_sample, state_gla, state_s5_re, state_s5_im, meta_tokens, norm_mix_g, norm_ffn_g, norm_final_g, gla_w_in, gla_w_a2, gla_b_a, gla_g_o, gla_w_o, s5_lambda_re, s5_lambda_im, s5_log_dt, s5_b_re, s5_b_im, s5_c_re, s5_c_im, s5_d, s5_w_glu, s5_b_glu, moe_w_rg, moe_b_rg, moe_w_re, moe_b_re, moe_w_gate, moe_w_up, moe_w_down):
    row = lambda v: v.reshape(1, -1)
    x_main = x_prompt.reshape(N_MAIN, D_MODEL)
    x_tail = jnp.concatenate([
        jnp.tile(meta_tokens.astype(x_prompt.dtype), (BATCH, 1)),
        x_sample.reshape(N_SAMPLE, D_MODEL),
        jnp.zeros((N_ROWS - N_REAL, D_MODEL), x_prompt.dtype)], axis=0)
    wg = moe_w_gate.reshape(-1, D_MODEL, D_EXPERT)
    wu = moe_w_up.reshape(-1, D_MODEL, D_EXPERT)
    wd = moe_w_down.reshape(-1, D_EXPERT, D_MODEL)

    w_in = jnp.swapaxes(gla_w_in, 1, 2)
    wa1 = jnp.pad(gla_w_in[0, :, GLA_QKVR:], ((0, 0), (0, LANES - GLA_RANK)))
    wa2 = jnp.pad(gla_w_a2.reshape(GLA_RANK, GLA_KEY), ((0, LANES - GLA_RANK), (0, 0)))
    xn, glog = _norm_gate_call(x_main, x_tail, row(norm_mix_g[0]), wa1, wa2, row(gla_b_a))
    proj = _proj_call(xn, w_in, GLA_QKVR)
    go = row(gla_g_o)
    og = jnp.zeros((N_ROWS, GLA_VAL), bf16)
    s_zero = jnp.zeros((BATCH, GLA_HEADS, GLA_DK, GLA_DV), f32)
    og, s_meta = _gla_seq_call(proj, glog, go, s_zero, og, row0=ROW_META, C=N_META, n_chunks=1, d=GLA_DIRECT)
    og, s_prompt = _gla_seq_call(proj, glog, go, s_meta, og, row0=0, C=GLA_CHUNK, n_chunks=SEQ // GLA_CHUNK,
                                 d=GLA_DIRECT)
    og, s_sample = _gla_sample_call(proj, glog, go, state_gla.reshape(DEC_BATCH, GLA_HEADS, GLA_DK, GLA_DV), og,
                                    row0=ROW_SAMPLE)
    wr, br = _router_weights(moe_w_rg[0], moe_b_rg[0], moe_w_re[0], moe_b_re[0])
    tile_row = lambda i: (i, 0)
    x1, xnf, route, plan, cnt = _mix_out_call(
        _gla_out_kernel, "gla_out", N_ROWS, [og, x_main, x_tail], [tile_row, *_main_tail_maps(MIX_TILE)],
        [gla_w_o.reshape(GLA_VAL, D_MODEL), row(norm_ffn_g[0]), wr, br])
    x2, xn2 = _moe_layer(0, x1, xnf, route, plan, cnt, wg, wu, wd, row(norm_mix_g[1]), bf16, emit_x=True, split=False)

    bb2, cc, abv = _s5_weights(s5_lambda_re[0], s5_lambda_im[0], s5_log_dt[0], s5_b_re[0], s5_b_im[0],
                               s5_c_re[0], s5_c_im[0])
    y_meta, h_meta = _s5_seq_call(xn2, jnp.zeros((SUB, D_MODEL), bf16), jnp.zeros((S5_NB, SUB, 2 * S5_SB), f32),
                                  bb2, cc, abv, row0=ROW_META, seq_len=N_META, tl=N_META)
    halo = xn2[ROW_META:ROW_SAMPLE].reshape(BATCH, N_META, D_MODEL)[:, N_META - 2:].reshape(SUB, D_MODEL)
    y_main, h_main = _s5_seq_call(xn2, halo, h_meta, bb2, cc, abv, row0=0, seq_len=SEQ, tl=S5_TL)
    xt_sample = xn2[ROW_SAMPLE:N_REAL].reshape(DEC_BATCH, DEC_SEQ, D_MODEL).transpose(1, 0, 2).reshape(N_SAMPLE, D_MODEL)
    y_samp, s5r_s, s5i_s = _s5_sample_call(
        xt_sample, state_s5_re.reshape(DEC_BATCH, S5_GROUPS * S5_STATE),
        state_s5_im.reshape(DEC_BATCH, S5_GROUPS * S5_STATE), bb2, cc, abv)
    ys_tail = jnp.concatenate([
        y_meta.reshape(N_METAROWS, D_MODEL),
        y_samp.reshape(DEC_SEQ, DEC_BATCH, D_MODEL).transpose(1, 0, 2).reshape(N_SAMPLE, D_MODEL),
        jnp.zeros((N_ROWS - N_REAL, D_MODEL), f32)], axis=0)
    wr, br = _router_weights(moe_w_rg[1], moe_b_rg[1], moe_w_re[1], moe_b_re[1])
    x3, xnf, route, plan, cnt = _mix_out_call(
        _s5_out_kernel, "s5_out", N_ROWS, [y_main.reshape(N_MAIN, D_MODEL), ys_tail, x2],
        [*_main_tail_maps(MIX_TILE), tile_row],
        [s5_w_glu.reshape(D_MODEL, D_MODEL), row(norm_mix_g[1]), row(s5_d), row(s5_b_glu),
         row(norm_ffn_g[1]), wr, br])
    y_main_out, y_tail_out = _moe_layer(1, x3, xnf, route, plan, cnt, wg, wu, wd, row(norm_final_g), f32,
                                        emit_x=False, split=True)

    y_prompt = y_main_out.reshape(BATCH, SEQ, D_MODEL)
    y_sample = y_tail_out[N_METAROWS:N_METAROWS + N_SAMPLE].reshape(DEC_BATCH, DEC_SEQ, D_MODEL)
    hfin = h_main.reshape(S5_NB, BATCH, 2, 2, S5_GPB, S5_STATE)[:, :, 1]
    s5r_p = hfin[:, :, 0].transpose(1, 0, 2, 3).reshape(1, BATCH, S5_GROUPS, S5_STATE)
    s5i_p = hfin[:, :, 1].transpose(1, 0, 2, 3).reshape(1, BATCH, S5_GROUPS, S5_STATE)
    return (y_prompt, y_sample, s_prompt[None], s5r_p, s5i_p, s_sample[None],
            s5r_s.reshape(1, DEC_BATCH, S5_GROUPS, S5_STATE), s5i_s.reshape(1, DEC_BATCH, S5_GROUPS, S5_STATE))
```

```python
import functools

import jax
import jax.numpy as jnp
from jax import lax
from jax.experimental import pallas as pl
from jax.experimental.pallas import tpu as pltpu

f32 = jnp.float32
bf16 = jnp.bfloat16

D_MODEL = 2048
BATCH = 4
SEQ = 2048
DEC_BATCH = 128
DEC_SEQ = 4
N_META = 16
EPS = 1e-6
GLA_HEADS = 4
GLA_DK = 256
GLA_DV = 512
GLA_KEY = GLA_HEADS * GLA_DK
GLA_VAL = GLA_HEADS * GLA_DV
GLA_RANK = 16
GLA_TAU = 16.0
GLA_QKVR = 2 * GLA_KEY + 2 * GLA_VAL

N_MAIN = BATCH * SEQ
N_METAROWS = BATCH * N_META
N_SAMPLE = DEC_BATCH * DEC_SEQ
ROW_META = N_MAIN
ROW_SAMPLE = N_MAIN + N_METAROWS
N_REAL = ROW_SAMPLE + N_SAMPLE
ROW_TILE = 256
N_ROWS = -(-N_REAL // ROW_TILE) * ROW_TILE

VMEM_LIMIT = 56 * 1024 * 1024


def _cp(sem, vmem=VMEM_LIMIT):
    return pltpu.CompilerParams(dimension_semantics=sem, vmem_limit_bytes=vmem)


def _dot(a, b):
    return jnp.dot(a, b, preferred_element_type=f32)


def _dot_nt(a, b):
    return lax.dot_general(a, b, (((1,), (1,)), ((), ())), preferred_element_type=f32)


def _dot_tn(a, b):
    return lax.dot_general(a, b, (((0,), (0,)), ((), ())), preferred_element_type=f32)


def _sigmoid(x):
    return 1.0 / (1.0 + jnp.exp(-x))


def _split3(x):
    hi = x.astype(bf16)
    r1 = x - hi.astype(f32)
    mid = r1.astype(bf16)
    lo = (r1 - mid.astype(f32)).astype(bf16)
    return hi, mid, lo


def _cumsum_rows(g, C):
    if C <= 16:
        row = lax.broadcasted_iota(jnp.int32, (C, 1), 0)
        b = jnp.zeros_like(g)
        for s in range(C):
            b = b + jnp.where(row >= s, g[s:s + 1, :], 0.0)
        return b
    row = lax.broadcasted_iota(jnp.int32, (C, C), 0)
    col = lax.broadcasted_iota(jnp.int32, (C, C), 1)
    tri = jnp.where(row >= col, 1.0, 0.0).astype(bf16)
    hi, mid, lo = _split3(g)
    return _dot(tri, hi) + _dot(tri, mid) + _dot(tri, lo)


PAIR_LEVEL = 1000


def _gla_pair_code(C, d, tree=True):
    ti = lax.broadcasted_iota(jnp.int32, (C, C), 0)
    si = lax.broadcasted_iota(jnp.int32, (C, C), 1)
    code = jnp.where((ti // d == si // d) & (si <= ti), 1 + ti - si, 0)
    h = d
    while tree and h < C:
        tb = ti // h
        code = jnp.where(((tb % 2) == 1) & ((si // h) == tb - 1), PAIR_LEVEL + h, code)
        h *= 2
    return code


def _gla_scores(q, k, b, code, C, d, tree=True):
    row = lax.broadcasted_iota(jnp.int32, (C, 1), 0)
    scores = jnp.zeros((C, C), f32)
    for dl in range(d):
        ks = k if dl == 0 else pltpu.roll(k, dl, axis=0)
        bs = b if dl == 0 else pltpu.roll(b, dl, axis=0)
        term = q * ks * jnp.exp(jnp.minimum(b - bs, 0.0))
        colv = jnp.sum(term, axis=1, keepdims=True)
        scores = jnp.where(code == 1 + dl, colv, scores)
    z = b
    s = 1
    while tree and 2 * s < C:
        z = jnp.where((row & s) != 0, pltpu.roll(z, s, axis=0), z)
        s *= 2
        h = s
        if h < d:
            continue
        bnext = pltpu.roll(z, C - h, axis=0)
        qh = (q * jnp.exp(b - z)).astype(bf16)
        kh = (k * jnp.exp(jnp.minimum(bnext - b, 0.0))).astype(bf16)
        scores = jnp.where(code == PAIR_LEVEL + h, _dot_nt(qh, kh), scores)
    return scores


def _gla_chunk(q, k, v, g, S, code, C, d, tree=True):
    b = _cumsum_rows(g, C)
    o = _dot((q * jnp.exp(b)).astype(bf16), S.astype(bf16))
    scores = _gla_scores(q, k, b, code, C, d, tree)
    o = o + _dot(scores.astype(bf16), v)
    b_last = b[C - 1:C, :]
    kd = (k * jnp.exp(b_last - b)).astype(bf16)
    if C == GLA_DK:
        eye = code == 1
    else:
        eye = (lax.broadcasted_iota(jnp.int32, (GLA_DK, GLA_DK), 0)
               == lax.broadcasted_iota(jnp.int32, (GLA_DK, GLA_DK), 1))
    dec_col = jnp.sum(jnp.where(eye, jnp.exp(b_last), 0.0), axis=1, keepdims=True)
    S_new = dec_col * S + _dot_tn(kd, v)
    return o, S_new


def _gla_head_epilogue(o, r, go):
    ms = jnp.mean(o * o, axis=1, keepdims=True)
    on = o * lax.rsqrt(ms + EPS) * go
    return on * (r * _sigmoid(r))


def _gla_seq_kernel(q_ref, k_ref, v_ref, r_ref, g_ref, go_ref, s0_ref, _og_in, og_ref, sout_ref, s_scr, *, C, d):
    c = pl.program_id(1)

    @pl.when(c == 0)
    def _():
        s_scr[...] = s0_ref[0]

    code = _gla_pair_code(C, d)

    def head(h, carry):
        ck = pl.ds(pl.multiple_of(h * GLA_DK, GLA_DK), GLA_DK)
        cv = pl.ds(pl.multiple_of(h * GLA_DV, GLA_DV), GLA_DV)
        q = q_ref[:, ck].astype(f32) * (GLA_DK ** -0.5)
        k = k_ref[:, ck].astype(f32)
        o, S_new = _gla_chunk(q, k, v_ref[:, cv], g_ref[:, ck], s_scr[h], code, C, d)
        s_scr[h] = S_new
        og_ref[:, cv] = _gla_head_epilogue(o, r_ref[:, cv].astype(f32), go_ref[:, cv]).astype(og_ref.dtype)
        return carry

    lax.fori_loop(0, GLA_HEADS, head, 0)

    @pl.when(c == pl.num_programs(1) - 1)
    def _():
        sout_ref[0] = s_scr[...]


def _gla_seq_call(proj, glog, go, s0, og_buf, *, row0, C, n_chunks, d):
    blk0 = row0 // C
    rows = lambda b, c: blk0 + b * n_chunks + c
    kern = functools.partial(_gla_seq_kernel, C=C, d=d)
    return pl.pallas_call(
        kern,
        grid=(BATCH, n_chunks),
        in_specs=[
            pl.BlockSpec((C, GLA_KEY), lambda b, c: (rows(b, c), 0)),
            pl.BlockSpec((C, GLA_KEY), lambda b, c: (rows(b, c), 1)),
            pl.BlockSpec((C, GLA_VAL), lambda b, c: (rows(b, c), 1)),
            pl.BlockSpec((C, GLA_VAL), lambda b, c: (rows(b, c), 2)),
            pl.BlockSpec((C, GLA_KEY), lambda b, c: (rows(b, c), 0)),
            pl.BlockSpec((1, GLA_VAL), lambda b, c: (0, 0)),
            pl.BlockSpec((1, GLA_HEADS, GLA_DK, GLA_DV), lambda b, c: (b, 0, 0, 0)),
            pl.BlockSpec(memory_space=pl.ANY),
        ],
        out_specs=[
            pl.BlockSpec((C, GLA_VAL), lambda b, c: (rows(b, c), 0)),
            pl.BlockSpec((1, GLA_HEADS, GLA_DK, GLA_DV), lambda b, c: (b, 0, 0, 0)),
        ],
        out_shape=[
            jax.ShapeDtypeStruct(og_buf.shape, og_buf.dtype),
            jax.ShapeDtypeStruct((BATCH, GLA_HEADS, GLA_DK, GLA_DV), f32),
        ],
        scratch_shapes=[pltpu.VMEM((GLA_HEADS, GLA_DK, GLA_DV), f32)],
        input_output_aliases={7: 0},
        compiler_params=_cp(("arbitrary", "arbitrary")),
        name=f"gla_seq_c{C}",
    )(proj, proj, proj, proj, glog, go, s0, og_buf)


SAMPLE_BB = 4
SAMPLE_C = SAMPLE_BB * DEC_SEQ


def _gla_sample_kernel(q_ref, k_ref, v_ref, r_ref, g_ref, go_ref, s0_ref, _og_in, og_ref, sout_ref):
    row = lax.broadcasted_iota(jnp.int32, (SAMPLE_C, 1), 0)

    code = _gla_pair_code(SAMPLE_C, DEC_SEQ, tree=False)

    def take(x, bb):
        sh = (SAMPLE_C - DEC_SEQ * bb) % SAMPLE_C
        return jnp.where(row < DEC_SEQ, pltpu.roll(x, sh, axis=0) if sh else x, 0.0)

    def head(h, carry):
        ck = pl.ds(pl.multiple_of(h * GLA_DK, GLA_DK), GLA_DK)
        cv = pl.ds(pl.multiple_of(h * GLA_DV, GLA_DV), GLA_DV)
        q_all = q_ref[:, ck].astype(f32) * (GLA_DK ** -0.5)
        k_all = k_ref[:, ck].astype(f32)
        v_all = v_ref[:, cv].astype(f32)
        r_all = r_ref[:, cv].astype(f32)
        g_all = g_ref[:, ck]
        go = go_ref[:, cv]
        acc = jnp.zeros((SAMPLE_C, GLA_DV), f32)
        for bb in range(SAMPLE_BB):
            o, S_new = _gla_chunk(take(q_all, bb), take(k_all, bb), take(v_all, bb).astype(bf16),
                                  take(g_all, bb), s0_ref[bb, h], code, SAMPLE_C, DEC_SEQ, tree=False)
            sout_ref[bb, h] = S_new
            y = _gla_head_epilogue(o, take(r_all, bb), go)
            acc = jnp.where(row // DEC_SEQ == bb, pltpu.roll(y, DEC_SEQ * bb, axis=0) if bb else y, acc)
        og_ref[:, cv] = acc.astype(og_ref.dtype)
        return carry

    lax.fori_loop(0, GLA_HEADS, head, 0)


def _gla_sample_call(proj, glog, go, s0, og_buf, *, row0):
    n_seq = s0.shape[0]
    blk0 = row0 // SAMPLE_C
    st_spec = pl.BlockSpec((SAMPLE_BB, GLA_HEADS, GLA_DK, GLA_DV), lambda i: (i, 0, 0, 0))
    return pl.pallas_call(
        _gla_sample_kernel,
        grid=(n_seq // SAMPLE_BB,),
        in_specs=[
            pl.BlockSpec((SAMPLE_C, GLA_KEY), lambda i: (blk0 + i, 0)),
            pl.BlockSpec((SAMPLE_C, GLA_KEY), lambda i: (blk0 + i, 1)),
            pl.BlockSpec((SAMPLE_C, GLA_VAL), lambda i: (blk0 + i, 1)),
            pl.BlockSpec((SAMPLE_C, GLA_VAL), lambda i: (blk0 + i, 2)),
            pl.BlockSpec((SAMPLE_C, GLA_KEY), lambda i: (blk0 + i, 0)),
            pl.BlockSpec((1, GLA_VAL), lambda i: (0, 0)),
            st_spec,
            pl.BlockSpec(memory_space=pl.ANY),
        ],
        out_specs=[pl.BlockSpec((SAMPLE_C, GLA_VAL), lambda i: (blk0 + i, 0)), st_spec],
        out_shape=[jax.ShapeDtypeStruct(og_buf.shape, og_buf.dtype), jax.ShapeDtypeStruct(s0.shape, f32)],
        input_output_aliases={7: 0},
        compiler_params=_cp(("arbitrary",)),
        name="gla_sample",
    )(proj, proj, proj, proj, glog, go, s0, og_buf)


LANES = 128
SUB = 8
MOE_GROUPS = 4
MOE_EPG = 8
MOE_EXPERTS = MOE_GROUPS * MOE_EPG
D_EXPERT = 256
ROUTE_E0 = MOE_GROUPS
MIX_TILE = 256
PROJ_NT = 1024
PROJ_MT = N_ROWS // 7


def _rms(x, g):
    r = lax.rsqrt(jnp.mean(x * x, axis=-1, keepdims=True) + EPS)
    return (x * r) * g


def _log_sigmoid(z):
    return jnp.minimum(z, 0.0) - jnp.log1p(jnp.exp(-jnp.abs(z)))


def _main_or_tail(tile, main_ref, tail_ref):
    return jnp.where(pl.program_id(0) < N_MAIN // tile, main_ref[...], tail_ref[...])


def _main_tail_maps(tile):
    nm = N_MAIN // tile
    return (lambda i: (jnp.minimum(i, nm - 1), 0)), (lambda i: (jnp.maximum(i - nm, 0), 0))


def _norm_gate_kernel(xm_ref, xt_ref, gn_ref, wa1_ref, wa2_ref, ba_ref, xn_ref, gl_ref):
    xnb = _rms(_main_or_tail(ROW_TILE, xm_ref, xt_ref), gn_ref[...]).astype(bf16)
    xn_ref[...] = xnb
    a = _dot(xnb, wa1_ref[...].astype(bf16))
    z = _dot(a.astype(bf16), wa2_ref[...].astype(bf16)) + ba_ref[...]
    gl_ref[...] = _log_sigmoid(z) * (1.0 / GLA_TAU)


def _norm_gate_call(x_main, x_tail, gn, wa1, wa2, ba):
    n = x_main.shape[0] + x_tail.shape[0]
    row = lambda i: (i, 0)
    fix = lambda i: (0, 0)
    main_map, tail_map = _main_tail_maps(ROW_TILE)
    return pl.pallas_call(
        _norm_gate_kernel,
        grid=(n // ROW_TILE,),
        in_specs=[pl.BlockSpec((ROW_TILE, D_MODEL), main_map), pl.BlockSpec((ROW_TILE, D_MODEL), tail_map),
                  pl.BlockSpec((1, D_MODEL), fix),
                  pl.BlockSpec((D_MODEL, LANES), fix), pl.BlockSpec((LANES, GLA_KEY), fix),
                  pl.BlockSpec((1, GLA_KEY), fix)],
        out_specs=[pl.BlockSpec((ROW_TILE, D_MODEL), row), pl.BlockSpec((ROW_TILE, GLA_KEY), row)],
        out_shape=[jax.ShapeDtypeStruct((n, D_MODEL), bf16), jax.ShapeDtypeStruct((n, GLA_KEY), f32)],
        compiler_params=_cp(("arbitrary",)),
        name="norm_gate",
    )(x_main, x_tail, gn, wa1, wa2, ba)


def _proj_kernel(xn_ref, w_ref, o_ref, wb_scr):
    @pl.when(pl.program_id(1) == 0)
    def _():
        wb_scr[...] = w_ref[0].astype(bf16)

    o_ref[...] = _dot_nt(xn_ref[...], wb_scr[...]).astype(o_ref.dtype)


def _proj_call(xn, wt, n_cols):
    n = xn.shape[0]
    return pl.pallas_call(
        _proj_kernel,
        grid=(n_cols // PROJ_NT, n // PROJ_MT),
        in_specs=[pl.BlockSpec((PROJ_MT, D_MODEL), lambda j, i: (i, 0)),
                  pl.BlockSpec((1, PROJ_NT, D_MODEL), lambda j, i: (0, j, 0))],
        out_specs=pl.BlockSpec((PROJ_MT, PROJ_NT), lambda j, i: (i, j)),
        out_shape=jax.ShapeDtypeStruct((n, n_cols), bf16),
        scratch_shapes=[pltpu.VMEM((PROJ_NT, D_MODEL), bf16)],
        compiler_params=_cp(("arbitrary", "arbitrary")),
        name="gla_proj",
    )(xn, wt)


def _route(xn, wr, br, cnt_ref):
    R = xn.shape[0]
    xh = xn.astype(bf16)
    xl = (xn - xh.astype(f32)).astype(bf16)
    wh = wr.astype(bf16)
    wl = (wr - wh.astype(f32)).astype(bf16)
    hi_terms = _dot(xh, jnp.concatenate([wh, wl], axis=1))
    logits = hi_terms[:, :LANES] + hi_terms[:, LANES:] + _dot(xl, wh) + br
    lane_i = lax.broadcasted_iota(jnp.int32, (R, LANES), 1)
    lane = lane_i.astype(f32)
    neg = -jnp.inf
    big = float(LANES)
    is_g = lane_i < MOE_GROUPS
    lg = jnp.where(is_g, logits, neg)
    mg = jnp.max(lg, axis=1, keepdims=True)
    gidx = jnp.min(jnp.where(lg == mg, lane, big), axis=1, keepdims=True)
    ptop = 1.0 / jnp.sum(jnp.where(is_g, jnp.exp(logits - mg), 0.0), axis=1, keepdims=True)
    lo = ROUTE_E0 + MOE_EPG * gidx
    le = jnp.where((lane >= lo) & (lane < lo + MOE_EPG), logits, neg)
    v1 = jnp.max(le, axis=1, keepdims=True)
    i1 = jnp.min(jnp.where(le == v1, lane, big), axis=1, keepdims=True)
    le2 = jnp.where(lane == i1, neg, le)
    v2 = jnp.max(le2, axis=1, keepdims=True)
    i2 = jnp.min(jnp.where(le2 == v2, lane, big), axis=1, keepdims=True)
    s = jnp.exp(v2 - v1)
    w0 = ptop / (1.0 + s)
    w1 = ptop * s / (1.0 + s)
    oh = jnp.where((lane == i1) | (lane == i2), 1.0, 0.0)
    ri = lax.broadcasted_iota(jnp.int32, (R, R), 0)
    ci = lax.broadcasted_iota(jnp.int32, (R, R), 1)
    before = jnp.where(ri > ci, 1.0, 0.0).astype(bf16)
    tot = _dot(before, oh.astype(bf16)) + cnt_ref[...]
    rank0 = jnp.sum(jnp.where(lane == i1, tot, 0.0), axis=1, keepdims=True)
    rank1 = jnp.sum(jnp.where(lane == i2, tot, 0.0), axis=1, keepdims=True)
    cnt_ref[...] = cnt_ref[...] + jnp.sum(oh, axis=0, keepdims=True)
    vals = (i1 - ROUTE_E0, i2 - ROUTE_E0, rank0, rank1, w0, w1)
    slab = jnp.zeros((R, LANES), f32)
    for j, v in enumerate(vals):
        slab = jnp.where(lane_i == j, v, slab)
    sub = lax.broadcasted_iota(jnp.int32, (SUB, R), 0)
    plan = jnp.zeros((SUB, R), f32)
    for j, v in enumerate(vals[:4]):
        as_row = jnp.sum(jnp.where(ri == ci, v, 0.0), axis=0, keepdims=True)
        plan = jnp.where(sub == j, as_row, plan)
    return slab, plan


def _router_weights(w_rg, b_rg, w_re, b_re):
    w = jnp.concatenate([w_rg, jnp.moveaxis(w_re, 0, 1).reshape(D_MODEL, MOE_EXPERTS)], axis=1)
    b = jnp.concatenate([b_rg, b_re.reshape(MOE_EXPERTS)])
    pad = LANES - w.shape[1]
    return jnp.pad(w, ((0, 0), (0, pad))), jnp.pad(b, (0, pad))[None]


def _mix_out_tail(x1, gffn_ref, wr_ref, br_ref, x1_ref, xn_ref, route_ref, plan_ref, cnt_ref):
    @pl.when(pl.program_id(0) == 0)
    def _():
        cnt_ref[...] = jnp.zeros_like(cnt_ref)

    x1_ref[...] = x1
    xn = _rms(x1, gffn_ref[...])
    xn_ref[...] = xn
    route_ref[...], plan_ref[...] = _route(xn, wr_ref[...], br_ref[...], cnt_ref)


def _cast_once(w_ref, w_scr):
    @pl.when(pl.program_id(0) == 0)
    def _():
        w_scr[...] = w_ref[...].astype(bf16)


def _gla_out_kernel(og_ref, xm_ref, xt_ref, wo_ref, gffn_ref, wr_ref, br_ref, *rest):
    outs, w_scr = rest[:-1], rest[-1]
    _cast_once(wo_ref, w_scr)
    x1 = _main_or_tail(MIX_TILE, xm_ref, xt_ref) + _dot(og_ref[...], w_scr[...])
    _mix_out_tail(x1, gffn_ref, wr_ref, br_ref, *outs)


def _gelu_tanh(x):
    return x * (0.5 * (1.0 + jnp.tanh(0.7978845608028654 * (x + 0.044715 * (x * x * x)))))


def _s5_out_kernel(ys_main_ref, ys_tail_ref, x_ref, wglu_ref, gmix_ref, d_ref, bglu_ref, gffn_ref, wr_ref, br_ref,
                   *rest):
    outs, w_scr = rest[:-1], rest[-1]
    _cast_once(wglu_ref, w_scr)
    x = x_ref[...]
    u = _rms(x, gmix_ref[...])
    y = _gelu_tanh(_main_or_tail(MIX_TILE, ys_main_ref, ys_tail_ref) + d_ref[...] * u)
    z = _dot(y.astype(bf16), w_scr[...]) + bglu_ref[...]
    _mix_out_tail(x + y * _sigmoid(z), gffn_ref, wr_ref, br_ref, *outs)


def _mix_out_call(kern, name, n, row_ins, row_maps, fix_ins):
    row = lambda i: (i, 0)
    fix = lambda i: (0, 0)
    out_row = lambda w: pl.BlockSpec((MIX_TILE, w), row)
    nt = n // MIX_TILE
    return pl.pallas_call(
        kern,
        grid=(nt,),
        in_specs=[pl.BlockSpec((MIX_TILE, a.shape[1]), m) for a, m in zip(row_ins, row_maps)]
        + [pl.BlockSpec(a.shape, fix, pipeline_mode=pl.Buffered(1)) for a in fix_ins],
        out_specs=[out_row(D_MODEL), out_row(D_MODEL), out_row(LANES), pl.BlockSpec((SUB, MIX_TILE), row),
                   pl.BlockSpec((1, LANES), fix)],
        out_shape=[jax.ShapeDtypeStruct((n, D_MODEL), f32), jax.ShapeDtypeStruct((n, D_MODEL), f32),
                   jax.ShapeDtypeStruct((n, LANES), f32), jax.ShapeDtypeStruct((nt * SUB, MIX_TILE), f32),
                   jax.ShapeDtypeStruct((1, LANES), f32)],
        scratch_shapes=[pltpu.VMEM((D_MODEL, D_MODEL), bf16)],
        compiler_params=_cp(("arbitrary",)),
        name=name,
    )(*row_ins, *fix_ins)


EXPERT_TM = 256
MOVE_TILE = MIX_TILE
COMBINE_CHUNK = 32


def _moe_plan(plan, cnt, n):
    i32 = jnp.int32
    v = plan.reshape(n // MOVE_TILE, SUB, MOVE_TILE)[:, :4].astype(i32)
    counts = cnt[0, ROUTE_E0:ROUTE_E0 + MOE_EXPERTS].astype(i32)
    ends = jnp.cumsum(counts)
    off = ends - counts
    ids = jnp.arange(MOE_EXPERTS, dtype=i32)
    pos = jnp.sum(jnp.where(v[:, 0:2, :, None] == ids, off, 0), axis=-1) + v[:, 2:4]
    total = 2 * n
    n_tiles = total // EXPERT_TM
    n_items = n_tiles + MOE_EXPERTS
    inner = (counts > 0) & (off % EXPERT_TM != 0)
    keys = jnp.concatenate([jnp.arange(n_tiles, dtype=i32) * EXPERT_TM, jnp.where(inner, off, total)])
    idx = jnp.arange(n_items, dtype=i32)
    before = (keys[None, :] < keys[:, None]) | ((keys[None, :] == keys[:, None]) & (idx[None, :] < idx[:, None]))
    order = jnp.sum(before.astype(i32), axis=1)
    starts = jnp.sum(jnp.where(order[:, None] == idx[None, :], keys[:, None], 0), axis=0)
    stops = jnp.concatenate([starts[1:], jnp.full((1,), total, i32)])
    tile = starts // EXPERT_TM
    expert = jnp.sum((ends[None, :] <= starts[:, None]).astype(i32), axis=1)
    expert = jnp.minimum(expert, MOE_EXPERTS - 1)
    used = n_tiles + jnp.sum(inner.astype(i32))
    first = (expert != jnp.concatenate([jnp.full((1,), -1, i32), expert[:-1]])) & (idx < used)
    parity = (jnp.cumsum(first.astype(i32)) - 1) % 2
    later = first[None, :] & (idx[None, :] > idx[:, None])
    nxt = jnp.min(jnp.where(later, idx[None, :], n_items), axis=1)
    next_expert = jnp.sum(jnp.where(idx[None, :] == nxt[:, None], expert[None, :], 0), axis=1)
    next_expert = jnp.where(nxt < n_items, next_expert, -1)
    keep = jnp.minimum(idx, used - 1)
    items = jnp.stack([tile, expert, starts - tile * EXPERT_TM, stops - tile * EXPERT_TM,
                       first.astype(i32), parity, next_expert, jnp.zeros_like(tile)])[:, keep]
    return pos.reshape(n // MOVE_TILE, 1, 2 * MOVE_TILE), items, used.reshape(1)


IT_TILE, IT_EXPERT, IT_LO, IT_HI, IT_FIRST, IT_PARITY, IT_NEXT = range(7)


def _row_copy(src, src_row, dst, dst_row, sem):
    return pltpu.make_async_copy(src.at[pl.ds(src_row, 1)], dst.at[pl.ds(dst_row, 1)], sem)


def _source_rows_kernel(pos_ref, src_ref):
    base = pl.program_id(0) * MOVE_TILE

    def body(r, c):
        for s in range(2):
            src_ref[pos_ref[0, 0, s * MOVE_TILE + r]] = base + r
        return c

    lax.fori_loop(0, MOVE_TILE, body, 0, unroll=8)


def _source_rows(pos, n):
    src = pl.pallas_call(
        _source_rows_kernel,
        grid=(n // MOVE_TILE,),
        in_specs=[pl.BlockSpec((1, 1, 2 * MOVE_TILE), lambda i: (i, 0, 0), memory_space=pltpu.SMEM)],
        out_specs=pl.BlockSpec(memory_space=pltpu.SMEM),
        out_shape=jax.ShapeDtypeStruct((2 * n,), jnp.int32),
        compiler_params=_cp(("arbitrary",)),
        name="moe_source_rows",
    )(pos)
    return src.reshape(2 * n // EXPERT_TM, 1, EXPERT_TM)


def _experts_kernel(items_ref, n_ref, layer_ref, src_ref, src_next_ref, x_hbm, wg_hbm, wu_hbm, wd_hbm, os_ref,
                    x_buf, wg_buf, wu_buf, wd_buf, wg_scr, wu_scr, wd_scr, xsem, sem):
    i = pl.program_id(0)
    n_tiles = x_hbm.shape[0] * 2 // EXPERT_TM

    def gather(s_ref, half, op):
        def body(r, c):
            op(_row_copy(x_hbm, s_ref[0, 0, r], x_buf.at[half], r, xsem.at[half]))
            return c

        lax.fori_loop(0, EXPERT_TM, body, 0, unroll=8)

    def weight_copies(expert, half):
        e = layer_ref[0] * MOE_EXPERTS + expert
        return [pltpu.make_async_copy(src.at[e], dst.at[half], sem.at[half])
                for src, dst in ((wg_hbm, wg_buf), (wu_hbm, wu_buf), (wd_hbm, wd_buf))]

    @pl.when(i < n_ref[0])
    def _():
        @pl.when(items_ref[IT_FIRST, i] == 1)
        def _():
            half = items_ref[IT_PARITY, i]
            own = weight_copies(items_ref[IT_EXPERT, i], half)

            @pl.when(i == 0)
            def _():
                for cp in own:
                    cp.start()

            for cp in own:
                cp.wait()
            wg_scr[...] = wg_buf[half].astype(bf16)
            wu_scr[...] = wu_buf[half].astype(bf16)
            wd_scr[...] = wd_buf[half].astype(bf16)
            nxt = items_ref[IT_NEXT, i]

            @pl.when(nxt >= 0)
            def _():
                for cp in weight_copies(nxt, 1 - half):
                    cp.start()

        tile = items_ref[IT_TILE, i]
        xhalf = tile % 2

        lo = items_ref[IT_LO, i]

        @pl.when(lo == 0)
        def _():
            @pl.when(i == 0)
            def _():
                gather(src_ref, 0, lambda cp: cp.start())

            gather(src_ref, xhalf, lambda cp: cp.wait())

        x = x_buf[xhalf].astype(bf16)
        hg = _dot(x, wg_scr[...])
        hu = _dot(x, wu_scr[...])
        out = _dot((hg * _sigmoid(hg) * hu).astype(bf16), wd_scr[...])
        prefetch = (lo == 0) & (tile + 1 < n_tiles)
        for r in range(EXPERT_TM):
            @pl.when(prefetch)
            def _():
                _row_copy(x_hbm, src_next_ref[0, 0, r], x_buf.at[1 - xhalf], r, xsem.at[1 - xhalf]).start()
        row = lax.broadcasted_iota(jnp.int32, (EXPERT_TM, 1), 0)
        mine = (row >= lo) & (row < items_ref[IT_HI, i])

        @pl.when(lo == 0)
        def _():
            os_ref[...] = jnp.where(mine, out, 0.0)

        @pl.when(lo != 0)
        def _():
            os_ref[...] = jnp.where(mine, out, os_ref[...])


def _experts_call(xn, src, items, n_items, layer, wg, wu, wd):
    n_tiles = src.shape[0]
    rows = lambda i, items, n, layer: (items[IT_TILE, i], 0)
    src_spec = lambda m: pl.BlockSpec((1, 1, EXPERT_TM), m, memory_space=pltpu.SMEM)
    hbm = pl.BlockSpec(memory_space=pl.ANY)
    return pl.pallas_call(
        _experts_kernel,
        grid_spec=pltpu.PrefetchScalarGridSpec(
            num_scalar_prefetch=3,
            grid=(items.shape[1],),
            in_specs=[src_spec(lambda i, items, n, layer: (items[IT_TILE, i], 0, 0)),
                      src_spec(lambda i, items, n, layer: (jnp.minimum(items[IT_TILE, i] + 1, n_tiles - 1), 0, 0)),
                      hbm, hbm, hbm, hbm],
            out_specs=pl.BlockSpec((EXPERT_TM, D_MODEL), rows),
            scratch_shapes=[pltpu.VMEM((2, EXPERT_TM, D_MODEL), f32),
                            pltpu.VMEM((2, D_MODEL, D_EXPERT), f32), pltpu.VMEM((2, D_MODEL, D_EXPERT), f32),
                            pltpu.VMEM((2, D_EXPERT, D_MODEL), f32),
                            pltpu.VMEM((D_MODEL, D_EXPERT), bf16), pltpu.VMEM((D_MODEL, D_EXPERT), bf16),
                            pltpu.VMEM((D_EXPERT, D_MODEL), bf16),
                            pltpu.SemaphoreType.DMA((2,)), pltpu.SemaphoreType.DMA((2,))],
        ),
        out_shape=jax.ShapeDtypeStruct((2 * xn.shape[0], D_MODEL), f32),
        compiler_params=_cp(("arbitrary",)),
        name="moe_experts",
    )(items, n_items, layer, src, src, xn, wg, wu, wd)


def _combine_kernel(pos_ref, pos_next_ref, x_ref, route_ref, gn_ref, os_ref, *rest, emit_x, split):
    outs, (buf, sem, xn_scr) = rest[:-3], rest[-3:]
    i = pl.program_id(0)
    half = i % 2
    last = i == pl.num_programs(0) - 1

    def copy(p_ref, hf, r, s):
        return _row_copy(os_ref, p_ref[0, 0, s * MOVE_TILE + r], buf.at[hf, s], r, sem.at[hf])

    def loop_all(p_ref, hf, op):
        def body(r, c):
            for s in range(2):
                op(copy(p_ref, hf, r, s), s)
            return c

        lax.fori_loop(0, MOVE_TILE, body, 0, unroll=8)

    begin = lambda cp, s: cp.start(priority=s)
    finish = lambda cp, s: cp.wait()

    @pl.when(i == 0)
    def _():
        loop_all(pos_ref, 0, begin)

    loop_all(pos_ref, half, finish)
    xn_ref = xn_scr if split else outs[-1]
    for c in range(MOVE_TILE // COMBINE_CHUNK):
        rows = slice(c * COMBINE_CHUNK, (c + 1) * COMBINE_CHUNK)
        route = route_ref[rows, :]
        x2 = x_ref[rows, :] + route[:, 4:5] * buf[half, 0, rows, :] + route[:, 5:6] * buf[half, 1, rows, :]
        if emit_x:
            outs[0][rows, :] = x2
        xn_ref[rows, :] = _rms(x2, gn_ref[...]).astype(xn_ref.dtype)
        for r in range(rows.start, rows.stop):
            for s in range(2):
                begin(copy(pos_next_ref, 1 - half, r, s), s)

    @pl.when(last)
    def _():
        loop_all(pos_next_ref, 1 - half, finish)

    if split:
        main_ref, tail_ref = outs[-2:]
        is_main = i < N_MAIN // MOVE_TILE

        @pl.when(is_main)
        def _():
            main_ref[...] = xn_scr[...]

        @pl.when(jnp.logical_not(is_main))
        def _():
            tail_ref[...] = xn_scr[...]


def _combine_call(x1, route, pos, os_rows, gn, xn_dtype, *, emit_x, split):
    n = x1.shape[0]
    row = lambda i: (i, 0)
    blk = lambda m: pl.BlockSpec((MOVE_TILE, D_MODEL), m)
    nm = N_MAIN // MOVE_TILE
    out_specs, out_shape = [], []
    if emit_x:
        out_specs.append(blk(row))
        out_shape.append(jax.ShapeDtypeStruct((n, D_MODEL), f32))
    if split:
        out_specs += [blk(lambda i: (jnp.minimum(i, nm - 1), 0)), blk(lambda i: (jnp.maximum(i - nm, 0), 0))]
        out_shape += [jax.ShapeDtypeStruct((N_MAIN, D_MODEL), xn_dtype),
                      jax.ShapeDtypeStruct((n - N_MAIN, D_MODEL), xn_dtype)]
    else:
        out_specs.append(blk(row))
        out_shape.append(jax.ShapeDtypeStruct((n, D_MODEL), xn_dtype))
    nt = n // MOVE_TILE
    pos_spec = lambda m: pl.BlockSpec((1, 1, 2 * MOVE_TILE), m, memory_space=pltpu.SMEM)
    return pl.pallas_call(
        functools.partial(_combine_kernel, emit_x=emit_x, split=split),
        grid=(nt,),
        in_specs=[pos_spec(lambda i: (i, 0, 0)), pos_spec(lambda i: (jnp.minimum(i + 1, nt - 1), 0, 0)),
                  blk(row), pl.BlockSpec((MOVE_TILE, LANES), row),
                  pl.BlockSpec((1, D_MODEL), lambda i: (0, 0)), pl.BlockSpec(memory_space=pl.ANY)],
        out_specs=out_specs,
        out_shape=out_shape,
        scratch_shapes=[pltpu.VMEM((2, 2, MOVE_TILE, D_MODEL), f32), pltpu.SemaphoreType.DMA((2,)),
                        pltpu.VMEM((MOVE_TILE, D_MODEL), xn_dtype)],
        compiler_params=_cp(("arbitrary",)),
        name="moe_combine",
    )(pos, pos, x1, route, gn, os_rows)


S5_GROUP = 16
S5_GROUPS = D_MODEL // S5_GROUP
S5_STATE = 64
S5_CB = 128
S5_NB = D_MODEL // S5_CB
S5_GPB = S5_CB // S5_GROUP
S5_SB = S5_GPB * S5_STATE
S5_BC = S5_STATE * S5_GROUP
S5_CPS = 2


def _dot_exact01(x, sel):
    hi, mid, lo = _split3(x)
    return _dot(hi, sel) + _dot(mid, sel) + _dot(lo, sel)


def _s5_prep_kernel(lr_ref, li_ref, ldt_ref, bre_ref, bim_ref, cre_ref, cim_ref, bb2t_ref, cct_ref, abv_ref):
    lr, li = lr_ref[...], li_ref[...]
    dt = jnp.exp(ldt_ref[...])
    mag = jnp.exp(lr * dt)
    ang = li * dt
    ab_re, ab_im = mag * jnp.cos(ang), mag * jnp.sin(ang)
    nr, ni = ab_re - 1.0, ab_im
    den = lr * lr + li * li
    f_re = (nr * lr + ni * li) / den
    f_im = (ni * lr - nr * li) / den
    ab2_re = ab_re * ab_re - ab_im * ab_im
    ab2_im = 2.0 * (ab_re * ab_im)

    def iota(shape, axis):
        return lax.broadcasted_iota(jnp.int32, shape, axis)

    def as_col(v):
        rep = jnp.concatenate([jnp.broadcast_to(v[g:g + 1, :], (S5_STATE, S5_STATE)) for g in range(S5_GPB)], axis=0)
        pick = iota((S5_SB, S5_STATE), 1) == iota((S5_SB, S5_STATE), 0) % S5_STATE
        return jnp.sum(jnp.where(pick, rep, 0.0), axis=1, keepdims=True)

    tile_c = jnp.where(iota((S5_GROUP, S5_CB), 1) % S5_GROUP == iota((S5_GROUP, S5_CB), 0), 1.0, 0.0).astype(bf16)
    tile_p = jnp.where(iota((S5_STATE, S5_SB), 1) % S5_STATE == iota((S5_STATE, S5_SB), 0), 1.0, 0.0).astype(bf16)
    own_c = iota((S5_SB, S5_CB), 1) // S5_GROUP == iota((S5_SB, S5_CB), 0) // S5_STATE
    own_p = iota((S5_CB, S5_SB), 1) // S5_STATE == iota((S5_CB, S5_SB), 0) // S5_GROUP

    fr, fi, ar, ai = as_col(f_re), as_col(f_im), as_col(ab_re), as_col(ab_im)
    br, bi = bre_ref[...], bim_ref[...]
    bb_re = fr * br - fi * bi
    bb_im = fr * bi + fi * br
    abb_re = ar * bb_re - ai * bb_im
    abb_im = ar * bb_im + ai * bb_re
    spread_c = lambda m: jnp.where(own_c, _dot_exact01(m, tile_c), 0.0)
    spread_p = lambda m: jnp.where(own_p, _dot_exact01(m, tile_p), 0.0)
    bb2t_ref[0] = jnp.concatenate([
        jnp.concatenate([spread_c(bb_re), spread_c(abb_re)], axis=1),
        jnp.concatenate([spread_c(bb_im), spread_c(abb_im)], axis=1)], axis=0).astype(bf16)
    cct_ref[0] = jnp.concatenate([spread_p(cre_ref[...]), -spread_p(cim_ref[...])], axis=1).astype(bf16)
    grp = iota((S5_GPB, S5_SB), 1) // S5_STATE == iota((S5_GPB, S5_SB), 0)
    as_row = lambda v: jnp.sum(jnp.where(grp, _dot_exact01(v, tile_p), 0.0), axis=0, keepdims=True)
    rows = [as_row(v) for v in (ab_re, ab_im, ab2_re, ab2_im)]
    abv_ref[0] = jnp.concatenate(rows + [jnp.zeros((SUB - len(rows), S5_SB), f32)], axis=0)


def _s5_weights(lam_re, lam_im, log_dt, b_re, b_im, c_re, c_im):
    blk = lambda r, c: pl.BlockSpec((r, c), lambda j: (j, 0))
    out = lambda r, c: pl.BlockSpec((1, r, c), lambda j: (j, 0, 0))
    return pl.pallas_call(
        _s5_prep_kernel,
        grid=(S5_NB,),
        in_specs=[blk(S5_GPB, S5_STATE), blk(S5_GPB, S5_STATE), blk(S5_GPB, 1),
                  blk(S5_SB, S5_GROUP), blk(S5_SB, S5_GROUP), blk(S5_CB, S5_STATE), blk(S5_CB, S5_STATE)],
        out_specs=[out(2 * S5_SB, 2 * S5_CB), out(S5_CB, 2 * S5_SB), out(SUB, S5_SB)],
        out_shape=[jax.ShapeDtypeStruct((S5_NB, 2 * S5_SB, 2 * S5_CB), bf16),
                   jax.ShapeDtypeStruct((S5_NB, S5_CB, 2 * S5_SB), bf16),
                   jax.ShapeDtypeStruct((S5_NB, SUB, S5_SB), f32)],
        compiler_params=_cp(("arbitrary",)),
        name="s5_prep",
    )(lam_re, lam_im, log_dt[:, None], b_re.reshape(-1, S5_GROUP), b_im.reshape(-1, S5_GROUP),
      c_re.reshape(-1, S5_STATE), c_im.reshape(-1, S5_STATE))


def _s5_seq_kernel(x0_ref, x1_ref, x2_ref, x3_ref, halo0_ref, h0_ref, bb2_ref, cc_ref, abv_ref, y_ref, hout_ref,
                   xf_scr, xp_scr, bu_scr, hs_scr, yp_scr, yn_scr, h_scr, halo_scr):
    tb = pl.program_id(1)
    TL = x0_ref.shape[0]
    KB = TL // 2
    RB = BATCH * TL

    @pl.when(tb == 0)
    def _():
        h_scr[...] = h0_ref[...]
        halo_scr[...] = halo0_ref[...].astype(f32)

    chans = [slice(c * S5_CB, (c + 1) * S5_CB) for c in range(S5_CPS)]
    for c, ch in enumerate(chans):
        for b, xb_ref in enumerate((x0_ref, x1_ref, x2_ref, x3_ref)):
            xf_scr[c, b * TL:(b + 1) * TL, :] = xb_ref[:, ch].astype(f32)
        for b in range(BATCH):
            for p in range(2):
                xp_scr[c, pl.ds(2 * b + p, KB, stride=SUB), :] = xf_scr[c, pl.ds(b * TL + p, KB, stride=2), :]
    x = jnp.concatenate([xp_scr[c] for c in range(S5_CPS)], axis=1)
    xc = jnp.concatenate([halo_scr[...], x], axis=0)
    odd = (lax.broadcasted_iota(jnp.int32, (RB + SUB, 1), 0) & 1) == 1
    xprev = jnp.where(odd, pltpu.roll(xc, 1, axis=0), pltpu.roll(xc, SUB - 1, axis=0))[SUB:]
    halo_scr[...] = x[RB - SUB:]
    for c, ch in enumerate(chans):
        lhs = jnp.concatenate([x[:, ch], xprev[:, ch]], axis=1).astype(bf16)
        bu_scr[c] = _dot_nt(lhs, bb2_ref[c])
    a2 = [(abv_ref[c, 2:3, :], abv_ref[c, 3:4, :]) for c in range(S5_CPS)]

    def step(k, carry):
        r0 = pl.multiple_of(k * SUB, SUB)
        out = []
        for c in range(S5_CPS):
            hr, hi = carry[2 * c], carry[2 * c + 1]
            a2r, a2i = a2[c]
            bu = bu_scr[c, pl.ds(r0, SUB), :]
            nr = a2r * hr - a2i * hi + bu[:, :S5_SB]
            ni = a2r * hi + a2i * hr + bu[:, S5_SB:]
            hs_scr[c, pl.ds(r0, SUB), :S5_SB] = nr
            hs_scr[c, pl.ds(r0, SUB), S5_SB:] = ni
            out += [nr, ni]
        return tuple(out)

    init = []
    for c in range(S5_CPS):
        init += [h_scr[c, :, :S5_SB], h_scr[c, :, S5_SB:]]
    fin = lax.fori_loop(0, RB // SUB, step, tuple(init), unroll=4)
    for c, ch in enumerate(chans):
        h_scr[c, :, :S5_SB] = fin[2 * c]
        h_scr[c, :, S5_SB:] = fin[2 * c + 1]
        yp_scr[c] = _dot_nt(hs_scr[c].astype(bf16), cc_ref[c])
        for b in range(BATCH):
            for p in range(2):
                yn_scr[c, pl.ds(b * TL + p, KB, stride=2), :] = yp_scr[c, pl.ds(2 * b + p, KB, stride=SUB), :]
            y_ref[b, :, ch] = yn_scr[c, b * TL:(b + 1) * TL, :]

    @pl.when(tb == pl.num_programs(1) - 1)
    def _():
        hout_ref[...] = h_scr[...]


def _s5_seq_call(x, halo0, h0, bb2, cc, abv, *, row0, seq_len, tl):
    wsel = lambda j, t: (j, 0, 0)
    rb = BATCH * tl
    cw = S5_CPS * S5_CB
    xspec = lambda b: pl.BlockSpec((tl, cw), lambda j, t: ((row0 + b * seq_len) // tl + t, j))
    return pl.pallas_call(
        _s5_seq_kernel,
        grid=(S5_NB // S5_CPS, seq_len // tl),
        in_specs=[xspec(b) for b in range(BATCH)]
        + [pl.BlockSpec((SUB, cw), lambda j, t: (0, j)),
           pl.BlockSpec((S5_CPS, SUB, 2 * S5_SB), wsel),
           pl.BlockSpec((S5_CPS, 2 * S5_SB, 2 * S5_CB), wsel),
           pl.BlockSpec((S5_CPS, S5_CB, 2 * S5_SB), wsel),
           pl.BlockSpec((S5_CPS, SUB, S5_SB), wsel)],
        out_specs=[pl.BlockSpec((BATCH, tl, cw), lambda j, t: (0, t, j)),
                   pl.BlockSpec((S5_CPS, SUB, 2 * S5_SB), wsel)],
        out_shape=[jax.ShapeDtypeStruct((BATCH, seq_len, D_MODEL), f32),
                   jax.ShapeDtypeStruct((S5_NB, SUB, 2 * S5_SB), f32)],
        scratch_shapes=[pltpu.VMEM((S5_CPS, rb, S5_CB), f32), pltpu.VMEM((S5_CPS, rb, S5_CB), f32),
                        pltpu.VMEM((S5_CPS, rb, 2 * S5_SB), f32), pltpu.VMEM((S5_CPS, rb, 2 * S5_SB), f32),
                        pltpu.VMEM((S5_CPS, rb, S5_CB), f32), pltpu.VMEM((S5_CPS, rb, S5_CB), f32),
                        pltpu.VMEM((S5_CPS, SUB, 2 * S5_SB), f32), pltpu.VMEM((SUB, cw), f32)],
        compiler_params=_cp(("arbitrary", "arbitrary")),
        name=f"s5_seq_{seq_len}",
    )(x, x, x, x, halo0, h0, bb2, cc, abv)


def _s5_sample_kernel(x_ref, hre_ref, him_ref, bb2_ref, cc_ref, abv_ref, y_ref, ore_ref, oim_ref, hs_scr):
    nb = hre_ref.shape[0]
    bu = _dot_nt(x_ref[...], bb2_ref[0, :, :S5_CB])
    ar = abv_ref[0, 0:1, :]
    ai = abv_ref[0, 1:2, :]
    hr, hi = hre_ref[...], him_ref[...]
    for t in range(DEC_SEQ):
        rows = slice(t * nb, (t + 1) * nb)
        hr, hi = (ar * hr - ai * hi + bu[rows, :S5_SB], ar * hi + ai * hr + bu[rows, S5_SB:])
        hs_scr[rows, :S5_SB] = hr
        hs_scr[rows, S5_SB:] = hi
    y_ref[...] = _dot_nt(hs_scr[...].astype(bf16), cc_ref[0])
    ore_ref[...] = hr
    oim_ref[...] = hi


def _s5_sample_call(xt, h_re, h_im, bb2, cc, abv):
    n = xt.shape[0]
    nb = h_re.shape[0]
    wsel = lambda j: (j, 0, 0)
    st = pl.BlockSpec((nb, S5_SB), lambda j: (0, j))
    return pl.pallas_call(
        _s5_sample_kernel,
        grid=(S5_NB,),
        in_specs=[pl.BlockSpec((n, S5_CB), lambda j: (0, j)), st, st,
                  pl.BlockSpec((1, 2 * S5_SB, 2 * S5_CB), wsel),
                  pl.BlockSpec((1, S5_CB, 2 * S5_SB), wsel),
                  pl.BlockSpec((1, SUB, S5_SB), wsel)],
        out_specs=[pl.BlockSpec((n, S5_CB), lambda j: (0, j)), st, st],
        out_shape=[jax.ShapeDtypeStruct((n, D_MODEL), f32),
                   jax.ShapeDtypeStruct(h_re.shape, f32), jax.ShapeDtypeStruct(h_im.shape, f32)],
        scratch_shapes=[pltpu.VMEM((n, 2 * S5_SB), f32)],
        compiler_params=_cp(("arbitrary",)),
        name="s5_sample",
    )(xt, h_re, h_im, bb2, cc, abv)


GLA_CHUNK = 256
GLA_DIRECT = 2
S5_TL = 256


def _moe_layer(layer, x1, xnf, route, plan, cnt, wg, wu, wd, gn, xn_dtype, *, emit_x, split):
    pos, items, n_items = _moe_plan(plan, cnt, x1.shape[0])
    src = _source_rows(pos, x1.shape[0])
    os_rows = _experts_call(xnf, src, items, n_items, jnp.full((1,), layer, jnp.int32), wg, wu, wd)
    return _combine_call(x1, route, pos, os_rows, gn, xn_dtype, emit_x=emit_x, split=split)


def kernel(x_prompt, x_sample, state_gla, state_s5_re, state_s5_im, meta_tokens, norm_mix_g, norm_ffn_g, norm_final_g, gla_w_in, gla_w_a2, gla_b_a, gla_g_o, gla_w_o, s5_lambda_re, s5_lambda_im, s5_log_dt, s5_b_re, s5_b_im, s5_c_re, s5_c_im, s5_d, s5_w_glu, s5_b_glu, moe_w_rg, moe_b_rg, moe_w_re, moe_b_re, moe_w_gate, moe_w_up, moe_w_down):
    row = lambda v: v.reshape(1, -1)
    x_main = x_prompt.reshape(N_MAIN, D_MODEL)
    x_tail = jnp.concatenate([
        jnp.tile(meta_tokens.astype(x_prompt.dtype), (BATCH, 1)),
        x_sample.reshape(N_SAMPLE, D_MODEL),
        jnp.zeros((N_ROWS - N_REAL, D_MODEL), x_prompt.dtype)], axis=0)
    wg = moe_w_gate.reshape(-1, D_MODEL, D_EXPERT)
    wu = moe_w_up.reshape(-1, D_MODEL, D_EXPERT)
    wd = moe_w_down.reshape(-1, D_EXPERT, D_MODEL)

    w_in = jnp.swapaxes(gla_w_in, 1, 2)
    wa1 = jnp.pad(gla_w_in[0, :, GLA_QKVR:], ((0, 0), (0, LANES - GLA_RANK)))
    wa2 = jnp.pad(gla_w_a2.reshape(GLA_RANK, GLA_KEY), ((0, LANES - GLA_RANK), (0, 0)))
    xn, glog = _norm_gate_call(x_main, x_tail, row(norm_mix_g[0]), wa1, wa2, row(gla_b_a))
    proj = _proj_call(xn, w_in, GLA_QKVR)
    go = row(gla_g_o)
    og = jnp.zeros((N_ROWS, GLA_VAL), bf16)
    s_zero = jnp.zeros((BATCH, GLA_HEADS, GLA_DK, GLA_DV), f32)
    og, s_meta = _gla_seq_call(proj, glog, go, s_zero, og, row0=ROW_META, C=N_META, n_chunks=1, d=GLA_DIRECT)
    og, s_prompt = _gla_seq_call(proj, glog, go, s_meta, og, row0=0, C=GLA_CHUNK, n_chunks=SEQ // GLA_CHUNK,
                                 d=GLA_DIRECT)
    og, s_sample = _gla_sample_call(proj, glog, go, state_gla.reshape(DEC_BATCH, GLA_HEADS, GLA_DK, GLA_DV), og,
                                    row0=ROW_SAMPLE)
    wr, br = _router_weights(moe_w_rg[0], moe_b_rg[0], moe_w_re[0], moe_b_re[0])
    tile_row = lambda i: (i, 0)
    x1, xnf, route, plan, cnt = _mix_out_call(
        _gla_out_kernel, "gla_out", N_ROWS, [og, x_main, x_tail], [tile_row, *_main_tail_maps(MIX_TILE)],
        [gla_w_o.reshape(GLA_VAL, D_MODEL), row(norm_ffn_g[0]), wr, br])
    x2, xn2 = _moe_layer(0, x1, xnf, route, plan, cnt, wg, wu, wd, row(norm_mix_g[1]), bf16, emit_x=True, split=False)

    bb2, cc, abv = _s5_weights(s5_lambda_re[0], s5_lambda_im[0], s5_log_dt[0], s5_b_re[0], s5_b_im[0],
                               s5_c_re[0], s5_c_im[0])
    y_meta, h_meta = _s5_seq_call(xn2, jnp.zeros((SUB, D_MODEL), bf16), jnp.zeros((S5_NB, SUB, 2 * S5_SB), f32),
                                  bb2, cc, abv, row0=ROW_META, seq_len=N_META, tl=N_META)
    halo = xn2[ROW_META:ROW_SAMPLE].reshape(BATCH, N_META, D_MODEL)[:, N_META - 2:].reshape(SUB, D_MODEL)
    y_main, h_main = _s5_seq_call(xn2, halo, h_meta, bb2, cc, abv, row0=0, seq_len=SEQ, tl=S5_TL)
    xt_sample = xn2[ROW_SAMPLE:N_REAL].reshape(DEC_BATCH, DEC_SEQ, D_MODEL).transpose(1, 0, 2).reshape(N_SAMPLE, D_MODEL)
    y_samp, s5r_s, s5i_s = _s5_sample_call(
        xt_sample, state_s5_re.reshape(DEC_BATCH, S5_GROUPS * S5_STATE),
        state_s5_im.reshape(DEC_BATCH, S5_GROUPS * S5_STATE), bb2, cc, abv)
    ys_tail = jnp.concatenate([
        y_meta.reshape(N_METAROWS, D_MODEL),
        y_samp.reshape(DEC_SEQ, DEC_BATCH, D_MODEL).transpose(1, 0, 2).reshape(N_SAMPLE, D_MODEL),
        jnp.zeros((N_ROWS - N_REAL, D_MODEL), f32)], axis=0)
    wr, br = _router_weights(moe_w_rg[1], moe_b_rg[1], moe_w_re[1], moe_b_re[1])
    x3, xnf, route, plan, cnt = _mix_out_call(
        _s5_out_kernel, "s5_out", N_ROWS, [y_main.reshape(N_MAIN, D_MODEL), ys_tail, x2],
        [*_main_tail_maps(MIX_TILE), tile_row],
        [s5_w_glu.reshape(D_MODEL, D_MODEL), row(norm_mix_g[1]), row(s5_d), row(s5_b_glu),
         row(norm_ffn_g[1]), wr, br])
    y_main_out, y_tail_out = _moe_layer(1, x3, xnf, route, plan, cnt, wg, wu, wd, row(norm_final_g), f32,
                                        emit_x=False, split=True)

    y_prompt = y_main_out.reshape(BATCH, SEQ, D_MODEL)
    y_sample = y_tail_out[N_METAROWS:N_METAROWS + N_SAMPLE].reshape(DEC_BATCH, DEC_SEQ, D_MODEL)
    hfin = h_main.reshape(S5_NB, BATCH, 2, 2, S5_GPB, S5_STATE)[:, :, 1]
    s5r_p = hfin[:, :, 0].transpose(1, 0, 2, 3).reshape(1, BATCH, S5_GROUPS, S5_STATE)
    s5i_p = hfin[:, :, 1].transpose(1, 0, 2, 3).reshape(1, BATCH, S5_GROUPS, S5_STATE)
    return (y_prompt, y_sample, s_prompt[None], s5r_p, s5i_p, s_sample[None],
            s5r_s.reshape(1, DEC_BATCH, S5_GROUPS, S5_STATE), s5i_s.reshape(1, DEC_BATCH, S5_GROUPS, S5_STATE))
```

```python
import functools

import jax
import jax.numpy as jnp
from jax import lax
from jax.experimental import pallas as pl
from jax.experimental.pallas import tpu as pltpu

f32 = jnp.float32
bf16 = jnp.bfloat16

D_MODEL = 2048
BATCH = 4
SEQ = 2048
DEC_BATCH = 128
DEC_SEQ = 4
N_META = 16
EPS = 1e-6
GLA_HEADS = 4
GLA_DK = 256
GLA_DV = 512
GLA_KEY = GLA_HEADS * GLA_DK
GLA_VAL = GLA_HEADS * GLA_DV
GLA_RANK = 16
GLA_TAU = 16.0
GLA_QKVR = 2 * GLA_KEY + 2 * GLA_VAL

N_MAIN = BATCH * SEQ
N_METAROWS = BATCH * N_META
N_SAMPLE = DEC_BATCH * DEC_SEQ
ROW_META = N_MAIN
ROW_SAMPLE = N_MAIN + N_METAROWS
N_REAL = ROW_SAMPLE + N_SAMPLE
ROW_TILE = 256
N_ROWS = -(-N_REAL // ROW_TILE) * ROW_TILE

VMEM_LIMIT = 56 * 1024 * 1024


def _cp(sem, vmem=VMEM_LIMIT):
    return pltpu.CompilerParams(dimension_semantics=sem, vmem_limit_bytes=vmem)


def _dot(a, b):
    return jnp.dot(a, b, preferred_element_type=f32)


def _dot_nt(a, b):
    return lax.dot_general(a, b, (((1,), (1,)), ((), ())), preferred_element_type=f32)


def _dot_tn(a, b):
    return lax.dot_general(a, b, (((0,), (0,)), ((), ())), preferred_element_type=f32)


def _sigmoid(x):
    return 1.0 / (1.0 + jnp.exp(-x))


def _split3(x):
    hi = x.astype(bf16)
    r1 = x - hi.astype(f32)
    mid = r1.astype(bf16)
    lo = (r1 - mid.astype(f32)).astype(bf16)
    return hi, mid, lo


def _cumsum_rows(g, C):
    if C <= 16:
        row = lax.broadcasted_iota(jnp.int32, (C, 1), 0)
        b = jnp.zeros_like(g)
        for s in range(C):
            b = b + jnp.where(row >= s, g[s:s + 1, :], 0.0)
        return b
    row = lax.broadcasted_iota(jnp.int32, (C, C), 0)
    col = lax.broadcasted_iota(jnp.int32, (C, C), 1)
    tri = jnp.where(row >= col, 1.0, 0.0).astype(bf16)
    hi, mid, lo = _split3(g)
    return _dot(tri, hi) + _dot(tri, mid) + _dot(tri, lo)


PAIR_LEVEL = 1000


def _gla_pair_code(C, d, tree=True):
    ti = lax.broadcasted_iota(jnp.int32, (C, C), 0)
    si = lax.broadcasted_iota(jnp.int32, (C, C), 1)
    code = jnp.where((ti // d == si // d) & (si <= ti), 1 + ti - si, 0)
    h = d
    while tree and h < C:
        tb = ti // h
        code = jnp.where(((tb % 2) == 1) & ((si // h) == tb - 1), PAIR_LEVEL + h, code)
        h *= 2
    return code


def _gla_scores(q, k, b, code, C, d, tree=True):
    row = lax.broadcasted_iota(jnp.int32, (C, 1), 0)
    scores = jnp.zeros((C, C), f32)
    for dl in range(d):
        ks = k if dl == 0 else pltpu.roll(k, dl, axis=0)
        bs = b if dl == 0 else pltpu.roll(b, dl, axis=0)
        term = q * ks * jnp.exp(jnp.minimum(b - bs, 0.0))
        colv = jnp.sum(term, axis=1, keepdims=True)
        scores = jnp.where(code == 1 + dl, colv, scores)
    z = b
    s = 1
    while tree and 2 * s < C:
        z = jnp.where((row & s) != 0, pltpu.roll(z, s, axis=0), z)
        s *= 2
        h = s
        if h < d:
            continue
        bnext = pltpu.roll(z, C - h, axis=0)
        qh = (q * jnp.exp(b - z)).astype(bf16)
        kh = (k * jnp.exp(jnp.minimum(bnext - b, 0.0))).astype(bf16)
        scores = jnp.where(code == PAIR_LEVEL + h, _dot_nt(qh, kh), scores)
    return scores


def _gla_chunk(q, k, v, g, S, code, C, d, tree=True):
    b = _cumsum_rows(g, C)
    o = _dot((q * jnp.exp(b)).astype(bf16), S.astype(bf16))
    scores = _gla_scores(q, k, b, code, C, d, tree)
    o = o + _dot(scores.astype(bf16), v)
    b_last = b[C - 1:C, :]
    kd = (k * jnp.exp(b_last - b)).astype(bf16)
    if C == GLA_DK:
        eye = code == 1
    else:
        eye = (lax.broadcasted_iota(jnp.int32, (GLA_DK, GLA_DK), 0)
               == lax.broadcasted_iota(jnp.int32, (GLA_DK, GLA_DK), 1))
    dec_col = jnp.sum(jnp.where(eye, jnp.exp(b_last), 0.0), axis=1, keepdims=True)
    S_new = dec_col * S + _dot_tn(kd, v)
    return o, S_new


def _gla_head_epilogue(o, r, go):
    ms = jnp.mean(o * o, axis=1, keepdims=True)
    on = o * lax.rsqrt(ms + EPS) * go
    return on * (r * _sigmoid(r))


def _gla_seq_kernel(q_ref, k_ref, v_ref, r_ref, g_ref, go_ref, s0_ref, _og_in, og_ref, sout_ref, s_scr, *, C, d):
    c = pl.program_id(1)

    @pl.when(c == 0)
    def _():
        s_scr[...] = s0_ref[0]

    code = _gla_pair_code(C, d)

    def head(h, carry):
        ck = pl.ds(pl.multiple_of(h * GLA_DK, GLA_DK), GLA_DK)
        cv = pl.ds(pl.multiple_of(h * GLA_DV, GLA_DV), GLA_DV)
        q = q_ref[:, ck].astype(f32) * (GLA_DK ** -0.5)
        k = k_ref[:, ck].astype(f32)
        o, S_new = _gla_chunk(q, k, v_ref[:, cv], g_ref[:, ck], s_scr[h], code, C, d)
        s_scr[h] = S_new
        og_ref[:, cv] = _gla_head_epilogue(o, r_ref[:, cv].astype(f32), go_ref[:, cv]).astype(og_ref.dtype)
        return carry

    lax.fori_loop(0, GLA_HEADS, head, 0)

    @pl.when(c == pl.num_programs(1) - 1)
    def _():
        sout_ref[0] = s_scr[...]


def _gla_seq_call(proj, glog, go, s0, og_buf, *, row0, C, n_chunks, d):
    blk0 = row0 // C
    rows = lambda b, c: blk0 + b * n_chunks + c
    kern = functools.partial(_gla_seq_kernel, C=C, d=d)
    return pl.pallas_call(
        kern,
        grid=(BATCH, n_chunks),
        in_specs=[
            pl.BlockSpec((C, GLA_KEY), lambda b, c: (rows(b, c), 0)),
            pl.BlockSpec((C, GLA_KEY), lambda b, c: (rows(b, c), 1)),
            pl.BlockSpec((C, GLA_VAL), lambda b, c: (rows(b, c), 1)),
            pl.BlockSpec((C, GLA_VAL), lambda b, c: (rows(b, c), 2)),
            pl.BlockSpec((C, GLA_KEY), lambda b, c: (rows(b, c), 0)),
            pl.BlockSpec((1, GLA_VAL), lambda b, c: (0, 0)),
            pl.BlockSpec((1, GLA_HEADS, GLA_DK, GLA_DV), lambda b, c: (b, 0, 0, 0)),
            pl.BlockSpec(memory_space=pl.ANY),
        ],
        out_specs=[
            pl.BlockSpec((C, GLA_VAL), lambda b, c: (rows(b, c), 0)),
            pl.BlockSpec((1, GLA_HEADS, GLA_DK, GLA_DV), lambda b, c: (b, 0, 0, 0)),
        ],
        out_shape=[
            jax.ShapeDtypeStruct(og_buf.shape, og_buf.dtype),
            jax.ShapeDtypeStruct((BATCH, GLA_HEADS, GLA_DK, GLA_DV), f32),
        ],
        scratch_shapes=[pltpu.VMEM((GLA_HEADS, GLA_DK, GLA_DV), f32)],
        input_output_aliases={7: 0},
        compiler_params=_cp(("arbitrary", "arbitrary")),
        name=f"gla_seq_c{C}",
    )(proj, proj, proj, proj, glog, go, s0, og_buf)


SAMPLE_BB = 4
SAMPLE_C = SAMPLE_BB * DEC_SEQ


def _gla_sample_kernel(q_ref, k_ref, v_ref, r_ref, g_ref, go_ref, s0_ref, _og_in, og_ref, sout_ref):
    row = lax.broadcasted_iota(jnp.int32, (SAMPLE_C, 1), 0)

    code = _gla_pair_code(SAMPLE_C, DEC_SEQ, tree=False)

    def take(x, bb):
        sh = (SAMPLE_C - DEC_SEQ * bb) % SAMPLE_C
        return jnp.where(row < DEC_SEQ, pltpu.roll(x, sh, axis=0) if sh else x, 0.0)

    def head(h, carry):
        ck = pl.ds(pl.multiple_of(h * GLA_DK, GLA_DK), GLA_DK)
        cv = pl.ds(pl.multiple_of(h * GLA_DV, GLA_DV), GLA_DV)
        q_all = q_ref[:, ck].astype(f32) * (GLA_DK ** -0.5)
        k_all = k_ref[:, ck].astype(f32)
        v_all = v_ref[:, cv].astype(f32)
        r_all = r_ref[:, cv].astype(f32)
        g_all = g_ref[:, ck]
        go = go_ref[:, cv]
        acc = jnp.zeros((SAMPLE_C, GLA_DV), f32)
        for bb in range(SAMPLE_BB):
            o, S_new = _gla_chunk(take(q_all, bb), take(k_all, bb), take(v_all, bb).astype(bf16),
                                  take(g_all, bb), s0_ref[bb, h], code, SAMPLE_C, DEC_SEQ, tree=False)
            sout_ref[bb, h] = S_new
            y = _gla_head_epilogue(o, take(r_all, bb), go)
            acc = jnp.where(row // DEC_SEQ == bb, pltpu.roll(y, DEC_SEQ * bb, axis=0) if bb else y, acc)
        og_ref[:, cv] = acc.astype(og_ref.dtype)
        return carry

    lax.fori_loop(0, GLA_HEADS, head, 0)


def _gla_sample_call(proj, glog, go, s0, og_buf, *, row0):
    n_seq = s0.shape[0]
    blk0 = row0 // SAMPLE_C
    st_spec = pl.BlockSpec((SAMPLE_BB, GLA_HEADS, GLA_DK, GLA_DV), lambda i: (i, 0, 0, 0))
    return pl.pallas_call(
        _gla_sample_kernel,
        grid=(n_seq // SAMPLE_BB,),
        in_specs=[
            pl.BlockSpec((SAMPLE_C, GLA_KEY), lambda i: (blk0 + i, 0)),
            pl.BlockSpec((SAMPLE_C, GLA_KEY), lambda i: (blk0 + i, 1)),
            pl.BlockSpec((SAMPLE_C, GLA_VAL), lambda i: (blk0 + i, 1)),
            pl.BlockSpec((SAMPLE_C, GLA_VAL), lambda i: (blk0 + i, 2)),
            pl.BlockSpec((SAMPLE_C, GLA_KEY), lambda i: (blk0 + i, 0)),
            pl.BlockSpec((1, GLA_VAL), lambda i: (0, 0)),
            st_spec,
            pl.BlockSpec(memory_space=pl.ANY),
        ],
        out_specs=[pl.BlockSpec((SAMPLE_C, GLA_VAL), lambda i: (blk0 + i, 0)), st_spec],
        out_shape=[jax.ShapeDtypeStruct(og_buf.shape, og_buf.dtype), jax.ShapeDtypeStruct(s0.shape, f32)],
        input_output_aliases={7: 0},
        compiler_params=_cp(("arbitrary",)),
        name="gla_sample",
    )(proj, proj, proj, proj, glog, go, s0, og_buf)


LANES = 128
SUB = 8
MOE_GROUPS = 4
MOE_EPG = 8
MOE_EXPERTS = MOE_GROUPS * MOE_EPG
D_EXPERT = 256
ROUTE_E0 = MOE_GROUPS
MIX_TILE = 256
PROJ_NT = 1024
PROJ_MT = N_ROWS // 7


def _rms(x, g):
    r = lax.rsqrt(jnp.mean(x * x, axis=-1, keepdims=True) + EPS)
    return (x * r) * g


def _log_sigmoid(z):
    return jnp.minimum(z, 0.0) - jnp.log1p(jnp.exp(-jnp.abs(z)))


def _main_or_tail(tile, main_ref, tail_ref):
    return jnp.where(pl.program_id(0) < N_MAIN // tile, main_ref[...], tail_ref[...])


def _main_tail_maps(tile):
    nm = N_MAIN // tile
    return (lambda i: (jnp.minimum(i, nm - 1), 0)), (lambda i: (jnp.maximum(i - nm, 0), 0))


def _norm_gate_kernel(xm_ref, xt_ref, gn_ref, wa1_ref, wa2_ref, ba_ref, xn_ref, gl_ref):
    xnb = _rms(_main_or_tail(ROW_TILE, xm_ref, xt_ref), gn_ref[...]).astype(bf16)
    xn_ref[...] = xnb
    a = _dot(xnb, wa1_ref[...].astype(bf16))
    z = _dot(a.astype(bf16), wa2_ref[...].astype(bf16)) + ba_ref[...]
    gl_ref[...] = _log_sigmoid(z) * (1.0 / GLA_TAU)


def _norm_gate_call(x_main, x_tail, gn, wa1, wa2, ba):
    n = x_main.shape[0] + x_tail.shape[0]
    row = lambda i: (i, 0)
    fix = lambda i: (0, 0)
    main_map, tail_map = _main_tail_maps(ROW_TILE)
    return pl.pallas_call(
        _norm_gate_kernel,
        grid=(n // ROW_TILE,),
        in_specs=[pl.BlockSpec((ROW_TILE, D_MODEL), main_map), pl.BlockSpec((ROW_TILE, D_MODEL), tail_map),
                  pl.BlockSpec((1, D_MODEL), fix),
                  pl.BlockSpec((D_MODEL, LANES), fix), pl.BlockSpec((LANES, GLA_KEY), fix),
                  pl.BlockSpec((1, GLA_KEY), fix)],
        out_specs=[pl.BlockSpec((ROW_TILE, D_MODEL), row), pl.BlockSpec((ROW_TILE, GLA_KEY), row)],
        out_shape=[jax.ShapeDtypeStruct((n, D_MODEL), bf16), jax.ShapeDtypeStruct((n, GLA_KEY), f32)],
        compiler_params=_cp(("arbitrary",)),
        name="norm_gate",
    )(x_main, x_tail, gn, wa1, wa2, ba)


def _proj_kernel(xn_ref, w_ref, o_ref, wb_scr):
    @pl.when(pl.program_id(1) == 0)
    def _():
        wb_scr[...] = w_ref[0].astype(bf16)

    o_ref[...] = _dot_nt(xn_ref[...], wb_scr[...]).astype(o_ref.dtype)


def _proj_call(xn, wt, n_cols):
    n = xn.shape[0]
    return pl.pallas_call(
        _proj_kernel,
        grid=(n_cols // PROJ_NT, n // PROJ_MT),
        in_specs=[pl.BlockSpec((PROJ_MT, D_MODEL), lambda j, i: (i, 0)),
                  pl.BlockSpec((1, PROJ_NT, D_MODEL), lambda j, i: (0, j, 0))],
        out_specs=pl.BlockSpec((PROJ_MT, PROJ_NT), lambda j, i: (i, j)),
        out_shape=jax.ShapeDtypeStruct((n, n_cols), bf16),
        scratch_shapes=[pltpu.VMEM((PROJ_NT, D_MODEL), bf16)],
        compiler_params=_cp(("arbitrary", "arbitrary")),
        name="gla_proj",
    )(xn, wt)


def _route(xn, wr, br, cnt_ref):
    R = xn.shape[0]
    xh = xn.astype(bf16)
    xl = (xn - xh.astype(f32)).astype(bf16)
    wh = wr.astype(bf16)
    wl = (wr - wh.astype(f32)).astype(bf16)
    hi_terms = _dot(xh, jnp.concatenate([wh, wl], axis=1))
    logits = hi_terms[:, :LANES] + hi_terms[:, LANES:] + _dot(xl, wh) + br
    lane_i = lax.broadcasted_iota(jnp.int32, (R, LANES), 1)
    lane = lane_i.astype(f32)
    neg = -jnp.inf
    big = float(LANES)
    is_g = lane_i < MOE_GROUPS
    lg = jnp.where(is_g, logits, neg)
    mg = jnp.max(lg, axis=1, keepdims=True)
    gidx = jnp.min(jnp.where(lg == mg, lane, big), axis=1, keepdims=True)
    ptop = 1.0 / jnp.sum(jnp.where(is_g, jnp.exp(logits - mg), 0.0), axis=1, keepdims=True)
    lo = ROUTE_E0 + MOE_EPG * gidx
    le = jnp.where((lane >= lo) & (lane < lo + MOE_EPG), logits, neg)
    v1 = jnp.max(le, axis=1, keepdims=True)
    i1 = jnp.min(jnp.where(le == v1, lane, big), axis=1, keepdims=True)
    le2 = jnp.where(lane == i1, neg, le)
    v2 = jnp.max(le2, axis=1, keepdims=True)
    i2 = jnp.min(jnp.where(le2 == v2, lane, big), axis=1, keepdims=True)
    s = jnp.exp(v2 - v1)
    w0 = ptop / (1.0 + s)
    w1 = ptop * s / (1.0 + s)
    oh = jnp.where((lane == i1) | (lane == i2), 1.0, 0.0)
    ri = lax.broadcasted_iota(jnp.int32, (R, R), 0)
    ci = lax.broadcasted_iota(jnp.int32, (R, R), 1)
    before = jnp.where(ri > ci, 1.0, 0.0).astype(bf16)
    tot = _dot(before, oh.astype(bf16)) + cnt_ref[...]
    rank0 = jnp.sum(jnp.where(lane == i1, tot, 0.0), axis=1, keepdims=True)
    rank1 = jnp.sum(jnp.where(lane == i2, tot, 0.0), axis=1, keepdims=True)
    cnt_ref[...] = cnt_ref[...] + jnp.sum(oh, axis=0, keepdims=True)
    vals = (i1 - ROUTE_E0, i2 - ROUTE_E0, rank0, rank1, w0, w1)
    slab = jnp.zeros((R, LANES), f32)
    for j, v in enumerate(vals):
        slab = jnp.where(lane_i == j, v, slab)
    sub = lax.broadcasted_iota(jnp.int32, (SUB, R), 0)
    plan = jnp.zeros((SUB, R), f32)
    for j, v in enumerate(vals[:4]):
        as_row = jnp.sum(jnp.where(ri == ci, v, 0.0), axis=0, keepdims=True)
        plan = jnp.where(sub == j, as_row, plan)
    return slab, plan


def _router_weights(w_rg, b_rg, w_re, b_re):
    w = jnp.concatenate([w_rg, jnp.moveaxis(w_re, 0, 1).reshape(D_MODEL, MOE_EXPERTS)], axis=1)
    b = jnp.concatenate([b_rg, b_re.reshape(MOE_EXPERTS)])
    pad = LANES - w.shape[1]
    return jnp.pad(w, ((0, 0), (0, pad))), jnp.pad(b, (0, pad))[None]


def _mix_out_tail(x1, gffn_ref, wr_ref, br_ref, x1_ref, xn_ref, route_ref, plan_ref, cnt_ref):
    @pl.when(pl.program_id(0) == 0)
    def _():
        cnt_ref[...] = jnp.zeros_like(cnt_ref)

    x1_ref[...] = x1
    xn = _rms(x1, gffn_ref[...])
    xn_ref[...] = xn
    route_ref[...], plan_ref[...] = _route(xn, wr_ref[...], br_ref[...], cnt_ref)


def _cast_once(w_ref, w_scr):
    @pl.when(pl.program_id(0) == 0)
    def _():
        w_scr[...] = w_ref[...].astype(bf16)


def _gla_out_kernel(og_ref, xm_ref, xt_ref, wo_ref, gffn_ref, wr_ref, br_ref, *rest):
    outs, w_scr = rest[:-1], rest[-1]
    _cast_once(wo_ref, w_scr)
    x1 = _main_or_tail(MIX_TILE, xm_ref, xt_ref) + _dot(og_ref[...], w_scr[...])
    _mix_out_tail(x1, gffn_ref, wr_ref, br_ref, *outs)


def _gelu_tanh(x):
    return x * (0.5 * (1.0 + jnp.tanh(0.7978845608028654 * (x + 0.044715 * (x * x * x)))))


def _s5_out_kernel(ys_main_ref, ys_tail_ref, x_ref, wglu_ref, gmix_ref, d_ref, bglu_ref, gffn_ref, wr_ref, br_ref,
                   *rest):
    outs, w_scr = rest[:-1], rest[-1]
    _cast_once(wglu_ref, w_scr)
    x = x_ref[...]
    u = _rms(x, gmix_ref[...])
    y = _gelu_tanh(_main_or_tail(MIX_TILE, ys_main_ref, ys_tail_ref) + d_ref[...] * u)
    z = _dot(y.astype(bf16), w_scr[...]) + bglu_ref[...]
    _mix_out_tail(x + y * _sigmoid(z), gffn_ref, wr_ref, br_ref, *outs)


def _mix_out_call(kern, name, n, row_ins, row_maps, fix_ins):
    row = lambda i: (i, 0)
    fix = lambda i: (0, 0)
    out_row = lambda w: pl.BlockSpec((MIX_TILE, w), row)
    nt = n // MIX_TILE
    return pl.pallas_call(
        kern,
        grid=(nt,),
        in_specs=[pl.BlockSpec((MIX_TILE, a.shape[1]), m) for a, m in zip(row_ins, row_maps)]
        + [pl.BlockSpec(a.shape, fix, pipeline_mode=pl.Buffered(1)) for a in fix_ins],
        out_specs=[out_row(D_MODEL), out_row(D_MODEL), out_row(LANES), pl.BlockSpec((SUB, MIX_TILE), row),
                   pl.BlockSpec((1, LANES), fix)],
        out_shape=[jax.ShapeDtypeStruct((n, D_MODEL), f32), jax.ShapeDtypeStruct((n, D_MODEL), f32),
                   jax.ShapeDtypeStruct((n, LANES), f32), jax.ShapeDtypeStruct((nt * SUB, MIX_TILE), f32),
                   jax.ShapeDtypeStruct((1, LANES), f32)],
        scratch_shapes=[pltpu.VMEM((D_MODEL, D_MODEL), bf16)],
        compiler_params=_cp(("arbitrary",)),
        name=name,
    )(*row_ins, *fix_ins)


EXPERT_TM = 256
MOVE_TILE = MIX_TILE
COMBINE_CHUNK = 32


def _moe_plan(plan, cnt, n):
    i32 = jnp.int32
    v = plan.reshape(n // MOVE_TILE, SUB, MOVE_TILE)[:, :4].astype(i32)
    counts = cnt[0, ROUTE_E0:ROUTE_E0 + MOE_EXPERTS].astype(i32)
    ends = jnp.cumsum(counts)
    off = ends - counts
    ids = jnp.arange(MOE_EXPERTS, dtype=i32)
    pos = jnp.sum(jnp.where(v[:, 0:2, :, None] == ids, off, 0), axis=-1) + v[:, 2:4]
    total = 2 * n
    n_tiles = total // EXPERT_TM
    n_items = n_tiles + MOE_EXPERTS
    inner = (counts > 0) & (off % EXPERT_TM != 0)
    keys = jnp.concatenate([jnp.arange(n_tiles, dtype=i32) * EXPERT_TM, jnp.where(inner, off, total)])
    idx = jnp.arange(n_items, dtype=i32)
    before = (keys[None, :] < keys[:, None]) | ((keys[None, :] == keys[:, None]) & (idx[None, :] < idx[:, None]))
    order = jnp.sum(before.astype(i32), axis=1)
    starts = jnp.sum(jnp.where(order[:, None] == idx[None, :], keys[:, None], 0), axis=0)
    stops = jnp.concatenate([starts[1:], jnp.full((1,), total, i32)])
    tile = starts // EXPERT_TM
    expert = jnp.sum((ends[None, :] <= starts[:, None]).astype(i32), axis=1)
    expert = jnp.minimum(expert, MOE_EXPERTS - 1)
    used = n_tiles + jnp.sum(inner.astype(i32))
    first = (expert != jnp.concatenate([jnp.full((1,), -1, i32), expert[:-1]])) & (idx < used)
    parity = (jnp.cumsum(first.astype(i32)) - 1) % 2
    later = first[None, :] & (idx[None, :] > idx[:, None])
    nxt = jnp.min(jnp.where(later, idx[None, :], n_items), axis=1)
    next_expert = jnp.sum(jnp.where(idx[None, :] == nxt[:, None], expert[None, :], 0), axis=1)
    next_expert = jnp.where(nxt < n_items, next_expert, -1)
    keep = jnp.minimum(idx, used - 1)
    items = jnp.stack([tile, expert, starts - tile * EXPERT_TM, stops - tile * EXPERT_TM,
                       first.astype(i32), parity, next_expert, jnp.zeros_like(tile)])[:, keep]
    return pos.reshape(n // MOVE_TILE, 1, 2 * MOVE_TILE), items, used.reshape(1)


IT_TILE, IT_EXPERT, IT_LO, IT_HI, IT_FIRST, IT_PARITY, IT_NEXT = range(7)


def _row_copy(src, src_row, dst, dst_row, sem):
    return pltpu.make_async_copy(src.at[pl.ds(src_row, 1)], dst.at[pl.ds(dst_row, 1)], sem)


def _source_rows_kernel(pos_ref, src_ref):
    base = pl.program_id(0) * MOVE_TILE

    def body(r, c):
        for s in range(2):
            src_ref[pos_ref[0, 0, s * MOVE_TILE + r]] = base + r
        return c

    lax.fori_loop(0, MOVE_TILE, body, 0, unroll=8)


def _source_rows(pos, n):
    src = pl.pallas_call(
        _source_rows_kernel,
        grid=(n // MOVE_TILE,),
        in_specs=[pl.BlockSpec((1, 1, 2 * MOVE_TILE), lambda i: (i, 0, 0), memory_space=pltpu.SMEM)],
        out_specs=pl.BlockSpec(memory_space=pltpu.SMEM),
        out_shape=jax.ShapeDtypeStruct((2 * n,), jnp.int32),
        compiler_params=_cp(("arbitrary",)),
        name="moe_source_rows",
    )(pos)
    return src.reshape(2 * n // EXPERT_TM, 1, EXPERT_TM)


def _experts_kernel(items_ref, n_ref, layer_ref, src_ref, src_next_ref, x_hbm, wg_hbm, wu_hbm, wd_hbm, os_ref,
                    x_buf, wg_buf, wu_buf, wd_buf, wg_scr, wu_scr, wd_scr, xsem, sem):
    i = pl.program_id(0)
    n_tiles = x_hbm.shape[0] * 2 // EXPERT_TM

    def gather(s_ref, half, op):
        def body(r, c):
            op(_row_copy(x_hbm, s_ref[0, 0, r], x_buf.at[half], r, xsem.at[half]))
            return c

        lax.fori_loop(0, EXPERT_TM, body, 0, unroll=8)

    def weight_copies(expert, half):
        e = layer_ref[0] * MOE_EXPERTS + expert
        return [pltpu.make_async_copy(src.at[e], dst.at[half], sem.at[half])
                for src, dst in ((wg_hbm, wg_buf), (wu_hbm, wu_buf), (wd_hbm, wd_buf))]

    @pl.when(i < n_ref[0])
    def _():
        @pl.when(items_ref[IT_FIRST, i] == 1)
        def _():
            half = items_ref[IT_PARITY, i]
            own = weight_copies(items_ref[IT_EXPERT, i], half)

            @pl.when(i == 0)
            def _():
                for cp in own:
                    cp.start()

            for cp in own:
                cp.wait()
            wg_scr[...] = wg_buf[half].astype(bf16)
            wu_scr[...] = wu_buf[half].astype(bf16)
            wd_scr[...] = wd_buf[half].astype(bf16)
            nxt = items_ref[IT_NEXT, i]

            @pl.when(nxt >= 0)
            def _():
                for cp in weight_copies(nxt, 1 - half):
                    cp.start()

        tile = items_ref[IT_TILE, i]
        xhalf = tile % 2

        lo = items_ref[IT_LO, i]

        @pl.when(lo == 0)
        def _():
            @pl.when(i == 0)
            def _():
                gather(src_ref, 0, lambda cp: cp.start())

            gather(src_ref, xhalf, lambda cp: cp.wait())

        x = x_buf[xhalf].astype(bf16)
        hg = _dot(x, wg_scr[...])
        hu = _dot(x, wu_scr[...])
        out = _dot((hg * _sigmoid(hg) * hu).astype(bf16), wd_scr[...])
        prefetch = (lo == 0) & (tile + 1 < n_tiles)
        for r in range(EXPERT_TM):
            @pl.when(prefetch)
            def _():
                _row_copy(x_hbm, src_next_ref[0, 0, r], x_buf.at[1 - xhalf], r, xsem.at[1 - xhalf]).start()
        row = lax.broadcasted_iota(jnp.int32, (EXPERT_TM, 1), 0)
        mine = (row >= lo) & (row < items_ref[IT_HI, i])

        @pl.when(lo == 0)
        def _():
            os_ref[...] = jnp.where(mine, out, 0.0)

        @pl.when(lo != 0)
        def _():
            os_ref[...] = jnp.where(mine, out, os_ref[...])


def _experts_call(xn, src, items, n_items, layer, wg, wu, wd):
    n_tiles = src.shape[0]
    rows = lambda i, items, n, layer: (items[IT_TILE, i], 0)
    src_spec = lambda m: pl.BlockSpec((1, 1, EXPERT_TM), m, memory_space=pltpu.SMEM)
    hbm = pl.BlockSpec(memory_space=pl.ANY)
    return pl.pallas_call(
        _experts_kernel,
        grid_spec=pltpu.PrefetchScalarGridSpec(
            num_scalar_prefetch=3,
            grid=(items.shape[1],),
            in_specs=[src_spec(lambda i, items, n, layer: (items[IT_TILE, i], 0, 0)),
                      src_spec(lambda i, items, n, layer: (jnp.minimum(items[IT_TILE, i] + 1, n_tiles - 1), 0, 0)),
                      hbm, hbm, hbm, hbm],
            out_specs=pl.BlockSpec((EXPERT_TM, D_MODEL), rows),
            scratch_shapes=[pltpu.VMEM((2, EXPERT_TM, D_MODEL), f32),
                            pltpu.VMEM((2, D_MODEL, D_EXPERT), f32), pltpu.VMEM((2, D_MODEL, D_EXPERT), f32),
                            pltpu.VMEM((2, D_EXPERT, D_MODEL), f32),
                            pltpu.VMEM((D_MODEL, D_EXPERT), bf16), pltpu.VMEM((D_MODEL, D_EXPERT), bf16),
                            pltpu.VMEM((D_EXPERT, D_MODEL), bf16),
                            pltpu.SemaphoreType.DMA((2,)), pltpu.SemaphoreType.DMA((2,))],
        ),
        out_shape=jax.ShapeDtypeStruct((2 * xn.shape[0], D_MODEL), f32),
        compiler_params=_cp(("arbitrary",)),
        name="moe_experts",
    )(items, n_items, layer, src, src, xn, wg, wu, wd)


def _combine_kernel(pos_ref, pos_next_ref, x_ref, route_ref, gn_ref, os_ref, *rest, emit_x, split):
    outs, (buf, sem, xn_scr) = rest[:-3], rest[-3:]
    i = pl.program_id(0)
    half = i % 2
    last = i == pl.num_programs(0) - 1

    def copy(p_ref, hf, r, s):
        return _row_copy(os_ref, p_ref[0, 0, s * MOVE_TILE + r], buf.at[hf, s], r, sem.at[hf])

    def loop_all(p_ref, hf, op):
        def body(r, c):
            for s in range(2):
                op(copy(p_ref, hf, r, s), s)
            return c

        lax.fori_loop(0, MOVE_TILE, body, 0, unroll=8)

    begin = lambda cp, s: cp.start(priority=s)
    finish = lambda cp, s: cp.wait()

    @pl.when(i == 0)
    def _():
        loop_all(pos_ref, 0, begin)

    loop_all(pos_ref, half, finish)
    xn_ref = xn_scr if split else outs[-1]
    for c in range(MOVE_TILE // COMBINE_CHUNK):
        rows = slice(c * COMBINE_CHUNK, (c + 1) * COMBINE_CHUNK)
        route = route_ref[rows, :]
        x2 = x_ref[rows, :] + route[:, 4:5] * buf[half, 0, rows, :] + route[:, 5:6] * buf[half, 1, rows, :]
        if emit_x:
            outs[0][rows, :] = x2
        xn_ref[rows, :] = _rms(x2, gn_ref[...]).astype(xn_ref.dtype)
        for r in range(rows.start, rows.stop):
            for s in range(2):
                begin(copy(pos_next_ref, 1 - half, r, s), s)

    @pl.when(last)
    def _():
        loop_all(pos_next_ref, 1 - half, finish)

    if split:
        main_ref, tail_ref = outs[-2:]
        is_main = i < N_MAIN // MOVE_TILE

        @pl.when(is_main)
        def _():
            main_ref[...] = xn_scr[...]

        @pl.when(jnp.logical_not(is_main))
        def _():
            tail_ref[...] = xn_scr[...]


def _combine_call(x1, route, pos, os_rows, gn, xn_dtype, *, emit_x, split):
    n = x1.shape[0]
    row = lambda i: (i, 0)
    blk = lambda m: pl.BlockSpec((MOVE_TILE, D_MODEL), m)
    nm = N_MAIN // MOVE_TILE
    out_specs, out_shape = [], []
    if emit_x:
        out_specs.append(blk(row))
        out_shape.append(jax.ShapeDtypeStruct((n, D_MODEL), f32))
    if split:
        out_specs += [blk(lambda i: (jnp.minimum(i, nm - 1), 0)), blk(lambda i: (jnp.maximum(i - nm, 0), 0))]
        out_shape += [jax.ShapeDtypeStruct((N_MAIN, D_MODEL), xn_dtype),
                      jax.ShapeDtypeStruct((n - N_MAIN, D_MODEL), xn_dtype)]
    else:
        out_specs.append(blk(row))
        out_shape.append(jax.ShapeDtypeStruct((n, D_MODEL), xn_dtype))
    nt = n // MOVE_TILE
    pos_spec = lambda m: pl.BlockSpec((1, 1, 2 * MOVE_TILE), m, memory_space=pltpu.SMEM)
    return pl.pallas_call(
        functools.partial(_combine_kernel, emit_x=emit_x, split=split),
        grid=(nt,),
        in_specs=[pos_spec(lambda i: (i, 0, 0)), pos_spec(lambda i: (jnp.minimum(i + 1, nt - 1), 0, 0)),
                  blk(row), pl.BlockSpec((MOVE_TILE, LANES), row),
                  pl.BlockSpec((1, D_MODEL), lambda i: (0, 0)), pl.BlockSpec(memory_space=pl.ANY)],
        out_specs=out_specs,
        out_shape=out_shape,
        scratch_shapes=[pltpu.VMEM((2, 2, MOVE_TILE, D_MODEL), f32), pltpu.SemaphoreType.DMA((2,)),
                        pltpu.VMEM((MOVE_TILE, D_MODEL), xn_dtype)],
        compiler_params=_cp(("arbitrary",)),
        name="moe_combine",
    )(pos, pos, x1, route, gn, os_rows)


S5_GROUP = 16
S5_GROUPS = D_MODEL // S5_GROUP
S5_STATE = 64
S5_CB = 128
S5_NB = D_MODEL // S5_CB
S5_GPB = S5_CB // S5_GROUP
S5_SB = S5_GPB * S5_STATE
S5_BC = S5_STATE * S5_GROUP
S5_CPS = 2


def _dot_exact01(x, sel):
    hi, mid, lo = _split3(x)
    return _dot(hi, sel) + _dot(mid, sel) + _dot(lo, sel)


def _s5_prep_kernel(lr_ref, li_ref, ldt_ref, bre_ref, bim_ref, cre_ref, cim_ref, bb2t_ref, cct_ref, abv_ref):
    lr, li = lr_ref[...], li_ref[...]
    dt = jnp.exp(ldt_ref[...])
    mag = jnp.exp(lr * dt)
    ang = li * dt
    ab_re, ab_im = mag * jnp.cos(ang), mag * jnp.sin(ang)
    nr, ni = ab_re - 1.0, ab_im
    den = lr * lr + li * li
    f_re = (nr * lr + ni * li) / den
    f_im = (ni * lr - nr * li) / den
    ab2_re = ab_re * ab_re - ab_im * ab_im
    ab2_im = 2.0 * (ab_re * ab_im)

    def iota(shape, axis):
        return lax.broadcasted_iota(jnp.int32, shape, axis)

    def as_col(v):
        rep = jnp.concatenate([jnp.broadcast_to(v[g:g + 1, :], (S5_STATE, S5_STATE)) for g in range(S5_GPB)], axis=0)
        pick = iota((S5_SB, S5_STATE), 1) == iota((S5_SB, S5_STATE), 0) % S5_STATE
        return jnp.sum(jnp.where(pick, rep, 0.0), axis=1, keepdims=True)

    tile_c = jnp.where(iota((S5_GROUP, S5_CB), 1) % S5_GROUP == iota((S5_GROUP, S5_CB), 0), 1.0, 0.0).astype(bf16)
    tile_p = jnp.where(iota((S5_STATE, S5_SB), 1) % S5_STATE == iota((S5_STATE, S5_SB), 0), 1.0, 0.0).astype(bf16)
    own_c = iota((S5_SB, S5_CB), 1) // S5_GROUP == iota((S5_SB, S5_CB), 0) // S5_STATE
    own_p = iota((S5_CB, S5_SB), 1) // S5_STATE == iota((S5_CB, S5_SB), 0) // S5_GROUP

    fr, fi, ar, ai = as_col(f_re), as_col(f_im), as_col(ab_re), as_col(ab_im)
    br, bi = bre_ref[...], bim_ref[...]
    bb_re = fr * br - fi * bi
    bb_im = fr * bi + fi * br
    abb_re = ar * bb_re - ai * bb_im
    abb_im = ar * bb_im + ai * bb_re
    spread_c = lambda m: jnp.where(own_c, _dot_exact01(m, tile_c), 0.0)
    spread_p = lambda m: jnp.where(own_p, _dot_exact01(m, tile_p), 0.0)
    bb2t_ref[0] = jnp.concatenate([
        jnp.concatenate([spread_c(bb_re), spread_c(abb_re)], axis=1),
        jnp.concatenate([spread_c(bb_im), spread_c(abb_im)], axis=1)], axis=0).astype(bf16)
    cct_ref[0] = jnp.concatenate([spread_p(cre_ref[...]), -spread_p(cim_ref[...])], axis=1).astype(bf16)
    grp = iota((S5_GPB, S5_SB), 1) // S5_STATE == iota((S5_GPB, S5_SB), 0)
    as_row = lambda v: jnp.sum(jnp.where(grp, _dot_exact01(v, tile_p), 0.0), axis=0, keepdims=True)
    rows = [as_row(v) for v in (ab_re, ab_im, ab2_re, ab2_im)]
    abv_ref[0] = jnp.concatenate(rows + [jnp.zeros((SUB - len(rows), S5_SB), f32)], axis=0)


def _s5_weights(lam_re, lam_im, log_dt, b_re, b_im, c_re, c_im):
    blk = lambda r, c: pl.BlockSpec((r, c), lambda j: (j, 0))
    out = lambda r, c: pl.BlockSpec((1, r, c), lambda j: (j, 0, 0))
    return pl.pallas_call(
        _s5_prep_kernel,
        grid=(S5_NB,),
        in_specs=[blk(S5_GPB, S5_STATE), blk(S5_GPB, S5_STATE), blk(S5_GPB, 1),
                  blk(S5_SB, S5_GROUP), blk(S5_SB, S5_GROUP), blk(S5_CB, S5_STATE), blk(S5_CB, S5_STATE)],
        out_specs=[out(2 * S5_SB, 2 * S5_CB), out(S5_CB, 2 * S5_SB), out(SUB, S5_SB)],
        out_shape=[jax.ShapeDtypeStruct((S5_NB, 2 * S5_SB, 2 * S5_CB), bf16),
                   jax.ShapeDtypeStruct((S5_NB, S5_CB, 2 * S5_SB), bf16),
                   jax.ShapeDtypeStruct((S5_NB, SUB, S5_SB), f32)],
        compiler_params=_cp(("arbitrary",)),
        name="s5_prep",
    )(lam_re, lam_im, log_dt[:, None], b_re.reshape(-1, S5_GROUP), b_im.reshape(-1, S5_GROUP),
      c_re.reshape(-1, S5_STATE), c_im.reshape(-1, S5_STATE))


def _s5_seq_kernel(x0_ref, x1_ref, x2_ref, x3_ref, halo0_ref, h0_ref, bb2_ref, cc_ref, abv_ref, y_ref, hout_ref,
                   xf_scr, xp_scr, bu_scr, hs_scr, yp_scr, yn_scr, h_scr, halo_scr):
    tb = pl.program_id(1)
    TL = x0_ref.shape[0]
    KB = TL // 2
    RB = BATCH * TL

    @pl.when(tb == 0)
    def _():
        h_scr[...] = h0_ref[...]
        halo_scr[...] = halo0_ref[...].astype(f32)

    chans = [slice(c * S5_CB, (c + 1) * S5_CB) for c in range(S5_CPS)]
    for c, ch in enumerate(chans):
        for b, xb_ref in enumerate((x0_ref, x1_ref, x2_ref, x3_ref)):
            xf_scr[c, b * TL:(b + 1) * TL, :] = xb_ref[:, ch].astype(f32)
        for b in range(BATCH):
            for p in range(2):
                xp_scr[c, pl.ds(2 * b + p, KB, stride=SUB), :] = xf_scr[c, pl.ds(b * TL + p, KB, stride=2), :]
    x = jnp.concatenate([xp_scr[c] for c in range(S5_CPS)], axis=1)
    xc = jnp.concatenate([halo_scr[...], x], axis=0)
    odd = (lax.broadcasted_iota(jnp.int32, (RB + SUB, 1), 0) & 1) == 1
    xprev = jnp.where(odd, pltpu.roll(xc, 1, axis=0), pltpu.roll(xc, SUB - 1, axis=0))[SUB:]
    halo_scr[...] = x[RB - SUB:]
    for c, ch in enumerate(chans):
        lhs = jnp.concatenate([x[:, ch], xprev[:, ch]], axis=1).astype(bf16)
        bu_scr[c] = _dot_nt(lhs, bb2_ref[c])
    a2 = [(abv_ref[c, 2:3, :], abv_ref[c, 3:4, :]) for c in range(S5_CPS)]

    def step(k, carry):
        r0 = pl.multiple_of(k * SUB, SUB)
        out = []
        for c in range(S5_CPS):
            hr, hi = carry[2 * c], carry[2 * c + 1]
            a2r, a2i = a2[c]
            bu = bu_scr[c, pl.ds(r0, SUB), :]
            nr = a2r * hr - a2i * hi + bu[:, :S5_SB]
            ni = a2r * hi + a2i * hr + bu[:, S5_SB:]
            hs_scr[c, pl.ds(r0, SUB), :S5_SB] = nr
            hs_scr[c, pl.ds(r0, SUB), S5_SB:] = ni
            out += [nr, ni]
        return tuple(out)

    init = []
    for c in range(S5_CPS):
        init += [h_scr[c, :, :S5_SB], h_scr[c, :, S5_SB:]]
    fin = lax.fori_loop(0, RB // SUB, step, tuple(init), unroll=4)
    for c, ch in enumerate(chans):
        h_scr[c, :, :S5_SB] = fin[2 * c]
        h_scr[c, :, S5_SB:] = fin[2 * c + 1]
        yp_scr[c] = _dot_nt(hs_scr[c].astype(bf16), cc_ref[c])
        for b in range(BATCH):
            for p in range(2):
                yn_scr[c, pl.ds(b * TL + p, KB, stride=2), :] = yp_scr[c, pl.ds(2 * b + p, KB, stride=SUB), :]
            y_ref[b, :, ch] = yn_scr[c, b * TL:(b + 1) * TL, :]

    @pl.when(tb == pl.num_programs(1) - 1)
    def _():
        hout_ref[...] = h_scr[...]


def _s5_seq_call(x, halo0, h0, bb2, cc, abv, *, row0, seq_len, tl):
    wsel = lambda j, t: (j, 0, 0)
    rb = BATCH * tl
    cw = S5_CPS * S5_CB
    xspec = lambda b: pl.BlockSpec((tl, cw), lambda j, t: ((row0 + b * seq_len) // tl + t, j))
    return pl.pallas_call(
        _s5_seq_kernel,
        grid=(S5_NB // S5_CPS, seq_len // tl),
        in_specs=[xspec(b) for b in range(BATCH)]
        + [pl.BlockSpec((SUB, cw), lambda j, t: (0, j)),
           pl.BlockSpec((S5_CPS, SUB, 2 * S5_SB), wsel),
           pl.BlockSpec((S5_CPS, 2 * S5_SB, 2 * S5_CB), wsel),
           pl.BlockSpec((S5_CPS, S5_CB, 2 * S5_SB), wsel),
           pl.BlockSpec((S5_CPS, SUB, S5_SB), wsel)],
        out_specs=[pl.BlockSpec((BATCH, tl, cw), lambda j, t: (0, t, j)),
                   pl.BlockSpec((S5_CPS, SUB, 2 * S5_SB), wsel)],
        out_shape=[jax.ShapeDtypeStruct((BATCH, seq_len, D_MODEL), f32),
                   jax.ShapeDtypeStruct((S5_NB, SUB, 2 * S5_SB), f32)],
        scratch_shapes=[pltpu.VMEM((S5_CPS, rb, S5_CB), f32), pltpu.VMEM((S5_CPS, rb, S5_CB), f32),
                        pltpu.VMEM((S5_CPS, rb, 2 * S5_SB), f32), pltpu.VMEM((S5_CPS, rb, 2 * S5_SB), f32),
                        pltpu.VMEM((S5_CPS, rb, S5_CB), f32), pltpu.VMEM((S5_CPS, rb, S5_CB), f32),
                        pltpu.VMEM((S5_CPS, SUB, 2 * S5_SB), f32), pltpu.VMEM((SUB, cw), f32)],
        compiler_params=_cp(("arbitrary", "arbitrary")),
        name=f"s5_seq_{seq_len}",
    )(x, x, x, x, halo0, h0, bb2, cc, abv)


def _s5_sample_kernel(x_ref, hre_ref, him_ref, bb2_ref, cc_ref, abv_ref, y_ref, ore_ref, oim_ref, hs_scr):
    nb = hre_ref.shape[0]
    bu = _dot_nt(x_ref[...], bb2_ref[0, :, :S5_CB])
    ar = abv_ref[0, 0:1, :]
    ai = abv_ref[0, 1:2, :]
    hr, hi = hre_ref[...], him_ref[...]
    for t in range(DEC_SEQ):
        rows = slice(t * nb, (t + 1) * nb)
        hr, hi = (ar * hr - ai * hi + bu[rows, :S5_SB], ar * hi + ai * hr + bu[rows, S5_SB:])
        hs_scr[rows, :S5_SB] = hr
        hs_scr[rows, S5_SB:] = hi
    y_ref[...] = _dot_nt(hs_scr[...].astype(bf16), cc_ref[0])
    ore_ref[...] = hr
    oim_ref[...] = hi


def _s5_sample_call(xt, h_re, h_im, bb2, cc, abv):
    n = xt.shape[0]
    nb = h_re.shape[0]
    wsel = lambda j: (j, 0, 0)
    st = pl.BlockSpec((nb, S5_SB), lambda j: (0, j))
    return pl.pallas_call(
        _s5_sample_kernel,
        grid=(S5_NB,),
        in_specs=[pl.BlockSpec((n, S5_CB), lambda j: (0, j)), st, st,
                  pl.BlockSpec((1, 2 * S5_SB, 2 * S5_CB), wsel),
                  pl.BlockSpec((1, S5_CB, 2 * S5_SB), wsel),
                  pl.BlockSpec((1, SUB, S5_SB), wsel)],
        out_specs=[pl.BlockSpec((n, S5_CB), lambda j: (0, j)), st, st],
        out_shape=[jax.ShapeDtypeStruct((n, D_MODEL), f32),
                   jax.ShapeDtypeStruct(h_re.shape, f32), jax.ShapeDtypeStruct(h_im.shape, f32)],
        scratch_shapes=[pltpu.VMEM((n, 2 * S5_SB), f32)],
        compiler_params=_cp(("arbitrary",)),
        name="s5_sample",
    )(xt, h_re, h_im, bb2, cc, abv)


GLA_CHUNK = 256
GLA_DIRECT = 2
S5_TL = 512


def _moe_layer(layer, x1, xnf, route, plan, cnt, wg, wu, wd, gn, xn_dtype, *, emit_x, split):
    pos, items, n_items = _moe_plan(plan, cnt, x1.shape[0])
    src = _source_rows(pos, x1.shape[0])
    os_rows = _experts_call(xnf, src, items, n_items, jnp.full((1,), layer, jnp.int32), wg, wu, wd)
    return _combine_call(x1, route, pos, os_rows, gn, xn_dtype, emit_x=emit_x, split=split)


def kernel(x_prompt, x_sample, state_gla, state_s5_re, state_s5_im, meta_tokens, norm_mix_g, norm_ffn_g, norm_final_g, gla_w_in, gla_w_a2, gla_b_a, gla_g_o, gla_w_o, s5_lambda_re, s5_lambda_im, s5_log_dt, s5_b_re, s5_b_im, s5_c_re, s5_c_im, s5_d, s5_w_glu, s5_b_glu, moe_w_rg, moe_b_rg, moe_w_re, moe_b_re, moe_w_gate, moe_w_up, moe_w_down):
    row = lambda v: v.reshape(1, -1)
    x_main = x_prompt.reshape(N_MAIN, D_MODEL)
    x_tail = jnp.concatenate([
        jnp.tile(meta_tokens.astype(x_prompt.dtype), (BATCH, 1)),
        x_sample.reshape(N_SAMPLE, D_MODEL),
        jnp.zeros((N_ROWS - N_REAL, D_MODEL), x_prompt.dtype)], axis=0)
    wg = moe_w_gate.reshape(-1, D_MODEL, D_EXPERT)
    wu = moe_w_up.reshape(-1, D_MODEL, D_EXPERT)
    wd = moe_w_down.reshape(-1, D_EXPERT, D_MODEL)

    w_in = jnp.swapaxes(gla_w_in, 1, 2)
    wa1 = jnp.pad(gla_w_in[0, :, GLA_QKVR:], ((0, 0), (0, LANES - GLA_RANK)))
    wa2 = jnp.pad(gla_w_a2.reshape(GLA_RANK, GLA_KEY), ((0, LANES - GLA_RANK), (0, 0)))
    xn, glog = _norm_gate_call(x_main, x_tail, row(norm_mix_g[0]), wa1, wa2, row(gla_b_a))
    proj = _proj_call(xn, w_in, GLA_QKVR)
    go = row(gla_g_o)
    og = jnp.zeros((N_ROWS, GLA_VAL), bf16)
    s_zero = jnp.zeros((BATCH, GLA_HEADS, GLA_DK, GLA_DV), f32)
    og, s_meta = _gla_seq_call(proj, glog, go, s_zero, og, row0=ROW_META, C=N_META, n_chunks=1, d=GLA_DIRECT)
    og, s_prompt = _gla_seq_call(proj, glog, go, s_meta, og, row0=0, C=GLA_CHUNK, n_chunks=SEQ // GLA_CHUNK,
                                 d=GLA_DIRECT)
    og, s_sample = _gla_sample_call(proj, glog, go, state_gla.reshape(DEC_BATCH, GLA_HEADS, GLA_DK, GLA_DV), og,
                                    row0=ROW_SAMPLE)
    wr, br = _router_weights(moe_w_rg[0], moe_b_rg[0], moe_w_re[0], moe_b_re[0])
    tile_row = lambda i: (i, 0)
    x1, xnf, route, plan, cnt = _mix_out_call(
        _gla_out_kernel, "gla_out", N_ROWS, [og, x_main, x_tail], [tile_row, *_main_tail_maps(MIX_TILE)],
        [gla_w_o.reshape(GLA_VAL, D_MODEL), row(norm_ffn_g[0]), wr, br])
    x2, xn2 = _moe_layer(0, x1, xnf, route, plan, cnt, wg, wu, wd, row(norm_mix_g[1]), bf16, emit_x=True, split=False)

    bb2, cc, abv = _s5_weights(s5_lambda_re[0], s5_lambda_im[0], s5_log_dt[0], s5_b_re[0], s5_b_im[0],
                               s5_c_re[0], s5_c_im[0])
    y_meta, h_meta = _s5_seq_call(xn2, jnp.zeros((SUB, D_MODEL), bf16), jnp.zeros((S5_NB, SUB, 2 * S5_SB), f32),
                                  bb2, cc, abv, row0=ROW_META, seq_len=N_META, tl=N_META)
    halo = xn2[ROW_META:ROW_SAMPLE].reshape(BATCH, N_META, D_MODEL)[:, N_META - 2:].reshape(SUB, D_MODEL)
    y_main, h_main = _s5_seq_call(xn2, halo, h_meta, bb2, cc, abv, row0=0, seq_len=SEQ, tl=S5_TL)
    xt_sample = xn2[ROW_SAMPLE:N_REAL].reshape(DEC_BATCH, DEC_SEQ, D_MODEL).transpose(1, 0, 2).reshape(N_SAMPLE, D_MODEL)
    y_samp, s5r_s, s5i_s = _s5_sample_call(
        xt_sample, state_s5_re.reshape(DEC_BATCH, S5_GROUPS * S5_STATE),
        state_s5_im.reshape(DEC_BATCH, S5_GROUPS * S5_STATE), bb2, cc, abv)
    ys_tail = jnp.concatenate([
        y_meta.reshape(N_METAROWS, D_MODEL),
        y_samp.reshape(DEC_SEQ, DEC_BATCH, D_MODEL).transpose(1, 0, 2).reshape(N_SAMPLE, D_MODEL),
        jnp.zeros((N_ROWS - N_REAL, D_MODEL), f32)], axis=0)
    wr, br = _router_weights(moe_w_rg[1], moe_b_rg[1], moe_w_re[1], moe_b_re[1])
    x3, xnf, route, plan, cnt = _mix_out_call(
        _s5_out_kernel, "s5_out", N_ROWS, [y_main.reshape(N_MAIN, D_MODEL), ys_tail, x2],
        [*_main_tail_maps(MIX_TILE), tile_row],
        [s5_w_glu.reshape(D_MODEL, D_MODEL), row(norm_mix_g[1]), row(s5_d), row(s5_b_glu),
         row(norm_ffn_g[1]), wr, br])
    y_main_out, y_tail_out = _moe_layer(1, x3, xnf, route, plan, cnt, wg, wu, wd, row(norm_final_g), f32,
                                        emit_x=False, split=True)

    y_prompt = y_main_out.reshape(BATCH, SEQ, D_MODEL)
    y_sample = y_tail_out[N_METAROWS:N_METAROWS + N_SAMPLE].reshape(DEC_BATCH, DEC_SEQ, D_MODEL)
    hfin = h_main.reshape(S5_NB, BATCH, 2, 2, S5_GPB, S5_STATE)[:, :, 1]
    s5r_p = hfin[:, :, 0].transpose(1, 0, 2, 3).reshape(1, BATCH, S5_GROUPS, S5_STATE)
    s5i_p = hfin[:, :, 1].transpose(1, 0, 2, 3).reshape(1, BATCH, S5_GROUPS, S5_STATE)
    return (y_prompt, y_sample, s_prompt[None], s5r_p, s5i_p, s_sample[None],
            s5r_s.reshape(1, DEC_BATCH, S5_GROUPS, S5_STATE), s5i_s.reshape(1, DEC_BATCH, S5_GROUPS, S5_STATE))
```

```python
import functools

import jax
import jax.numpy as jnp
from jax import lax
from jax.experimental import pallas as pl
from jax.experimental.pallas import tpu as pltpu

f32 = jnp.float32
bf16 = jnp.bfloat16

D_MODEL = 2048
BATCH = 4
SEQ = 2048
DEC_BATCH = 128
DEC_SEQ = 4
N_META = 16
EPS = 1e-6
GLA_HEADS = 4
GLA_DK = 256
GLA_DV = 512
GLA_KEY = GLA_HEADS * GLA_DK
GLA_VAL = GLA_HEADS * GLA_DV
GLA_RANK = 16
GLA_TAU = 16.0
GLA_QKVR = 2 * GLA_KEY + 2 * GLA_VAL

N_MAIN = BATCH * SEQ
N_METAROWS = BATCH * N_META
N_SAMPLE = DEC_BATCH * DEC_SEQ
ROW_META = N_MAIN
ROW_SAMPLE = N_MAIN + N_METAROWS
N_REAL = ROW_SAMPLE + N_SAMPLE
ROW_TILE = 256
N_ROWS = -(-N_REAL // ROW_TILE) * ROW_TILE

VMEM_LIMIT = 56 * 1024 * 1024


def _cp(sem, vmem=VMEM_LIMIT):
    return pltpu.CompilerParams(dimension_semantics=sem, vmem_limit_bytes=vmem)


def _dot(a, b):
    return jnp.dot(a, b, preferred_element_type=f32)


def _dot_nt(a, b):
    return lax.dot_general(a, b, (((1,), (1,)), ((), ())), preferred_element_type=f32)


def _dot_tn(a, b):
    return lax.dot_general(a, b, (((0,), (0,)), ((), ())), preferred_element_type=f32)


def _sigmoid(x):
    return 1.0 / (1.0 + jnp.exp(-x))


def _split3(x):
    hi = x.astype(bf16)
    r1 = x - hi.astype(f32)
    mid = r1.astype(bf16)
    lo = (r1 - mid.astype(f32)).astype(bf16)
    return hi, mid, lo


def _cumsum_rows(g, C):
    if C <= 16:
        row = lax.broadcasted_iota(jnp.int32, (C, 1), 0)
        b = jnp.zeros_like(g)
        for s in range(C):
            b = b + jnp.where(row >= s, g[s:s + 1, :], 0.0)
        return b
    row = lax.broadcasted_iota(jnp.int32, (C, C), 0)
    col = lax.broadcasted_iota(jnp.int32, (C, C), 1)
    tri = jnp.where(row >= col, 1.0, 0.0).astype(bf16)
    hi, mid, lo = _split3(g)
    return _dot(tri, hi) + _dot(tri, mid) + _dot(tri, lo)


PAIR_LEVEL = 1000


def _gla_pair_code(C, d, tree=True):
    ti = lax.broadcasted_iota(jnp.int32, (C, C), 0)
    si = lax.broadcasted_iota(jnp.int32, (C, C), 1)
    code = jnp.where((ti // d == si // d) & (si <= ti), 1 + ti - si, 0)
    h = d
    while tree and h < C:
        tb = ti // h
        code = jnp.where(((tb % 2) == 1) & ((si // h) == tb - 1), PAIR_LEVEL + h, code)
        h *= 2
    return code


def _gla_scores(q, k, b, code, C, d, tree=True):
    row = lax.broadcasted_iota(jnp.int32, (C, 1), 0)
    scores = jnp.zeros((C, C), f32)
    for dl in range(d):
        ks = k if dl == 0 else pltpu.roll(k, dl, axis=0)
        bs = b if dl == 0 else pltpu.roll(b, dl, axis=0)
        term = q * ks * jnp.exp(jnp.minimum(b - bs, 0.0))
        colv = jnp.sum(term, axis=1, keepdims=True)
        scores = jnp.where(code == 1 + dl, colv, scores)
    z = b
    s = 1
    while tree and 2 * s < C:
        z = jnp.where((row & s) != 0, pltpu.roll(z, s, axis=0), z)
        s *= 2
        h = s
        if h < d:
            continue
        bnext = pltpu.roll(z, C - h, axis=0)
        qh = (q * jnp.exp(b - z)).astype(bf16)
        kh = (k * jnp.exp(jnp.minimum(bnext - b, 0.0))).astype(bf16)
        scores = jnp.where(code == PAIR_LEVEL + h, _dot_nt(qh, kh), scores)
    return scores


def _gla_chunk(q, k, v, g, S, code, C, d, tree=True):
    b = _cumsum_rows(g, C)
    o = _dot((q * jnp.exp(b)).astype(bf16), S.astype(bf16))
    scores = _gla_scores(q, k, b, code, C, d, tree)
    o = o + _dot(scores.astype(bf16), v)
    b_last = b[C - 1:C, :]
    kd = (k * jnp.exp(b_last - b)).astype(bf16)
    if C == GLA_DK:
        eye = code == 1
    else:
        eye = (lax.broadcasted_iota(jnp.int32, (GLA_DK, GLA_DK), 0)
               == lax.broadcasted_iota(jnp.int32, (GLA_DK, GLA_DK), 1))
    dec_col = jnp.sum(jnp.where(eye, jnp.exp(b_last), 0.0), axis=1, keepdims=True)
    S_new = dec_col * S + _dot_tn(kd, v)
    return o, S_new


def _gla_head_epilogue(o, r, go):
    ms = jnp.mean(o * o, axis=1, keepdims=True)
    on = o * lax.rsqrt(ms + EPS) * go
    return on * (r * _sigmoid(r))


def _gla_seq_kernel(q_ref, k_ref, v_ref, r_ref, g_ref, go_ref, s0_ref, _og_in, og_ref, sout_ref, s_scr, *, C, d):
    c = pl.program_id(1)

    @pl.when(c == 0)
    def _():
        s_scr[...] = s0_ref[0]

    code = _gla_pair_code(C, d)

    def head(h, carry):
        ck = pl.ds(pl.multiple_of(h * GLA_DK, GLA_DK), GLA_DK)
        cv = pl.ds(pl.multiple_of(h * GLA_DV, GLA_DV), GLA_DV)
        q = q_ref[:, ck].astype(f32) * (GLA_DK ** -0.5)
        k = k_ref[:, ck].astype(f32)
        o, S_new = _gla_chunk(q, k, v_ref[:, cv], g_ref[:, ck], s_scr[h], code, C, d)
        s_scr[h] = S_new
        og_ref[:, cv] = _gla_head_epilogue(o, r_ref[:, cv].astype(f32), go_ref[:, cv]).astype(og_ref.dtype)
        return carry

    lax.fori_loop(0, GLA_HEADS, head, 0)

    @pl.when(c == pl.num_programs(1) - 1)
    def _():
        sout_ref[0] = s_scr[...]


def _gla_seq_call(proj, glog, go, s0, og_buf, *, row0, C, n_chunks, d):
    blk0 = row0 // C
    rows = lambda b, c: blk0 + b * n_chunks + c
    kern = functools.partial(_gla_seq_kernel, C=C, d=d)
    return pl.pallas_call(
        kern,
        grid=(BATCH, n_chunks),
        in_specs=[
            pl.BlockSpec((C, GLA_KEY), lambda b, c: (rows(b, c), 0)),
            pl.BlockSpec((C, GLA_KEY), lambda b, c: (rows(b, c), 1)),
            pl.BlockSpec((C, GLA_VAL), lambda b, c: (rows(b, c), 1)),
            pl.BlockSpec((C, GLA_VAL), lambda b, c: (rows(b, c), 2)),
            pl.BlockSpec((C, GLA_KEY), lambda b, c: (rows(b, c), 0)),
            pl.BlockSpec((1, GLA_VAL), lambda b, c: (0, 0)),
            pl.BlockSpec((1, GLA_HEADS, GLA_DK, GLA_DV), lambda b, c: (b, 0, 0, 0)),
            pl.BlockSpec(memory_space=pl.ANY),
        ],
        out_specs=[
            pl.BlockSpec((C, GLA_VAL), lambda b, c: (rows(b, c), 0)),
            pl.BlockSpec((1, GLA_HEADS, GLA_DK, GLA_DV), lambda b, c: (b, 0, 0, 0)),
        ],
        out_shape=[
            jax.ShapeDtypeStruct(og_buf.shape, og_buf.dtype),
            jax.ShapeDtypeStruct((BATCH, GLA_HEADS, GLA_DK, GLA_DV), f32),
        ],
        scratch_shapes=[pltpu.VMEM((GLA_HEADS, GLA_DK, GLA_DV), f32)],
        input_output_aliases={7: 0},
        compiler_params=_cp(("arbitrary", "arbitrary")),
        name=f"gla_seq_c{C}",
    )(proj, proj, proj, proj, glog, go, s0, og_buf)


SAMPLE_BB = 4
SAMPLE_C = SAMPLE_BB * DEC_SEQ


def _gla_sample_kernel(q_ref, k_ref, v_ref, r_ref, g_ref, go_ref, s0_ref, _og_in, og_ref, sout_ref):
    row = lax.broadcasted_iota(jnp.int32, (SAMPLE_C, 1), 0)

    code = _gla_pair_code(SAMPLE_C, DEC_SEQ, tree=False)

    def take(x, bb):
        sh = (SAMPLE_C - DEC_SEQ * bb) % SAMPLE_C
        return jnp.where(row < DEC_SEQ, pltpu.roll(x, sh, axis=0) if sh else x, 0.0)

    def head(h, carry):
        ck = pl.ds(pl.multiple_of(h * GLA_DK, GLA_DK), GLA_DK)
        cv = pl.ds(pl.multiple_of(h * GLA_DV, GLA_DV), GLA_DV)
        q_all = q_ref[:, ck].astype(f32) * (GLA_DK ** -0.5)
        k_all = k_ref[:, ck].astype(f32)
        v_all = v_ref[:, cv].astype(f32)
        r_all = r_ref[:, cv].astype(f32)
        g_all = g_ref[:, ck]
        go = go_ref[:, cv]
        acc = jnp.zeros((SAMPLE_C, GLA_DV), f32)
        for bb in range(SAMPLE_BB):
            o, S_new = _gla_chunk(take(q_all, bb), take(k_all, bb), take(v_all, bb).astype(bf16),
                                  take(g_all, bb), s0_ref[bb, h], code, SAMPLE_C, DEC_SEQ, tree=False)
            sout_ref[bb, h] = S_new
            y = _gla_head_epilogue(o, take(r_all, bb), go)
            acc = jnp.where(row // DEC_SEQ == bb, pltpu.roll(y, DEC_SEQ * bb, axis=0) if bb else y, acc)
        og_ref[:, cv] = acc.astype(og_ref.dtype)
        return carry

    lax.fori_loop(0, GLA_HEADS, head, 0)


def _gla_sample_call(proj, glog, go, s0, og_buf, *, row0):
    n_seq = s0.shape[0]
    blk0 = row0 // SAMPLE_C
    st_spec = pl.BlockSpec((SAMPLE_BB, GLA_HEADS, GLA_DK, GLA_DV), lambda i: (i, 0, 0, 0))
    return pl.pallas_call(
        _gla_sample_kernel,
        grid=(n_seq // SAMPLE_BB,),
        in_specs=[
            pl.BlockSpec((SAMPLE_C, GLA_KEY), lambda i: (blk0 + i, 0)),
            pl.BlockSpec((SAMPLE_C, GLA_KEY), lambda i: (blk0 + i, 1)),
            pl.BlockSpec((SAMPLE_C, GLA_VAL), lambda i: (blk0 + i, 1)),
            pl.BlockSpec((SAMPLE_C, GLA_VAL), lambda i: (blk0 + i, 2)),
            pl.BlockSpec((SAMPLE_C, GLA_KEY), lambda i: (blk0 + i, 0)),
            pl.BlockSpec((1, GLA_VAL), lambda i: (0, 0)),
            st_spec,
            pl.BlockSpec(memory_space=pl.ANY),
        ],
        out_specs=[pl.BlockSpec((SAMPLE_C, GLA_VAL), lambda i: (blk0 + i, 0)), st_spec],
        out_shape=[jax.ShapeDtypeStruct(og_buf.shape, og_buf.dtype), jax.ShapeDtypeStruct(s0.shape, f32)],
        input_output_aliases={7: 0},
        compiler_params=_cp(("arbitrary",)),
        name="gla_sample",
    )(proj, proj, proj, proj, glog, go, s0, og_buf)


LANES = 128
SUB = 8
MOE_GROUPS = 4
MOE_EPG = 8
MOE_EXPERTS = MOE_GROUPS * MOE_EPG
D_EXPERT = 256
ROUTE_E0 = SUB
MIX_TILE = 256
PROJ_NT = 1024
PROJ_MT = N_ROWS // 7


def _rms(x, g):
    r = lax.rsqrt(jnp.mean(x * x, axis=-1, keepdims=True) + EPS)
    return (x * r) * g


def _log_sigmoid(z):
    return jnp.minimum(z, 0.0) - jnp.log1p(jnp.exp(-jnp.abs(z)))


def _main_or_tail(tile, main_ref, tail_ref):
    return jnp.where(pl.program_id(0) < N_MAIN // tile, main_ref[...], tail_ref[...])


def _main_tail_maps(tile):
    nm = N_MAIN // tile
    return (lambda i: (jnp.minimum(i, nm - 1), 0)), (lambda i: (jnp.maximum(i - nm, 0), 0))


def _norm_gate_kernel(xm_ref, xt_ref, gn_ref, wa1_ref, wa2_ref, ba_ref, xn_ref, gl_ref):
    xnb = _rms(_main_or_tail(ROW_TILE, xm_ref, xt_ref), gn_ref[...]).astype(bf16)
    xn_ref[...] = xnb
    a = _dot(xnb, wa1_ref[...].astype(bf16))
    z = _dot(a.astype(bf16), wa2_ref[...].astype(bf16)) + ba_ref[...]
    gl_ref[...] = _log_sigmoid(z) * (1.0 / GLA_TAU)


def _norm_gate_call(x_main, x_tail, gn, wa1, wa2, ba):
    n = x_main.shape[0] + x_tail.shape[0]
    row = lambda i: (i, 0)
    fix = lambda i: (0, 0)
    main_map, tail_map = _main_tail_maps(ROW_TILE)
    return pl.pallas_call(
        _norm_gate_kernel,
        grid=(n // ROW_TILE,),
        in_specs=[pl.BlockSpec((ROW_TILE, D_MODEL), main_map), pl.BlockSpec((ROW_TILE, D_MODEL), tail_map),
                  pl.BlockSpec((1, D_MODEL), fix),
                  pl.BlockSpec((D_MODEL, LANES), fix), pl.BlockSpec((LANES, GLA_KEY), fix),
                  pl.BlockSpec((1, GLA_KEY), fix)],
        out_specs=[pl.BlockSpec((ROW_TILE, D_MODEL), row), pl.BlockSpec((ROW_TILE, GLA_KEY), row)],
        out_shape=[jax.ShapeDtypeStruct((n, D_MODEL), bf16), jax.ShapeDtypeStruct((n, GLA_KEY), f32)],
        compiler_params=_cp(("arbitrary",)),
        name="norm_gate",
    )(x_main, x_tail, gn, wa1, wa2, ba)


def _proj_kernel(xn_ref, w_ref, o_ref, wb_scr):
    @pl.when(pl.program_id(1) == 0)
    def _():
        wb_scr[...] = w_ref[0].astype(bf16)

    o_ref[...] = _dot_nt(xn_ref[...], wb_scr[...]).astype(o_ref.dtype)


def _proj_call(xn, wt, n_cols):
    n = xn.shape[0]
    return pl.pallas_call(
        _proj_kernel,
        grid=(n_cols // PROJ_NT, n // PROJ_MT),
        in_specs=[pl.BlockSpec((PROJ_MT, D_MODEL), lambda j, i: (i, 0)),
                  pl.BlockSpec((1, PROJ_NT, D_MODEL), lambda j, i: (0, j, 0))],
        out_specs=pl.BlockSpec((PROJ_MT, PROJ_NT), lambda j, i: (i, j)),
        out_shape=jax.ShapeDtypeStruct((n, n_cols), bf16),
        scratch_shapes=[pltpu.VMEM((PROJ_NT, D_MODEL), bf16)],
        compiler_params=_cp(("arbitrary", "arbitrary")),
        name="gla_proj",
    )(xn, wt)


def _route(xn, wr, br, cnt_ref):
    R = xn.shape[0]
    xh = xn.astype(bf16)
    xl = (xn - xh.astype(f32)).astype(bf16)
    wh = wr.astype(bf16)
    wl = (wr - wh.astype(f32)).astype(bf16)
    hi_terms = _dot(xh, jnp.concatenate([wh, wl], axis=1))
    logits = hi_terms[:, :LANES] + hi_terms[:, LANES:] + _dot(xl, wh) + br
    lt = logits.T
    neg = -jnp.inf
    big = float(LANES)
    sub = lax.broadcasted_iota(jnp.int32, (SUB, R), 0)
    subf = sub.astype(f32)
    colmax = lambda a: jnp.max(a, axis=0, keepdims=True)
    first_at = lambda a, m: jnp.min(jnp.where(a == m, subf, big), axis=0, keepdims=True)
    groups = lt[0:SUB]
    lg = jnp.where(sub < MOE_GROUPS, groups, neg)
    mg = colmax(lg)
    gidx = first_at(lg, mg)
    ptop = 1.0 / jnp.sum(jnp.where(sub < MOE_GROUPS, jnp.exp(groups - mg), 0.0), axis=0, keepdims=True)
    le = lt[ROUTE_E0:ROUTE_E0 + MOE_EPG]
    for g in range(1, MOE_GROUPS):
        le = jnp.where(gidx == g, lt[ROUTE_E0 + g * MOE_EPG:ROUTE_E0 + (g + 1) * MOE_EPG], le)
    v1 = colmax(le)
    j1 = first_at(le, v1)
    le2 = jnp.where(subf == j1, neg, le)
    v2 = colmax(le2)
    j2 = first_at(le2, v2)
    s = jnp.exp(v2 - v1)
    w0 = ptop / (1.0 + s)
    w1 = ptop * s / (1.0 + s)
    e0 = gidx * MOE_EPG + j1
    e1 = gidx * MOE_EPG + j2
    col = lax.broadcasted_iota(jnp.int32, (LANES, R), 0).astype(f32)
    at0 = col == e0 + ROUTE_E0
    at1 = col == e1 + ROUTE_E0
    oh = jnp.where(at0 | at1, 1.0, 0.0)
    ri = lax.broadcasted_iota(jnp.int32, (R, R), 0)
    ci = lax.broadcasted_iota(jnp.int32, (R, R), 1)
    earlier = jnp.where(ri < ci, 1.0, 0.0).astype(bf16)
    tot = _dot(oh.astype(bf16), earlier) + cnt_ref[...]
    rank0 = jnp.sum(jnp.where(at0, tot, 0.0), axis=0, keepdims=True)
    rank1 = jnp.sum(jnp.where(at1, tot, 0.0), axis=0, keepdims=True)
    cnt_ref[...] = cnt_ref[...] + jnp.sum(oh, axis=1, keepdims=True)
    vals = (e0, e1, rank0, rank1, w0, w1)
    plan = jnp.zeros((SUB, R), f32)
    for j, v in enumerate(vals):
        plan = jnp.where(sub == j, v, plan)
    slab_t = jnp.concatenate([plan, jnp.zeros((LANES - SUB, R), f32)], axis=0)
    return slab_t.T, plan


def _router_weights(w_rg, b_rg, w_re, b_re):
    zeros = lambda k: jnp.zeros((D_MODEL, k), w_rg.dtype)
    w = jnp.concatenate([w_rg, zeros(ROUTE_E0 - MOE_GROUPS), jnp.moveaxis(w_re, 0, 1).reshape(D_MODEL, MOE_EXPERTS),
                         zeros(LANES - ROUTE_E0 - MOE_EXPERTS)], axis=1)
    b = jnp.concatenate([b_rg, jnp.zeros((ROUTE_E0 - MOE_GROUPS,), b_rg.dtype), b_re.reshape(MOE_EXPERTS),
                         jnp.zeros((LANES - ROUTE_E0 - MOE_EXPERTS,), b_rg.dtype)])
    return w, b[None]


def _mix_out_tail(x1, gffn_ref, wr_ref, br_ref, x1_ref, xn_ref, route_ref, plan_ref, cnt_ref):
    @pl.when(pl.program_id(0) == 0)
    def _():
        cnt_ref[...] = jnp.zeros_like(cnt_ref)

    x1_ref[...] = x1
    xn = _rms(x1, gffn_ref[...])
    xn_ref[...] = xn
    route_ref[...], plan_ref[...] = _route(xn, wr_ref[...], br_ref[...], cnt_ref)


def _cast_once(w_ref, w_scr):
    @pl.when(pl.program_id(0) == 0)
    def _():
        w_scr[...] = w_ref[...].astype(bf16)


def _gla_out_kernel(og_ref, xm_ref, xt_ref, wo_ref, gffn_ref, wr_ref, br_ref, *rest):
    outs, w_scr = rest[:-1], rest[-1]
    _cast_once(wo_ref, w_scr)
    x1 = _main_or_tail(MIX_TILE, xm_ref, xt_ref) + _dot(og_ref[...], w_scr[...])
    _mix_out_tail(x1, gffn_ref, wr_ref, br_ref, *outs)


def _gelu_tanh(x):
    return x * (0.5 * (1.0 + jnp.tanh(0.7978845608028654 * (x + 0.044715 * (x * x * x)))))


def _s5_out_kernel(ys_main_ref, ys_tail_ref, x_ref, wglu_ref, gmix_ref, d_ref, bglu_ref, gffn_ref, wr_ref, br_ref,
                   *rest):
    outs, w_scr = rest[:-1], rest[-1]
    _cast_once(wglu_ref, w_scr)
    x = x_ref[...]
    u = _rms(x, gmix_ref[...])
    y = _gelu_tanh(_main_or_tail(MIX_TILE, ys_main_ref, ys_tail_ref) + d_ref[...] * u)
    z = _dot(y.astype(bf16), w_scr[...]) + bglu_ref[...]
    _mix_out_tail(x + y * _sigmoid(z), gffn_ref, wr_ref, br_ref, *outs)


def _mix_out_call(kern, name, n, row_ins, row_maps, fix_ins):
    row = lambda i: (i, 0)
    fix = lambda i: (0, 0)
    out_row = lambda w: pl.BlockSpec((MIX_TILE, w), row)
    nt = n // MIX_TILE
    return pl.pallas_call(
        kern,
        grid=(nt,),
        in_specs=[pl.BlockSpec((MIX_TILE, a.shape[1]), m) for a, m in zip(row_ins, row_maps)]
        + [pl.BlockSpec(a.shape, fix, pipeline_mode=pl.Buffered(1)) for a in fix_ins],
        out_specs=[out_row(D_MODEL), out_row(D_MODEL), out_row(LANES), pl.BlockSpec((SUB, MIX_TILE), row),
                   pl.BlockSpec((LANES, 1), fix)],
        out_shape=[jax.ShapeDtypeStruct((n, D_MODEL), f32), jax.ShapeDtypeStruct((n, D_MODEL), f32),
                   jax.ShapeDtypeStruct((n, LANES), f32), jax.ShapeDtypeStruct((nt * SUB, MIX_TILE), f32),
                   jax.ShapeDtypeStruct((LANES, 1), f32)],
        scratch_shapes=[pltpu.VMEM((D_MODEL, D_MODEL), bf16)],
        compiler_params=_cp(("arbitrary",)),
        name=name,
    )(*row_ins, *fix_ins)


EXPERT_TM = 256
MOVE_TILE = MIX_TILE
COMBINE_CHUNK = 32


def _moe_plan(plan, cnt, n):
    i32 = jnp.int32
    v = plan.reshape(n // MOVE_TILE, SUB, MOVE_TILE)[:, :4].astype(i32)
    counts = cnt[ROUTE_E0:ROUTE_E0 + MOE_EXPERTS, 0].astype(i32)
    ends = jnp.cumsum(counts)
    off = ends - counts
    ids = jnp.arange(MOE_EXPERTS, dtype=i32)
    pos = jnp.sum(jnp.where(v[:, 0:2, :, None] == ids, off, 0), axis=-1) + v[:, 2:4]
    total = 2 * n
    n_tiles = total // EXPERT_TM
    n_items = n_tiles + MOE_EXPERTS
    inner = (counts > 0) & (off % EXPERT_TM != 0)
    keys = jnp.concatenate([jnp.arange(n_tiles, dtype=i32) * EXPERT_TM, jnp.where(inner, off, total)])
    idx = jnp.arange(n_items, dtype=i32)
    before = (keys[None, :] < keys[:, None]) | ((keys[None, :] == keys[:, None]) & (idx[None, :] < idx[:, None]))
    order = jnp.sum(before.astype(i32), axis=1)
    starts = jnp.sum(jnp.where(order[:, None] == idx[None, :], keys[:, None], 0), axis=0)
    stops = jnp.concatenate([starts[1:], jnp.full((1,), total, i32)])
    tile = starts // EXPERT_TM
    expert = jnp.sum((ends[None, :] <= starts[:, None]).astype(i32), axis=1)
    expert = jnp.minimum(expert, MOE_EXPERTS - 1)
    used = n_tiles + jnp.sum(inner.astype(i32))
    first = (expert != jnp.concatenate([jnp.full((1,), -1, i32), expert[:-1]])) & (idx < used)
    parity = (jnp.cumsum(first.astype(i32)) - 1) % 2
    later = first[None, :] & (idx[None, :] > idx[:, None])
    nxt = jnp.min(jnp.where(later, idx[None, :], n_items), axis=1)
    next_expert = jnp.sum(jnp.where(idx[None, :] == nxt[:, None], expert[None, :], 0), axis=1)
    next_expert = jnp.where(nxt < n_items, next_expert, -1)
    keep = jnp.minimum(idx, used - 1)
    items = jnp.stack([tile, expert, starts - tile * EXPERT_TM, stops - tile * EXPERT_TM,
                       first.astype(i32), parity, next_expert, jnp.zeros_like(tile)])[:, keep]
    return pos.reshape(n // MOVE_TILE, 1, 2 * MOVE_TILE), items, used.reshape(1)


IT_TILE, IT_EXPERT, IT_LO, IT_HI, IT_FIRST, IT_PARITY, IT_NEXT = range(7)


def _row_copy(src, src_row, dst, dst_row, sem):
    return pltpu.make_async_copy(src.at[pl.ds(src_row, 1)], dst.at[pl.ds(dst_row, 1)], sem)


def _source_rows_kernel(pos_ref, src_ref):
    base = pl.program_id(0) * MOVE_TILE

    def body(r, c):
        for s in range(2):
            src_ref[pos_ref[0, 0, s * MOVE_TILE + r]] = base + r
        return c

    lax.fori_loop(0, MOVE_TILE, body, 0, unroll=8)


def _source_rows(pos, n):
    src = pl.pallas_call(
        _source_rows_kernel,
        grid=(n // MOVE_TILE,),
        in_specs=[pl.BlockSpec((1, 1, 2 * MOVE_TILE), lambda i: (i, 0, 0), memory_space=pltpu.SMEM)],
        out_specs=pl.BlockSpec(memory_space=pltpu.SMEM),
        out_shape=jax.ShapeDtypeStruct((2 * n,), jnp.int32),
        compiler_params=_cp(("arbitrary",)),
        name="moe_source_rows",
    )(pos)
    return src.reshape(2 * n // EXPERT_TM, 1, EXPERT_TM)


def _experts_kernel(items_ref, n_ref, layer_ref, src_ref, src_next_ref, x_hbm, wg_hbm, wu_hbm, wd_hbm, os_ref,
                    x_buf, wg_buf, wu_buf, wd_buf, wg_scr, wu_scr, wd_scr, xsem, sem):
    i = pl.program_id(0)
    n_tiles = x_hbm.shape[0] * 2 // EXPERT_TM

    def gather(s_ref, half, op):
        def body(r, c):
            op(_row_copy(x_hbm, s_ref[0, 0, r], x_buf.at[half], r, xsem.at[half]))
            return c

        lax.fori_loop(0, EXPERT_TM, body, 0, unroll=8)

    def weight_copies(expert, half):
        e = layer_ref[0] * MOE_EXPERTS + expert
        return [pltpu.make_async_copy(src.at[e], dst.at[half], sem.at[half])
                for src, dst in ((wg_hbm, wg_buf), (wu_hbm, wu_buf), (wd_hbm, wd_buf))]

    @pl.when(i < n_ref[0])
    def _():
        @pl.when(items_ref[IT_FIRST, i] == 1)
        def _():
            half = items_ref[IT_PARITY, i]
            own = weight_copies(items_ref[IT_EXPERT, i], half)

            @pl.when(i == 0)
            def _():
                for cp in own:
                    cp.start()

            for cp in own:
                cp.wait()
            wg_scr[...] = wg_buf[half].astype(bf16)
            wu_scr[...] = wu_buf[half].astype(bf16)
            wd_scr[...] = wd_buf[half].astype(bf16)
            nxt = items_ref[IT_NEXT, i]

            @pl.when(nxt >= 0)
            def _():
                for cp in weight_copies(nxt, 1 - half):
                    cp.start()

        tile = items_ref[IT_TILE, i]
        xhalf = tile % 2

        lo = items_ref[IT_LO, i]

        @pl.when(lo == 0)
        def _():
            @pl.when(i == 0)
            def _():
                gather(src_ref, 0, lambda cp: cp.start())

            gather(src_ref, xhalf, lambda cp: cp.wait())

        x = x_buf[xhalf].astype(bf16)
        hg = _dot(x, wg_scr[...])
        hu = _dot(x, wu_scr[...])
        out = _dot((hg * _sigmoid(hg) * hu).astype(bf16), wd_scr[...])
        prefetch = (lo == 0) & (tile + 1 < n_tiles)
        for r in range(EXPERT_TM):
            @pl.when(prefetch)
            def _():
                _row_copy(x_hbm, src_next_ref[0, 0, r], x_buf.at[1 - xhalf], r, xsem.at[1 - xhalf]).start()
        row = lax.broadcasted_iota(jnp.int32, (EXPERT_TM, 1), 0)
        mine = (row >= lo) & (row < items_ref[IT_HI, i])

        @pl.when(lo == 0)
        def _():
            os_ref[...] = jnp.where(mine, out, 0.0)

        @pl.when(lo != 0)
        def _():
            os_ref[...] = jnp.where(mine, out, os_ref[...])


def _experts_call(xn, src, items, n_items, layer, wg, wu, wd):
    n_tiles = src.shape[0]
    rows = lambda i, items, n, layer: (items[IT_TILE, i], 0)
    src_spec = lambda m: pl.BlockSpec((1, 1, EXPERT_TM), m, memory_space=pltpu.SMEM)
    hbm = pl.BlockSpec(memory_space=pl.ANY)
    return pl.pallas_call(
        _experts_kernel,
        grid_spec=pltpu.PrefetchScalarGridSpec(
            num_scalar_prefetch=3,
            grid=(items.shape[1],),
            in_specs=[src_spec(lambda i, items, n, layer: (items[IT_TILE, i], 0, 0)),
                      src_spec(lambda i, items, n, layer: (jnp.minimum(items[IT_TILE, i] + 1, n_tiles - 1), 0, 0)),
                      hbm, hbm, hbm, hbm],
            out_specs=pl.BlockSpec((EXPERT_TM, D_MODEL), rows),
            scratch_shapes=[pltpu.VMEM((2, EXPERT_TM, D_MODEL), f32),
                            pltpu.VMEM((2, D_MODEL, D_EXPERT), f32), pltpu.VMEM((2, D_MODEL, D_EXPERT), f32),
                            pltpu.VMEM((2, D_EXPERT, D_MODEL), f32),
                            pltpu.VMEM((D_MODEL, D_EXPERT), bf16), pltpu.VMEM((D_MODEL, D_EXPERT), bf16),
                            pltpu.VMEM((D_EXPERT, D_MODEL), bf16),
                            pltpu.SemaphoreType.DMA((2,)), pltpu.SemaphoreType.DMA((2,))],
        ),
        out_shape=jax.ShapeDtypeStruct((2 * xn.shape[0], D_MODEL), f32),
        compiler_params=_cp(("arbitrary",)),
        name="moe_experts",
    )(items, n_items, layer, src, src, xn, wg, wu, wd)


def _combine_kernel(pos_ref, pos_next_ref, x_ref, route_ref, gn_ref, os_ref, *rest, emit_x, split):
    outs, (buf, sem, xn_scr) = rest[:-3], rest[-3:]
    i = pl.program_id(0)
    half = i % 2
    last = i == pl.num_programs(0) - 1

    def copy(p_ref, hf, r, s):
        return _row_copy(os_ref, p_ref[0, 0, s * MOVE_TILE + r], buf.at[hf, s], r, sem.at[hf])

    def loop_all(p_ref, hf, op):
        def body(r, c):
            for s in range(2):
                op(copy(p_ref, hf, r, s), s)
            return c

        lax.fori_loop(0, MOVE_TILE, body, 0, unroll=8)

    begin = lambda cp, s: cp.start(priority=s)
    finish = lambda cp, s: cp.wait()

    @pl.when(i == 0)
    def _():
        loop_all(pos_ref, 0, begin)

    loop_all(pos_ref, half, finish)
    xn_ref = xn_scr if split else outs[-1]
    for c in range(MOVE_TILE // COMBINE_CHUNK):
        rows = slice(c * COMBINE_CHUNK, (c + 1) * COMBINE_CHUNK)
        route = route_ref[rows, :]
        x2 = x_ref[rows, :] + route[:, 4:5] * buf[half, 0, rows, :] + route[:, 5:6] * buf[half, 1, rows, :]
        if emit_x:
            outs[0][rows, :] = x2
        xn_ref[rows, :] = _rms(x2, gn_ref[...]).astype(xn_ref.dtype)
        for r in range(rows.start, rows.stop):
            for s in range(2):
                begin(copy(pos_next_ref, 1 - half, r, s), s)

    @pl.when(last)
    def _():
        loop_all(pos_next_ref, 1 - half, finish)

    if split:
        main_ref, tail_ref = outs[-2:]
        is_main = i < N_MAIN // MOVE_TILE

        @pl.when(is_main)
        def _():
            main_ref[...] = xn_scr[...]

        @pl.when(jnp.logical_not(is_main))
        def _():
            tail_ref[...] = xn_scr[...]


def _combine_call(x1, route, pos, os_rows, gn, xn_dtype, *, emit_x, split):
    n = x1.shape[0]
    row = lambda i: (i, 0)
    blk = lambda m: pl.BlockSpec((MOVE_TILE, D_MODEL), m)
    nm = N_MAIN // MOVE_TILE
    out_specs, out_shape = [], []
    if emit_x:
        out_specs.append(blk(row))
        out_shape.append(jax.ShapeDtypeStruct((n, D_MODEL), f32))
    if split:
        out_specs += [blk(lambda i: (jnp.minimum(i, nm - 1), 0)), blk(lambda i: (jnp.maximum(i - nm, 0), 0))]
        out_shape += [jax.ShapeDtypeStruct((N_MAIN, D_MODEL), xn_dtype),
                      jax.ShapeDtypeStruct((n - N_MAIN, D_MODEL), xn_dtype)]
    else:
        out_specs.append(blk(row))
        out_shape.append(jax.ShapeDtypeStruct((n, D_MODEL), xn_dtype))
    nt = n // MOVE_TILE
    pos_spec = lambda m: pl.BlockSpec((1, 1, 2 * MOVE_TILE), m, memory_space=pltpu.SMEM)
    return pl.pallas_call(
        functools.partial(_combine_kernel, emit_x=emit_x, split=split),
        grid=(nt,),
        in_specs=[pos_spec(lambda i: (i, 0, 0)), pos_spec(lambda i: (jnp.minimum(i + 1, nt - 1), 0, 0)),
                  blk(row), pl.BlockSpec((MOVE_TILE, LANES), row),
                  pl.BlockSpec((1, D_MODEL), lambda i: (0, 0)), pl.BlockSpec(memory_space=pl.ANY)],
        out_specs=out_specs,
        out_shape=out_shape,
        scratch_shapes=[pltpu.VMEM((2, 2, MOVE_TILE, D_MODEL), f32), pltpu.SemaphoreType.DMA((2,)),
                        pltpu.VMEM((MOVE_TILE, D_MODEL), xn_dtype)],
        compiler_params=_cp(("arbitrary",)),
        name="moe_combine",
    )(pos, pos, x1, route, gn, os_rows)


S5_GROUP = 16
S5_GROUPS = D_MODEL // S5_GROUP
S5_STATE = 64
S5_CB = 128
S5_NB = D_MODEL // S5_CB
S5_GPB = S5_CB // S5_GROUP
S5_SB = S5_GPB * S5_STATE
S5_BC = S5_STATE * S5_GROUP
S5_CPS = 2


def _dot_exact01(x, sel):
    hi, mid, lo = _split3(x)
    return _dot(hi, sel) + _dot(mid, sel) + _dot(lo, sel)


def _s5_prep_kernel(lr_ref, li_ref, ldt_ref, bre_ref, bim_ref, cre_ref, cim_ref, bb2t_ref, cct_ref, abv_ref):
    lr, li = lr_ref[...], li_ref[...]
    dt = jnp.exp(ldt_ref[...])
    mag = jnp.exp(lr * dt)
    ang = li * dt
    ab_re, ab_im = mag * jnp.cos(ang), mag * jnp.sin(ang)
    nr, ni = ab_re - 1.0, ab_im
    den = lr * lr + li * li
    f_re = (nr * lr + ni * li) / den
    f_im = (ni * lr - nr * li) / den
    ab2_re = ab_re * ab_re - ab_im * ab_im
    ab2_im = 2.0 * (ab_re * ab_im)

    def iota(shape, axis):
        return lax.broadcasted_iota(jnp.int32, shape, axis)

    def as_col(v):
        rep = jnp.concatenate([jnp.broadcast_to(v[g:g + 1, :], (S5_STATE, S5_STATE)) for g in range(S5_GPB)], axis=0)
        pick = iota((S5_SB, S5_STATE), 1) == iota((S5_SB, S5_STATE), 0) % S5_STATE
        return jnp.sum(jnp.where(pick, rep, 0.0), axis=1, keepdims=True)

    tile_c = jnp.where(iota((S5_GROUP, S5_CB), 1) % S5_GROUP == iota((S5_GROUP, S5_CB), 0), 1.0, 0.0).astype(bf16)
    tile_p = jnp.where(iota((S5_STATE, S5_SB), 1) % S5_STATE == iota((S5_STATE, S5_SB), 0), 1.0, 0.0).astype(bf16)
    own_c = iota((S5_SB, S5_CB), 1) // S5_GROUP == iota((S5_SB, S5_CB), 0) // S5_STATE
    own_p = iota((S5_CB, S5_SB), 1) // S5_STATE == iota((S5_CB, S5_SB), 0) // S5_GROUP

    fr, fi, ar, ai = as_col(f_re), as_col(f_im), as_col(ab_re), as_col(ab_im)
    br, bi = bre_ref[...], bim_ref[...]
    bb_re = fr * br - fi * bi
    bb_im = fr * bi + fi * br
    abb_re = ar * bb_re - ai * bb_im
    abb_im = ar * bb_im + ai * bb_re
    spread_c = lambda m: jnp.where(own_c, _dot_exact01(m, tile_c), 0.0)
    spread_p = lambda m: jnp.where(own_p, _dot_exact01(m, tile_p), 0.0)
    bb2t_ref[0] = jnp.concatenate([
        jnp.concatenate([spread_c(bb_re), spread_c(abb_re)], axis=1),
        jnp.concatenate([spread_c(bb_im), spread_c(abb_im)], axis=1)], axis=0).astype(bf16)
    cct_ref[0] = jnp.concatenate([spread_p(cre_ref[...]), -spread_p(cim_ref[...])], axis=1).astype(bf16)
    grp = iota((S5_GPB, S5_SB), 1) // S5_STATE == iota((S5_GPB, S5_SB), 0)
    as_row = lambda v: jnp.sum(jnp.where(grp, _dot_exact01(v, tile_p), 0.0), axis=0, keepdims=True)
    rows = [as_row(v) for v in (ab_re, ab_im, ab2_re, ab2_im)]
    abv_ref[0] = jnp.concatenate(rows + [jnp.zeros((SUB - len(rows), S5_SB), f32)], axis=0)


def _s5_weights(lam_re, lam_im, log_dt, b_re, b_im, c_re, c_im):
    blk = lambda r, c: pl.BlockSpec((r, c), lambda j: (j, 0))
    out = lambda r, c: pl.BlockSpec((1, r, c), lambda j: (j, 0, 0))
    return pl.pallas_call(
        _s5_prep_kernel,
        grid=(S5_NB,),
        in_specs=[blk(S5_GPB, S5_STATE), blk(S5_GPB, S5_STATE), blk(S5_GPB, 1),
                  blk(S5_SB, S5_GROUP), blk(S5_SB, S5_GROUP), blk(S5_CB, S5_STATE), blk(S5_CB, S5_STATE)],
        out_specs=[out(2 * S5_SB, 2 * S5_CB), out(S5_CB, 2 * S5_SB), out(SUB, S5_SB)],
        out_shape=[jax.ShapeDtypeStruct((S5_NB, 2 * S5_SB, 2 * S5_CB), bf16),
                   jax.ShapeDtypeStruct((S5_NB, S5_CB, 2 * S5_SB), bf16),
                   jax.ShapeDtypeStruct((S5_NB, SUB, S5_SB), f32)],
        compiler_params=_cp(("arbitrary",)),
        name="s5_prep",
    )(lam_re, lam_im, log_dt[:, None], b_re.reshape(-1, S5_GROUP), b_im.reshape(-1, S5_GROUP),
      c_re.reshape(-1, S5_STATE), c_im.reshape(-1, S5_STATE))


def _s5_seq_kernel(x0_ref, x1_ref, x2_ref, x3_ref, halo0_ref, h0_ref, bb2_ref, cc_ref, abv_ref, y_ref, hout_ref,
                   xf_scr, xp_scr, bu_scr, hs_scr, yp_scr, yn_scr, h_scr, halo_scr):
    tb = pl.program_id(1)
    TL = x0_ref.shape[0]
    KB = TL // 2
    RB = BATCH * TL

    @pl.when(tb == 0)
    def _():
        h_scr[...] = h0_ref[...]
        halo_scr[...] = halo0_ref[...].astype(f32)

    chans = [slice(c * S5_CB, (c + 1) * S5_CB) for c in range(S5_CPS)]
    for c, ch in enumerate(chans):
        for b, xb_ref in enumerate((x0_ref, x1_ref, x2_ref, x3_ref)):
            xf_scr[c, b * TL:(b + 1) * TL, :] = xb_ref[:, ch].astype(f32)
        for b in range(BATCH):
            for p in range(2):
                xp_scr[c, pl.ds(2 * b + p, KB, stride=SUB), :] = xf_scr[c, pl.ds(b * TL + p, KB, stride=2), :]
    x = jnp.concatenate([xp_scr[c] for c in range(S5_CPS)], axis=1)
    xc = jnp.concatenate([halo_scr[...], x], axis=0)
    odd = (lax.broadcasted_iota(jnp.int32, (RB + SUB, 1), 0) & 1) == 1
    xprev = jnp.where(odd, pltpu.roll(xc, 1, axis=0), pltpu.roll(xc, SUB - 1, axis=0))[SUB:]
    halo_scr[...] = x[RB - SUB:]
    for c, ch in enumerate(chans):
        lhs = jnp.concatenate([x[:, ch], xprev[:, ch]], axis=1).astype(bf16)
        bu_scr[c] = _dot_nt(lhs, bb2_ref[c])
    a2 = [(abv_ref[c, 2:3, :], abv_ref[c, 3:4, :]) for c in range(S5_CPS)]

    def step(k, carry):
        r0 = k * SUB
        out = []
        for c in range(S5_CPS):
            hr, hi = carry[2 * c], carry[2 * c + 1]
            a2r, a2i = a2[c]
            bu = bu_scr[c, pl.ds(r0, SUB), :]
            nr = a2r * hr - a2i * hi + bu[:, :S5_SB]
            ni = a2r * hi + a2i * hr + bu[:, S5_SB:]
            hs_scr[c, pl.ds(r0, SUB), :S5_SB] = nr
            hs_scr[c, pl.ds(r0, SUB), S5_SB:] = ni
            out += [nr, ni]
        return tuple(out)

    init = []
    for c in range(S5_CPS):
        init += [h_scr[c, :, :S5_SB], h_scr[c, :, S5_SB:]]
    fin = tuple(init)
    for k in range(RB // SUB):
        fin = step(k, fin)
    for c, ch in enumerate(chans):
        h_scr[c, :, :S5_SB] = fin[2 * c]
        h_scr[c, :, S5_SB:] = fin[2 * c + 1]
        yp_scr[c] = _dot_nt(hs_scr[c].astype(bf16), cc_ref[c])
        for b in range(BATCH):
            for p in range(2):
                yn_scr[c, pl.ds(b * TL + p, KB, stride=2), :] = yp_scr[c, pl.ds(2 * b + p, KB, stride=SUB), :]
            y_ref[b, :, ch] = yn_scr[c, b * TL:(b + 1) * TL, :]

    @pl.when(tb == pl.num_programs(1) - 1)
    def _():
        hout_ref[...] = h_scr[...]


def _s5_seq_call(x, halo0, h0, bb2, cc, abv, *, row0, seq_len, tl):
    wsel = lambda j, t: (j, 0, 0)
    rb = BATCH * tl
    cw = S5_CPS * S5_CB
    xspec = lambda b: pl.BlockSpec((tl, cw), lambda j, t: ((row0 + b * seq_len) // tl + t, j))
    return pl.pallas_call(
        _s5_seq_kernel,
        grid=(S5_NB // S5_CPS, seq_len // tl),
        in_specs=[xspec(b) for b in range(BATCH)]
        + [pl.BlockSpec((SUB, cw), lambda j, t: (0, j)),
           pl.BlockSpec((S5_CPS, SUB, 2 * S5_SB), wsel),
           pl.BlockSpec((S5_CPS, 2 * S5_SB, 2 * S5_CB), wsel),
           pl.BlockSpec((S5_CPS, S5_CB, 2 * S5_SB), wsel),
           pl.BlockSpec((S5_CPS, SUB, S5_SB), wsel)],
        out_specs=[pl.BlockSpec((BATCH, tl, cw), lambda j, t: (0, t, j)),
                   pl.BlockSpec((S5_CPS, SUB, 2 * S5_SB), wsel)],
        out_shape=[jax.ShapeDtypeStruct((BATCH, seq_len, D_MODEL), f32),
                   jax.ShapeDtypeStruct((S5_NB, SUB, 2 * S5_SB), f32)],
        scratch_shapes=[pltpu.VMEM((S5_CPS, rb, S5_CB), f32), pltpu.VMEM((S5_CPS, rb, S5_CB), f32),
                        pltpu.VMEM((S5_CPS, rb, 2 * S5_SB), f32), pltpu.VMEM((S5_CPS, rb, 2 * S5_SB), f32),
                        pltpu.VMEM((S5_CPS, rb, S5_CB), f32), pltpu.VMEM((S5_CPS, rb, S5_CB), f32),
                        pltpu.VMEM((S5_CPS, SUB, 2 * S5_SB), f32), pltpu.VMEM((SUB, cw), f32)],
        compiler_params=_cp(("arbitrary", "arbitrary")),
        name=f"s5_seq_{seq_len}",
    )(x, x, x, x, halo0, h0, bb2, cc, abv)


def _s5_sample_kernel(x_ref, hre_ref, him_ref, bb2_ref, cc_ref, abv_ref, y_ref, ore_ref, oim_ref, hs_scr):
    nb = hre_ref.shape[0]
    bu = _dot_nt(x_ref[...], bb2_ref[0, :, :S5_CB])
    ar = abv_ref[0, 0:1, :]
    ai = abv_ref[0, 1:2, :]
    hr, hi = hre_ref[...], him_ref[...]
    for t in range(DEC_SEQ):
        rows = slice(t * nb, (t + 1) * nb)
        hr, hi = (ar * hr - ai * hi + bu[rows, :S5_SB], ar * hi + ai * hr + bu[rows, S5_SB:])
        hs_scr[rows, :S5_SB] = hr
        hs_scr[rows, S5_SB:] = hi
    y_ref[...] = _dot_nt(hs_scr[...].astype(bf16), cc_ref[0])
    ore_ref[...] = hr
    oim_ref[...] = hi


def _s5_sample_call(xt, h_re, h_im, bb2, cc, abv):
    n = xt.shape[0]
    nb = h_re.shape[0]
    wsel = lambda j: (j, 0, 0)
    st = pl.BlockSpec((nb, S5_SB), lambda j: (0, j))
    return pl.pallas_call(
        _s5_sample_kernel,
        grid=(S5_NB,),
        in_specs=[pl.BlockSpec((n, S5_CB), lambda j: (0, j)), st, st,
                  pl.BlockSpec((1, 2 * S5_SB, 2 * S5_CB), wsel),
                  pl.BlockSpec((1, S5_CB, 2 * S5_SB), wsel),
                  pl.BlockSpec((1, SUB, S5_SB), wsel)],
        out_specs=[pl.BlockSpec((n, S5_CB), lambda j: (0, j)), st, st],
        out_shape=[jax.ShapeDtypeStruct((n, D_MODEL), f32),
                   jax.ShapeDtypeStruct(h_re.shape, f32), jax.ShapeDtypeStruct(h_im.shape, f32)],
        scratch_shapes=[pltpu.VMEM((n, 2 * S5_SB), f32)],
        compiler_params=_cp(("arbitrary",)),
        name="s5_sample",
    )(xt, h_re, h_im, bb2, cc, abv)


GLA_CHUNK = 256
GLA_DIRECT = 2
S5_TL = 512


def _moe_layer(layer, x1, xnf, route, plan, cnt, wg, wu, wd, gn, xn_dtype, *, emit_x, split):
    pos, items, n_items = _moe_plan(plan, cnt, x1.shape[0])
    src = _source_rows(pos, x1.shape[0])
    os_rows = _experts_call(xnf, src, items, n_items, jnp.full((1,), layer, jnp.int32), wg, wu, wd)
    return _combine_call(x1, route, pos, os_rows, gn, xn_dtype, emit_x=emit_x, split=split)


def kernel(x_prompt, x_sample, state_gla, state_s5_re, state_s5_im, meta_tokens, norm_mix_g, norm_ffn_g, norm_final_g, gla_w_in, gla_w_a2, gla_b_a, gla_g_o, gla_w_o, s5_lambda_re, s5_lambda_im, s5_log_dt, s5_b_re, s5_b_im, s5_c_re, s5_c_im, s5_d, s5_w_glu, s5_b_glu, moe_w_rg, moe_b_rg, moe_w_re, moe_b_re, moe_w_gate, moe_w_up, moe_w_down):
    row = lambda v: v.reshape(1, -1)
    x_main = x_prompt.reshape(N_MAIN, D_MODEL)
    x_tail = jnp.concatenate([
        jnp.tile(meta_tokens.astype(x_prompt.dtype), (BATCH, 1)),
        x_sample.reshape(N_SAMPLE, D_MODEL),
        jnp.zeros((N_ROWS - N_REAL, D_MODEL), x_prompt.dtype)], axis=0)
    wg = moe_w_gate.reshape(-1, D_MODEL, D_EXPERT)
    wu = moe_w_up.reshape(-1, D_MODEL, D_EXPERT)
    wd = moe_w_down.reshape(-1, D_EXPERT, D_MODEL)

    w_in = jnp.swapaxes(gla_w_in, 1, 2)
    wa1 = jnp.pad(gla_w_in[0, :, GLA_QKVR:], ((0, 0), (0, LANES - GLA_RANK)))
    wa2 = jnp.pad(gla_w_a2.reshape(GLA_RANK, GLA_KEY), ((0, LANES - GLA_RANK), (0, 0)))
    xn, glog = _norm_gate_call(x_main, x_tail, row(norm_mix_g[0]), wa1, wa2, row(gla_b_a))
    proj = _proj_call(xn, w_in, GLA_QKVR)
    go = row(gla_g_o)
    og = jnp.zeros((N_ROWS, GLA_VAL), bf16)
    s_zero = jnp.zeros((BATCH, GLA_HEADS, GLA_DK, GLA_DV), f32)
    og, s_meta = _gla_seq_call(proj, glog, go, s_zero, og, row0=ROW_META, C=N_META, n_chunks=1, d=GLA_DIRECT)
    og, s_prompt = _gla_seq_call(proj, glog, go, s_meta, og, row0=0, C=GLA_CHUNK, n_chunks=SEQ // GLA_CHUNK,
                                 d=GLA_DIRECT)
    og, s_sample = _gla_sample_call(proj, glog, go, state_gla.reshape(DEC_BATCH, GLA_HEADS, GLA_DK, GLA_DV), og,
                                    row0=ROW_SAMPLE)
    wr, br = _router_weights(moe_w_rg[0], moe_b_rg[0], moe_w_re[0], moe_b_re[0])
    tile_row = lambda i: (i, 0)
    x1, xnf, route, plan, cnt = _mix_out_call(
        _gla_out_kernel, "gla_out", N_ROWS, [og, x_main, x_tail], [tile_row, *_main_tail_maps(MIX_TILE)],
        [gla_w_o.reshape(GLA_VAL, D_MODEL), row(norm_ffn_g[0]), wr, br])
    x2, xn2 = _moe_layer(0, x1, xnf, route, plan, cnt, wg, wu, wd, row(norm_mix_g[1]), bf16, emit_x=True, split=False)

    bb2, cc, abv = _s5_weights(s5_lambda_re[0], s5_lambda_im[0], s5_log_dt[0], s5_b_re[0], s5_b_im[0],
                               s5_c_re[0], s5_c_im[0])
    y_meta, h_meta = _s5_seq_call(xn2, jnp.zeros((SUB, D_MODEL), bf16), jnp.zeros((S5_NB, SUB, 2 * S5_SB), f32),
                                  bb2, cc, abv, row0=ROW_META, seq_len=N_META, tl=N_META)
    halo = xn2[ROW_META:ROW_SAMPLE].reshape(BATCH, N_META, D_MODEL)[:, N_META - 2:].reshape(SUB, D_MODEL)
    y_main, h_main = _s5_seq_call(xn2, halo, h_meta, bb2, cc, abv, row0=0, seq_len=SEQ, tl=S5_TL)
    xt_sample = xn2[ROW_SAMPLE:N_REAL].reshape(DEC_BATCH, DEC_SEQ, D_MODEL).transpose(1, 0, 2).reshape(N_SAMPLE, D_MODEL)
    y_samp, s5r_s, s5i_s = _s5_sample_call(
        xt_sample, state_s5_re.reshape(DEC_BATCH, S5_GROUPS * S5_STATE),
        state_s5_im.reshape(DEC_BATCH, S5_GROUPS * S5_STATE), bb2, cc, abv)
    ys_tail = jnp.concatenate([
        y_meta.reshape(N_METAROWS, D_MODEL),
        y_samp.reshape(DEC_SEQ, DEC_BATCH, D_MODEL).transpose(1, 0, 2).reshape(N_SAMPLE, D_MODEL),
        jnp.zeros((N_ROWS - N_REAL, D_MODEL), f32)], axis=0)
    wr, br = _router_weights(moe_w_rg[1], moe_b_rg[1], moe_w_re[1], moe_b_re[1])
    x3, xnf, route, plan, cnt = _mix_out_call(
        _s5_out_kernel, "s5_out", N_ROWS, [y_main.reshape(N_MAIN, D_MODEL), ys_tail, x2],
        [*_main_tail_maps(MIX_TILE), tile_row],
        [s5_w_glu.reshape(D_MODEL, D_MODEL), row(norm_mix_g[1]), row(s5_d), row(s5_b_glu),
         row(norm_ffn_g[1]), wr, br])
    y_main_out, y_tail_out = _moe_layer(1, x3, xnf, route, plan, cnt, wg, wu, wd, row(norm_final_g), f32,
                                        emit_x=False, split=True)

    y_prompt = y_main_out.reshape(BATCH, SEQ, D_MODEL)
    y_sample = y_tail_out[N_METAROWS:N_METAROWS + N_SAMPLE].reshape(DEC_BATCH, DEC_SEQ, D_MODEL)
    hfin = h_main.reshape(S5_NB, BATCH, 2, 2, S5_GPB, S5_STATE)[:, :, 1]
    s5r_p = hfin[:, :, 0].transpose(1, 0, 2, 3).reshape(1, BATCH, S5_GROUPS, S5_STATE)
    s5i_p = hfin[:, :, 1].transpose(1, 0, 2, 3).reshape(1, BATCH, S5_GROUPS, S5_STATE)
    return (y_prompt, y_sample, s_prompt[None], s5r_p, s5i_p, s_sample[None],
            s5r_s.reshape(1, DEC_BATCH, S5_GROUPS, S5_STATE), s5i_s.reshape(1, DEC_BATCH, S5_GROUPS, S5_STATE))
```

```python
import functools

import jax
import jax.numpy as jnp
from jax import lax
from jax.experimental import pallas as pl
from jax.experimental.pallas import tpu as pltpu

f32 = jnp.float32
bf16 = jnp.bfloat16

D_MODEL = 2048
BATCH = 4
SEQ = 2048
DEC_BATCH = 128
DEC_SEQ = 4
N_META = 16
EPS = 1e-6
GLA_HEADS = 4
GLA_DK = 256
GLA_DV = 512
GLA_KEY = GLA_HEADS * GLA_DK
GLA_VAL = GLA_HEADS * GLA_DV
GLA_RANK = 16
GLA_TAU = 16.0
GLA_QKVR = 2 * GLA_KEY + 2 * GLA_VAL

N_MAIN = BATCH * SEQ
N_METAROWS = BATCH * N_META
N_SAMPLE = DEC_BATCH * DEC_SEQ
ROW_META = N_MAIN
ROW_SAMPLE = N_MAIN + N_METAROWS
N_REAL = ROW_SAMPLE + N_SAMPLE
ROW_TILE = 256
N_ROWS = -(-N_REAL // ROW_TILE) * ROW_TILE

VMEM_LIMIT = 56 * 1024 * 1024


def _cp(sem, vmem=VMEM_LIMIT):
    return pltpu.CompilerParams(dimension_semantics=sem, vmem_limit_bytes=vmem)


def _dot(a, b):
    return jnp.dot(a, b, preferred_element_type=f32)


def _dot_nt(a, b):
    return lax.dot_general(a, b, (((1,), (1,)), ((), ())), preferred_element_type=f32)


def _dot_tn(a, b):
    return lax.dot_general(a, b, (((0,), (0,)), ((), ())), preferred_element_type=f32)


def _sigmoid(x):
    return 1.0 / (1.0 + jnp.exp(-x))


def _split3(x):
    hi = x.astype(bf16)
    r1 = x - hi.astype(f32)
    mid = r1.astype(bf16)
    lo = (r1 - mid.astype(f32)).astype(bf16)
    return hi, mid, lo


def _cumsum_rows(g, C):
    if C <= 16:
        row = lax.broadcasted_iota(jnp.int32, (C, 1), 0)
        b = jnp.zeros_like(g)
        for s in range(C):
            b = b + jnp.where(row >= s, g[s:s + 1, :], 0.0)
        return b
    row = lax.broadcasted_iota(jnp.int32, (C, C), 0)
    col = lax.broadcasted_iota(jnp.int32, (C, C), 1)
    tri = jnp.where(row >= col, 1.0, 0.0).astype(bf16)
    hi, mid, lo = _split3(g)
    return _dot(tri, hi) + _dot(tri, mid) + _dot(tri, lo)


PAIR_LEVEL = 1000


def _gla_pair_code(C, d, tree=True):
    ti = lax.broadcasted_iota(jnp.int32, (C, C), 0)
    si = lax.broadcasted_iota(jnp.int32, (C, C), 1)
    code = jnp.where((ti // d == si // d) & (si <= ti), 1 + ti - si, 0)
    h = d
    while tree and h < C:
        tb = ti // h
        code = jnp.where(((tb % 2) == 1) & ((si // h) == tb - 1), PAIR_LEVEL + h, code)
        h *= 2
    return code


def _gla_scores(q, k, b, code, C, d, tree=True):
    row = lax.broadcasted_iota(jnp.int32, (C, 1), 0)
    scores = jnp.zeros((C, C), f32)
    for dl in range(d):
        ks = k if dl == 0 else pltpu.roll(k, dl, axis=0)
        bs = b if dl == 0 else pltpu.roll(b, dl, axis=0)
        term = q * ks * jnp.exp(jnp.minimum(b - bs, 0.0))
        colv = jnp.sum(term, axis=1, keepdims=True)
        scores = jnp.where(code == 1 + dl, colv, scores)
    z = b
    s = 1
    while tree and 2 * s < C:
        z = jnp.where((row & s) != 0, pltpu.roll(z, s, axis=0), z)
        s *= 2
        h = s
        if h < d:
            continue
        bnext = pltpu.roll(z, C - h, axis=0)
        qh = (q * jnp.exp(b - z)).astype(bf16)
        kh = (k * jnp.exp(jnp.minimum(bnext - b, 0.0))).astype(bf16)
        scores = jnp.where(code == PAIR_LEVEL + h, _dot_nt(qh, kh), scores)
    return scores


def _gla_chunk(q, k, v, g, S, code, C, d, tree=True):
    b = _cumsum_rows(g, C)
    o = _dot((q * jnp.exp(b)).astype(bf16), S.astype(bf16))
    scores = _gla_scores(q, k, b, code, C, d, tree)
    o = o + _dot(scores.astype(bf16), v)
    b_last = b[C - 1:C, :]
    kd = (k * jnp.exp(b_last - b)).astype(bf16)
    if C == GLA_DK:
        eye = code == 1
    else:
        eye = (lax.broadcasted_iota(jnp.int32, (GLA_DK, GLA_DK), 0)
               == lax.broadcasted_iota(jnp.int32, (GLA_DK, GLA_DK), 1))
    dec_col = jnp.sum(jnp.where(eye, jnp.exp(b_last), 0.0), axis=1, keepdims=True)
    S_new = dec_col * S + _dot_tn(kd, v)
    return o, S_new


def _gla_head_epilogue(o, r, go):
    ms = jnp.mean(o * o, axis=1, keepdims=True)
    on = o * lax.rsqrt(ms + EPS) * go
    return on * (r * _sigmoid(r))


def _gla_seq_kernel(q_ref, k_ref, v_ref, r_ref, g_ref, go_ref, s0_ref, _og_in, og_ref, sout_ref, s_scr, *, C, d):
    c = pl.program_id(1)

    @pl.when(c == 0)
    def _():
        s_scr[...] = s0_ref[0]

    code = _gla_pair_code(C, d)

    def head(h, carry):
        ck = pl.ds(pl.multiple_of(h * GLA_DK, GLA_DK), GLA_DK)
        cv = pl.ds(pl.multiple_of(h * GLA_DV, GLA_DV), GLA_DV)
        q = q_ref[:, ck].astype(f32) * (GLA_DK ** -0.5)
        k = k_ref[:, ck].astype(f32)
        o, S_new = _gla_chunk(q, k, v_ref[:, cv], g_ref[:, ck], s_scr[h], code, C, d)
        s_scr[h] = S_new
        og_ref[:, cv] = _gla_head_epilogue(o, r_ref[:, cv].astype(f32), go_ref[:, cv]).astype(og_ref.dtype)
        return carry

    lax.fori_loop(0, GLA_HEADS, head, 0)

    @pl.when(c == pl.num_programs(1) - 1)
    def _():
        sout_ref[0] = s_scr[...]


def _gla_seq_call(proj, glog, go, s0, og_buf, *, row0, C, n_chunks, d):
    blk0 = row0 // C
    rows = lambda b, c: blk0 + b * n_chunks + c
    kern = functools.partial(_gla_seq_kernel, C=C, d=d)
    return pl.pallas_call(
        kern,
        grid=(BATCH, n_chunks),
        in_specs=[
            pl.BlockSpec((C, GLA_KEY), lambda b, c: (rows(b, c), 0)),
            pl.BlockSpec((C, GLA_KEY), lambda b, c: (rows(b, c), 1)),
            pl.BlockSpec((C, GLA_VAL), lambda b, c: (rows(b, c), 1)),
            pl.BlockSpec((C, GLA_VAL), lambda b, c: (rows(b, c), 2)),
            pl.BlockSpec((C, GLA_KEY), lambda b, c: (rows(b, c), 0)),
            pl.BlockSpec((1, GLA_VAL), lambda b, c: (0, 0)),
            pl.BlockSpec((1, GLA_HEADS, GLA_DK, GLA_DV), lambda b, c: (b, 0, 0, 0)),
            pl.BlockSpec(memory_space=pl.ANY),
        ],
        out_specs=[
            pl.BlockSpec((C, GLA_VAL), lambda b, c: (rows(b, c), 0)),
            pl.BlockSpec((1, GLA_HEADS, GLA_DK, GLA_DV), lambda b, c: (b, 0, 0, 0)),
        ],
        out_shape=[
            jax.ShapeDtypeStruct(og_buf.shape, og_buf.dtype),
            jax.ShapeDtypeStruct((BATCH, GLA_HEADS, GLA_DK, GLA_DV), f32),
        ],
        scratch_shapes=[pltpu.VMEM((GLA_HEADS, GLA_DK, GLA_DV), f32)],
        input_output_aliases={7: 0},
        compiler_params=_cp(("arbitrary", "arbitrary")),
        name=f"gla_seq_c{C}",
    )(proj, proj, proj, proj, glog, go, s0, og_buf)


SAMPLE_BB = 4
SAMPLE_C = SAMPLE_BB * DEC_SEQ


def _gla_sample_kernel(q_ref, k_ref, v_ref, r_ref, g_ref, go_ref, s0_ref, _og_in, og_ref, sout_ref):
    row = lax.broadcasted_iota(jnp.int32, (SAMPLE_C, 1), 0)

    code = _gla_pair_code(SAMPLE_C, DEC_SEQ, tree=False)

    def take(x, bb):
        sh = (SAMPLE_C - DEC_SEQ * bb) % SAMPLE_C
        return jnp.where(row < DEC_SEQ, pltpu.roll(x, sh, axis=0) if sh else x, 0.0)

    def head(h, carry):
        ck = pl.ds(pl.multiple_of(h * GLA_DK, GLA_DK), GLA_DK)
        cv = pl.ds(pl.multiple_of(h * GLA_DV, GLA_DV), GLA_DV)
        q_all = q_ref[:, ck].astype(f32) * (GLA_DK ** -0.5)
        k_all = k_ref[:, ck].astype(f32)
        v_all = v_ref[:, cv].astype(f32)
        r_all = r_ref[:, cv].astype(f32)
        g_all = g_ref[:, ck]
        go = go_ref[:, cv]
        acc = jnp.zeros((SAMPLE_C, GLA_DV), f32)
        for bb in range(SAMPLE_BB):
            o, S_new = _gla_chunk(take(q_all, bb), take(k_all, bb), take(v_all, bb).astype(bf16),
                                  take(g_all, bb), s0_ref[bb, h], code, SAMPLE_C, DEC_SEQ, tree=False)
            sout_ref[bb, h] = S_new
            y = _gla_head_epilogue(o, take(r_all, bb), go)
            acc = jnp.where(row // DEC_SEQ == bb, pltpu.roll(y, DEC_SEQ * bb, axis=0) if bb else y, acc)
        og_ref[:, cv] = acc.astype(og_ref.dtype)
        return carry

    lax.fori_loop(0, GLA_HEADS, head, 0)


def _gla_sample_call(proj, glog, go, s0, og_buf, *, row0):
    n_seq = s0.shape[0]
    blk0 = row0 // SAMPLE_C
    st_spec = pl.BlockSpec((SAMPLE_BB, GLA_HEADS, GLA_DK, GLA_DV), lambda i: (i, 0, 0, 0))
    return pl.pallas_call(
        _gla_sample_kernel,
        grid=(n_seq // SAMPLE_BB,),
        in_specs=[
            pl.BlockSpec((SAMPLE_C, GLA_KEY), lambda i: (blk0 + i, 0)),
            pl.BlockSpec((SAMPLE_C, GLA_KEY), lambda i: (blk0 + i, 1)),
            pl.BlockSpec((SAMPLE_C, GLA_VAL), lambda i: (blk0 + i, 1)),
            pl.BlockSpec((SAMPLE_C, GLA_VAL), lambda i: (blk0 + i, 2)),
            pl.BlockSpec((SAMPLE_C, GLA_KEY), lambda i: (blk0 + i, 0)),
            pl.BlockSpec((1, GLA_VAL), lambda i: (0, 0)),
            st_spec,
            pl.BlockSpec(memory_space=pl.ANY),
        ],
        out_specs=[pl.BlockSpec((SAMPLE_C, GLA_VAL), lambda i: (blk0 + i, 0)), st_spec],
        out_shape=[jax.ShapeDtypeStruct(og_buf.shape, og_buf.dtype), jax.ShapeDtypeStruct(s0.shape, f32)],
        input_output_aliases={7: 0},
        compiler_params=_cp(("arbitrary",)),
        name="gla_sample",
    )(proj, proj, proj, proj, glog, go, s0, og_buf)


LANES = 128
SUB = 8
MOE_GROUPS = 4
MOE_EPG = 8
MOE_EXPERTS = MOE_GROUPS * MOE_EPG
D_EXPERT = 256
ROUTE_E0 = SUB
MIX_TILE = 256
PROJ_NT = 1024
PROJ_MT = N_ROWS // 7


def _rms(x, g):
    r = lax.rsqrt(jnp.mean(x * x, axis=-1, keepdims=True) + EPS)
    return (x * r) * g


def _log_sigmoid(z):
    return jnp.minimum(z, 0.0) - jnp.log1p(jnp.exp(-jnp.abs(z)))


def _main_or_tail(tile, main_ref, tail_ref):
    return jnp.where(pl.program_id(0) < N_MAIN // tile, main_ref[...], tail_ref[...])


def _main_tail_maps(tile):
    nm = N_MAIN // tile
    return (lambda i: (jnp.minimum(i, nm - 1), 0)), (lambda i: (jnp.maximum(i - nm, 0), 0))


def _norm_gate_kernel(xm_ref, xt_ref, gn_ref, wa1_ref, wa2_ref, ba_ref, xn_ref, gl_ref):
    xnb = _rms(_main_or_tail(ROW_TILE, xm_ref, xt_ref), gn_ref[...]).astype(bf16)
    xn_ref[...] = xnb
    a = _dot(xnb, wa1_ref[...].astype(bf16))
    z = _dot(a.astype(bf16), wa2_ref[...].astype(bf16)) + ba_ref[...]
    gl_ref[...] = _log_sigmoid(z) * (1.0 / GLA_TAU)


def _norm_gate_call(x_main, x_tail, gn, wa1, wa2, ba):
    n = x_main.shape[0] + x_tail.shape[0]
    row = lambda i: (i, 0)
    fix = lambda i: (0, 0)
    main_map, tail_map = _main_tail_maps(ROW_TILE)
    return pl.pallas_call(
        _norm_gate_kernel,
        grid=(n // ROW_TILE,),
        in_specs=[pl.BlockSpec((ROW_TILE, D_MODEL), main_map), pl.BlockSpec((ROW_TILE, D_MODEL), tail_map),
                  pl.BlockSpec((1, D_MODEL), fix),
                  pl.BlockSpec((D_MODEL, LANES), fix), pl.BlockSpec((LANES, GLA_KEY), fix),
                  pl.BlockSpec((1, GLA_KEY), fix)],
        out_specs=[pl.BlockSpec((ROW_TILE, D_MODEL), row), pl.BlockSpec((ROW_TILE, GLA_KEY), row)],
        out_shape=[jax.ShapeDtypeStruct((n, D_MODEL), bf16), jax.ShapeDtypeStruct((n, GLA_KEY), f32)],
        compiler_params=_cp(("arbitrary",)),
        name="norm_gate",
    )(x_main, x_tail, gn, wa1, wa2, ba)


def _proj_kernel(xn_ref, w_ref, o_ref, wb_scr):
    @pl.when(pl.program_id(1) == 0)
    def _():
        wb_scr[...] = w_ref[0].astype(bf16)

    o_ref[...] = _dot_nt(xn_ref[...], wb_scr[...]).astype(o_ref.dtype)


def _proj_call(xn, wt, n_cols):
    n = xn.shape[0]
    return pl.pallas_call(
        _proj_kernel,
        grid=(n_cols // PROJ_NT, n // PROJ_MT),
        in_specs=[pl.BlockSpec((PROJ_MT, D_MODEL), lambda j, i: (i, 0)),
                  pl.BlockSpec((1, PROJ_NT, D_MODEL), lambda j, i: (0, j, 0))],
        out_specs=pl.BlockSpec((PROJ_MT, PROJ_NT), lambda j, i: (i, j)),
        out_shape=jax.ShapeDtypeStruct((n, n_cols), bf16),
        scratch_shapes=[pltpu.VMEM((PROJ_NT, D_MODEL), bf16)],
        compiler_params=_cp(("arbitrary", "arbitrary")),
        name="gla_proj",
    )(xn, wt)


def _route(xn, wr, br, cnt_ref):
    R = xn.shape[0]
    xh = xn.astype(bf16)
    xl = (xn - xh.astype(f32)).astype(bf16)
    wh = wr.astype(bf16)
    wl = (wr - wh.astype(f32)).astype(bf16)
    hi_terms = _dot(xh, jnp.concatenate([wh, wl], axis=1))
    logits = hi_terms[:, :LANES] + hi_terms[:, LANES:] + _dot(xl, wh) + br
    lt = logits.T
    neg = -jnp.inf
    big = float(LANES)
    sub = lax.broadcasted_iota(jnp.int32, (SUB, R), 0)
    subf = sub.astype(f32)
    colmax = lambda a: jnp.max(a, axis=0, keepdims=True)
    first_at = lambda a, m: jnp.min(jnp.where(a == m, subf, big), axis=0, keepdims=True)
    groups = lt[0:SUB]
    lg = jnp.where(sub < MOE_GROUPS, groups, neg)
    mg = colmax(lg)
    gidx = first_at(lg, mg)
    ptop = 1.0 / jnp.sum(jnp.where(sub < MOE_GROUPS, jnp.exp(groups - mg), 0.0), axis=0, keepdims=True)
    le = lt[ROUTE_E0:ROUTE_E0 + MOE_EPG]
    for g in range(1, MOE_GROUPS):
        le = jnp.where(gidx == g, lt[ROUTE_E0 + g * MOE_EPG:ROUTE_E0 + (g + 1) * MOE_EPG], le)
    v1 = colmax(le)
    j1 = first_at(le, v1)
    le2 = jnp.where(subf == j1, neg, le)
    v2 = colmax(le2)
    j2 = first_at(le2, v2)
    s = jnp.exp(v2 - v1)
    w0 = ptop / (1.0 + s)
    w1 = ptop * s / (1.0 + s)
    e0 = gidx * MOE_EPG + j1
    e1 = gidx * MOE_EPG + j2
    col = lax.broadcasted_iota(jnp.int32, (LANES, R), 0).astype(f32)
    at0 = col == e0 + ROUTE_E0
    at1 = col == e1 + ROUTE_E0
    oh = jnp.where(at0 | at1, 1.0, 0.0)
    ri = lax.broadcasted_iota(jnp.int32, (R, R), 0)
    ci = lax.broadcasted_iota(jnp.int32, (R, R), 1)
    earlier = jnp.where(ri < ci, 1.0, 0.0).astype(bf16)
    tot = _dot(oh.astype(bf16), earlier) + cnt_ref[...]
    rank0 = jnp.sum(jnp.where(at0, tot, 0.0), axis=0, keepdims=True)
    rank1 = jnp.sum(jnp.where(at1, tot, 0.0), axis=0, keepdims=True)
    cnt_ref[...] = cnt_ref[...] + jnp.sum(oh, axis=1, keepdims=True)
    vals = (e0, e1, rank0, rank1, w0, w1)
    plan = jnp.zeros((SUB, R), f32)
    for j, v in enumerate(vals):
        plan = jnp.where(sub == j, v, plan)
    slab_t = jnp.concatenate([plan, jnp.zeros((LANES - SUB, R), f32)], axis=0)
    return slab_t.T, plan


def _router_weights(w_rg, b_rg, w_re, b_re):
    zeros = lambda k: jnp.zeros((D_MODEL, k), w_rg.dtype)
    w = jnp.concatenate([w_rg, zeros(ROUTE_E0 - MOE_GROUPS), jnp.moveaxis(w_re, 0, 1).reshape(D_MODEL, MOE_EXPERTS),
                         zeros(LANES - ROUTE_E0 - MOE_EXPERTS)], axis=1)
    b = jnp.concatenate([b_rg, jnp.zeros((ROUTE_E0 - MOE_GROUPS,), b_rg.dtype), b_re.reshape(MOE_EXPERTS),
                         jnp.zeros((LANES - ROUTE_E0 - MOE_EXPERTS,), b_rg.dtype)])
    return w, b[None]


def _mix_out_tail(x1, gffn_ref, wr_ref, br_ref, x1_ref, xn_ref, route_ref, plan_ref, cnt_ref):
    @pl.when(pl.program_id(0) == 0)
    def _():
        cnt_ref[...] = jnp.zeros_like(cnt_ref)

    x1_ref[...] = x1
    xn = _rms(x1, gffn_ref[...])
    xn_ref[...] = xn
    route_ref[...], plan_ref[...] = _route(xn, wr_ref[...], br_ref[...], cnt_ref)


def _cast_once(w_ref, w_scr):
    @pl.when(pl.program_id(0) == 0)
    def _():
        w_scr[...] = w_ref[...].astype(bf16)


def _gla_out_kernel(og_ref, xm_ref, xt_ref, wo_ref, gffn_ref, wr_ref, br_ref, *rest):
    outs, w_scr = rest[:-1], rest[-1]
    _cast_once(wo_ref, w_scr)
    x1 = _main_or_tail(MIX_TILE, xm_ref, xt_ref) + _dot(og_ref[...], w_scr[...])
    _mix_out_tail(x1, gffn_ref, wr_ref, br_ref, *outs)


def _gelu_tanh(x):
    return x * (0.5 * (1.0 + jnp.tanh(0.7978845608028654 * (x + 0.044715 * (x * x * x)))))


def _s5_out_kernel(ys_main_ref, ys_tail_ref, x_ref, wglu_ref, gmix_ref, d_ref, bglu_ref, gffn_ref, wr_ref, br_ref,
                   *rest):
    outs, w_scr = rest[:-1], rest[-1]
    _cast_once(wglu_ref, w_scr)
    x = x_ref[...]
    u = _rms(x, gmix_ref[...])
    y = _gelu_tanh(_main_or_tail(MIX_TILE, ys_main_ref, ys_tail_ref) + d_ref[...] * u)
    z = _dot(y.astype(bf16), w_scr[...]) + bglu_ref[...]
    _mix_out_tail(x + y * _sigmoid(z), gffn_ref, wr_ref, br_ref, *outs)


def _mix_out_call(kern, name, n, row_ins, row_maps, fix_ins):
    row = lambda i: (i, 0)
    fix = lambda i: (0, 0)
    out_row = lambda w: pl.BlockSpec((MIX_TILE, w), row)
    nt = n // MIX_TILE
    return pl.pallas_call(
        kern,
        grid=(nt,),
        in_specs=[pl.BlockSpec((MIX_TILE, a.shape[1]), m) for a, m in zip(row_ins, row_maps)]
        + [pl.BlockSpec(a.shape, fix, pipeline_mode=pl.Buffered(1)) for a in fix_ins],
        out_specs=[out_row(D_MODEL), out_row(D_MODEL), out_row(LANES), pl.BlockSpec((SUB, MIX_TILE), row),
                   pl.BlockSpec((LANES, 1), fix)],
        out_shape=[jax.ShapeDtypeStruct((n, D_MODEL), f32), jax.ShapeDtypeStruct((n, D_MODEL), f32),
                   jax.ShapeDtypeStruct((n, LANES), f32), jax.ShapeDtypeStruct((nt * SUB, MIX_TILE), f32),
                   jax.ShapeDtypeStruct((LANES, 1), f32)],
        scratch_shapes=[pltpu.VMEM((D_MODEL, D_MODEL), bf16)],
        compiler_params=_cp(("arbitrary",)),
        name=name,
    )(*row_ins, *fix_ins)


EXPERT_TM = 256
MOVE_TILE = MIX_TILE
COMBINE_CHUNK = 32


def _moe_plan(plan, cnt, n):
    i32 = jnp.int32
    v = plan.reshape(n // MOVE_TILE, SUB, MOVE_TILE)[:, :4].astype(i32)
    counts = cnt[ROUTE_E0:ROUTE_E0 + MOE_EXPERTS, 0].astype(i32)
    ends = jnp.cumsum(counts)
    off = ends - counts
    ids = jnp.arange(MOE_EXPERTS, dtype=i32)
    pos = jnp.sum(jnp.where(v[:, 0:2, :, None] == ids, off, 0), axis=-1) + v[:, 2:4]
    total = 2 * n
    n_tiles = total // EXPERT_TM
    n_items = n_tiles + MOE_EXPERTS
    inner = (counts > 0) & (off % EXPERT_TM != 0)
    keys = jnp.concatenate([jnp.arange(n_tiles, dtype=i32) * EXPERT_TM, jnp.where(inner, off, total)])
    idx = jnp.arange(n_items, dtype=i32)
    before = (keys[None, :] < keys[:, None]) | ((keys[None, :] == keys[:, None]) & (idx[None, :] < idx[:, None]))
    order = jnp.sum(before.astype(i32), axis=1)
    starts = jnp.sum(jnp.where(order[:, None] == idx[None, :], keys[:, None], 0), axis=0)
    stops = jnp.concatenate([starts[1:], jnp.full((1,), total, i32)])
    tile = starts // EXPERT_TM
    expert = jnp.sum((ends[None, :] <= starts[:, None]).astype(i32), axis=1)
    expert = jnp.minimum(expert, MOE_EXPERTS - 1)
    used = n_tiles + jnp.sum(inner.astype(i32))
    first = (expert != jnp.concatenate([jnp.full((1,), -1, i32), expert[:-1]])) & (idx < used)
    parity = (jnp.cumsum(first.astype(i32)) - 1) % 2
    later = first[None, :] & (idx[None, :] > idx[:, None])
    nxt = jnp.min(jnp.where(later, idx[None, :], n_items), axis=1)
    next_expert = jnp.sum(jnp.where(idx[None, :] == nxt[:, None], expert[None, :], 0), axis=1)
    next_expert = jnp.where(nxt < n_items, next_expert, -1)
    keep = jnp.minimum(idx, used - 1)
    items = jnp.stack([tile, expert, starts - tile * EXPERT_TM, stops - tile * EXPERT_TM,
                       first.astype(i32), parity, next_expert, jnp.zeros_like(tile)])[:, keep]
    return pos.reshape(n // MOVE_TILE, 1, 2 * MOVE_TILE), items, used.reshape(1)


IT_TILE, IT_EXPERT, IT_LO, IT_HI, IT_FIRST, IT_PARITY, IT_NEXT = range(7)


def _row_copy(src, src_row, dst, dst_row, sem):
    return pltpu.make_async_copy(src.at[pl.ds(src_row, 1)], dst.at[pl.ds(dst_row, 1)], sem)


def _source_rows_kernel(pos_ref, src_ref):
    base = pl.program_id(0) * MOVE_TILE

    def body(r, c):
        for s in range(2):
            src_ref[pos_ref[0, 0, s * MOVE_TILE + r]] = base + r
        return c

    lax.fori_loop(0, MOVE_TILE, body, 0, unroll=8)


def _source_rows(pos, n):
    src = pl.pallas_call(
        _source_rows_kernel,
        grid=(n // MOVE_TILE,),
        in_specs=[pl.BlockSpec((1, 1, 2 * MOVE_TILE), lambda i: (i, 0, 0), memory_space=pltpu.SMEM)],
        out_specs=pl.BlockSpec(memory_space=pltpu.SMEM),
        out_shape=jax.ShapeDtypeStruct((2 * n,), jnp.int32),
        compiler_params=_cp(("arbitrary",)),
        name="moe_source_rows",
    )(pos)
    return src.reshape(2 * n // EXPERT_TM, 1, EXPERT_TM)


def _experts_kernel(items_ref, n_ref, layer_ref, src_ref, src_next_ref, x_hbm, wg_hbm, wu_hbm, wd_hbm, os_ref,
                    x_buf, wg_buf, wu_buf, wd_buf, wg_scr, wu_scr, wd_scr, xsem, sem):
    i = pl.program_id(0)
    n_tiles = x_hbm.shape[0] * 2 // EXPERT_TM

    def gather(s_ref, half, op):
        def body(r, c):
            op(_row_copy(x_hbm, s_ref[0, 0, r], x_buf.at[half], r, xsem.at[half]))
            return c

        lax.fori_loop(0, EXPERT_TM, body, 0, unroll=8)

    def weight_copies(expert, half):
        e = layer_ref[0] * MOE_EXPERTS + expert
        return [pltpu.make_async_copy(src.at[e], dst.at[half], sem.at[half])
                for src, dst in ((wg_hbm, wg_buf), (wu_hbm, wu_buf), (wd_hbm, wd_buf))]

    @pl.when(i < n_ref[0])
    def _():
        @pl.when(items_ref[IT_FIRST, i] == 1)
        def _():
            half = items_ref[IT_PARITY, i]
            own = weight_copies(items_ref[IT_EXPERT, i], half)

            @pl.when(i == 0)
            def _():
                for cp in own:
                    cp.start()

            for cp in own:
                cp.wait()
            wg_scr[...] = wg_buf[half].astype(bf16)
            wu_scr[...] = wu_buf[half].astype(bf16)
            wd_scr[...] = wd_buf[half].astype(bf16)
            nxt = items_ref[IT_NEXT, i]

            @pl.when(nxt >= 0)
            def _():
                for cp in weight_copies(nxt, 1 - half):
                    cp.start()

        tile = items_ref[IT_TILE, i]
        xhalf = tile % 2

        lo = items_ref[IT_LO, i]

        @pl.when(lo == 0)
        def _():
            @pl.when(i == 0)
            def _():
                gather(src_ref, 0, lambda cp: cp.start())

            gather(src_ref, xhalf, lambda cp: cp.wait())

        x = x_buf[xhalf].astype(bf16)
        hg = _dot(x, wg_scr[...])
        hu = _dot(x, wu_scr[...])
        out = _dot((hg * _sigmoid(hg) * hu).astype(bf16), wd_scr[...])
        prefetch = (lo == 0) & (tile + 1 < n_tiles)
        for r in range(EXPERT_TM):
            @pl.when(prefetch)
            def _():
                _row_copy(x_hbm, src_next_ref[0, 0, r], x_buf.at[1 - xhalf], r, xsem.at[1 - xhalf]).start()
        row = lax.broadcasted_iota(jnp.int32, (EXPERT_TM, 1), 0)
        mine = (row >= lo) & (row < items_ref[IT_HI, i])

        @pl.when(lo == 0)
        def _():
            os_ref[...] = jnp.where(mine, out, 0.0)

        @pl.when(lo != 0)
        def _():
            os_ref[...] = jnp.where(mine, out, os_ref[...])


def _experts_call(xn, src, items, n_items, layer, wg, wu, wd):
    n_tiles = src.shape[0]
    rows = lambda i, items, n, layer: (items[IT_TILE, i], 0)
    src_spec = lambda m: pl.BlockSpec((1, 1, EXPERT_TM), m, memory_space=pltpu.SMEM)
    hbm = pl.BlockSpec(memory_space=pl.ANY)
    return pl.pallas_call(
        _experts_kernel,
        grid_spec=pltpu.PrefetchScalarGridSpec(
            num_scalar_prefetch=3,
            grid=(items.shape[1],),
            in_specs=[src_spec(lambda i, items, n, layer: (items[IT_TILE, i], 0, 0)),
                      src_spec(lambda i, items, n, layer: (jnp.minimum(items[IT_TILE, i] + 1, n_tiles - 1), 0, 0)),
                      hbm, hbm, hbm, hbm],
            out_specs=pl.BlockSpec((EXPERT_TM, D_MODEL), rows),
            scratch_shapes=[pltpu.VMEM((2, EXPERT_TM, D_MODEL), f32),
                            pltpu.VMEM((2, D_MODEL, D_EXPERT), f32), pltpu.VMEM((2, D_MODEL, D_EXPERT), f32),
                            pltpu.VMEM((2, D_EXPERT, D_MODEL), f32),
                            pltpu.VMEM((D_MODEL, D_EXPERT), bf16), pltpu.VMEM((D_MODEL, D_EXPERT), bf16),
                            pltpu.VMEM((D_EXPERT, D_MODEL), bf16),
                            pltpu.SemaphoreType.DMA((2,)), pltpu.SemaphoreType.DMA((2,))],
        ),
        out_shape=jax.ShapeDtypeStruct((2 * xn.shape[0], D_MODEL), f32),
        compiler_params=_cp(("arbitrary",)),
        name="moe_experts",
    )(items, n_items, layer, src, src, xn, wg, wu, wd)


def _combine_kernel(pos_ref, pos_next_ref, x_ref, route_ref, gn_ref, os_ref, *rest, emit_x, split):
    outs, (buf, sem, xn_scr) = rest[:-3], rest[-3:]
    i = pl.program_id(0)
    half = i % 2
    last = i == pl.num_programs(0) - 1

    def copy(p_ref, hf, r, s):
        return _row_copy(os_ref, p_ref[0, 0, s * MOVE_TILE + r], buf.at[hf, s], r, sem.at[hf])

    def loop_all(p_ref, hf, op):
        def body(r, c):
            for s in range(2):
                op(copy(p_ref, hf, r, s), s)
            return c

        lax.fori_loop(0, MOVE_TILE, body, 0, unroll=8)

    begin = lambda cp, s: cp.start(priority=s)
    finish = lambda cp, s: cp.wait()

    @pl.when(i == 0)
    def _():
        loop_all(pos_ref, 0, begin)

    loop_all(pos_ref, half, finish)
    xn_ref = xn_scr if split else outs[-1]
    for c in range(MOVE_TILE // COMBINE_CHUNK):
        rows = slice(c * COMBINE_CHUNK, (c + 1) * COMBINE_CHUNK)
        route = route_ref[rows, :]
        x2 = x_ref[rows, :] + route[:, 4:5] * buf[half, 0, rows, :] + route[:, 5:6] * buf[half, 1, rows, :]
        if emit_x:
            outs[0][rows, :] = x2
        xn_ref[rows, :] = _rms(x2, gn_ref[...]).astype(xn_ref.dtype)
        for r in range(rows.start, rows.stop):
            for s in range(2):
                begin(copy(pos_next_ref, 1 - half, r, s), s)

    @pl.when(last)
    def _():
        loop_all(pos_next_ref, 1 - half, finish)

    if split:
        main_ref, tail_ref = outs[-2:]
        is_main = i < N_MAIN // MOVE_TILE

        @pl.when(is_main)
        def _():
            main_ref[...] = xn_scr[...]

        @pl.when(jnp.logical_not(is_main))
        def _():
            tail_ref[...] = xn_scr[...]


def _combine_call(x1, route, pos, os_rows, gn, xn_dtype, *, emit_x, split):
    n = x1.shape[0]
    row = lambda i: (i, 0)
    blk = lambda m: pl.BlockSpec((MOVE_TILE, D_MODEL), m)
    nm = N_MAIN // MOVE_TILE
    out_specs, out_shape = [], []
    if emit_x:
        out_specs.append(blk(row))
        out_shape.append(jax.ShapeDtypeStruct((n, D_MODEL), f32))
    if split:
        out_specs += [blk(lambda i: (jnp.minimum(i, nm - 1), 0)), blk(lambda i: (jnp.maximum(i - nm, 0), 0))]
        out_shape += [jax.ShapeDtypeStruct((N_MAIN, D_MODEL), xn_dtype),
                      jax.ShapeDtypeStruct((n - N_MAIN, D_MODEL), xn_dtype)]
    else:
        out_specs.append(blk(row))
        out_shape.append(jax.ShapeDtypeStruct((n, D_MODEL), xn_dtype))
    nt = n // MOVE_TILE
    pos_spec = lambda m: pl.BlockSpec((1, 1, 2 * MOVE_TILE), m, memory_space=pltpu.SMEM)
    return pl.pallas_call(
        functools.partial(_combine_kernel, emit_x=emit_x, split=split),
        grid=(nt,),
        in_specs=[pos_spec(lambda i: (i, 0, 0)), pos_spec(lambda i: (jnp.minimum(i + 1, nt - 1), 0, 0)),
                  blk(row), pl.BlockSpec((MOVE_TILE, LANES), row),
                  pl.BlockSpec((1, D_MODEL), lambda i: (0, 0)), pl.BlockSpec(memory_space=pl.ANY)],
        out_specs=out_specs,
        out_shape=out_shape,
        scratch_shapes=[pltpu.VMEM((2, 2, MOVE_TILE, D_MODEL), f32), pltpu.SemaphoreType.DMA((2,)),
                        pltpu.VMEM((MOVE_TILE, D_MODEL), xn_dtype)],
        compiler_params=_cp(("arbitrary",)),
        name="moe_combine",
    )(pos, pos, x1, route, gn, os_rows)


S5_GROUP = 16
S5_GROUPS = D_MODEL // S5_GROUP
S5_STATE = 64
S5_CB = 128
S5_NB = D_MODEL // S5_CB
S5_GPB = S5_CB // S5_GROUP
S5_SB = S5_GPB * S5_STATE
S5_BC = S5_STATE * S5_GROUP
S5_CPS = 4


def _dot_exact01(x, sel):
    hi, mid, lo = _split3(x)
    return _dot(hi, sel) + _dot(mid, sel) + _dot(lo, sel)


def _s5_prep_kernel(lr_ref, li_ref, ldt_ref, bre_ref, bim_ref, cre_ref, cim_ref, bb2t_ref, cct_ref, abv_ref):
    lr, li = lr_ref[...], li_ref[...]
    dt = jnp.exp(ldt_ref[...])
    mag = jnp.exp(lr * dt)
    ang = li * dt
    ab_re, ab_im = mag * jnp.cos(ang), mag * jnp.sin(ang)
    nr, ni = ab_re - 1.0, ab_im
    den = lr * lr + li * li
    f_re = (nr * lr + ni * li) / den
    f_im = (ni * lr - nr * li) / den
    ab2_re = ab_re * ab_re - ab_im * ab_im
    ab2_im = 2.0 * (ab_re * ab_im)

    def iota(shape, axis):
        return lax.broadcasted_iota(jnp.int32, shape, axis)

    def as_col(v):
        rep = jnp.concatenate([jnp.broadcast_to(v[g:g + 1, :], (S5_STATE, S5_STATE)) for g in range(S5_GPB)], axis=0)
        pick = iota((S5_SB, S5_STATE), 1) == iota((S5_SB, S5_STATE), 0) % S5_STATE
        return jnp.sum(jnp.where(pick, rep, 0.0), axis=1, keepdims=True)

    tile_c = jnp.where(iota((S5_GROUP, S5_CB), 1) % S5_GROUP == iota((S5_GROUP, S5_CB), 0), 1.0, 0.0).astype(bf16)
    tile_p = jnp.where(iota((S5_STATE, S5_SB), 1) % S5_STATE == iota((S5_STATE, S5_SB), 0), 1.0, 0.0).astype(bf16)
    own_c = iota((S5_SB, S5_CB), 1) // S5_GROUP == iota((S5_SB, S5_CB), 0) // S5_STATE
    own_p = iota((S5_CB, S5_SB), 1) // S5_STATE == iota((S5_CB, S5_SB), 0) // S5_GROUP

    fr, fi, ar, ai = as_col(f_re), as_col(f_im), as_col(ab_re), as_col(ab_im)
    br, bi = bre_ref[...], bim_ref[...]
    bb_re = fr * br - fi * bi
    bb_im = fr * bi + fi * br
    abb_re = ar * bb_re - ai * bb_im
    abb_im = ar * bb_im + ai * bb_re
    spread_c = lambda m: jnp.where(own_c, _dot_exact01(m, tile_c), 0.0)
    spread_p = lambda m: jnp.where(own_p, _dot_exact01(m, tile_p), 0.0)
    bb2t_ref[0] = jnp.concatenate([
        jnp.concatenate([spread_c(bb_re), spread_c(abb_re)], axis=1),
        jnp.concatenate([spread_c(bb_im), spread_c(abb_im)], axis=1)], axis=0).astype(bf16)
    cct_ref[0] = jnp.concatenate([spread_p(cre_ref[...]), -spread_p(cim_ref[...])], axis=1).astype(bf16)
    grp = iota((S5_GPB, S5_SB), 1) // S5_STATE == iota((S5_GPB, S5_SB), 0)
    as_row = lambda v: jnp.sum(jnp.where(grp, _dot_exact01(v, tile_p), 0.0), axis=0, keepdims=True)
    rows = [as_row(v) for v in (ab_re, ab_im, ab2_re, ab2_im)]
    abv_ref[0] = jnp.concatenate(rows + [jnp.zeros((SUB - len(rows), S5_SB), f32)], axis=0)


def _s5_weights(lam_re, lam_im, log_dt, b_re, b_im, c_re, c_im):
    blk = lambda r, c: pl.BlockSpec((r, c), lambda j: (j, 0))
    out = lambda r, c: pl.BlockSpec((1, r, c), lambda j: (j, 0, 0))
    return pl.pallas_call(
        _s5_prep_kernel,
        grid=(S5_NB,),
        in_specs=[blk(S5_GPB, S5_STATE), blk(S5_GPB, S5_STATE), blk(S5_GPB, 1),
                  blk(S5_SB, S5_GROUP), blk(S5_SB, S5_GROUP), blk(S5_CB, S5_STATE), blk(S5_CB, S5_STATE)],
        out_specs=[out(2 * S5_SB, 2 * S5_CB), out(S5_CB, 2 * S5_SB), out(SUB, S5_SB)],
        out_shape=[jax.ShapeDtypeStruct((S5_NB, 2 * S5_SB, 2 * S5_CB), bf16),
                   jax.ShapeDtypeStruct((S5_NB, S5_CB, 2 * S5_SB), bf16),
                   jax.ShapeDtypeStruct((S5_NB, SUB, S5_SB), f32)],
        compiler_params=_cp(("arbitrary",)),
        name="s5_prep",
    )(lam_re, lam_im, log_dt[:, None], b_re.reshape(-1, S5_GROUP), b_im.reshape(-1, S5_GROUP),
      c_re.reshape(-1, S5_STATE), c_im.reshape(-1, S5_STATE))


def _s5_seq_kernel(x0_ref, x1_ref, x2_ref, x3_ref, halo0_ref, h0_ref, bb2_ref, cc_ref, abv_ref, y_ref, hout_ref,
                   xf_scr, xp_scr, bu_scr, hs_scr, yp_scr, yn_scr, h_scr, halo_scr):
    tb = pl.program_id(1)
    TL = x0_ref.shape[0]
    KB = TL // 2
    RB = BATCH * TL

    @pl.when(tb == 0)
    def _():
        h_scr[...] = h0_ref[...]
        halo_scr[...] = halo0_ref[...].astype(f32)

    chans = [slice(c * S5_CB, (c + 1) * S5_CB) for c in range(S5_CPS)]
    for c, ch in enumerate(chans):
        for b, xb_ref in enumerate((x0_ref, x1_ref, x2_ref, x3_ref)):
            xf_scr[c, b * TL:(b + 1) * TL, :] = xb_ref[:, ch].astype(f32)
        for b in range(BATCH):
            for p in range(2):
                xp_scr[c, pl.ds(2 * b + p, KB, stride=SUB), :] = xf_scr[c, pl.ds(b * TL + p, KB, stride=2), :]
    x = jnp.concatenate([xp_scr[c] for c in range(S5_CPS)], axis=1)
    xc = jnp.concatenate([halo_scr[...], x], axis=0)
    odd = (lax.broadcasted_iota(jnp.int32, (RB + SUB, 1), 0) & 1) == 1
    xprev = jnp.where(odd, pltpu.roll(xc, 1, axis=0), pltpu.roll(xc, SUB - 1, axis=0))[SUB:]
    halo_scr[...] = x[RB - SUB:]
    for c, ch in enumerate(chans):
        lhs = jnp.concatenate([x[:, ch], xprev[:, ch]], axis=1).astype(bf16)
        bu_scr[c] = _dot_nt(lhs, bb2_ref[c])
    a2 = [(abv_ref[c, 2:3, :], abv_ref[c, 3:4, :]) for c in range(S5_CPS)]

    def step(k, carry):
        r0 = k * SUB
        out = []
        for c in range(S5_CPS):
            hr, hi = carry[2 * c], carry[2 * c + 1]
            a2r, a2i = a2[c]
            bu = bu_scr[c, pl.ds(r0, SUB), :]
            nr = a2r * hr - a2i * hi + bu[:, :S5_SB]
            ni = a2r * hi + a2i * hr + bu[:, S5_SB:]
            hs_scr[c, pl.ds(r0, SUB), :S5_SB] = nr
            hs_scr[c, pl.ds(r0, SUB), S5_SB:] = ni
            out += [nr, ni]
        return tuple(out)

    init = []
    for c in range(S5_CPS):
        init += [h_scr[c, :, :S5_SB], h_scr[c, :, S5_SB:]]
    fin = tuple(init)
    for k in range(RB // SUB):
        fin = step(k, fin)
    for c, ch in enumerate(chans):
        h_scr[c, :, :S5_SB] = fin[2 * c]
        h_scr[c, :, S5_SB:] = fin[2 * c + 1]
        yp_scr[c] = _dot_nt(hs_scr[c].astype(bf16), cc_ref[c])
        for b in range(BATCH):
            for p in range(2):
                yn_scr[c, pl.ds(b * TL + p, KB, stride=2), :] = yp_scr[c, pl.ds(2 * b + p, KB, stride=SUB), :]
            y_ref[b, :, ch] = yn_scr[c, b * TL:(b + 1) * TL, :]

    @pl.when(tb == pl.num_programs(1) - 1)
    def _():
        hout_ref[...] = h_scr[...]


def _s5_seq_call(x, halo0, h0, bb2, cc, abv, *, row0, seq_len, tl):
    wsel = lambda j, t: (j, 0, 0)
    rb = BATCH * tl
    cw = S5_CPS * S5_CB
    xspec = lambda b: pl.BlockSpec((tl, cw), lambda j, t: ((row0 + b * seq_len) // tl + t, j))
    return pl.pallas_call(
        _s5_seq_kernel,
        grid=(S5_NB // S5_CPS, seq_len // tl),
        in_specs=[xspec(b) for b in range(BATCH)]
        + [pl.BlockSpec((SUB, cw), lambda j, t: (0, j)),
           pl.BlockSpec((S5_CPS, SUB, 2 * S5_SB), wsel),
           pl.BlockSpec((S5_CPS, 2 * S5_SB, 2 * S5_CB), wsel),
           pl.BlockSpec((S5_CPS, S5_CB, 2 * S5_SB), wsel),
           pl.BlockSpec((S5_CPS, SUB, S5_SB), wsel)],
        out_specs=[pl.BlockSpec((BATCH, tl, cw), lambda j, t: (0, t, j)),
                   pl.BlockSpec((S5_CPS, SUB, 2 * S5_SB), wsel)],
        out_shape=[jax.ShapeDtypeStruct((BATCH, seq_len, D_MODEL), f32),
                   jax.ShapeDtypeStruct((S5_NB, SUB, 2 * S5_SB), f32)],
        scratch_shapes=[pltpu.VMEM((S5_CPS, rb, S5_CB), f32), pltpu.VMEM((S5_CPS, rb, S5_CB), f32),
                        pltpu.VMEM((S5_CPS, rb, 2 * S5_SB), f32), pltpu.VMEM((S5_CPS, rb, 2 * S5_SB), f32),
                        pltpu.VMEM((S5_CPS, rb, S5_CB), f32), pltpu.VMEM((S5_CPS, rb, S5_CB), f32),
                        pltpu.VMEM((S5_CPS, SUB, 2 * S5_SB), f32), pltpu.VMEM((SUB, cw), f32)],
        compiler_params=_cp(("arbitrary", "arbitrary")),
        name=f"s5_seq_{seq_len}",
    )(x, x, x, x, halo0, h0, bb2, cc, abv)


def _s5_sample_kernel(x_ref, hre_ref, him_ref, bb2_ref, cc_ref, abv_ref, y_ref, ore_ref, oim_ref, hs_scr):
    nb = hre_ref.shape[0]
    bu = _dot_nt(x_ref[...], bb2_ref[0, :, :S5_CB])
    ar = abv_ref[0, 0:1, :]
    ai = abv_ref[0, 1:2, :]
    hr, hi = hre_ref[...], him_ref[...]
    for t in range(DEC_SEQ):
        rows = slice(t * nb, (t + 1) * nb)
        hr, hi = (ar * hr - ai * hi + bu[rows, :S5_SB], ar * hi + ai * hr + bu[rows, S5_SB:])
        hs_scr[rows, :S5_SB] = hr
        hs_scr[rows, S5_SB:] = hi
    y_ref[...] = _dot_nt(hs_scr[...].astype(bf16), cc_ref[0])
    ore_ref[...] = hr
    oim_ref[...] = hi


def _s5_sample_call(xt, h_re, h_im, bb2, cc, abv):
    n = xt.shape[0]
    nb = h_re.shape[0]
    wsel = lambda j: (j, 0, 0)
    st = pl.BlockSpec((nb, S5_SB), lambda j: (0, j))
    return pl.pallas_call(
        _s5_sample_kernel,
        grid=(S5_NB,),
        in_specs=[pl.BlockSpec((n, S5_CB), lambda j: (0, j)), st, st,
                  pl.BlockSpec((1, 2 * S5_SB, 2 * S5_CB), wsel),
                  pl.BlockSpec((1, S5_CB, 2 * S5_SB), wsel),
                  pl.BlockSpec((1, SUB, S5_SB), wsel)],
        out_specs=[pl.BlockSpec((n, S5_CB), lambda j: (0, j)), st, st],
        out_shape=[jax.ShapeDtypeStruct((n, D_MODEL), f32),
                   jax.ShapeDtypeStruct(h_re.shape, f32), jax.ShapeDtypeStruct(h_im.shape, f32)],
        scratch_shapes=[pltpu.VMEM((n, 2 * S5_SB), f32)],
        compiler_params=_cp(("arbitrary",)),
        name="s5_sample",
    )(xt, h_re, h_im, bb2, cc, abv)


GLA_CHUNK = 256
GLA_DIRECT = 2
S5_TL = 256


def _moe_layer(layer, x1, xnf, route, plan, cnt, wg, wu, wd, gn, xn_dtype, *, emit_x, split):
    pos, items, n_items = _moe_plan(plan, cnt, x1.shape[0])
    src = _source_rows(pos, x1.shape[0])
    os_rows = _experts_call(xnf, src, items, n_items, jnp.full((1,), layer, jnp.int32), wg, wu, wd)
    return _combine_call(x1, route, pos, os_rows, gn, xn_dtype, emit_x=emit_x, split=split)


def kernel(x_prompt, x_sample, state_gla, state_s5_re, state_s5_im, meta_tokens, norm_mix_g, norm_ffn_g, norm_final_g, gla_w_in, gla_w_a2, gla_b_a, gla_g_o, gla_w_o, s5_lambda_re, s5_lambda_im, s5_log_dt, s5_b_re, s5_b_im, s5_c_re, s5_c_im, s5_d, s5_w_glu, s5_b_glu, moe_w_rg, moe_b_rg, moe_w_re, moe_b_re, moe_w_gate, moe_w_up, moe_w_down):
    row = lambda v: v.reshape(1, -1)
    x_main = x_prompt.reshape(N_MAIN, D_MODEL)
    x_tail = jnp.concatenate([
        jnp.tile(meta_tokens.astype(x_prompt.dtype), (BATCH, 1)),
        x_sample.reshape(N_SAMPLE, D_MODEL),
        jnp.zeros((N_ROWS - N_REAL, D_MODEL), x_prompt.dtype)], axis=0)
    wg = moe_w_gate.reshape(-1, D_MODEL, D_EXPERT)
    wu = moe_w_up.reshape(-1, D_MODEL, D_EXPERT)
    wd = moe_w_down.reshape(-1, D_EXPERT, D_MODEL)

    w_in = jnp.swapaxes(gla_w_in, 1, 2)
    wa1 = jnp.pad(gla_w_in[0, :, GLA_QKVR:], ((0, 0), (0, LANES - GLA_RANK)))
    wa2 = jnp.pad(gla_w_a2.reshape(GLA_RANK, GLA_KEY), ((0, LANES - GLA_RANK), (0, 0)))
    xn, glog = _norm_gate_call(x_main, x_tail, row(norm_mix_g[0]), wa1, wa2, row(gla_b_a))
    proj = _proj_call(xn, w_in, GLA_QKVR)
    go = row(gla_g_o)
    og = jnp.zeros((N_ROWS, GLA_VAL), bf16)
    s_zero = jnp.zeros((BATCH, GLA_HEADS, GLA_DK, GLA_DV), f32)
    og, s_meta = _gla_seq_call(proj, glog, go, s_zero, og, row0=ROW_META, C=N_META, n_chunks=1, d=GLA_DIRECT)
    og, s_prompt = _gla_seq_call(proj, glog, go, s_meta, og, row0=0, C=GLA_CHUNK, n_chunks=SEQ // GLA_CHUNK,
                                 d=GLA_DIRECT)
    og, s_sample = _gla_sample_call(proj, glog, go, state_gla.reshape(DEC_BATCH, GLA_HEADS, GLA_DK, GLA_DV), og,
                                    row0=ROW_SAMPLE)
    wr, br = _router_weights(moe_w_rg[0], moe_b_rg[0], moe_w_re[0], moe_b_re[0])
    tile_row = lambda i: (i, 0)
    x1, xnf, route, plan, cnt = _mix_out_call(
        _gla_out_kernel, "gla_out", N_ROWS, [og, x_main, x_tail], [tile_row, *_main_tail_maps(MIX_TILE)],
        [gla_w_o.reshape(GLA_VAL, D_MODEL), row(norm_ffn_g[0]), wr, br])
    x2, xn2 = _moe_layer(0, x1, xnf, route, plan, cnt, wg, wu, wd, row(norm_mix_g[1]), bf16, emit_x=True, split=False)

    bb2, cc, abv = _s5_weights(s5_lambda_re[0], s5_lambda_im[0], s5_log_dt[0], s5_b_re[0], s5_b_im[0],
                               s5_c_re[0], s5_c_im[0])
    y_meta, h_meta = _s5_seq_call(xn2, jnp.zeros((SUB, D_MODEL), bf16), jnp.zeros((S5_NB, SUB, 2 * S5_SB), f32),
                                  bb2, cc, abv, row0=ROW_META, seq_len=N_META, tl=N_META)
    halo = xn2[ROW_META:ROW_SAMPLE].reshape(BATCH, N_META, D_MODEL)[:, N_META - 2:].reshape(SUB, D_MODEL)
    y_main, h_main = _s5_seq_call(xn2, halo, h_meta, bb2, cc, abv, row0=0, seq_len=SEQ, tl=S5_TL)
    xt_sample = xn2[ROW_SAMPLE:N_REAL].reshape(DEC_BATCH, DEC_SEQ, D_MODEL).transpose(1, 0, 2).reshape(N_SAMPLE, D_MODEL)
    y_samp, s5r_s, s5i_s = _s5_sample_call(
        xt_sample, state_s5_re.reshape(DEC_BATCH, S5_GROUPS * S5_STATE),
        state_s5_im.reshape(DEC_BATCH, S5_GROUPS * S5_STATE), bb2, cc, abv)
    ys_tail = jnp.concatenate([
        y_meta.reshape(N_METAROWS, D_MODEL),
        y_samp.reshape(DEC_SEQ, DEC_BATCH, D_MODEL).transpose(1, 0, 2).reshape(N_SAMPLE, D_MODEL),
        jnp.zeros((N_ROWS - N_REAL, D_MODEL), f32)], axis=0)
    wr, br = _router_weights(moe_w_rg[1], moe_b_rg[1], moe_w_re[1], moe_b_re[1])
    x3, xnf, route, plan, cnt = _mix_out_call(
        _s5_out_kernel, "s5_out", N_ROWS, [y_main.reshape(N_MAIN, D_MODEL), ys_tail, x2],
        [*_main_tail_maps(MIX_TILE), tile_row],
        [s5_w_glu.reshape(D_MODEL, D_MODEL), row(norm_mix_g[1]), row(s5_d), row(s5_b_glu),
         row(norm_ffn_g[1]), wr, br])
    y_main_out, y_tail_out = _moe_layer(1, x3, xnf, route, plan, cnt, wg, wu, wd, row(norm_final_g), f32,
                                        emit_x=False, split=True)

    y_prompt = y_main_out.reshape(BATCH, SEQ, D_MODEL)
    y_sample = y_tail_out[N_METAROWS:N_METAROWS + N_SAMPLE].reshape(DEC_BATCH, DEC_SEQ, D_MODEL)
    hfin = h_main.reshape(S5_NB, BATCH, 2, 2, S5_GPB, S5_STATE)[:, :, 1]
    s5r_p = hfin[:, :, 0].transpose(1, 0, 2, 3).reshape(1, BATCH, S5_GROUPS, S5_STATE)
    s5i_p = hfin[:, :, 1].transpose(1, 0, 2, 3).reshape(1, BATCH, S5_GROUPS, S5_STATE)
    return (y_prompt, y_sample, s_prompt[None], s5r_p, s5i_p, s_sample[None],
            s5r_s.reshape(1, DEC_BATCH, S5_GROUPS, S5_STATE), s5i_s.reshape(1, DEC_BATCH, S5_GROUPS, S5_STATE))
```
